```python
import math
import jax
import jax.numpy as jnp
from jax import lax
import numpy as np


D_MODEL = 1024
BATCH = 16
SEQ = 2048
DEPTH = 4

CTX_LEN = 256
GRID_W = 64
EPS = 1e-6
NEG = -1e30
LB_FLOOR = 1e-30
ML_HEADS = 4
ML_HD = 96
ML_W = ML_HEADS * ML_HD
ML_CHUNK = 64
M_INIT = -1e30
HG_HEADS = 4
HG_DV = 64
HG_DK = 128
HG_W = HG_HEADS * HG_DV
HG_FW = HG_HEADS * HG_DK
HG_CHUNK = 32
DA_HEADS = 4
DA_HD = 48
DA_VD = 2 * DA_HD
DA_W = DA_HEADS * DA_VD
Q_BLOCK = 128
ROPE_BASE = 10000.0

D_MIX = ML_W + HG_W + DA_W
D_FF = ((8 * D_MODEL + 3 * 256 - 1) // (3 * 256)) * 256

IN_SIZES = (ML_W, ML_W, ML_W, ML_W, 2 * ML_HEADS, 2 * ML_HEADS,
            HG_FW, 2 * HG_FW, HG_W, HG_W,
            2 * DA_HEADS * DA_HD, 2 * DA_HEADS * DA_HD, DA_W)
D_IN = sum(IN_SIZES)

kernel_name = 'hybrid_mlstm_hgrn2_diffattn_dit_block'


def split_cols(p):
    idx = np.cumsum(IN_SIZES)[:-1].tolist()
    return jnp.split(p, idx, axis=-1)


def rmsnorm(x, g):
    xf = x.astype(jnp.float32)
    y = xf * lax.rsqrt(jnp.mean(xf * xf, axis=-1, keepdims=True) + EPS)
    return (y * g.astype(jnp.float32)).astype(x.dtype)


def head_rmsnorm(h, gain):
    d = h.shape[-1]
    y = h * lax.rsqrt(jnp.mean(h * h, axis=-1, keepdims=True) + EPS)
    return y * gain.astype(jnp.float32).reshape(-1, 1, d)


def to_heads(t, h):
    b, n, _ = t.shape
    return t.reshape(b, n, h, -1).transpose(0, 2, 1, 3)


def from_heads(t):
    b, h, n, d = t.shape
    return t.transpose(0, 2, 1, 3).reshape(b, n, h * d)


def to_chunks(t, size):
    b, h, n = t.shape[:3]
    t = t.reshape(b, h, n // size, size, *t.shape[3:])
    return jnp.moveaxis(t, 2, 0)


def from_chunks(t):
    t = jnp.moveaxis(t, 0, 2)
    return t.reshape(t.shape[0], t.shape[1], -1, *t.shape[4:])


def _flip(t, rev):
    return jnp.flip(t, axis=2) if rev else t


def prefix_bidirectional(scan_fn, ctx_shared, ctx_dir, lat_shared, lat_dir, state0):
    out_c = None
    out_x = None
    for rev in (0, 1):
        oc, st = scan_fn(*[_flip(t, rev) for t in ctx_shared + ctx_dir[rev]], state0)
        ox, _ = scan_fn(*[_flip(t, rev) for t in lat_shared + lat_dir[rev]], st)
        oc = _flip(oc, rev)
        ox = _flip(ox, rev)
        out_c = oc if out_c is None else out_c + oc
        out_x = ox if out_x is None else out_x + ox
    return out_c, out_x


def mlstm_scan(q, k, v, ig, lf, state):
    size = ML_CHUNK
    mask = jnp.tril(jnp.ones((size, size), bool))

    def step(carry, inp):
        S, nv, m = carry
        qc, kc, vc, ic, fc = inp
        b = jnp.cumsum(fc, axis=-1)
        a = b + m[..., None]
        dmat = jnp.where(mask, b[..., :, None] - b[..., None, :] + ic[..., None, :], NEG)
        mt = jnp.maximum(a, jnp.max(dmat, axis=-1))
        w_inter = jnp.exp(a - mt)
        qk = jnp.einsum('bhtd,bhsd->bhts', qc, kc) * jnp.exp(dmat - mt[..., None])
        num = jnp.einsum('bhts,bhsv->bhtv', qk, vc) + w_inter[..., None] * jnp.einsum('bhtd,bhdv->bhtv', qc, S)
        den = jnp.sum(qk, axis=-1) + w_inter * jnp.einsum('bhtd,bhd->bht', qc, nv)
        h = num / jnp.maximum(jnp.abs(den), jnp.exp(-mt))[..., None]
        m_new = mt[..., -1]
        decay = jnp.exp(b[..., -1] + m - m_new)
        wk = jnp.exp(b[..., -1:] - b + ic - m_new[..., None])
        S_new = decay[..., None, None] * S + jnp.einsum('bhs,bhsd,bhsv->bhdv', wk, kc, vc)
        n_new = decay[..., None] * nv + jnp.einsum('bhs,bhsd->bhd', wk, kc)
        return (S_new, n_new, m_new), h

    xs = tuple(to_chunks(t, size) for t in (q, k, v, ig, lf))
    state, h = lax.scan(step, state, xs)
    return from_chunks(h), state


def mlstm_prep(parts, f_bias):
    q, k, v, _, i, f = parts
    b, n, _ = q.shape
    f32 = jnp.float32
    qh = to_heads(q, ML_HEADS).astype(f32)
    kh = to_heads(k, ML_HEADS).astype(f32) * (ML_HD ** -0.5)
    vh = to_heads(v, ML_HEADS).astype(f32)
    gi = i.astype(f32).reshape(b, n, 2, ML_HEADS).transpose(0, 2, 3, 1)
    gf = jax.nn.log_sigmoid((f + f_bias).astype(f32)).reshape(b, n, 2, ML_HEADS).transpose(0, 2, 3, 1)
    return (qh, kh, vh), ((gi[:, 0], gf[:, 0]), (gi[:, 1], gf[:, 1]))


def mlstm_mixer(pc, px, f_bias, gain, with_ctx):
    sc, dc = mlstm_prep(pc, f_bias)
    sx, dx = mlstm_prep(px, f_bias)
    b = sx[0].shape[0]
    state0 = (jnp.zeros((b, ML_HEADS, ML_HD, ML_HD), jnp.float32),
              jnp.zeros((b, ML_HEADS, ML_HD), jnp.float32),
              jnp.full((b, ML_HEADS), M_INIT, jnp.float32))
    hc, hx = prefix_bidirectional(mlstm_scan, sc, dc, sx, dx, state0)

    def finish(h, o):
        y = from_heads(head_rmsnorm(h, gain)) * jax.nn.sigmoid(o.astype(jnp.float32))
        return y.astype(o.dtype)

    yc = finish(hc, pc[3]) if with_ctx else None
    return yc, finish(hx, px[3])


def hgrn2_scan(q, v, k, lf, S):
    size = HG_CHUNK
    mask = jnp.tril(jnp.ones((size, size), bool))[..., None]

    def step(S, inp):
        qc, vc, kc, fc = inp
        bc = jnp.cumsum(fc, axis=2)
        rel = jnp.where(mask, bc[:, :, :, None] - bc[:, :, None], NEG)
        att = jnp.einsum('bhtk,bhtsk->bhts', qc, jnp.exp(rel) * kc[:, :, None])
        o = jnp.einsum('bhts,bhsv->bhtv', att, vc) + jnp.einsum('bhtk,bhkv->bhtv', qc * jnp.exp(bc), S)
        last = bc[:, :, -1:]
        S_new = jnp.exp(last[:, :, 0])[..., None] * S + jnp.einsum('bhsk,bhsv->bhkv', kc * jnp.exp(last - bc), vc)
        return S_new, o

    xs = tuple(to_chunks(t, size) for t in (q, v, k, lf))
    S, o = lax.scan(step, S, xs)
    return from_chunks(o), S


def hgrn2_prep(parts, lb):
    q, f, i, _ = parts
    b, n, _ = q.shape
    f32 = jnp.float32
    qh = to_heads(jax.nn.silu(q.astype(f32)), HG_HEADS)
    vh = to_heads(i.astype(f32), HG_HEADS)
    log_lb = jnp.log(jnp.maximum(lb, LB_FLOOR))
    logf = jnp.logaddexp(log_lb, jnp.log1p(-lb) + jax.nn.log_sigmoid(f.astype(f32)))
    logf = logf.reshape(b, n, 2, HG_HEADS, HG_DK).transpose(0, 2, 3, 1, 4)
    kk = -jnp.expm1(logf)
    return (qh, vh), ((kk[:, 0], logf[:, 0]), (kk[:, 1], logf[:, 1]))


def hgrn2_mixer(pc, px, lb, gain, with_ctx):
    sc, dc = hgrn2_prep(pc, lb)
    sx, dx = hgrn2_prep(px, lb)
    b = sx[0].shape[0]
    state0 = jnp.zeros((b, HG_HEADS, HG_DK, HG_DV), jnp.float32)
    oc, ox = prefix_bidirectional(hgrn2_scan, sc, dc, sx, dx, state0)

    def finish(o, g):
        y = from_heads(head_rmsnorm(o, gain)) * jax.nn.silu(g.astype(jnp.float32))
        return y.astype(g.dtype)

    yc = finish(oc, pc[3]) if with_ctx else None
    return yc, finish(ox, px[3])


def axial_angles(n, d):
    rows = n // GRID_W
    row = jnp.repeat(jnp.arange(rows), GRID_W).astype(jnp.float32)
    col = jnp.tile(jnp.arange(GRID_W), rows).astype(jnp.float32)
    half = d // 2
    inv = ROPE_BASE ** (-jnp.arange(0, half, 2, dtype=jnp.float32) / half)
    return row[:, None] * inv, col[:, None] * inv


def rope_half(x, ang):
    x1, x2 = jnp.split(x.astype(jnp.float32), 2, axis=-1)
    c, s = jnp.cos(ang), jnp.sin(ang)
    return jnp.concatenate([x1 * c - x2 * s, x1 * s + x2 * c], axis=-1)


def axial_rope(x, ang_r, ang_c):
    xr, xc = jnp.split(x, 2, axis=-1)
    return jnp.concatenate([rope_half(xr, ang_r), rope_half(xc, ang_c)], axis=-1).astype(x.dtype)


def diff_heads(t):
    b, n, _ = t.shape
    return t.reshape(b, n, DA_HEADS, 2, DA_HD).transpose(0, 2, 3, 1, 4)


def diff_attn(q, k, v, lam):
    s = jnp.einsum('bhjqd,bhjkd->bhjqk', q, k).astype(jnp.float32) * (DA_HD ** -0.5)
    p = jax.nn.softmax(s, axis=-1)
    a = p[:, :, 0] - lam * p[:, :, 1]
    return jnp.einsum('bhqk,bhkv->bhqv', a.astype(v.dtype), v)


def blocked_diff_attn(q, k, v, lam):
    b, h, _, n, d = q.shape
    qb = jnp.moveaxis(q.reshape(b, h, 2, n // Q_BLOCK, Q_BLOCK, d), 3, 0)
    ob = lax.map(lambda qq: diff_attn(qq, k, v, lam), qb)
    return jnp.moveaxis(ob, 0, 2).reshape(b, h, n, -1)


def diff_mixer(pc, px, lam_vec, gain, lam_init, ang_r, ang_c, with_ctx):
    qc, kc, vc = diff_heads(pc[0]), diff_heads(pc[1]), to_heads(pc[2], DA_HEADS)
    qx = axial_rope(diff_heads(px[0]), ang_r, ang_c)
    kx = axial_rope(diff_heads(px[1]), ang_r, ang_c)
    vx = to_heads(px[2], DA_HEADS)
    lv = lam_vec.astype(jnp.float32)
    lam = jnp.exp(jnp.sum(lv[0] * lv[1])) - jnp.exp(jnp.sum(lv[2] * lv[3])) + lam_init
    k_all = jnp.concatenate([kc, kx], axis=3)
    v_all = jnp.concatenate([vc, vx], axis=2)
    ox = blocked_diff_attn(qx, k_all, v_all, lam)

    def finish(o, dt):
        return (from_heads(head_rmsnorm(o.astype(jnp.float32), gain)) * (1.0 - lam_init)).astype(dt)

    yc = finish(diff_attn(qc, kc, vc, lam), pc[2].dtype) if with_ctx else None
    return yc, finish(ox, px[2].dtype)


def token_mixer(hc, hx, w_in, b_in, w_out, ml_fb, ml_g, lb, hg_g, da_lam, da_g, lam_init, ang_r, ang_c, with_ctx):
    pc = split_cols(hc @ w_in + b_in)
    px = split_cols(hx @ w_in + b_in)
    mc, mx = mlstm_mixer(pc[0:6], px[0:6], ml_fb, ml_g, with_ctx)
    gc, gx = hgrn2_mixer(pc[6:10], px[6:10], lb, hg_g, with_ctx)
    ac, ax = diff_mixer(pc[10:13], px[10:13], da_lam, da_g, lam_init, ang_r, ang_c, with_ctx)
    yx = jnp.concatenate([mx, gx, ax], axis=-1) @ w_out
    yc = jnp.concatenate([mc, gc, ac], axis=-1) @ w_out if with_ctx else None
    return yc, yx


def swiglu(h, wg, wu, wd):
    return (jax.nn.silu(h @ wg) * (h @ wu)) @ wd


def setup_inputs(seed: int = 0) -> dict:
    key = jax.random.key(seed)
    ks = jax.random.split(key, 24)
    f32 = jnp.float32
    L = DEPTH
    D = D_MODEL

    def nrm(k, shape, s):
        return s * jax.random.normal(k, shape, f32)

    return {
        'x': nrm(ks[0], (BATCH, SEQ, D), 1.0),
        'c': nrm(ks[1], (BATCH, D), 1.0),
        'ctx': nrm(ks[2], (BATCH, CTX_LEN, D), 1.0),
        'c_ctx': nrm(ks[3], (D,), 1.0),
        'w_ada': nrm(ks[4], (L, D, 6 * D), 0.5 * D ** -0.5),
        'b_ada': nrm(ks[5], (L, 6 * D), 0.02),
        'g_pre_mix': 1.0 + nrm(ks[6], (L, D), 0.1),
        'g_post_mix': 1.0 + nrm(ks[7], (L, D), 0.1),
        'g_pre_ffn': 1.0 + nrm(ks[8], (L, D), 0.1),
        'g_post_ffn': 1.0 + nrm(ks[9], (L, D), 0.1),
        'w_in': nrm(ks[10], (L, D, D_IN), D ** -0.5),
        'b_in': nrm(ks[11], (L, D_IN), 0.02),
        'w_out': nrm(ks[12], (L, D_MIX, D), D_MIX ** -0.5),
        'ml_f_bias': jnp.linspace(3.0, 6.0, 2 * ML_HEADS, dtype=f32)[None] + nrm(ks[13], (L, 2 * ML_HEADS), 0.1),
        'ml_norm': 1.0 + nrm(ks[14], (L, ML_W), 0.1),
        'hg_lb': nrm(ks[15], (L, 2 * HG_FW), 1.0),
        'hg_norm': 1.0 + nrm(ks[16], (L, HG_W), 0.1),
        'da_lambda': nrm(ks[17], (L, 4, DA_HD), 0.1),
        'da_norm': 1.0 + nrm(ks[18], (L, DA_VD), 0.1),
        'w_ffn_gate': nrm(ks[19], (L, D, D_FF), D ** -0.5),
        'w_ffn_up': nrm(ks[20], (L, D, D_FF), D ** -0.5),
        'w_ffn_down': nrm(ks[21], (L, D_FF, D), D_FF ** -0.5),
    }


def reference(x, c, ctx, c_ctx, w_ada, b_ada, g_pre_mix, g_post_mix, g_pre_ffn, g_post_ffn,
              w_in, b_in, w_out, ml_f_bias, ml_norm, hg_lb, hg_norm, da_lambda, da_norm,
              w_ffn_gate, w_ffn_up, w_ffn_down):
    n = x.shape[1]
    ang_r, ang_c = axial_angles(n, DA_HD)
    sm = jax.nn.softmax(hg_lb.astype(jnp.float32), axis=0)
    lbs = jnp.cumsum(sm, axis=0) - sm[0:1]
    sc_ = jax.nn.silu(c)
    scc = jax.nn.silu(c_ctx)
    for l in range(DEPTH):
        with_ctx = l < DEPTH - 1
        lam_init = 0.8 - 0.6 * math.exp(-0.3 * l)
        mx_ = (sc_ @ w_ada[l] + b_ada[l])[:, None, :]
        mc_ = scc @ w_ada[l] + b_ada[l]
        sx1, ax1, gx1, sx2, ax2, gx2 = jnp.split(mx_, 6, axis=-1)
        sc1, ac1, gc1, sc2, ac2, gc2 = jnp.split(mc_, 6, axis=-1)
        hx = rmsnorm(x, g_pre_mix[l]) * (1.0 + ax1) + sx1
        hc = rmsnorm(ctx, g_pre_mix[l]) * (1.0 + ac1) + sc1
        yc, yx = token_mixer(hc, hx, w_in[l], b_in[l], w_out[l], ml_f_bias[l], ml_norm[l], lbs[l],
                             hg_norm[l], da_lambda[l], da_norm[l], lam_init, ang_r, ang_c, with_ctx)
        x = x + gx1 * rmsnorm(yx, g_post_mix[l])
        hx = rmsnorm(x, g_pre_ffn[l]) * (1.0 + ax2) + sx2
        x = x + gx2 * rmsnorm(swiglu(hx, w_ffn_gate[l], w_ffn_up[l], w_ffn_down[l]), g_post_ffn[l])
        if with_ctx:
            ctx = ctx + gc1 * rmsnorm(yc, g_post_mix[l])
            hcf = rmsnorm(ctx, g_pre_ffn[l]) * (1.0 + ac2) + sc2
            ctx = ctx + gc2 * rmsnorm(swiglu(hcf, w_ffn_gate[l], w_ffn_up[l], w_ffn_down[l]), g_post_ffn[l])
    return x
```

```python
import functools
import math

import jax
import jax.numpy as jnp
from jax import lax
from jax.experimental import pallas as pl
from jax.experimental.pallas import tpu as pltpu

F32 = jnp.float32
BF16 = jnp.bfloat16
HIGHEST = lax.Precision.HIGHEST

LANE = 128
VMEM_LIMIT = 52 * 1024 * 1024

EPS = 1e-6
NEG = -1e30
LB_FLOOR = 1e-30
M_INIT = -1e30
GRID_W = 64
ROPE_BASE = 10000.0

HEADS = 4
ML_HD = 96
HG_DV = 64
HG_DK = 128
DA_HD = 48
DA_VD = 2 * DA_HD
SLAB = HEADS * LANE

ML_CHUNK = 256
HG_CHUNK = 128
HG_DIAG = 8
ATT_TQ = 256

OFF_MLQ, OFF_MLK, OFF_MLV, OFF_MLO = 0, SLAB, 2 * SLAB, 3 * SLAB
OFF_GATE = 4 * SLAB
OFF_HGQ = OFF_GATE + LANE
OFF_HGF = OFF_HGQ + SLAB
OFF_HGV = OFF_HGF + 2 * SLAB
OFF_HGG = OFF_HGV + SLAB
OFF_DAQ = OFF_HGG + SLAB
OFF_DAK = OFF_DAQ + SLAB
OFF_DAV = OFF_DAK + SLAB
NP_IN = OFF_DAV + SLAB


def _cparams(sem):
    return pltpu.CompilerParams(dimension_semantics=sem, vmem_limit_bytes=VMEM_LIMIT)


def _silu(x):
    return x * jax.nn.sigmoid(x)


def _log_sigmoid(z):
    return jnp.minimum(z, 0.0) - jnp.log1p(jnp.exp(-jnp.abs(z)))


def _expm1(x):
    return jnp.where(jnp.abs(x) < 1e-2, x * (1.0 + x * (0.5 + x * (1.0 / 6.0))), jnp.exp(x) - 1.0)


def _rms(x, g):
    return x * lax.rsqrt(jnp.mean(x * x, axis=-1, keepdims=True) + EPS) * g


def _dot(a, b):
    return jnp.dot(a, b, preferred_element_type=F32)


def _dot_nt(a, b):
    return lax.dot_general(a, b, (((1,), (1,)), ((), ())), preferred_element_type=F32)


def _dot_tn(a, b):
    return lax.dot_general(a, b, (((0,), (0,)), ((), ())), preferred_element_type=F32)


def _ada_kernel(s_ref, w_ref, b_ref, o_ref):
    s = _silu(s_ref[...])
    o_ref[0] = jnp.dot(s, w_ref[0], precision=HIGHEST, preferred_element_type=F32) + b_ref[0]


def _ada(cc, w_ada, b_ada):
    depth, d, d6 = w_ada.shape
    mp = cc.shape[0]
    tn = 1024
    return pl.pallas_call(
        _ada_kernel,
        grid=(depth, d6 // tn),
        in_specs=[pl.BlockSpec((mp, d), lambda l, j: (0, 0)),
                  pl.BlockSpec((1, d, tn), lambda l, j: (l, 0, j)),
                  pl.BlockSpec((1, 1, tn), lambda l, j: (l, 0, j))],
        out_specs=pl.BlockSpec((1, mp, tn), lambda l, j: (l, 0, j)),
        out_shape=jax.ShapeDtypeStruct((depth, mp, d6), F32),
        compiler_params=_cparams(("arbitrary", "arbitrary")),
        name="ada_mod",
    )(cc, w_ada, b_ada.reshape(depth, 1, d6))


def _mod_specs(layer, nb, idxs, d):
    specs = []
    for j in idxs:
        specs.append(pl.BlockSpec((1, 1, 1, d), lambda b, t, j=j: (layer, b, 0, j)))
        specs.append(pl.BlockSpec((1, 1, 1, d), lambda b, t, j=j: (layer, nb, 0, j)))
    return specs


def _row_is_ctx(tm, ctx_len):
    rows = pl.program_id(1) * tm + lax.broadcasted_iota(jnp.int32, (tm, 1), 0)
    return rows < ctx_len


def _inproj_kernel(x_ref, sx_ref, sc_ref, ax_ref, ac_ref, g_ref, w_ref, b_ref,
                   cos_ref, sna_ref, snb_ref, llb_ref, l1m_ref,
                   mlq_ref, mlk_ref, mlv_ref, mlo_ref, gate_ref,
                   hgq_ref, hgf_ref, hgv_ref, hgg_ref, daq_ref, dak_ref, dav_ref,
                   *, tm, ctx_len):
    is_ctx = _row_is_ctx(tm, ctx_len)
    shift = jnp.where(is_ctx, sc_ref[0, 0], sx_ref[0, 0])
    scale = 1.0 + jnp.where(is_ctx, ac_ref[0, 0], ax_ref[0, 0])
    h = (_rms(x_ref[0], g_ref[0]) * scale + shift).astype(BF16)

    def proj(off, width):
        return _dot(h, w_ref[0, :, off:off + width]) + b_ref[0, :, off:off + width]

    mlq_ref[0] = proj(OFF_MLQ, SLAB).astype(BF16)
    mlk_ref[0] = (proj(OFF_MLK, SLAB) * (ML_HD ** -0.5)).astype(BF16)
    mlv_ref[0] = proj(OFF_MLV, SLAB).astype(BF16)
    mlo_ref[0] = jax.nn.sigmoid(proj(OFF_MLO, SLAB)).astype(BF16)

    gt = proj(OFF_GATE, LANE)
    lane = lax.broadcasted_iota(jnp.int32, gt.shape, 1)
    gate_ref[0] = jnp.where((lane >= 2 * HEADS) & (lane < 4 * HEADS), _log_sigmoid(gt), gt)

    hgq_ref[0] = _silu(proj(OFF_HGQ, SLAB)).astype(BF16)
    t2 = l1m_ref[0] + _log_sigmoid(proj(OFF_HGF, 2 * SLAB))
    t1 = llb_ref[0]
    hgf_ref[0] = jnp.maximum(t1, t2) + jnp.log1p(jnp.exp(-jnp.abs(t1 - t2)))
    hgv_ref[0] = proj(OFF_HGV, SLAB).astype(BF16)
    hgg_ref[0] = _silu(proj(OFF_HGG, SLAB)).astype(BF16)

    cos, sna, snb = cos_ref[...], sna_ref[...], snb_ref[...]
    half = DA_HD // 4

    def rope(p):
        outs = []
        for hd in range(HEADS):
            xh = p[:, hd * LANE:(hd + 1) * LANE]
            outs.append(xh * cos + pltpu.roll(xh, LANE - half, 1) * sna + pltpu.roll(xh, half, 1) * snb)
        return jnp.concatenate(outs, axis=1)

    daq_ref[0] = (rope(proj(OFF_DAQ, SLAB)) * (DA_HD ** -0.5)).astype(BF16)
    dak_ref[0] = rope(proj(OFF_DAK, SLAB)).astype(BF16)
    dav_ref[0] = proj(OFF_DAV, SLAB).astype(BF16)


def _inproj(xs, mod4, g_pre, w_pad, b_pad, rope_tabs, llb, l1m, layer, ctx_len, tm):
    nb, n_all, d = xs.shape
    kern = functools.partial(_inproj_kernel, tm=tm, ctx_len=ctx_len)
    lsel = lambda b, t: (layer, 0, 0)
    tile = lambda width: pl.BlockSpec((1, tm, width), lambda b, t: (b, t, 0))
    tab = pl.BlockSpec((tm, LANE), lambda b, t: (t, 0))
    bf = lambda width: jax.ShapeDtypeStruct((nb, n_all, width), BF16)
    f32 = lambda width: jax.ShapeDtypeStruct((nb, n_all, width), F32)
    out_shapes = [bf(SLAB)] * 4 + [f32(LANE)] + [bf(SLAB), f32(2 * SLAB), bf(SLAB), bf(SLAB)] + [bf(SLAB)] * 3
    out_specs = [tile(s.shape[-1]) for s in out_shapes]
    return pl.pallas_call(
        kern,
        grid=(nb, n_all // tm),
        in_specs=[tile(d)] + _mod_specs(layer, nb, (0, 1), d) + [
            pl.BlockSpec((1, 1, d), lsel),
            pl.BlockSpec((1, d, NP_IN), lsel),
            pl.BlockSpec((1, 1, NP_IN), lsel),
            tab, tab, tab,
            pl.BlockSpec((1, 1, 2 * SLAB), lsel),
            pl.BlockSpec((1, 1, 2 * SLAB), lsel)],
        out_specs=out_specs,
        out_shape=out_shapes,
        compiler_params=_cparams(("parallel", "arbitrary")),
        name="in_proj",
    )(xs, mod4, mod4, mod4, mod4, g_pre, w_pad, b_pad, *rope_tabs, llb, l1m)


def _ml_chunk(q, k, v_ext, i_row, lf_row, s_ext, m, rev):
    size = q.shape[0]
    ii = lax.broadcasted_iota(jnp.int32, (size, size), 0)
    jj = lax.broadcasted_iota(jnp.int32, (size, size), 1)
    eye = ii == jj
    cum = jnp.where((ii >= jj) if rev else (ii <= jj), 1.0, 0.0).astype(F32)
    b_row = jnp.dot(jnp.broadcast_to(lf_row, (8, size)), cum, precision=HIGHEST,
                    preferred_element_type=F32)[0:1]
    g_row = i_row - b_row
    b_col = jnp.sum(jnp.where(eye, b_row, 0.0), axis=1, keepdims=True)
    g_col = jnp.sum(jnp.where(eye, g_row, 0.0), axis=1, keepdims=True)
    seen = (jj >= ii) if rev else (jj <= ii)
    dmat = jnp.where(seen, b_col + g_row, NEG)
    a_col = b_col + m
    mt = jnp.maximum(a_col, jnp.max(dmat, axis=1, keepdims=True))
    w_inter = jnp.exp(a_col - mt)
    qk = _dot_nt(q, k) * jnp.exp(dmat - mt)
    res = _dot(qk.astype(BF16), v_ext) + w_inter * _dot(q, s_ext.astype(BF16))
    lane = lax.broadcasted_iota(jnp.int32, res.shape, 1)
    den = jnp.sum(jnp.where(lane == ML_HD, res, 0.0), axis=1, keepdims=True)
    hout = res * (1.0 / jnp.maximum(jnp.abs(den), jnp.exp(-mt)))
    last = 0 if rev else size - 1
    m_new = mt[last:last + 1]
    b_last = b_col[last:last + 1]
    decay = jnp.exp(b_last + m - m_new)
    wk = jnp.exp(b_last + g_col - m_new)
    s_new = decay * s_ext + _dot_tn((wk * k.astype(F32)).astype(BF16), v_ext)
    return hout, s_new, m_new


def _mlstm_kernel(q_ref, k_ref, v_ref, o_ref, gt_ref, gain_ref, y_ref, hf_ref, hb_ref, *, nc_ctx, nc_all, size):
    hd = pl.program_id(1)
    lane = lax.broadcasted_iota(jnp.int32, (size, LANE), 1)
    nc_lat = nc_all - nc_ctx

    def load(c):
        sl = pl.ds(pl.multiple_of(c * size, size), size)
        v = v_ref[0, sl, :]
        return q_ref[0, sl, :], k_ref[0, sl, :], jnp.where(lane == ML_HD, jnp.ones_like(v), v), sl

    def step(c, carry):
        s_f, m_f, s_b, m_b = carry
        q, k, v, sl = load(c)
        h, s_f, m_f = _ml_chunk(q, k, v, gt_ref[0, hd, pl.ds(c, 1), :], gt_ref[0, 2 * HEADS + hd, pl.ds(c, 1), :],
                                s_f, m_f, False)
        hf_ref[sl, :] = h
        cb = jnp.where(c < nc_ctx, nc_ctx - 1 - c, nc_ctx + nc_lat - 1 - (c - nc_ctx))
        q, k, v, sl = load(cb)
        h, s_b, m_b = _ml_chunk(q, k, v, gt_ref[0, HEADS + hd, pl.ds(cb, 1), :],
                                gt_ref[0, 3 * HEADS + hd, pl.ds(cb, 1), :], s_b, m_b, True)
        hb_ref[sl, :] = h
        return s_f, m_f, s_b, m_b

    s0 = jnp.zeros((LANE, LANE), F32)
    m0 = jnp.full((1, 1), M_INIT, F32)
    lax.fori_loop(0, nc_all, step, (s0, m0, s0, m0))

    def finish(c, _):
        sl = pl.ds(pl.multiple_of(c * size, size), size)
        h = jnp.where(lane < ML_HD, hf_ref[sl, :] + hb_ref[sl, :], 0.0)
        y = h * lax.rsqrt(jnp.sum(h * h, axis=1, keepdims=True) * (1.0 / ML_HD) + EPS) * gain_ref[0]
        y_ref[0, sl, :] = (y * o_ref[0, sl, :].astype(F32)).astype(BF16)
        return 0

    lax.fori_loop(0, nc_all, finish, 0)


def _mlstm(mlq, mlk, mlv, mlo, gates_t, gain, layer, ctx_len):
    nb, n_all, _ = mlq.shape
    size = ML_CHUNK
    nc_all = n_all // size
    kern = functools.partial(_mlstm_kernel, nc_ctx=ctx_len // size, nc_all=nc_all, size=size)
    slab = pl.BlockSpec((1, n_all, LANE), lambda b, h: (b, 0, h))
    return pl.pallas_call(
        kern,
        grid=(nb, HEADS),
        in_specs=[slab, slab, slab, slab,
                  pl.BlockSpec((1, 4 * HEADS, nc_all, size), lambda b, h: (b, 0, 0, 0)),
                  pl.BlockSpec((1, 1, LANE), lambda b, h: (layer * HEADS + h, 0, 0))],
        out_specs=slab,
        out_shape=jax.ShapeDtypeStruct((nb, n_all, SLAB), BF16),
        scratch_shapes=[pltpu.VMEM((n_all, LANE), F32), pltpu.VMEM((n_all, LANE), F32)],
        compiler_params=_cparams(("parallel", "arbitrary")),
        name="mlstm",
    )(mlq, mlk, mlv, mlo, gates_t, gain)


def _hg_chunk(q, lf, v, st, rev):
    size = q.shape[0]
    row = lax.broadcasted_iota(jnp.int32, (size, LANE), 0)
    kk = -_expm1(lf)
    bc = lf
    s = 1
    while s < size:
        if rev:
            bc = bc + jnp.where(row < size - s, pltpu.roll(bc, size - s, 0), 0.0)
        else:
            bc = bc + jnp.where(row >= s, pltpu.roll(bc, s, 0), 0.0)
        s *= 2

    ii = lax.broadcasted_iota(jnp.int32, (size, size), 0)
    jj = lax.broadcasted_iota(jnp.int32, (size, size), 1)
    att = jnp.zeros((size, size), F32)

    c = HG_DIAG
    while c < size:
        blk = 2 * c
        ridx = c if rev else c - 1
        bc3 = bc.reshape(size // blk, blk, LANE)
        ref = jnp.broadcast_to(bc3[:, ridx:ridx + 1, :], bc3.shape).reshape(size, LANE)
        later = (row % blk) >= c
        is_q = (~later) if rev else later
        e = jnp.exp(jnp.where(is_q, bc - ref, ref - bc))
        qc = jnp.where(is_q, q * e, 0.0).astype(BF16)
        kc = jnp.where(is_q, 0.0, kk * e).astype(BF16)
        att = att + jnp.where((ii // blk) == (jj // blk), _dot_nt(qc, kc), 0.0)
        c = blk

    nblk = size // HG_DIAG
    bc3 = bc.reshape(nblk, HG_DIAG, LANE)
    kk3 = kk.reshape(nblk, HG_DIAG, LANE)
    rin = row % HG_DIAG
    for j in range(HG_DIAG):
        bcj = jnp.broadcast_to(bc3[:, j:j + 1, :], bc3.shape).reshape(size, LANE)
        kkj = jnp.broadcast_to(kk3[:, j:j + 1, :], kk3.shape).reshape(size, LANE)
        valid = (rin <= j) if rev else (rin >= j)
        p = jnp.where(valid, q * jnp.exp(jnp.minimum(bc - bcj, 0.0)) * kkj, 0.0)
        aj = jnp.sum(p, axis=1, keepdims=True)
        att = att + jnp.where(jj == (ii // HG_DIAG) * HG_DIAG + j, aj, 0.0)

    last = 0 if rev else size - 1
    bl = bc[last:last + 1]
    o = _dot(att.astype(BF16), v) + _dot_nt((q * jnp.exp(bc)).astype(BF16), st.astype(BF16))
    st_new = st * jnp.exp(bl) + _dot_tn(v, (kk * jnp.exp(bl - bc)).astype(BF16))
    return o, st_new


def _hgrn_kernel(q_ref, lf0_ref, lf1_ref, v_ref, g_ref, gain_ref, y_ref, of_ref, ob_ref, *, nc_ctx, nc_all, size):
    nc_lat = nc_all - nc_ctx

    def step(c, carry):
        st_f, st_b = carry
        sl = pl.ds(pl.multiple_of(c * size, size), size)
        o, st_f = _hg_chunk(q_ref[0, sl, :].astype(F32), lf0_ref[0, sl, :], v_ref[0, sl, :], st_f, False)
        of_ref[sl, :] = o
        cb = jnp.where(c < nc_ctx, nc_ctx - 1 - c, nc_ctx + nc_lat - 1 - (c - nc_ctx))
        sl = pl.ds(pl.multiple_of(cb * size, size), size)
        o, st_b = _hg_chunk(q_ref[0, sl, :].astype(F32), lf1_ref[0, sl, :], v_ref[0, sl, :], st_b, True)
        ob_ref[sl, :] = o
        return st_f, st_b

    st0 = jnp.zeros((LANE, LANE), F32)
    lax.fori_loop(0, nc_all, step, (st0, st0))

    def finish(c, _):
        sl = pl.ds(pl.multiple_of(c * size, size), size)
        o = of_ref[sl, :] + ob_ref[sl, :]
        y = o * lax.rsqrt(jnp.sum(o * o, axis=1, keepdims=True) * (1.0 / HG_DV) + EPS) * gain_ref[0]
        y_ref[0, sl, :] = (y * g_ref[0, sl, :].astype(F32)).astype(BF16)
        return 0

    lax.fori_loop(0, nc_all, finish, 0)


def _hgrn(hgq, hgf, hgv, hgg, gain, layer, ctx_len):
    nb, n_all, _ = hgq.shape
    size = HG_CHUNK
    nc_all = n_all // size
    kern = functools.partial(_hgrn_kernel, nc_ctx=ctx_len // size, nc_all=nc_all, size=size)
    slab = pl.BlockSpec((1, n_all, LANE), lambda b, h: (b, 0, h))
    return pl.pallas_call(
        kern,
        grid=(nb, HEADS),
        in_specs=[slab, slab,
                  pl.BlockSpec((1, n_all, LANE), lambda b, h: (b, 0, HEADS + h)),
                  slab, slab,
                  pl.BlockSpec((1, 1, LANE), lambda b, h: (layer * HEADS + h, 0, 0))],
        out_specs=slab,
        out_shape=jax.ShapeDtypeStruct((nb, n_all, SLAB), BF16),
        scratch_shapes=[pltpu.VMEM((n_all, LANE), F32), pltpu.VMEM((n_all, LANE), F32)],
        compiler_params=_cparams(("parallel", "arbitrary")),
        name="hgrn2",
    )(hgq, hgf, hgf, hgv, hgg, gain)


def _attn_kernel(q_ref, k_ref, v_ref, lam_ref, gain_ref, y_ref, *, tq, ctx_len, lam_init):
    lv = lam_ref[0]
    lam = (jnp.exp(jnp.sum(lv[0:1] * lv[1:2], axis=1, keepdims=True))
           - jnp.exp(jnp.sum(lv[2:3] * lv[3:4], axis=1, keepdims=True)) + lam_init)
    q = q_ref[0]
    lane = lax.broadcasted_iota(jnp.int32, q.shape, 1)
    q0 = jnp.where(lane < DA_HD, q, jnp.zeros_like(q))
    q1 = jnp.where(lane < DA_HD, jnp.zeros_like(q), q)

    def attend(nk):
        k = k_ref[0, 0:nk, :]
        v = v_ref[0, 0:nk, :]

        def probs(qm):
            s = _dot_nt(qm, k)
            p = jnp.exp(s - jnp.max(s, axis=1, keepdims=True))
            return p, 1.0 / jnp.sum(p, axis=1, keepdims=True)

        p0, r0 = probs(q0)
        p1, r1 = probs(q1)
        a = p0 * r0 - (lam * r1) * p1
        o = _dot(a.astype(BF16), v)
        y = o * lax.rsqrt(jnp.sum(o * o, axis=1, keepdims=True) * (1.0 / DA_VD) + EPS) * gain_ref[0]
        y_ref[0] = (y * (1.0 - lam_init)).astype(BF16)

    is_ctx = pl.program_id(2) * tq < ctx_len

    @pl.when(is_ctx)
    def _():
        attend(ctx_len)

    @pl.when(jnp.logical_not(is_ctx))
    def _():
        attend(k_ref.shape[1])


def _attn(daq, dak, dav, lam_pad, gain, layer, ctx_len, lam_init):
    nb, n_all, _ = daq.shape
    tq = ATT_TQ
    kern = functools.partial(_attn_kernel, tq=tq, ctx_len=ctx_len, lam_init=lam_init)
    full = pl.BlockSpec((1, n_all, LANE), lambda b, h, t: (b, 0, h))
    qtile = pl.BlockSpec((1, tq, LANE), lambda b, h, t: (b, t, h))
    return pl.pallas_call(
        kern,
        grid=(nb, HEADS, n_all // tq),
        in_specs=[qtile, full, full,
                  pl.BlockSpec((1, 4, LANE), lambda b, h, t: (layer, 0, 0)),
                  pl.BlockSpec((1, 1, LANE), lambda b, h, t: (layer, 0, 0))],
        out_specs=qtile,
        out_shape=jax.ShapeDtypeStruct((nb, n_all, SLAB), BF16),
        compiler_params=_cparams(("parallel", "arbitrary", "arbitrary")),
        name="diff_attn",
    )(daq, dak, dav, lam_pad, gain)


def _outproj_kernel(yml_ref, yhg_ref, yda_ref, x_ref, gx_ref, gc_ref, ax_ref, ac_ref, sx_ref, sc_ref,
                    w_ref, gpost_ref, gpre_ref, x1_ref, h2_ref, *, tm, ctx_len):
    is_ctx = _row_is_ctx(tm, ctx_len)
    y = (_dot(yml_ref[0], w_ref[0, 0:SLAB, :]) + _dot(yhg_ref[0], w_ref[0, SLAB:2 * SLAB, :])
         + _dot(yda_ref[0], w_ref[0, 2 * SLAB:3 * SLAB, :]))
    gate = jnp.where(is_ctx, gc_ref[0, 0], gx_ref[0, 0])
    x1 = x_ref[0] + gate * _rms(y, gpost_ref[0])
    x1_ref[0] = x1
    scale = 1.0 + jnp.where(is_ctx, ac_ref[0, 0], ax_ref[0, 0])
    shift = jnp.where(is_ctx, sc_ref[0, 0], sx_ref[0, 0])
    h2_ref[0] = (_rms(x1, gpre_ref[0]) * scale + shift).astype(BF16)


def _outproj(yml, yhg, yda, xs, mod4, w_out_pad, g_post, g_pre, layer, ctx_len, tm):
    nb, n_all, d = xs.shape
    kern = functools.partial(_outproj_kernel, tm=tm, ctx_len=ctx_len)
    lsel = lambda b, t: (layer, 0, 0)
    tile = lambda width: pl.BlockSpec((1, tm, width), lambda b, t: (b, t, 0))
    return pl.pallas_call(
        kern,
        grid=(nb, n_all // tm),
        in_specs=[tile(SLAB), tile(SLAB), tile(SLAB), tile(d)] + _mod_specs(layer, nb, (2, 4, 3), d) + [
            pl.BlockSpec((1, 3 * SLAB, d), lsel),
            pl.BlockSpec((1, 1, d), lsel),
            pl.BlockSpec((1, 1, d), lsel)],
        out_specs=[tile(d), tile(d)],
        out_shape=[jax.ShapeDtypeStruct((nb, n_all, d), F32), jax.ShapeDtypeStruct((nb, n_all, d), BF16)],
        compiler_params=_cparams(("parallel", "arbitrary")),
        name="out_proj",
    )(yml, yhg, yda, xs, mod4, mod4, mod4, mod4, mod4, mod4, w_out_pad, g_post, g_pre)


def _ffn_kernel(h_ref, x_ref, gx_ref, gc_ref, wg_ref, wu_ref, wd_ref, gpost_ref, o_ref, acc_ref, *, tm, ctx_len):
    j = pl.program_id(2)

    @pl.when(j == 0)
    def _():
        acc_ref[...] = jnp.zeros_like(acc_ref)

    h = h_ref[0]
    act = _silu(_dot(h, wg_ref[0])) * _dot(h, wu_ref[0])
    acc_ref[...] += _dot(act.astype(BF16), wd_ref[0])

    @pl.when(j == pl.num_programs(2) - 1)
    def _():
        gate = jnp.where(_row_is_ctx(tm, ctx_len), gc_ref[0, 0], gx_ref[0, 0])
        o_ref[0] = x_ref[0] + gate * _rms(acc_ref[...], gpost_ref[0])


def _ffn(h2, x1, mod4, wg, wu, wd, g_post, layer, ctx_len, tm, tf):
    nb, n_all, d = x1.shape
    dff = wg.shape[-1]
    kern = functools.partial(_ffn_kernel, tm=tm, ctx_len=ctx_len)
    tile = pl.BlockSpec((1, tm, d), lambda b, t, j: (b, t, 0))
    mods = [pl.BlockSpec((1, 1, 1, d), lambda b, t, j: (layer, b, 0, 5)),
            pl.BlockSpec((1, 1, 1, d), lambda b, t, j: (layer, nb, 0, 5))]
    return pl.pallas_call(
        kern,
        grid=(nb, n_all // tm, dff // tf),
        in_specs=[tile, tile] + mods + [
            pl.BlockSpec((1, d, tf), lambda b, t, j: (layer, 0, j)),
            pl.BlockSpec((1, d, tf), lambda b, t, j: (layer, 0, j)),
            pl.BlockSpec((1, tf, d), lambda b, t, j: (layer, j, 0)),
            pl.BlockSpec((1, 1, d), lambda b, t, j: (layer, 0, 0))],
        out_specs=tile,
        out_shape=jax.ShapeDtypeStruct((nb, n_all, d), F32),
        scratch_shapes=[pltpu.VMEM((tm, d), F32)],
        compiler_params=_cparams(("parallel", "arbitrary", "arbitrary")),
        name="ffn",
    )(h2, x1, mod4, mod4, wg, wu, wd, g_post)


def _pad_heads(w, hd):
    lead = w.shape[:-1]
    w = w.reshape(*lead, HEADS, hd)
    w = jnp.pad(w, [(0, 0)] * len(lead) + [(0, 0), (0, LANE - hd)])
    return w.reshape(*lead, SLAB)


def _pad_in_cols(w, extra_f=None):
    sizes = (4 * ML_HD,) * 4 + (2 * HEADS, 2 * HEADS, HEADS * HG_DK, 2 * HEADS * HG_DK, HEADS * HG_DV, HEADS * HG_DV,
             2 * HEADS * DA_HD, 2 * HEADS * DA_HD, HEADS * DA_VD)
    parts = []
    off = 0
    for s in sizes:
        parts.append(w[..., off:off + s])
        off += s
    gi, gf = parts[4], parts[5]
    if extra_f is not None:
        gf = gf + extra_f
    gate = jnp.concatenate([gi, gf], axis=-1)
    gate = jnp.pad(gate, [(0, 0)] * (gate.ndim - 1) + [(0, LANE - 4 * HEADS)])
    return jnp.concatenate([
        _pad_heads(parts[0], ML_HD), _pad_heads(parts[1], ML_HD), _pad_heads(parts[2], ML_HD), _pad_heads(parts[3], ML_HD),
        gate, parts[6], parts[7], _pad_heads(parts[8], HG_DV), _pad_heads(parts[9], HG_DV),
        _pad_heads(parts[10], DA_VD), _pad_heads(parts[11], DA_VD), _pad_heads(parts[12], DA_VD)], axis=-1)


def _pad_out_rows(w_out):
    depth, _, d = w_out.shape
    wt = jnp.swapaxes(w_out, 1, 2)
    ml, hg, da = wt[..., :4 * ML_HD], wt[..., 4 * ML_HD:4 * ML_HD + HEADS * HG_DV], wt[..., 4 * ML_HD + HEADS * HG_DV:]
    wp = jnp.concatenate([_pad_heads(ml, ML_HD), _pad_heads(hg, HG_DV), _pad_heads(da, DA_VD)], axis=-1)
    return jnp.swapaxes(wp, 1, 2)


def _rope_tables(n_lat, ctx_len):
    rows = n_lat // GRID_W
    row = jnp.repeat(jnp.arange(rows), GRID_W).astype(F32)
    col = jnp.tile(jnp.arange(GRID_W), rows).astype(F32)
    half = DA_HD // 2
    inv = ROPE_BASE ** (-jnp.arange(0, half, 2, dtype=F32) / half)
    ang_r, ang_c = row[:, None] * inv, col[:, None] * inv
    zero = jnp.zeros_like(ang_r)

    def one_map(fr, fc, sel):
        r, c = fr(ang_r), fc(ang_c)
        if sel == "cos":
            return jnp.concatenate([r, r, c, c], axis=1)
        if sel == "a":
            return jnp.concatenate([-r, zero, -c, zero], axis=1)
        return jnp.concatenate([zero, r, zero, c], axis=1)

    tabs = []
    for sel, fn in (("cos", jnp.cos), ("a", jnp.sin), ("b", jnp.sin)):
        m = one_map(fn, fn, sel)
        lat = jnp.pad(jnp.concatenate([m, m], axis=1), ((0, 0), (0, LANE - 2 * DA_HD)))
        fill = 1.0 if sel == "cos" else 0.0
        ctx_rows = jnp.pad(jnp.full((ctx_len, 2 * DA_HD), fill, F32), ((0, 0), (0, LANE - 2 * DA_HD)))
        tabs.append(jnp.concatenate([ctx_rows, lat], axis=0))
    return tabs


def kernel(x, c, ctx, c_ctx, w_ada, b_ada, g_pre_mix, g_post_mix, g_pre_ffn, g_post_ffn, w_in, b_in, w_out,
           ml_f_bias, ml_norm, hg_lb, hg_norm, da_lambda, da_norm, w_ffn_gate, w_ffn_up, w_ffn_down):
    nb, n_lat, d = x.shape
    ctx_len = ctx.shape[1]
    depth = w_ada.shape[0]
    n_all = ctx_len + n_lat

    w_pad = _pad_in_cols(w_in).astype(BF16)
    b_pad = _pad_in_cols(b_in, extra_f=ml_f_bias)[:, None, :]
    w_out_pad = _pad_out_rows(w_out).astype(BF16)
    wg, wu, wd = w_ffn_gate.astype(BF16), w_ffn_up.astype(BF16), w_ffn_down.astype(BF16)
    ml_gain = _pad_heads(ml_norm, ML_HD).reshape(depth * HEADS, 1, LANE)
    hg_gain = _pad_heads(hg_norm, HG_DV).reshape(depth * HEADS, 1, LANE)
    da_gain = jnp.pad(da_norm, ((0, 0), (0, LANE - DA_VD)))[:, None, :]
    lam_pad = jnp.pad(da_lambda.astype(F32), ((0, 0), (0, 0), (0, LANE - DA_HD)))
    sm = jax.nn.softmax(hg_lb.astype(F32), axis=0)
    lbs = jnp.cumsum(sm, axis=0) - sm[0:1]
    llb = jnp.log(jnp.maximum(lbs, LB_FLOOR))[:, None, :]
    l1m = jnp.log1p(-lbs)[:, None, :]
    rope_tabs = _rope_tables(n_lat, ctx_len)
    r3 = lambda g: g[:, None, :]

    mp = -(-(nb + 1) // 8) * 8
    cc = jnp.concatenate([c, c_ctx[None, :], jnp.zeros((mp - nb - 1, d), F32)], axis=0)
    mod4 = _ada(cc, w_ada, b_ada).reshape(depth, mp, 1, 6 * d)

    xs = jnp.concatenate([ctx, x], axis=1)
    for l in range(depth):
        lam_init = 0.8 - 0.6 * math.exp(-0.3 * l)
        (mlq, mlk, mlv, mlo, gates, hgq, hgf, hgv, hgg, daq, dak, dav) = _inproj(
            xs, mod4, r3(g_pre_mix), w_pad, b_pad, rope_tabs, llb, l1m, l, ctx_len, 256)
        gates_t = jnp.swapaxes(gates[:, :, :4 * HEADS], 1, 2).reshape(nb, 4 * HEADS, n_all // ML_CHUNK, ML_CHUNK)
        yml = _mlstm(mlq, mlk, mlv, mlo, gates_t, ml_gain, l, ctx_len)
        yhg = _hgrn(hgq, hgf, hgv, hgg, hg_gain, l, ctx_len)
        yda = _attn(daq, dak, dav, lam_pad, da_gain, l, ctx_len, lam_init)
        x1, h2 = _outproj(yml, yhg, yda, xs, mod4, w_out_pad, r3(g_post_mix), r3(g_pre_ffn), l, ctx_len, 256)
        xs = _ffn(h2, x1, mod4, wg, wu, wd, r3(g_post_ffn), l, ctx_len, 768, 256)
    return xs[:, ctx_len:, :]
```

```python
import functools
import math

import numpy as np
import jax
import jax.numpy as jnp
from jax import lax
from jax.experimental import pallas as pl
from jax.experimental.pallas import tpu as pltpu

F32 = jnp.float32
BF16 = jnp.bfloat16
HIGHEST = lax.Precision.HIGHEST

LANE = 128
VMEM_LIMIT = 52 * 1024 * 1024

EPS = 1e-6
NEG = -1e30
LB_FLOOR = 1e-30
M_INIT = -1e30
GRID_W = 64
ROPE_BASE = 10000.0

HEADS = 4
ML_HD = 96
HG_DV = 64
HG_DK = 128
DA_HD = 48
DA_VD = 2 * DA_HD
SLAB = HEADS * LANE

ML_CHUNK = 256
ML_HPS = 2
HG_CHUNK = 128
HG_DIAG = 8
HG_HPS = 2
ATT_TQ = 256
ATT_KC = 256
LOG2E = 1.4426950408889634

OFF_MLQ, OFF_MLK, OFF_MLV, OFF_MLO = 0, SLAB, 2 * SLAB, 3 * SLAB
OFF_GATE = 4 * SLAB
OFF_HGQ = OFF_GATE + LANE
OFF_HGF = OFF_HGQ + SLAB
OFF_HGV = OFF_HGF + 2 * SLAB
OFF_HGG = OFF_HGV + SLAB
OFF_DAQ = OFF_HGG + SLAB
OFF_DAK = OFF_DAQ + SLAB
OFF_DAV = OFF_DAK + SLAB
NP_IN = OFF_DAV + SLAB


def _cparams(sem):
    return pltpu.CompilerParams(dimension_semantics=sem, vmem_limit_bytes=VMEM_LIMIT)


def _silu(x):
    return x * jax.nn.sigmoid(x)


def _log_sigmoid(z):
    return jnp.minimum(z, 0.0) - jnp.log1p(jnp.exp(-jnp.abs(z)))


def _rms(x, g):
    return x * lax.rsqrt(jnp.mean(x * x, axis=-1, keepdims=True) + EPS) * g


def _dot(a, b):
    return jnp.dot(a, b, preferred_element_type=F32)


def _dot_nt(a, b):
    return lax.dot_general(a, b, (((1,), (1,)), ((), ())), preferred_element_type=F32)


def _dot_tn(a, b):
    return lax.dot_general(a, b, (((0,), (0,)), ((), ())), preferred_element_type=F32)


def _ada_kernel(s_ref, w_ref, b_ref, o_ref):
    s = _silu(s_ref[...])
    o_ref[0] = jnp.dot(s, w_ref[0], precision=HIGHEST, preferred_element_type=F32) + b_ref[0]


def _ada(cc, w_ada, b_ada):
    depth, d, d6 = w_ada.shape
    mp = cc.shape[0]
    tn = 1024
    return pl.pallas_call(
        _ada_kernel,
        grid=(depth, d6 // tn),
        in_specs=[pl.BlockSpec((mp, d), lambda l, j: (0, 0)),
                  pl.BlockSpec((1, d, tn), lambda l, j: (l, 0, j)),
                  pl.BlockSpec((1, 1, tn), lambda l, j: (l, 0, j))],
        out_specs=pl.BlockSpec((1, mp, tn), lambda l, j: (l, 0, j)),
        out_shape=jax.ShapeDtypeStruct((depth, mp, d6), F32),
        compiler_params=_cparams(("arbitrary", "arbitrary")),
        name="ada_mod",
    )(cc, w_ada, b_ada.reshape(depth, 1, d6))


def _mod_specs(layer, nb, idxs, d):
    specs = []
    for j in idxs:
        specs.append(pl.BlockSpec((1, 1, 1, d), lambda b, t, j=j: (layer, b, 0, j)))
        specs.append(pl.BlockSpec((1, 1, 1, d), lambda b, t, j=j: (layer, nb, 0, j)))
    return specs


def _row_is_ctx(tm, ctx_len):
    rows = pl.program_id(1) * tm + lax.broadcasted_iota(jnp.int32, (tm, 1), 0)
    return rows < ctx_len


def _inproj_kernel(x_ref, sx_ref, sc_ref, ax_ref, ac_ref, g_ref, w_ref, b_ref,
                   cos_ref, sna_ref, snb_ref, llb_ref, l1m_ref,
                   mlq_ref, mlk_ref, mlv_ref, mlo_ref, gate_ref,
                   hgq_ref, hgf_ref, hgv_ref, hgg_ref, daq_ref, dak_ref, dav_ref,
                   *, tm, ctx_len):
    is_ctx = _row_is_ctx(tm, ctx_len)
    shift = jnp.where(is_ctx, sc_ref[0, 0], sx_ref[0, 0])
    scale = 1.0 + jnp.where(is_ctx, ac_ref[0, 0], ax_ref[0, 0])
    h = (_rms(x_ref[0], g_ref[0]) * scale + shift).astype(BF16)

    def proj(off, width):
        return _dot(h, w_ref[0, :, off:off + width]) + b_ref[0, :, off:off + width]

    mlq_ref[0] = proj(OFF_MLQ, SLAB).astype(BF16)
    mlk_ref[0] = (proj(OFF_MLK, SLAB) * (ML_HD ** -0.5)).astype(BF16)
    mlv_ref[0] = proj(OFF_MLV, SLAB).astype(BF16)
    mlo_ref[0] = jax.nn.sigmoid(proj(OFF_MLO, SLAB)).astype(BF16)

    gt = proj(OFF_GATE, LANE)
    lane = lax.broadcasted_iota(jnp.int32, gt.shape, 1)
    gate_ref[0] = jnp.where((lane >= 2 * HEADS) & (lane < 4 * HEADS), _log_sigmoid(gt), gt)

    hgq_ref[0] = _silu(proj(OFF_HGQ, SLAB)).astype(BF16)
    t2 = l1m_ref[0] + _log_sigmoid(proj(OFF_HGF, 2 * SLAB))
    t1 = llb_ref[0]
    hgf_ref[0] = jnp.maximum(t1, t2) + jnp.log1p(jnp.exp(-jnp.abs(t1 - t2)))
    hgv_ref[0] = proj(OFF_HGV, SLAB).astype(BF16)
    hgg_ref[0] = _silu(proj(OFF_HGG, SLAB)).astype(BF16)

    cos, sna, snb = cos_ref[...], sna_ref[...], snb_ref[...]
    half = DA_HD // 4

    def rope(p):
        outs = []
        for hd in range(HEADS):
            xh = p[:, hd * LANE:(hd + 1) * LANE]
            outs.append(xh * cos + pltpu.roll(xh, LANE - half, 1) * sna + pltpu.roll(xh, half, 1) * snb)
        return jnp.concatenate(outs, axis=1)

    daq_ref[0] = (rope(proj(OFF_DAQ, SLAB)) * (LOG2E * DA_HD ** -0.5)).astype(BF16)
    dak_ref[0] = rope(proj(OFF_DAK, SLAB)).astype(BF16)
    dav_ref[0] = proj(OFF_DAV, SLAB).astype(BF16)


def _inproj(xs, mod4, g_pre, w_pad, b_pad, rope_tabs, llb, l1m, layer, ctx_len, tm):
    nb, n_all, d = xs.shape
    kern = functools.partial(_inproj_kernel, tm=tm, ctx_len=ctx_len)
    lsel = lambda b, t: (layer, 0, 0)
    tile = lambda width: pl.BlockSpec((1, tm, width), lambda b, t: (b, t, 0))
    tab = pl.BlockSpec((tm, LANE), lambda b, t: (t, 0))
    bf = lambda width: jax.ShapeDtypeStruct((nb, n_all, width), BF16)
    f32 = lambda width: jax.ShapeDtypeStruct((nb, n_all, width), F32)
    out_shapes = [bf(SLAB)] * 4 + [f32(LANE)] + [bf(SLAB), f32(2 * SLAB), bf(SLAB), bf(SLAB)] + [bf(SLAB)] * 3
    out_specs = [tile(s.shape[-1]) for s in out_shapes]
    return pl.pallas_call(
        kern,
        grid=(nb, n_all // tm),
        in_specs=[tile(d)] + _mod_specs(layer, nb, (0, 1), d) + [
            pl.BlockSpec((1, 1, d), lsel),
            pl.BlockSpec((1, d, NP_IN), lsel),
            pl.BlockSpec((1, 1, NP_IN), lsel),
            tab, tab, tab,
            pl.BlockSpec((1, 1, 2 * SLAB), lsel),
            pl.BlockSpec((1, 1, 2 * SLAB), lsel)],
        out_specs=out_specs,
        out_shape=out_shapes,
        compiler_params=_cparams(("parallel", "arbitrary")),
        name="in_proj",
    )(xs, mod4, mod4, mod4, mod4, g_pre, w_pad, b_pad, *rope_tabs, llb, l1m)


def _ml_chunks(qs, ks, vs, i_rows, lf_rows, s_exts, ms, revs):
    n = len(qs)
    size = qs[0].shape[0]
    ii = lax.broadcasted_iota(jnp.int32, (size, size), 0)
    jj = lax.broadcasted_iota(jnp.int32, (size, size), 1)
    lane = lax.broadcasted_iota(jnp.int32, (size, LANE), 1)
    cums = {rev: jnp.where((ii >= jj) if rev else (ii <= jj), 1.0, 0.0).astype(F32) for rev in set(revs)}
    seens = {rev: (jj >= ii) if rev else (jj <= ii) for rev in set(revs)}

    b_cols, g_cols, g_rows = [], [], []
    for i in range(n):
        b_row = jnp.dot(jnp.broadcast_to(lf_rows[i], (8, size)), cums[revs[i]], precision=HIGHEST,
                        preferred_element_type=F32)[0:1]
        g_row = i_rows[i] - b_row
        b_cols.append(jnp.sum(jnp.where(ii == jj, b_row, 0.0), axis=1, keepdims=True))
        g_cols.append(jnp.sum(jnp.where(ii == jj, g_row, 0.0), axis=1, keepdims=True))
        g_rows.append(g_row)

    mts, dws, w_inters = [], [], []
    for i in range(n):
        dmat = jnp.where(seens[revs[i]], b_cols[i] + g_rows[i], NEG)
        a_col = b_cols[i] + ms[i]
        mt = jnp.maximum(a_col, jnp.max(dmat, axis=1, keepdims=True))
        mts.append(mt)
        w_inters.append(jnp.exp(a_col - mt))
        dws.append(jnp.exp(dmat - mt))

    ress = []
    for i in range(n):
        qk = _dot_nt(qs[i], ks[i]) * dws[i]
        ress.append(_dot(qk.astype(BF16), vs[i]) + w_inters[i] * _dot(qs[i], s_exts[i].astype(BF16)))

    houts, s_news, m_news = [], [], []
    for i in range(n):
        den = jnp.sum(jnp.where(lane == ML_HD, ress[i], 0.0), axis=1, keepdims=True)
        houts.append(ress[i] * (1.0 / jnp.maximum(jnp.abs(den), jnp.exp(-mts[i]))))
        last = 0 if revs[i] else size - 1
        m_new = mts[i][last:last + 1]
        b_last = b_cols[i][last:last + 1]
        decay = jnp.exp(b_last + ms[i] - m_new)
        wk = jnp.exp(b_last + g_cols[i] - m_new)
        s_news.append(decay * s_exts[i] + _dot_tn((wk * ks[i].astype(F32)).astype(BF16), vs[i]))
        m_news.append(m_new)
    return houts, s_news, m_news


def _mlstm_kernel(q_ref, k_ref, v_ref, o_ref, gt_ref, gain_ref, y_ref, hf_ref, hb_ref, *, nc_ctx, nc_all, size, nh):
    hd0 = pl.program_id(1) * nh
    lane = lax.broadcasted_iota(jnp.int32, (size, LANE), 1)
    nc_lat = nc_all - nc_ctx

    def step(c, carry):
        cb = jnp.where(c < nc_ctx, nc_ctx - 1 - c, nc_ctx + nc_lat - 1 - (c - nc_ctx))
        sf = pl.ds(pl.multiple_of(c * size, size), size)
        sb = pl.ds(pl.multiple_of(cb * size, size), size)
        qs, ks, vs, i_rows, lf_rows, revs = [], [], [], [], [], []
        for hh in range(nh):
            ln = slice(hh * LANE, (hh + 1) * LANE)
            for sl, cc, off in ((sf, c, 0), (sb, cb, HEADS)):
                v = v_ref[0, sl, ln]
                qs.append(q_ref[0, sl, ln])
                ks.append(k_ref[0, sl, ln])
                vs.append(jnp.where(lane == ML_HD, jnp.ones_like(v), v))
                i_rows.append(gt_ref[0, off + hd0 + hh, pl.ds(cc, 1), :])
                lf_rows.append(gt_ref[0, 2 * HEADS + off + hd0 + hh, pl.ds(cc, 1), :])
                revs.append(off > 0)
        houts, s_news, m_news = _ml_chunks(qs, ks, vs, i_rows, lf_rows, list(carry[0]), list(carry[1]), revs)
        for hh in range(nh):
            ln = slice(hh * LANE, (hh + 1) * LANE)
            hf_ref[sf, ln] = houts[2 * hh]
            hb_ref[sb, ln] = houts[2 * hh + 1]
        return tuple(s_news), tuple(m_news)

    s0 = jnp.zeros((LANE, LANE), F32)
    m0 = jnp.full((1, 1), M_INIT, F32)
    lax.fori_loop(0, nc_all, step, ((s0,) * (2 * nh), (m0,) * (2 * nh)))

    def finish(c, _):
        sl = pl.ds(pl.multiple_of(c * size, size), size)
        for hh in range(nh):
            ln = slice(hh * LANE, (hh + 1) * LANE)
            h = jnp.where(lane < ML_HD, hf_ref[sl, ln] + hb_ref[sl, ln], 0.0)
            y = h * lax.rsqrt(jnp.sum(h * h, axis=1, keepdims=True) * (1.0 / ML_HD) + EPS) * gain_ref[0, :, ln]
            y_ref[0, sl, ln] = (y * o_ref[0, sl, ln].astype(F32)).astype(BF16)
        return 0

    lax.fori_loop(0, nc_all, finish, 0)


def _mlstm(mlq, mlk, mlv, mlo, gates_t, gain, layer, ctx_len):
    nb, n_all, _ = mlq.shape
    size, nh = ML_CHUNK, ML_HPS
    nc_all = n_all // size
    width = nh * LANE
    ng = HEADS // nh
    kern = functools.partial(_mlstm_kernel, nc_ctx=ctx_len // size, nc_all=nc_all, size=size, nh=nh)
    slab = pl.BlockSpec((1, n_all, width), lambda b, h: (b, 0, h))
    return pl.pallas_call(
        kern,
        grid=(nb, ng),
        in_specs=[slab, slab, slab, slab,
                  pl.BlockSpec((1, 4 * HEADS, nc_all, size), lambda b, h: (b, 0, 0, 0)),
                  pl.BlockSpec((1, 1, width), lambda b, h: (layer * ng + h, 0, 0))],
        out_specs=slab,
        out_shape=jax.ShapeDtypeStruct((nb, n_all, SLAB), BF16),
        scratch_shapes=[pltpu.VMEM((n_all, width), F32), pltpu.VMEM((n_all, width), F32)],
        compiler_params=_cparams(("parallel", "arbitrary")),
        name="mlstm",
    )(mlq, mlk, mlv, mlo, gates_t, gain.reshape(-1, 1, width))


def _hg_tables(size):
    t = np.arange(size)
    tri = (t[None, :] <= t[:, None]).astype(np.float32)
    x = t[:, None] ^ t[None, :]
    lvl = np.where(x < HG_DIAG, 0, np.floor(np.log2(np.maximum(x, 1))).astype(np.int64) - 2)
    code = np.where(t[None, :] <= t[:, None], lvl, -1).astype(np.int32)
    j = np.arange(HG_DIAG * LANE) // LANE
    emat = (np.arange(size)[None, :] % HG_DIAG == j[:, None]).astype(np.float32)
    return (jnp.asarray(np.stack([tri, tri.T]), BF16), jnp.asarray(np.stack([code, code.T])),
            jnp.asarray(emat, BF16))


def _hg_chunks(qs, lfs, vs, sts, tris, codes, emat, revs):
    n = len(qs)
    size = qs[0].shape[0]
    bc2s, kks = [], []
    for i in range(n):
        lf2 = lfs[i] * LOG2E
        hi = lf2.astype(BF16)
        r1 = lf2 - hi.astype(F32)
        mid = r1.astype(BF16)
        lo = (r1 - mid.astype(F32)).astype(BF16)
        cs = _dot(tris[i], jnp.concatenate([hi, mid, lo], axis=1))
        bc2s.append((cs[:, 0:LANE] + cs[:, LANE:2 * LANE]) + cs[:, 2 * LANE:3 * LANE])
        kks.append(jnp.maximum(1.0 - jnp.exp2(lf2), 0.0))

    atts = [jnp.zeros((size, size), F32) for _ in range(n)]
    c, lvl = HG_DIAG, 1
    while c < size:
        blk = 2 * c
        for i in range(n):
            ridx = c if revs[i] else c - 1
            b3 = bc2s[i].reshape(size // blk, blk, LANE)
            ref = jnp.broadcast_to(b3[:, ridx:ridx + 1, :], b3.shape).reshape(size, LANE)
            e = jnp.exp2(-jnp.abs(bc2s[i] - ref))
            a = _dot_nt((qs[i] * e).astype(BF16), (kks[i] * e).astype(BF16))
            atts[i] = jnp.where(codes[i] == lvl, a, atts[i])
        c, lvl = blk, lvl + 1

    for i in range(n):
        w3 = (bc2s[i] - jnp.log2(kks[i])).reshape(size // HG_DIAG, HG_DIAG, LANE)
        ps = []
        for j in range(HG_DIAG):
            wj = jnp.broadcast_to(w3[:, j:j + 1, :], w3.shape).reshape(size, LANE)
            ps.append((qs[i] * jnp.exp2(jnp.minimum(bc2s[i] - wj, 0.0))).astype(BF16))
        atts[i] = jnp.where(codes[i] == 0, _dot(jnp.concatenate(ps, axis=1), emat), atts[i])

    outs, new_sts = [], []
    for i in range(n):
        last = 0 if revs[i] else size - 1
        bl = bc2s[i][last:last + 1]
        outs.append(_dot(atts[i].astype(BF16), vs[i])
                    + _dot_nt((qs[i] * jnp.exp2(bc2s[i])).astype(BF16), sts[i].astype(BF16)))
        new_sts.append(sts[i] * jnp.exp2(bl) + _dot_tn(vs[i], (kks[i] * jnp.exp2(bl - bc2s[i])).astype(BF16)))
    return outs, new_sts


def _hgrn_kernel(q_ref, lf0_ref, lf1_ref, v_ref, g_ref, gain_ref, tri_ref, code_ref, emat_ref,
                 y_ref, of_ref, ob_ref, *, nc_ctx, nc_all, size, nh):
    nc_lat = nc_all - nc_ctx

    def step(c, carry):
        cb = jnp.where(c < nc_ctx, nc_ctx - 1 - c, nc_ctx + nc_lat - 1 - (c - nc_ctx))
        sf = pl.ds(pl.multiple_of(c * size, size), size)
        sb = pl.ds(pl.multiple_of(cb * size, size), size)
        qs, lfs, vs, tris, codes, revs = [], [], [], [], [], []
        for hh in range(nh):
            ln = slice(hh * LANE, (hh + 1) * LANE)
            qs += [q_ref[0, sf, ln].astype(F32), q_ref[0, sb, ln].astype(F32)]
            lfs += [lf0_ref[0, sf, ln], lf1_ref[0, sb, ln]]
            vs += [v_ref[0, sf, ln], v_ref[0, sb, ln]]
            tris += [tri_ref[0], tri_ref[1]]
            codes += [code_ref[0], code_ref[1]]
            revs += [False, True]
        outs, sts = _hg_chunks(qs, lfs, vs, list(carry), tris, codes, emat_ref[...], revs)
        for hh in range(nh):
            ln = slice(hh * LANE, (hh + 1) * LANE)
            of_ref[sf, ln] = outs[2 * hh]
            ob_ref[sb, ln] = outs[2 * hh + 1]
        return tuple(sts)

    st0 = jnp.zeros((LANE, LANE), F32)
    lax.fori_loop(0, nc_all, step, (st0,) * (2 * nh))

    def finish(c, _):
        sl = pl.ds(pl.multiple_of(c * size, size), size)
        for hh in range(nh):
            ln = slice(hh * LANE, (hh + 1) * LANE)
            o = of_ref[sl, ln] + ob_ref[sl, ln]
            y = o * lax.rsqrt(jnp.sum(o * o, axis=1, keepdims=True) * (1.0 / HG_DV) + EPS) * gain_ref[0, :, ln]
            y_ref[0, sl, ln] = (y * g_ref[0, sl, ln].astype(F32)).astype(BF16)
        return 0

    lax.fori_loop(0, nc_all, finish, 0)


def _hgrn(hgq, hgf, hgv, hgg, gain, layer, ctx_len):
    nb, n_all, _ = hgq.shape
    size, nh = HG_CHUNK, HG_HPS
    nc_all = n_all // size
    width = nh * LANE
    ng = HEADS // nh
    kern = functools.partial(_hgrn_kernel, nc_ctx=ctx_len // size, nc_all=nc_all, size=size, nh=nh)
    slab = pl.BlockSpec((1, n_all, width), lambda b, h: (b, 0, h))
    tri, code, emat = _hg_tables(size)
    return pl.pallas_call(
        kern,
        grid=(nb, ng),
        in_specs=[slab, slab,
                  pl.BlockSpec((1, n_all, width), lambda b, h: (b, 0, ng + h)),
                  slab, slab,
                  pl.BlockSpec((1, 1, width), lambda b, h: (layer * ng + h, 0, 0)),
                  pl.BlockSpec(tri.shape, lambda b, h: (0, 0, 0)),
                  pl.BlockSpec(code.shape, lambda b, h: (0, 0, 0)),
                  pl.BlockSpec(emat.shape, lambda b, h: (0, 0))],
        out_specs=slab,
        out_shape=jax.ShapeDtypeStruct((nb, n_all, SLAB), BF16),
        scratch_shapes=[pltpu.VMEM((n_all, width), F32), pltpu.VMEM((n_all, width), F32)],
        compiler_params=_cparams(("parallel", "arbitrary")),
        name="hgrn2",
    )(hgq, hgf, hgf, hgv, hgg, gain.reshape(-1, 1, width), tri, code, emat)


def _attn_kernel(q_ref, k_ref, vt_ref, lam_ref, gain_ref, y_ref, s0_ref, s1_ref, p0_ref, p1_ref, m0_ref, m1_ref,
                 *, tq, kc, ctx_len, lam_init):
    n_all = k_ref.shape[1]
    n_tiles = (n_all - ctx_len) // tq
    s_refs, p_refs, m_refs = (s0_ref, s1_ref), (p0_ref, p1_ref), (m0_ref, m1_ref)
    lv = lam_ref[0]
    lam = (jnp.exp(jnp.sum(lv[0:1] * lv[1:2], axis=1, keepdims=True))
           - jnp.exp(jnp.sum(lv[2:3] * lv[3:4], axis=1, keepdims=True)) + lam_init)
    lane = lax.broadcasted_iota(jnp.int32, (tq, LANE), 1)

    def rows(t):
        return pl.ds(pl.multiple_of(ctx_len + t * tq, tq), tq)

    def load_qq(sl):
        q = q_ref[0, sl, :]
        zero = jnp.zeros_like(q)
        return jnp.concatenate([jnp.where(lane < DA_HD, q, zero), jnp.where(lane < DA_HD, zero, q)], axis=0)

    def stage_a_chunk(qq, slot, c, m):
        s = _dot_nt(k_ref[0, c * kc:(c + 1) * kc, :], qq)
        s_refs[slot][c * kc:(c + 1) * kc, :] = s
        cm = jnp.max(s, axis=0, keepdims=True)
        return cm if m is None else jnp.maximum(m, cm)

    def stage_b_chunk(slot, c, m):
        p_refs[slot][c * kc:(c + 1) * kc, :] = jnp.exp2(s_refs[slot][c * kc:(c + 1) * kc, :] - m).astype(BF16)

    def stage_c(slot, sl, nk):
        vt = vt_ref[0, :, 0:nk]
        vrow = lax.broadcasted_iota(jnp.int32, vt.shape, 0)
        acc = _dot(jnp.where(vrow == DA_VD, jnp.ones_like(vt), vt), p_refs[slot][0:nk, :])
        r0 = 1.0 / acc[DA_VD:DA_VD + 1, 0:tq]
        r1 = lam / acc[DA_VD:DA_VD + 1, tq:2 * tq]
        o = (acc[:, 0:tq] * r0 - acc[:, tq:2 * tq] * r1).T
        o = jnp.where(lane < DA_VD, o, 0.0)
        y = o * lax.rsqrt(jnp.sum(o * o, axis=1, keepdims=True) * (1.0 / DA_VD) + EPS) * gain_ref[0]
        y_ref[0, sl, :] = (y * (1.0 - lam_init)).astype(BF16)

    def stages_ab(t_a, t_b, slot_a, nk):
        slot_b = 1 - slot_a
        if t_a is not None:
            qq = load_qq(rows(t_a))
        if t_b is not None:
            m_b = m_refs[slot_b][0:1, :]
        m = None
        for c in range(nk // kc):
            if t_b is not None:
                stage_b_chunk(slot_b, c, m_b)
            if t_a is not None:
                m = stage_a_chunk(qq, slot_a, c, m)
        if t_a is not None:
            m_refs[slot_a][...] = jnp.broadcast_to(m, (8, 2 * tq))

    for i in range(ctx_len // tq):
        sl = pl.ds(i * tq, tq)
        qq = load_qq(sl)
        m = None
        for c in range(ctx_len // kc):
            m = stage_a_chunk(qq, 0, c, m)
        for c in range(ctx_len // kc):
            stage_b_chunk(0, c, m)
        stage_c(0, sl, ctx_len)

    stages_ab(0, None, 0, n_all)
    stages_ab(1, 0, 1, n_all)

    def body(i, _):
        t = 2 + 2 * i
        stage_c(0, rows(t - 2), n_all)
        stages_ab(t, t - 1, 0, n_all)
        stage_c(1, rows(t - 1), n_all)
        stages_ab(t + 1, t, 1, n_all)
        return 0

    lax.fori_loop(0, (n_tiles - 2) // 2, body, 0)
    stage_c(0, rows(n_tiles - 2), n_all)
    stages_ab(None, n_tiles - 1, 0, n_all)
    stage_c(1, rows(n_tiles - 1), n_all)


def _attn(daq, dak, dav_t, lam_pad, gain, layer, ctx_len, lam_init):
    nb, n_all, _ = daq.shape
    tq, kc = ATT_TQ, ATT_KC
    n_tiles = (n_all - ctx_len) // tq
    assert n_tiles >= 2 and n_tiles % 2 == 0 and ctx_len % tq == 0 and ctx_len % kc == 0 and n_all % kc == 0
    kern = functools.partial(_attn_kernel, tq=tq, kc=kc, ctx_len=ctx_len, lam_init=lam_init)
    slab = pl.BlockSpec((1, n_all, LANE), lambda b, h: (b, 0, h))
    sbuf, pbuf, mbuf = (pltpu.VMEM((n_all, 2 * tq), F32), pltpu.VMEM((n_all, 2 * tq), BF16),
                        pltpu.VMEM((8, 2 * tq), F32))
    return pl.pallas_call(
        kern,
        grid=(nb, HEADS),
        in_specs=[slab, slab,
                  pl.BlockSpec((1, LANE, n_all), lambda b, h: (b, h, 0)),
                  pl.BlockSpec((1, 4, LANE), lambda b, h: (layer, 0, 0)),
                  pl.BlockSpec((1, 1, LANE), lambda b, h: (layer, 0, 0))],
        out_specs=slab,
        out_shape=jax.ShapeDtypeStruct((nb, n_all, SLAB), BF16),
        scratch_shapes=[sbuf, sbuf, pbuf, pbuf, mbuf, mbuf],
        compiler_params=_cparams(("parallel", "arbitrary")),
        name="diff_attn",
    )(daq, dak, dav_t, lam_pad, gain)


def _outproj_kernel(yml_ref, yhg_ref, yda_ref, x_ref, gx_ref, gc_ref, ax_ref, ac_ref, sx_ref, sc_ref,
                    w_ref, gpost_ref, gpre_ref, x1_ref, h2_ref, *, tm, ctx_len):
    is_ctx = _row_is_ctx(tm, ctx_len)
    y = (_dot(yml_ref[0], w_ref[0, 0:SLAB, :]) + _dot(yhg_ref[0], w_ref[0, SLAB:2 * SLAB, :])
         + _dot(yda_ref[0], w_ref[0, 2 * SLAB:3 * SLAB, :]))
    gate = jnp.where(is_ctx, gc_ref[0, 0], gx_ref[0, 0])
    x1 = x_ref[0] + gate * _rms(y, gpost_ref[0])
    x1_ref[0] = x1
    scale = 1.0 + jnp.where(is_ctx, ac_ref[0, 0], ax_ref[0, 0])
    shift = jnp.where(is_ctx, sc_ref[0, 0], sx_ref[0, 0])
    h2_ref[0] = (_rms(x1, gpre_ref[0]) * scale + shift).astype(BF16)


def _outproj(yml, yhg, yda, xs, mod4, w_out_pad, g_post, g_pre, layer, ctx_len, tm):
    nb, n_all, d = xs.shape
    kern = functools.partial(_outproj_kernel, tm=tm, ctx_len=ctx_len)
    lsel = lambda b, t: (layer, 0, 0)
    tile = lambda width: pl.BlockSpec((1, tm, width), lambda b, t: (b, t, 0))
    return pl.pallas_call(
        kern,
        grid=(nb, n_all // tm),
        in_specs=[tile(SLAB), tile(SLAB), tile(SLAB), tile(d)] + _mod_specs(layer, nb, (2, 4, 3), d) + [
            pl.BlockSpec((1, 3 * SLAB, d), lsel),
            pl.BlockSpec((1, 1, d), lsel),
            pl.BlockSpec((1, 1, d), lsel)],
        out_specs=[tile(d), tile(d)],
        out_shape=[jax.ShapeDtypeStruct((nb, n_all, d), F32), jax.ShapeDtypeStruct((nb, n_all, d), BF16)],
        compiler_params=_cparams(("parallel", "arbitrary")),
        name="out_proj",
    )(yml, yhg, yda, xs, mod4, mod4, mod4, mod4, mod4, mod4, w_out_pad, g_post, g_pre)


def _ffn_kernel(h_ref, x_ref, gx_ref, gc_ref, wg_ref, wu_ref, wd_ref, gpost_ref, o_ref, acc_ref, *, tm, ctx_len):
    j = pl.program_id(2)

    @pl.when(j == 0)
    def _():
        acc_ref[...] = jnp.zeros_like(acc_ref)

    h = h_ref[0]
    act = _silu(_dot(h, wg_ref[0])) * _dot(h, wu_ref[0])
    acc_ref[...] += _dot(act.astype(BF16), wd_ref[0])

    @pl.when(j == pl.num_programs(2) - 1)
    def _():
        gate = jnp.where(_row_is_ctx(tm, ctx_len), gc_ref[0, 0], gx_ref[0, 0])
        o_ref[0] = x_ref[0] + gate * _rms(acc_ref[...], gpost_ref[0])


def _ffn(h2, x1, mod4, wg, wu, wd, g_post, layer, ctx_len, tm, tf):
    nb, n_all, d = x1.shape
    dff = wg.shape[-1]
    kern = functools.partial(_ffn_kernel, tm=tm, ctx_len=ctx_len)
    tile = pl.BlockSpec((1, tm, d), lambda b, t, j: (b, t, 0))
    mods = [pl.BlockSpec((1, 1, 1, d), lambda b, t, j: (layer, b, 0, 5)),
            pl.BlockSpec((1, 1, 1, d), lambda b, t, j: (layer, nb, 0, 5))]
    return pl.pallas_call(
        kern,
        grid=(nb, n_all // tm, dff // tf),
        in_specs=[tile, tile] + mods + [
            pl.BlockSpec((1, d, tf), lambda b, t, j: (layer, 0, j)),
            pl.BlockSpec((1, d, tf), lambda b, t, j: (layer, 0, j)),
            pl.BlockSpec((1, tf, d), lambda b, t, j: (layer, j, 0)),
            pl.BlockSpec((1, 1, d), lambda b, t, j: (layer, 0, 0))],
        out_specs=tile,
        out_shape=jax.ShapeDtypeStruct((nb, n_all, d), F32),
        scratch_shapes=[pltpu.VMEM((tm, d), F32)],
        compiler_params=_cparams(("parallel", "arbitrary", "arbitrary")),
        name="ffn",
    )(h2, x1, mod4, mod4, wg, wu, wd, g_post)


def _pad_heads(w, hd):
    lead = w.shape[:-1]
    w = w.reshape(*lead, HEADS, hd)
    w = jnp.pad(w, [(0, 0)] * len(lead) + [(0, 0), (0, LANE - hd)])
    return w.reshape(*lead, SLAB)


def _pad_in_cols(w, extra_f=None):
    sizes = (4 * ML_HD,) * 4 + (2 * HEADS, 2 * HEADS, HEADS * HG_DK, 2 * HEADS * HG_DK, HEADS * HG_DV, HEADS * HG_DV,
             2 * HEADS * DA_HD, 2 * HEADS * DA_HD, HEADS * DA_VD)
    parts = []
    off = 0
    for s in sizes:
        parts.append(w[..., off:off + s])
        off += s
    gi, gf = parts[4], parts[5]
    if extra_f is not None:
        gf = gf + extra_f
    gate = jnp.concatenate([gi, gf], axis=-1)
    gate = jnp.pad(gate, [(0, 0)] * (gate.ndim - 1) + [(0, LANE - 4 * HEADS)])
    return jnp.concatenate([
        _pad_heads(parts[0], ML_HD), _pad_heads(parts[1], ML_HD), _pad_heads(parts[2], ML_HD), _pad_heads(parts[3], ML_HD),
        gate, parts[6], parts[7], _pad_heads(parts[8], HG_DV), _pad_heads(parts[9], HG_DV),
        _pad_heads(parts[10], DA_VD), _pad_heads(parts[11], DA_VD), _pad_heads(parts[12], DA_VD)], axis=-1)


def _pad_out_rows(w_out):
    depth, _, d = w_out.shape
    wt = jnp.swapaxes(w_out, 1, 2)
    ml, hg, da = wt[..., :4 * ML_HD], wt[..., 4 * ML_HD:4 * ML_HD + HEADS * HG_DV], wt[..., 4 * ML_HD + HEADS * HG_DV:]
    wp = jnp.concatenate([_pad_heads(ml, ML_HD), _pad_heads(hg, HG_DV), _pad_heads(da, DA_VD)], axis=-1)
    return jnp.swapaxes(wp, 1, 2)


def _rope_tables(n_lat, ctx_len):
    rows = n_lat // GRID_W
    row = jnp.repeat(jnp.arange(rows), GRID_W).astype(F32)
    col = jnp.tile(jnp.arange(GRID_W), rows).astype(F32)
    half = DA_HD // 2
    inv = ROPE_BASE ** (-jnp.arange(0, half, 2, dtype=F32) / half)
    ang_r, ang_c = row[:, None] * inv, col[:, None] * inv
    zero = jnp.zeros_like(ang_r)

    def one_map(fr, fc, sel):
        r, c = fr(ang_r), fc(ang_c)
        if sel == "cos":
            return jnp.concatenate([r, r, c, c], axis=1)
        if sel == "a":
            return jnp.concatenate([-r, zero, -c, zero], axis=1)
        return jnp.concatenate([zero, r, zero, c], axis=1)

    tabs = []
    for sel, fn in (("cos", jnp.cos), ("a", jnp.sin), ("b", jnp.sin)):
        m = one_map(fn, fn, sel)
        lat = jnp.pad(jnp.concatenate([m, m], axis=1), ((0, 0), (0, LANE - 2 * DA_HD)))
        fill = 1.0 if sel == "cos" else 0.0
        ctx_rows = jnp.pad(jnp.full((ctx_len, 2 * DA_HD), fill, F32), ((0, 0), (0, LANE - 2 * DA_HD)))
        tabs.append(jnp.concatenate([ctx_rows, lat], axis=0))
    return tabs


def kernel(x, c, ctx, c_ctx, w_ada, b_ada, g_pre_mix, g_post_mix, g_pre_ffn, g_post_ffn, w_in, b_in, w_out,
           ml_f_bias, ml_norm, hg_lb, hg_norm, da_lambda, da_norm, w_ffn_gate, w_ffn_up, w_ffn_down):
    nb, n_lat, d = x.shape
    ctx_len = ctx.shape[1]
    depth = w_ada.shape[0]
    n_all = ctx_len + n_lat

    w_pad = _pad_in_cols(w_in).astype(BF16)
    b_pad = _pad_in_cols(b_in, extra_f=ml_f_bias)[:, None, :]
    w_out_pad = _pad_out_rows(w_out).astype(BF16)
    wg, wu, wd = w_ffn_gate.astype(BF16), w_ffn_up.astype(BF16), w_ffn_down.astype(BF16)
    ml_gain = _pad_heads(ml_norm, ML_HD).reshape(depth * HEADS, 1, LANE)
    hg_gain = _pad_heads(hg_norm, HG_DV).reshape(depth * HEADS, 1, LANE)
    da_gain = jnp.pad(da_norm, ((0, 0), (0, LANE - DA_VD)))[:, None, :]
    lam_pad = jnp.pad(da_lambda.astype(F32), ((0, 0), (0, 0), (0, LANE - DA_HD)))
    sm = jax.nn.softmax(hg_lb.astype(F32), axis=0)
    lbs = jnp.cumsum(sm, axis=0) - sm[0:1]
    llb = jnp.log(jnp.maximum(lbs, LB_FLOOR))[:, None, :]
    l1m = jnp.log1p(-lbs)[:, None, :]
    rope_tabs = _rope_tables(n_lat, ctx_len)
    r3 = lambda g: g[:, None, :]

    mp = -(-(nb + 1) // 8) * 8
    cc = jnp.concatenate([c, c_ctx[None, :], jnp.zeros((mp - nb - 1, d), F32)], axis=0)
    mod4 = _ada(cc, w_ada, b_ada).reshape(depth, mp, 1, 6 * d)

    xs = jnp.concatenate([ctx, x], axis=1)
    for l in range(depth):
        lam_init = 0.8 - 0.6 * math.exp(-0.3 * l)
        (mlq, mlk, mlv, mlo, gates, hgq, hgf, hgv, hgg, daq, dak, dav) = _inproj(
            xs, mod4, r3(g_pre_mix), w_pad, b_pad, rope_tabs, llb, l1m, l, ctx_len, 256)
        gates_t = jnp.swapaxes(gates[:, :, :4 * HEADS], 1, 2).reshape(nb, 4 * HEADS, n_all // ML_CHUNK, ML_CHUNK)
        yml = _mlstm(mlq, mlk, mlv, mlo, gates_t, ml_gain, l, ctx_len)
        yhg = _hgrn(hgq, hgf, hgv, hgg, hg_gain, l, ctx_len)
        yda = _attn(daq, dak, jnp.swapaxes(dav, 1, 2), lam_pad, da_gain, l, ctx_len, lam_init)
        x1, h2 = _outproj(yml, yhg, yda, xs, mod4, w_out_pad, r3(g_post_mix), r3(g_pre_ffn), l, ctx_len, 256)
        xs = _ffn(h2, x1, mod4, wg, wu, wd, r3(g_post_ffn), l, ctx_len, 768 if n_all % 768 == 0 else 256, 256)
    return xs[:, ctx_len:, :]
```

```python
import functools
import math

import numpy as np
import jax
import jax.numpy as jnp
from jax import lax
from jax.experimental import pallas as pl
from jax.experimental.pallas import tpu as pltpu

F32 = jnp.float32
BF16 = jnp.bfloat16
HIGHEST = lax.Precision.HIGHEST

LANE = 128
VMEM_LIMIT = 52 * 1024 * 1024

EPS = 1e-6
NEG = -1e30
LB_FLOOR = 1e-30
M_INIT = -1e30
GRID_W = 64
ROPE_BASE = 10000.0

HEADS = 4
ML_HD = 96
HG_DV = 64
HG_DK = 128
DA_HD = 48
DA_VD = 2 * DA_HD
SLAB = HEADS * LANE
HG_W = HEADS * HG_DV

ML_CHUNK = 256
ML_HPS = 2
HG_CHUNK = 128
HG_DIAG = 8
ATT_TQ = 256
IN_CW = 256
MIX_TM = 384
MIX_TM_LAST = 256
ATT_KC = 256
LOG2E = 1.4426950408889634

OFF_MLQ, OFF_MLK, OFF_MLV, OFF_MLO = 0, SLAB, 2 * SLAB, 3 * SLAB
OFF_GATE = 4 * SLAB
OFF_HGQ = OFF_GATE + LANE
OFF_HGF = OFF_HGQ + SLAB
OFF_HGV = OFF_HGF + 2 * SLAB
OFF_HGG = OFF_HGV + HG_W
OFF_DAQ = OFF_HGG + HG_W
OFF_DAK = OFF_DAQ + SLAB
OFF_DAV = OFF_DAK + SLAB
NP_IN = OFF_DAV + SLAB


def _cparams(sem):
    return pltpu.CompilerParams(dimension_semantics=sem, vmem_limit_bytes=VMEM_LIMIT)


def _silu(x):
    return x * jax.nn.sigmoid(x)


def _log_sigmoid(z):
    return jnp.minimum(z, 0.0) - jnp.log1p(jnp.exp(-jnp.abs(z)))


def _rms(x, g):
    return x * lax.rsqrt(jnp.mean(x * x, axis=-1, keepdims=True) + EPS) * g


def _dot(a, b):
    return jnp.dot(a, b, preferred_element_type=F32)


def _dot_nt(a, b):
    return lax.dot_general(a, b, (((1,), (1,)), ((), ())), preferred_element_type=F32)


def _dot_tn(a, b):
    return lax.dot_general(a, b, (((0,), (0,)), ((), ())), preferred_element_type=F32)


def _ada_kernel(s_ref, w_ref, b_ref, o_ref):
    s = _silu(s_ref[...])
    o_ref[0] = jnp.dot(s, w_ref[0], precision=HIGHEST, preferred_element_type=F32) + b_ref[0]


def _ada(cc, w_ada, b_ada):
    depth, d, d6 = w_ada.shape
    mp = cc.shape[0]
    tn = 1024
    return pl.pallas_call(
        _ada_kernel,
        grid=(depth, d6 // tn),
        in_specs=[pl.BlockSpec((mp, d), lambda l, j: (0, 0)),
                  pl.BlockSpec((1, d, tn), lambda l, j: (l, 0, j)),
                  pl.BlockSpec((1, 1, tn), lambda l, j: (l, 0, j))],
        out_specs=pl.BlockSpec((1, mp, tn), lambda l, j: (l, 0, j)),
        out_shape=jax.ShapeDtypeStruct((depth, mp, d6), F32),
        compiler_params=_cparams(("arbitrary", "arbitrary")),
        name="ada_mod",
    )(cc, w_ada, b_ada.reshape(depth, 1, d6))


def _mod_specs(layer, nb, idxs, d):
    specs = []
    for j in idxs:
        specs.append(pl.BlockSpec((1, 1, 1, d), lambda b, t, j=j: (layer, b, 0, j)))
        specs.append(pl.BlockSpec((1, 1, 1, d), lambda b, t, j=j: (layer, nb, 0, j)))
    return specs


def _row_is_ctx(tm, ctx_len, t0=0):
    rows = (pl.program_id(1) + t0) * tm + lax.broadcasted_iota(jnp.int32, (tm, 1), 0)
    return rows < ctx_len


def _inproj_kernel(x_ref, sx_ref, sc_ref, ax_ref, ac_ref, g_ref, w_ref, b_ref,
                   cos_ref, sna_ref, snb_ref, llb_ref, l1m_ref,
                   mlq_ref, mlk_ref, mlv_ref, mlo_ref, gate_ref,
                   hgq_ref, hgf_ref, hgv_ref, hgg_ref, daq_ref, dak_ref, dav_ref,
                   *, tm, ctx_len):
    is_ctx = _row_is_ctx(tm, ctx_len)
    shift = jnp.where(is_ctx, sc_ref[0, 0], sx_ref[0, 0])
    scale = 1.0 + jnp.where(is_ctx, ac_ref[0, 0], ax_ref[0, 0])
    h = (_rms(x_ref[0], g_ref[0]) * scale + shift).astype(BF16)

    cos, sna, snb = cos_ref[...], sna_ref[...], snb_ref[...]
    half = DA_HD // 4
    lane = lax.broadcasted_iota(jnp.int32, (tm, LANE), 1)

    def rope(p):
        outs = []
        for i in range(p.shape[1] // LANE):
            xh = p[:, i * LANE:(i + 1) * LANE]
            outs.append(xh * cos + pltpu.roll(xh, LANE - half, 1) * sna + pltpu.roll(xh, half, 1) * snb)
        return jnp.concatenate(outs, axis=1)

    def log_f(p, c0):
        t2 = l1m_ref[0, :, c0:c0 + p.shape[1]] + _log_sigmoid(p)
        t1 = llb_ref[0, :, c0:c0 + p.shape[1]]
        return jnp.maximum(t1, t2) + jnp.log1p(jnp.exp(-jnp.abs(t1 - t2)))

    def gates(p, c0):
        return jnp.where((lane >= 2 * HEADS) & (lane < 4 * HEADS), _log_sigmoid(p), p)

    plain = lambda p, c0: p
    groups = {
        "mlq": (mlq_ref, OFF_MLQ, SLAB, plain),
        "mlk": (mlk_ref, OFF_MLK, SLAB, lambda p, c0: p * (ML_HD ** -0.5)),
        "mlv": (mlv_ref, OFF_MLV, SLAB, plain),
        "mlo": (mlo_ref, OFF_MLO, SLAB, lambda p, c0: jax.nn.sigmoid(p)),
        "gate": (gate_ref, OFF_GATE, LANE, gates),
        "hgq": (hgq_ref, OFF_HGQ, SLAB, lambda p, c0: _silu(p)),
        "hgf": (hgf_ref, OFF_HGF, 2 * SLAB, log_f),
        "hgv": (hgv_ref, OFF_HGV, HG_W, plain),
        "hgg": (hgg_ref, OFF_HGG, HG_W, lambda p, c0: _silu(p)),
        "daq": (daq_ref, OFF_DAQ, SLAB, lambda p, c0: rope(p) * (LOG2E * DA_HD ** -0.5)),
        "dak": (dak_ref, OFF_DAK, SLAB, lambda p, c0: rope(p)),
        "dav": (dav_ref, OFF_DAV, SLAB, plain),
    }
    order = [("hgf", 0), ("mlq", 0), ("hgf", 1), ("mlq", 1), ("hgf", 2), ("mlk", 0), ("hgf", 3), ("mlk", 1),
             ("mlo", 0), ("mlv", 0), ("mlo", 1), ("mlv", 1), ("hgq", 0), ("dav", 0), ("hgq", 1), ("dav", 1),
             ("daq", 0), ("hgv", 0), ("daq", 1), ("gate", 0), ("dak", 0), ("hgg", 0), ("dak", 1)]
    for name, ci in order:
        ref, off, width, fn = groups[name]
        cw = min(width, IN_CW)
        c0 = ci * cw
        p = _dot(h, w_ref[0, :, off + c0:off + c0 + cw]) + b_ref[0, :, off + c0:off + c0 + cw]
        ref[0, :, c0:c0 + cw] = fn(p, c0).astype(ref.dtype)


def _inproj(xs, mod4, g_pre, w_pad, b_pad, rope_tabs, llb, l1m, layer, ctx_len, tm):
    nb, n_all, d = xs.shape
    kern = functools.partial(_inproj_kernel, tm=tm, ctx_len=ctx_len)
    lsel = lambda b, t: (layer, 0, 0)
    tile = lambda width: pl.BlockSpec((1, tm, width), lambda b, t: (b, t, 0))
    tab = pl.BlockSpec((tm, LANE), lambda b, t: (t, 0))
    bf = lambda width: jax.ShapeDtypeStruct((nb, n_all, width), BF16)
    f32 = lambda width: jax.ShapeDtypeStruct((nb, n_all, width), F32)
    out_shapes = [bf(SLAB)] * 4 + [f32(LANE)] + [bf(SLAB), f32(2 * SLAB), bf(HG_W), bf(HG_W)] + [bf(SLAB)] * 3
    out_specs = [tile(s.shape[-1]) for s in out_shapes]
    return pl.pallas_call(
        kern,
        grid=(nb, n_all // tm),
        in_specs=[tile(d)] + _mod_specs(layer, nb, (0, 1), d) + [
            pl.BlockSpec((1, 1, d), lsel),
            pl.BlockSpec((1, d, NP_IN), lsel),
            pl.BlockSpec((1, 1, NP_IN), lsel),
            tab, tab, tab,
            pl.BlockSpec((1, 1, 2 * SLAB), lsel),
            pl.BlockSpec((1, 1, 2 * SLAB), lsel)],
        out_specs=out_specs,
        out_shape=out_shapes,
        compiler_params=_cparams(("parallel", "arbitrary")),
        name="in_proj",
    )(xs, mod4, mod4, mod4, mod4, g_pre, w_pad, b_pad, *rope_tabs, llb, l1m)


def _ml_chunks(qs, ks, vs, i_rows, lf_rows, s_exts, ms, revs):
    n = len(qs)
    size = qs[0].shape[0]
    ii = lax.broadcasted_iota(jnp.int32, (size, size), 0)
    jj = lax.broadcasted_iota(jnp.int32, (size, size), 1)
    lane = lax.broadcasted_iota(jnp.int32, (size, LANE), 1)
    cums = {rev: jnp.where((ii >= jj) if rev else (ii <= jj), 1.0, 0.0).astype(F32) for rev in set(revs)}
    seens = {rev: (jj >= ii) if rev else (jj <= ii) for rev in set(revs)}

    b_cols, g_cols, g_rows = [], [], []
    for i in range(n):
        b_row = jnp.dot(jnp.broadcast_to(lf_rows[i], (8, size)), cums[revs[i]], precision=HIGHEST,
                        preferred_element_type=F32)[0:1]
        g_row = i_rows[i] - b_row
        b_cols.append(jnp.sum(jnp.where(ii == jj, b_row, 0.0), axis=1, keepdims=True))
        g_cols.append(jnp.sum(jnp.where(ii == jj, g_row, 0.0), axis=1, keepdims=True))
        g_rows.append(g_row)

    mts, dws, w_inters = [], [], []
    for i in range(n):
        dmat = jnp.where(seens[revs[i]], b_cols[i] + g_rows[i], NEG)
        a_col = b_cols[i] + ms[i]
        mt = jnp.maximum(a_col, jnp.max(dmat, axis=1, keepdims=True))
        mts.append(mt)
        w_inters.append(jnp.exp(a_col - mt))
        dws.append(jnp.exp(dmat - mt))

    ress = []
    for i in range(n):
        qk = _dot_nt(qs[i], ks[i]) * dws[i]
        ress.append(_dot(qk.astype(BF16), vs[i]) + w_inters[i] * _dot(qs[i], s_exts[i].astype(BF16)))

    houts, s_news, m_news = [], [], []
    for i in range(n):
        den = jnp.sum(jnp.where(lane == ML_HD, ress[i], 0.0), axis=1, keepdims=True)
        houts.append(ress[i] * (1.0 / jnp.maximum(jnp.abs(den), jnp.exp(-mts[i]))))
        last = 0 if revs[i] else size - 1
        m_new = mts[i][last:last + 1]
        b_last = b_cols[i][last:last + 1]
        decay = jnp.exp(b_last + ms[i] - m_new)
        wk = jnp.exp(b_last + g_cols[i] - m_new)
        s_news.append(decay * s_exts[i] + _dot_tn((wk * ks[i].astype(F32)).astype(BF16), vs[i]))
        m_news.append(m_new)
    return houts, s_news, m_news


def _mlstm_kernel(q_ref, k_ref, v_ref, o_ref, gt_ref, gain_ref, y_ref, hf_ref, hb_ref, *, nc_ctx, nc_all, size, nh):
    hd0 = pl.program_id(1) * nh
    lane = lax.broadcasted_iota(jnp.int32, (size, LANE), 1)
    nc_lat = nc_all - nc_ctx

    def step(c, carry):
        cb = jnp.where(c < nc_ctx, nc_ctx - 1 - c, nc_ctx + nc_lat - 1 - (c - nc_ctx))
        sf = pl.ds(pl.multiple_of(c * size, size), size)
        sb = pl.ds(pl.multiple_of(cb * size, size), size)
        qs, ks, vs, i_rows, lf_rows, revs = [], [], [], [], [], []
        for hh in range(nh):
            ln = slice(hh * LANE, (hh + 1) * LANE)
            for sl, cc, off in ((sf, c, 0), (sb, cb, HEADS)):
                v = v_ref[0, sl, ln]
                qs.append(q_ref[0, sl, ln])
                ks.append(k_ref[0, sl, ln])
                vs.append(jnp.where(lane == ML_HD, jnp.ones_like(v), v))
                i_rows.append(gt_ref[0, off + hd0 + hh, pl.ds(cc, 1), :])
                lf_rows.append(gt_ref[0, 2 * HEADS + off + hd0 + hh, pl.ds(cc, 1), :])
                revs.append(off > 0)
        houts, s_news, m_news = _ml_chunks(qs, ks, vs, i_rows, lf_rows, list(carry[0]), list(carry[1]), revs)
        for hh in range(nh):
            ln = slice(hh * LANE, (hh + 1) * LANE)
            hf_ref[sf, ln] = houts[2 * hh]
            hb_ref[sb, ln] = houts[2 * hh + 1]
        return tuple(s_news), tuple(m_news)

    s0 = jnp.zeros((LANE, LANE), F32)
    m0 = jnp.full((1, 1), M_INIT, F32)
    lax.fori_loop(0, nc_all, step, ((s0,) * (2 * nh), (m0,) * (2 * nh)))

    def finish(c, _):
        sl = pl.ds(pl.multiple_of(c * size, size), size)
        for hh in range(nh):
            ln = slice(hh * LANE, (hh + 1) * LANE)
            h = jnp.where(lane < ML_HD, hf_ref[sl, ln] + hb_ref[sl, ln], 0.0)
            y = h * lax.rsqrt(jnp.sum(h * h, axis=1, keepdims=True) * (1.0 / ML_HD) + EPS) * gain_ref[0, :, ln]
            y_ref[0, sl, ln] = (y * o_ref[0, sl, ln].astype(F32)).astype(BF16)
        return 0

    lax.fori_loop(0, nc_all, finish, 0)


def _mlstm(mlq, mlk, mlv, mlo, gates_t, gain, layer, ctx_len):
    nb, n_all, _ = mlq.shape
    size, nh = ML_CHUNK, ML_HPS
    nc_all = n_all // size
    width = nh * LANE
    ng = HEADS // nh
    kern = functools.partial(_mlstm_kernel, nc_ctx=ctx_len // size, nc_all=nc_all, size=size, nh=nh)
    slab = pl.BlockSpec((1, n_all, width), lambda b, h: (b, 0, h))
    return pl.pallas_call(
        kern,
        grid=(nb, ng),
        in_specs=[slab, slab, slab, slab,
                  pl.BlockSpec((1, 4 * HEADS, nc_all, size), lambda b, h: (b, 0, 0, 0)),
                  pl.BlockSpec((1, 1, width), lambda b, h: (layer * ng + h, 0, 0))],
        out_specs=slab,
        out_shape=jax.ShapeDtypeStruct((nb, n_all, SLAB), BF16),
        scratch_shapes=[pltpu.VMEM((n_all, width), F32), pltpu.VMEM((n_all, width), F32)],
        compiler_params=_cparams(("parallel", "arbitrary")),
        name="mlstm",
    )(mlq, mlk, mlv, mlo, gates_t, gain.reshape(-1, 1, width))


def _hg_tables(size):
    t = np.arange(size)
    tri = (t[None, :] <= t[:, None]).astype(np.float32)
    x = t[:, None] ^ t[None, :]
    lvl = np.where(x < HG_DIAG, 0, np.floor(np.log2(np.maximum(x, 1))).astype(np.int64) - 2)
    code = np.where(t[None, :] <= t[:, None], lvl, -1).astype(np.int32)
    j = np.arange(HG_DIAG * LANE) // LANE
    emat = (np.arange(size)[None, :] % HG_DIAG == j[:, None]).astype(np.float32)
    return (jnp.asarray(np.stack([tri, tri.T]), BF16), jnp.asarray(np.stack([code, code.T])),
            jnp.asarray(emat, BF16))


def _hg_chunks(qs, lfs, vs, sts, tris, codes, emat, revs):
    n = len(qs)
    size = qs[0].shape[0]
    bc2s, kks = [], []
    for i in range(n):
        lf2 = lfs[i] * LOG2E
        hi = lf2.astype(BF16)
        r1 = lf2 - hi.astype(F32)
        mid = r1.astype(BF16)
        lo = (r1 - mid.astype(F32)).astype(BF16)
        cs = _dot(tris[i], jnp.concatenate([hi, mid, lo], axis=1))
        bc2s.append((cs[:, 0:LANE] + cs[:, LANE:2 * LANE]) + cs[:, 2 * LANE:3 * LANE])
        kks.append(jnp.maximum(1.0 - jnp.exp2(lf2), 0.0))

    atts = [jnp.zeros((size, size), F32) for _ in range(n)]
    c, lvl = HG_DIAG, 1
    while c < size:
        blk = 2 * c
        for i in range(n):
            ridx = c if revs[i] else c - 1
            b3 = bc2s[i].reshape(size // blk, blk, LANE)
            ref = jnp.broadcast_to(b3[:, ridx:ridx + 1, :], b3.shape).reshape(size, LANE)
            e = jnp.exp2(-jnp.abs(bc2s[i] - ref))
            a = _dot_nt((qs[i] * e).astype(BF16), (kks[i] * e).astype(BF16))
            atts[i] = jnp.where(codes[i] == lvl, a, atts[i])
        c, lvl = blk, lvl + 1

    for i in range(n):
        w3 = (bc2s[i] - jnp.log2(kks[i])).reshape(size // HG_DIAG, HG_DIAG, LANE)
        ps = []
        for j in range(HG_DIAG):
            wj = jnp.broadcast_to(w3[:, j:j + 1, :], w3.shape).reshape(size, LANE)
            ps.append((qs[i] * jnp.exp2(jnp.minimum(bc2s[i] - wj, 0.0))).astype(BF16))
        atts[i] = jnp.where(codes[i] == 0, _dot(jnp.concatenate(ps, axis=1), emat), atts[i])

    outs, new_sts = [], []
    for i in range(n):
        last = 0 if revs[i] else size - 1
        bl = bc2s[i][last:last + 1]
        outs.append(_dot(atts[i].astype(BF16), vs[i])
                    + _dot_nt((qs[i] * jnp.exp2(bc2s[i])).astype(BF16), sts[i].astype(BF16)))
        new_sts.append(sts[i] * jnp.exp2(bl) + _dot_tn(vs[i], (kks[i] * jnp.exp2(bl - bc2s[i])).astype(BF16)))
    return outs, new_sts


def _hgrn_kernel(q_ref, lf0_ref, lf1_ref, v_ref, g_ref, gain_ref, tri_ref, code_ref, emat_ref,
                 y_ref, of_ref, ob_ref, *, nc_ctx, nc_all, size):
    nc_lat = nc_all - nc_ctx
    low = lax.broadcasted_iota(jnp.int32, (size, LANE), 1) < HG_DV

    def step(c, carry):
        cb = jnp.where(c < nc_ctx, nc_ctx - 1 - c, nc_ctx + nc_lat - 1 - (c - nc_ctx))
        sf = pl.ds(pl.multiple_of(c * size, size), size)
        sb = pl.ds(pl.multiple_of(cb * size, size), size)
        qs, lfs, vs, tris, codes, revs = [], [], [], [], [], []
        for hh in range(2):
            ln = slice(hh * LANE, (hh + 1) * LANE)
            qs += [q_ref[0, sf, ln].astype(F32), q_ref[0, sb, ln].astype(F32)]
            lfs += [lf0_ref[0, sf, ln], lf1_ref[0, sb, ln]]
            vs += [v_ref[0, sf, :], v_ref[0, sb, :]]
            tris += [tri_ref[0], tri_ref[1]]
            codes += [code_ref[0], code_ref[1]]
            revs += [False, True]
        outs, sts = _hg_chunks(qs, lfs, vs, list(carry), tris, codes, emat_ref[...], revs)
        of_ref[sf, :] = jnp.where(low, outs[0], outs[2])
        ob_ref[sb, :] = jnp.where(low, outs[1], outs[3])
        return tuple(sts)

    st0 = jnp.zeros((LANE, LANE), F32)
    lax.fori_loop(0, nc_all, step, (st0,) * 4)

    def finish(c, _):
        sl = pl.ds(pl.multiple_of(c * size, size), size)
        o = of_ref[sl, :] + ob_ref[sl, :]
        sq = o * o
        ms0 = jnp.sum(jnp.where(low, sq, 0.0), axis=1, keepdims=True) * (1.0 / HG_DV)
        ms1 = jnp.sum(jnp.where(low, 0.0, sq), axis=1, keepdims=True) * (1.0 / HG_DV)
        y = o * lax.rsqrt(jnp.where(low, ms0, ms1) + EPS) * gain_ref[0]
        y_ref[0, sl, :] = (y * g_ref[0, sl, :].astype(F32)).astype(BF16)
        return 0

    lax.fori_loop(0, nc_all, finish, 0)


def _hgrn(hgq, hgf, hgv, hgg, gain, layer, ctx_len):
    nb, n_all, _ = hgq.shape
    size = HG_CHUNK
    nc_all = n_all // size
    ng = HEADS // 2
    kern = functools.partial(_hgrn_kernel, nc_ctx=ctx_len // size, nc_all=nc_all, size=size)
    wide = pl.BlockSpec((1, n_all, 2 * LANE), lambda b, h: (b, 0, h))
    slab = pl.BlockSpec((1, n_all, LANE), lambda b, h: (b, 0, h))
    tri, code, emat = _hg_tables(size)
    return pl.pallas_call(
        kern,
        grid=(nb, ng),
        in_specs=[wide, wide,
                  pl.BlockSpec((1, n_all, 2 * LANE), lambda b, h: (b, 0, ng + h)),
                  slab, slab,
                  pl.BlockSpec((1, 1, LANE), lambda b, h: (layer * ng + h, 0, 0)),
                  pl.BlockSpec(tri.shape, lambda b, h: (0, 0, 0)),
                  pl.BlockSpec(code.shape, lambda b, h: (0, 0, 0)),
                  pl.BlockSpec(emat.shape, lambda b, h: (0, 0))],
        out_specs=slab,
        out_shape=jax.ShapeDtypeStruct((nb, n_all, HG_W), BF16),
        scratch_shapes=[pltpu.VMEM((n_all, LANE), F32), pltpu.VMEM((n_all, LANE), F32)],
        compiler_params=_cparams(("parallel", "arbitrary")),
        name="hgrn2",
    )(hgq, hgf, hgf, hgv, hgg, gain, tri, code, emat)


def _attn_kernel(q_ref, k_ref, vt_ref, lam_ref, gain_ref, y_ref, s0_ref, s1_ref, p0_ref, p1_ref, m0_ref, m1_ref,
                 *, tq, kc, ctx_len, lam_init):
    n_all = k_ref.shape[1]
    n_tiles = (n_all - ctx_len) // tq
    s_refs, p_refs, m_refs = (s0_ref, s1_ref), (p0_ref, p1_ref), (m0_ref, m1_ref)
    lv = lam_ref[0]
    lam = (jnp.exp(jnp.sum(lv[0:1] * lv[1:2], axis=1, keepdims=True))
           - jnp.exp(jnp.sum(lv[2:3] * lv[3:4], axis=1, keepdims=True)) + lam_init)
    lane = lax.broadcasted_iota(jnp.int32, (tq, LANE), 1)

    def rows(t):
        return pl.ds(pl.multiple_of(ctx_len + t * tq, tq), tq)

    def load_qq(sl):
        q = q_ref[0, sl, :]
        zero = jnp.zeros_like(q)
        return jnp.concatenate([jnp.where(lane < DA_HD, q, zero), jnp.where(lane < DA_HD, zero, q)], axis=0)

    def stage_a_chunk(qq, slot, c, m):
        s = _dot_nt(k_ref[0, c * kc:(c + 1) * kc, :], qq)
        s_refs[slot][c * kc:(c + 1) * kc, :] = s
        cm = jnp.max(s, axis=0, keepdims=True)
        return cm if m is None else jnp.maximum(m, cm)

    def stage_b_chunk(slot, c, m):
        p_refs[slot][c * kc:(c + 1) * kc, :] = jnp.exp2(s_refs[slot][c * kc:(c + 1) * kc, :] - m).astype(BF16)

    def stage_c(slot, sl, nk):
        vt = vt_ref[0, :, 0:nk]
        vrow = lax.broadcasted_iota(jnp.int32, vt.shape, 0)
        acc = _dot(jnp.where(vrow == DA_VD, jnp.ones_like(vt), vt), p_refs[slot][0:nk, :])
        r0 = 1.0 / acc[DA_VD:DA_VD + 1, 0:tq]
        r1 = lam / acc[DA_VD:DA_VD + 1, tq:2 * tq]
        o = (acc[:, 0:tq] * r0 - acc[:, tq:2 * tq] * r1).T
        o = jnp.where(lane < DA_VD, o, 0.0)
        y = o * lax.rsqrt(jnp.sum(o * o, axis=1, keepdims=True) * (1.0 / DA_VD) + EPS) * gain_ref[0]
        y_ref[0, sl, :] = (y * (1.0 - lam_init)).astype(BF16)

    def stages_ab(t_a, t_b, slot_a, nk):
        slot_b = 1 - slot_a
        if t_a is not None:
            qq = load_qq(rows(t_a))
        if t_b is not None:
            m_b = m_refs[slot_b][0:1, :]
        m = None
        for c in range(nk // kc):
            if t_b is not None:
                stage_b_chunk(slot_b, c, m_b)
            if t_a is not None:
                m = stage_a_chunk(qq, slot_a, c, m)
        if t_a is not None:
            m_refs[slot_a][...] = jnp.broadcast_to(m, (8, 2 * tq))

    for i in range(ctx_len // tq):
        sl = pl.ds(i * tq, tq)
        qq = load_qq(sl)
        m = None
        for c in range(ctx_len // kc):
            m = stage_a_chunk(qq, 0, c, m)
        for c in range(ctx_len // kc):
            stage_b_chunk(0, c, m)
        stage_c(0, sl, ctx_len)

    stages_ab(0, None, 0, n_all)
    stages_ab(1, 0, 1, n_all)

    def body(i, _):
        t = 2 + 2 * i
        stage_c(0, rows(t - 2), n_all)
        stages_ab(t, t - 1, 0, n_all)
        stage_c(1, rows(t - 1), n_all)
        stages_ab(t + 1, t, 1, n_all)
        return 0

    lax.fori_loop(0, (n_tiles - 2) // 2, body, 0)
    stage_c(0, rows(n_tiles - 2), n_all)
    stages_ab(None, n_tiles - 1, 0, n_all)
    stage_c(1, rows(n_tiles - 1), n_all)


def _attn(daq, dak, dav_t, lam_pad, gain, layer, ctx_len, lam_init):
    nb, n_all, _ = daq.shape
    tq, kc = ATT_TQ, ATT_KC
    n_tiles = (n_all - ctx_len) // tq
    assert n_tiles >= 2 and n_tiles % 2 == 0 and ctx_len % tq == 0 and ctx_len % kc == 0 and n_all % kc == 0
    kern = functools.partial(_attn_kernel, tq=tq, kc=kc, ctx_len=ctx_len, lam_init=lam_init)
    slab = pl.BlockSpec((1, n_all, LANE), lambda b, h: (b, 0, h))
    sbuf, pbuf, mbuf = (pltpu.VMEM((n_all, 2 * tq), F32), pltpu.VMEM((n_all, 2 * tq), BF16),
                        pltpu.VMEM((8, 2 * tq), F32))
    return pl.pallas_call(
        kern,
        grid=(nb, HEADS),
        in_specs=[slab, slab,
                  pl.BlockSpec((1, LANE, n_all), lambda b, h: (b, h, 0)),
                  pl.BlockSpec((1, 4, LANE), lambda b, h: (layer, 0, 0)),
                  pl.BlockSpec((1, 1, LANE), lambda b, h: (layer, 0, 0))],
        out_specs=slab,
        out_shape=jax.ShapeDtypeStruct((nb, n_all, SLAB), BF16),
        scratch_shapes=[sbuf, sbuf, pbuf, pbuf, mbuf, mbuf],
        compiler_params=_cparams(("parallel", "arbitrary")),
        name="diff_attn",
    )(daq, dak, dav_t, lam_pad, gain)


def _resident(shape, index_map):
    return pl.BlockSpec(shape, index_map, pipeline_mode=pl.Buffered(1))


def _mix_ffn_kernel(yml_ref, yhg_ref, yda_ref, x_ref, g1x_ref, g1c_ref, a2x_ref, a2c_ref, s2x_ref, s2c_ref,
                    g2x_ref, g2c_ref, wo_ref, gpm_ref, gpf_ref, wg_ref, wu_ref, wd_ref, gqf_ref, o_ref, *, tm, ctx_len, t0):
    is_ctx = _row_is_ctx(tm, ctx_len, t0)
    pick = lambda c_ref, x_ref_: jnp.where(is_ctx, c_ref[0, 0], x_ref_[0, 0])
    y = (_dot(yml_ref[0], wo_ref[0, 0:SLAB, :]) + _dot(yhg_ref[0], wo_ref[0, SLAB:SLAB + HG_W, :])
         + _dot(yda_ref[0], wo_ref[0, SLAB + HG_W:2 * SLAB + HG_W, :]))
    x1 = x_ref[0] + pick(g1c_ref, g1x_ref) * _rms(y, gpm_ref[0])
    h2 = (_rms(x1, gpf_ref[0]) * (1.0 + pick(a2c_ref, a2x_ref)) + pick(s2c_ref, s2x_ref)).astype(BF16)
    act = _silu(_dot(h2, wg_ref[0])) * _dot(h2, wu_ref[0])
    f = _dot(act.astype(BF16), wd_ref[0])
    o_ref[0] = x1 + pick(g2c_ref, g2x_ref) * _rms(f, gqf_ref[0])


def _mix_ffn(yml, yhg, yda, xs, mod4, w_out_pad, g_post_mix, g_pre_ffn, wg, wu, wd, g_post_ffn, layer, ctx_len, tm, t0):
    nb, n_all, d = xs.shape
    dff = wg.shape[-1]
    kern = functools.partial(_mix_ffn_kernel, tm=tm, ctx_len=ctx_len, t0=t0)
    lsel = lambda b, t: (layer, 0, 0)
    tile = lambda width: pl.BlockSpec((1, tm, width), lambda b, t: (b, t + t0, 0))
    return pl.pallas_call(
        kern,
        grid=(nb, n_all // tm - t0),
        in_specs=[tile(SLAB), tile(HG_W), tile(SLAB), tile(d)] + _mod_specs(layer, nb, (2, 4, 3, 5), d) + [
            _resident((1, 2 * SLAB + HG_W, d), lsel),
            pl.BlockSpec((1, 1, d), lsel),
            pl.BlockSpec((1, 1, d), lsel),
            _resident((1, d, dff), lsel),
            _resident((1, d, dff), lsel),
            _resident((1, dff, d), lsel),
            pl.BlockSpec((1, 1, d), lsel)],
        out_specs=pl.BlockSpec((1, tm, d), lambda b, t: (b, t, 0)),
        out_shape=jax.ShapeDtypeStruct((nb, n_all - t0 * tm, d), F32),
        compiler_params=_cparams(("parallel", "arbitrary")),
        name="mix_ffn",
    )(yml, yhg, yda, xs, *([mod4] * 8), w_out_pad, g_post_mix, g_pre_ffn, wg, wu, wd, g_post_ffn)


def _pad_heads(w, hd):
    lead = w.shape[:-1]
    w = w.reshape(*lead, HEADS, hd)
    w = jnp.pad(w, [(0, 0)] * len(lead) + [(0, 0), (0, LANE - hd)])
    return w.reshape(*lead, SLAB)


def _pad_in_cols(w, extra_f=None):
    sizes = (4 * ML_HD,) * 4 + (2 * HEADS, 2 * HEADS, HEADS * HG_DK, 2 * HEADS * HG_DK, HEADS * HG_DV, HEADS * HG_DV,
             2 * HEADS * DA_HD, 2 * HEADS * DA_HD, HEADS * DA_VD)
    parts = []
    off = 0
    for s in sizes:
        parts.append(w[..., off:off + s])
        off += s
    gi, gf = parts[4], parts[5]
    if extra_f is not None:
        gf = gf + extra_f
    gate = jnp.concatenate([gi, gf], axis=-1)
    gate = jnp.pad(gate, [(0, 0)] * (gate.ndim - 1) + [(0, LANE - 4 * HEADS)])
    return jnp.concatenate([
        _pad_heads(parts[0], ML_HD), _pad_heads(parts[1], ML_HD), _pad_heads(parts[2], ML_HD), _pad_heads(parts[3], ML_HD),
        gate, parts[6], parts[7], parts[8], parts[9],
        _pad_heads(parts[10], DA_VD), _pad_heads(parts[11], DA_VD), _pad_heads(parts[12], DA_VD)], axis=-1)


def _pad_out_rows(w_out):
    depth, _, d = w_out.shape
    wt = jnp.swapaxes(w_out, 1, 2)
    ml, hg, da = wt[..., :4 * ML_HD], wt[..., 4 * ML_HD:4 * ML_HD + HEADS * HG_DV], wt[..., 4 * ML_HD + HEADS * HG_DV:]
    wp = jnp.concatenate([_pad_heads(ml, ML_HD), hg, _pad_heads(da, DA_VD)], axis=-1)
    return jnp.swapaxes(wp, 1, 2)


def _rope_tables(n_lat, ctx_len):
    rows = n_lat // GRID_W
    row = jnp.repeat(jnp.arange(rows), GRID_W).astype(F32)
    col = jnp.tile(jnp.arange(GRID_W), rows).astype(F32)
    half = DA_HD // 2
    inv = ROPE_BASE ** (-jnp.arange(0, half, 2, dtype=F32) / half)
    ang_r, ang_c = row[:, None] * inv, col[:, None] * inv
    zero = jnp.zeros_like(ang_r)

    def one_map(fr, fc, sel):
        r, c = fr(ang_r), fc(ang_c)
        if sel == "cos":
            return jnp.concatenate([r, r, c, c], axis=1)
        if sel == "a":
            return jnp.concatenate([-r, zero, -c, zero], axis=1)
        return jnp.concatenate([zero, r, zero, c], axis=1)

    tabs = []
    for sel, fn in (("cos", jnp.cos), ("a", jnp.sin), ("b", jnp.sin)):
        m = one_map(fn, fn, sel)
        lat = jnp.pad(jnp.concatenate([m, m], axis=1), ((0, 0), (0, LANE - 2 * DA_HD)))
        fill = 1.0 if sel == "cos" else 0.0
        ctx_rows = jnp.pad(jnp.full((ctx_len, 2 * DA_HD), fill, F32), ((0, 0), (0, LANE - 2 * DA_HD)))
        tabs.append(jnp.concatenate([ctx_rows, lat], axis=0))
    return tabs


def kernel(x, c, ctx, c_ctx, w_ada, b_ada, g_pre_mix, g_post_mix, g_pre_ffn, g_post_ffn, w_in, b_in, w_out,
           ml_f_bias, ml_norm, hg_lb, hg_norm, da_lambda, da_norm, w_ffn_gate, w_ffn_up, w_ffn_down):
    nb, n_lat, d = x.shape
    ctx_len = ctx.shape[1]
    depth = w_ada.shape[0]
    n_all = ctx_len + n_lat

    w_pad = _pad_in_cols(w_in).astype(BF16)
    b_pad = _pad_in_cols(b_in, extra_f=ml_f_bias)[:, None, :]
    w_out_pad = _pad_out_rows(w_out).astype(BF16)
    wg, wu, wd = w_ffn_gate.astype(BF16), w_ffn_up.astype(BF16), w_ffn_down.astype(BF16)
    ml_gain = _pad_heads(ml_norm, ML_HD).reshape(depth * HEADS, 1, LANE)
    hg_gain = hg_norm.reshape(-1, 1, LANE)
    da_gain = jnp.pad(da_norm, ((0, 0), (0, LANE - DA_VD)))[:, None, :]
    lam_pad = jnp.pad(da_lambda.astype(F32), ((0, 0), (0, 0), (0, LANE - DA_HD)))
    sm = jax.nn.softmax(hg_lb.astype(F32), axis=0)
    lbs = jnp.cumsum(sm, axis=0) - sm[0:1]
    llb = jnp.log(jnp.maximum(lbs, LB_FLOOR))[:, None, :]
    l1m = jnp.log1p(-lbs)[:, None, :]
    rope_tabs = _rope_tables(n_lat, ctx_len)
    r3 = lambda g: g[:, None, :]

    mp = -(-(nb + 1) // 8) * 8
    cc = jnp.concatenate([c, c_ctx[None, :], jnp.zeros((mp - nb - 1, d), F32)], axis=0)
    mod4 = _ada(cc, w_ada, b_ada).reshape(depth, mp, 1, 6 * d)

    xs = jnp.concatenate([ctx, x], axis=1)
    for l in range(depth):
        lam_init = 0.8 - 0.6 * math.exp(-0.3 * l)
        (mlq, mlk, mlv, mlo, gates, hgq, hgf, hgv, hgg, daq, dak, dav) = _inproj(
            xs, mod4, r3(g_pre_mix), w_pad, b_pad, rope_tabs, llb, l1m, l, ctx_len, 256)
        gates_t = jnp.swapaxes(gates[:, :, :4 * HEADS], 1, 2).reshape(nb, 4 * HEADS, n_all // ML_CHUNK, ML_CHUNK)
        yml = _mlstm(mlq, mlk, mlv, mlo, gates_t, ml_gain, l, ctx_len)
        yhg = _hgrn(hgq, hgf, hgv, hgg, hg_gain, l, ctx_len)
        yda = _attn(daq, dak, jnp.swapaxes(dav, 1, 2), lam_pad, da_gain, l, ctx_len, lam_init)
        last = l == depth - 1
        tm = MIX_TM_LAST if last else (MIX_TM if n_all % MIX_TM == 0 else MIX_TM_LAST)
        xs = _mix_ffn(yml, yhg, yda, xs, mod4, w_out_pad, r3(g_post_mix), r3(g_pre_ffn), wg, wu, wd,
                      r3(g_post_ffn), l, ctx_len, tm, ctx_len // tm if last else 0)
    return xs
```

```python
import functools
import math

import numpy as np
import jax
import jax.numpy as jnp
from jax import lax
from jax.experimental import pallas as pl
from jax.experimental.pallas import tpu as pltpu

F32 = jnp.float32
BF16 = jnp.bfloat16
HIGHEST = lax.Precision.HIGHEST

LANE = 128
VMEM_LIMIT = 52 * 1024 * 1024

EPS = 1e-6
NEG = -1e30
LB_FLOOR = 1e-30
M_INIT = -1e30
GRID_W = 64
ROPE_BASE = 10000.0

HEADS = 4
ML_HD = 96
HG_DV = 64
HG_DK = 128
DA_HD = 48
DA_VD = 2 * DA_HD
SLAB = HEADS * LANE
HG_W = HEADS * HG_DV

ML_CHUNK = 256
ML_HPS = 2
HG_CHUNK = 128
HG_DIAG = 8
ATT_TQ = 256
IN_CW = 256
MIX_TM = 384
MIX_TM_LAST = 256
ATT_KC = 256
LOG2E = 1.4426950408889634

OFF_MLQ, OFF_MLK, OFF_MLV, OFF_MLO = 0, SLAB, 2 * SLAB, 3 * SLAB
OFF_GATE = 4 * SLAB
OFF_HGQ = OFF_GATE + LANE
OFF_HGF = OFF_HGQ + SLAB
OFF_HGV = OFF_HGF + 2 * SLAB
OFF_HGG = OFF_HGV + HG_W
OFF_DAQ = OFF_HGG + HG_W
OFF_DAK = OFF_DAQ + SLAB
OFF_DAV = OFF_DAK + SLAB
NP_IN = OFF_DAV + SLAB


def _cparams(sem):
    return pltpu.CompilerParams(dimension_semantics=sem, vmem_limit_bytes=VMEM_LIMIT)


def _silu(x):
    return x * jax.nn.sigmoid(x)


def _log_sigmoid(z):
    return jnp.minimum(z, 0.0) - jnp.log1p(jnp.exp(-jnp.abs(z)))


def _rms(x, g):
    return x * lax.rsqrt(jnp.mean(x * x, axis=-1, keepdims=True) + EPS) * g


def _dot(a, b):
    return jnp.dot(a, b, preferred_element_type=F32)


def _dot_nt(a, b):
    return lax.dot_general(a, b, (((1,), (1,)), ((), ())), preferred_element_type=F32)


def _dot_tn(a, b):
    return lax.dot_general(a, b, (((0,), (0,)), ((), ())), preferred_element_type=F32)


def _ada_kernel(s_ref, w_ref, b_ref, o_ref):
    s = _silu(s_ref[...])
    o_ref[0] = jnp.dot(s, w_ref[0], precision=HIGHEST, preferred_element_type=F32) + b_ref[0]


def _ada(cc, w_ada, b_ada):
    depth, d, d6 = w_ada.shape
    mp = cc.shape[0]
    tn = 1024
    return pl.pallas_call(
        _ada_kernel,
        grid=(depth, d6 // tn),
        in_specs=[pl.BlockSpec((mp, d), lambda l, j: (0, 0)),
                  pl.BlockSpec((1, d, tn), lambda l, j: (l, 0, j)),
                  pl.BlockSpec((1, 1, tn), lambda l, j: (l, 0, j))],
        out_specs=pl.BlockSpec((1, mp, tn), lambda l, j: (l, 0, j)),
        out_shape=jax.ShapeDtypeStruct((depth, mp, d6), F32),
        compiler_params=_cparams(("arbitrary", "arbitrary")),
        name="ada_mod",
    )(cc, w_ada, b_ada.reshape(depth, 1, d6))


def _mod_specs(layer, nb, idxs, d):
    specs = []
    for j in idxs:
        specs.append(pl.BlockSpec((1, 1, 1, d), lambda b, t, j=j: (layer, b, 0, j)))
        specs.append(pl.BlockSpec((1, 1, 1, d), lambda b, t, j=j: (layer, nb, 0, j)))
    return specs


def _row_is_ctx(tm, ctx_len, t0=0):
    rows = (pl.program_id(1) + t0) * tm + lax.broadcasted_iota(jnp.int32, (tm, 1), 0)
    return rows < ctx_len


def _inproj_kernel(x_ref, sx_ref, sc_ref, ax_ref, ac_ref, g_ref, w_ref, b_ref,
                   cos_ref, sna_ref, snb_ref, llb_ref, l1m_ref,
                   mlq_ref, mlk_ref, mlv_ref, mlo_ref, gate_ref,
                   hgq_ref, hgf_ref, hgv_ref, hgg_ref, daq_ref, dak_ref, dav_ref,
                   *, tm, ctx_len):
    is_ctx = _row_is_ctx(tm, ctx_len)
    shift = jnp.where(is_ctx, sc_ref[0, 0], sx_ref[0, 0])
    scale = 1.0 + jnp.where(is_ctx, ac_ref[0, 0], ax_ref[0, 0])
    h = (_rms(x_ref[0], g_ref[0]) * scale + shift).astype(BF16)

    cos, sna, snb = cos_ref[...], sna_ref[...], snb_ref[...]
    half = DA_HD // 4
    lane = lax.broadcasted_iota(jnp.int32, (tm, LANE), 1)

    def rope(p):
        outs = []
        for i in range(p.shape[1] // LANE):
            xh = p[:, i * LANE:(i + 1) * LANE]
            outs.append(xh * cos + pltpu.roll(xh, LANE - half, 1) * sna + pltpu.roll(xh, half, 1) * snb)
        return jnp.concatenate(outs, axis=1)

    def log_f(p, c0):
        t2 = l1m_ref[0, :, c0:c0 + p.shape[1]] + _log_sigmoid(p)
        t1 = llb_ref[0, :, c0:c0 + p.shape[1]]
        return jnp.maximum(t1, t2) + jnp.log1p(jnp.exp(-jnp.abs(t1 - t2)))

    def gates(p, c0):
        return jnp.where((lane >= 2 * HEADS) & (lane < 4 * HEADS), _log_sigmoid(p), p)

    plain = lambda p, c0: p
    groups = {
        "mlq": (mlq_ref, OFF_MLQ, SLAB, plain),
        "mlk": (mlk_ref, OFF_MLK, SLAB, lambda p, c0: p * (ML_HD ** -0.5)),
        "mlv": (mlv_ref, OFF_MLV, SLAB, plain),
        "mlo": (mlo_ref, OFF_MLO, SLAB, lambda p, c0: jax.nn.sigmoid(p)),
        "gate": (gate_ref, OFF_GATE, LANE, gates),
        "hgq": (hgq_ref, OFF_HGQ, SLAB, lambda p, c0: _silu(p)),
        "hgf": (hgf_ref, OFF_HGF, 2 * SLAB, log_f),
        "hgv": (hgv_ref, OFF_HGV, HG_W, plain),
        "hgg": (hgg_ref, OFF_HGG, HG_W, lambda p, c0: _silu(p)),
        "daq": (daq_ref, OFF_DAQ, SLAB, lambda p, c0: rope(p) * (LOG2E * DA_HD ** -0.5)),
        "dak": (dak_ref, OFF_DAK, SLAB, lambda p, c0: rope(p)),
        "dav": (dav_ref, OFF_DAV, SLAB, plain),
    }
    order = [("hgf", 0), ("mlq", 0), ("hgf", 1), ("mlq", 1), ("hgf", 2), ("mlk", 0), ("hgf", 3), ("mlk", 1),
             ("mlo", 0), ("mlv", 0), ("mlo", 1), ("mlv", 1), ("hgq", 0), ("dav", 0), ("hgq", 1), ("dav", 1),
             ("daq", 0), ("hgv", 0), ("daq", 1), ("gate", 0), ("dak", 0), ("hgg", 0), ("dak", 1)]
    for name, ci in order:
        ref, off, width, fn = groups[name]
        cw = min(width, IN_CW)
        c0 = ci * cw
        p = _dot(h, w_ref[0, :, off + c0:off + c0 + cw]) + b_ref[0, :, off + c0:off + c0 + cw]
        if name == "dav":
            ref[0, c0:c0 + cw, :] = p.T.astype(ref.dtype)
        else:
            ref[0, :, c0:c0 + cw] = fn(p, c0).astype(ref.dtype)


def _inproj(xs, mod4, g_pre, w_pad, b_pad, rope_tabs, llb, l1m, layer, ctx_len, tm):
    nb, n_all, d = xs.shape
    kern = functools.partial(_inproj_kernel, tm=tm, ctx_len=ctx_len)
    lsel = lambda b, t: (layer, 0, 0)
    tile = lambda width: pl.BlockSpec((1, tm, width), lambda b, t: (b, t, 0))
    tab = pl.BlockSpec((tm, LANE), lambda b, t: (t, 0))
    bf = lambda width: jax.ShapeDtypeStruct((nb, n_all, width), BF16)
    f32 = lambda width: jax.ShapeDtypeStruct((nb, n_all, width), F32)
    out_shapes = [bf(SLAB)] * 4 + [f32(LANE)] + [bf(SLAB), f32(2 * SLAB), bf(HG_W), bf(HG_W)] + [bf(SLAB)] * 3
    out_specs = [tile(s.shape[-1]) for s in out_shapes]
    out_shapes[-1] = jax.ShapeDtypeStruct((nb, SLAB, n_all), BF16)
    out_specs[-1] = pl.BlockSpec((1, SLAB, tm), lambda b, t: (b, 0, t))
    return pl.pallas_call(
        kern,
        grid=(nb, n_all // tm),
        in_specs=[tile(d)] + _mod_specs(layer, nb, (0, 1), d) + [
            pl.BlockSpec((1, 1, d), lsel),
            pl.BlockSpec((1, d, NP_IN), lsel),
            pl.BlockSpec((1, 1, NP_IN), lsel),
            tab, tab, tab,
            pl.BlockSpec((1, 1, 2 * SLAB), lsel),
            pl.BlockSpec((1, 1, 2 * SLAB), lsel)],
        out_specs=out_specs,
        out_shape=out_shapes,
        compiler_params=_cparams(("parallel", "arbitrary")),
        name="in_proj",
    )(xs, mod4, mod4, mod4, mod4, g_pre, w_pad, b_pad, *rope_tabs, llb, l1m)


def _ml_chunks(qs, ks, vs, i_rows, lf_rows, s_exts, ms, revs):
    n = len(qs)
    size = qs[0].shape[0]
    ii = lax.broadcasted_iota(jnp.int32, (size, size), 0)
    jj = lax.broadcasted_iota(jnp.int32, (size, size), 1)
    lane = lax.broadcasted_iota(jnp.int32, (size, LANE), 1)
    cums = {rev: jnp.where((ii >= jj) if rev else (ii <= jj), 1.0, 0.0).astype(F32) for rev in set(revs)}
    seens = {rev: (jj >= ii) if rev else (jj <= ii) for rev in set(revs)}

    b_cols, g_cols, g_rows = [], [], []
    for i in range(n):
        b_row = jnp.dot(jnp.broadcast_to(lf_rows[i], (8, size)), cums[revs[i]], precision=HIGHEST,
                        preferred_element_type=F32)[0:1]
        g_row = i_rows[i] - b_row
        b_cols.append(jnp.sum(jnp.where(ii == jj, b_row, 0.0), axis=1, keepdims=True))
        g_cols.append(jnp.sum(jnp.where(ii == jj, g_row, 0.0), axis=1, keepdims=True))
        g_rows.append(g_row)

    mts, dws, w_inters = [], [], []
    for i in range(n):
        dmat = jnp.where(seens[revs[i]], b_cols[i] + g_rows[i], NEG)
        a_col = b_cols[i] + ms[i]
        mt = jnp.maximum(a_col, jnp.max(dmat, axis=1, keepdims=True))
        mts.append(mt)
        w_inters.append(jnp.exp(a_col - mt))
        dws.append(jnp.exp(dmat - mt))

    ress = []
    for i in range(n):
        qk = _dot_nt(qs[i], ks[i]) * dws[i]
        ress.append(_dot(qk.astype(BF16), vs[i]) + w_inters[i] * _dot(qs[i], s_exts[i].astype(BF16)))

    houts, s_news, m_news = [], [], []
    for i in range(n):
        den = jnp.sum(jnp.where(lane == ML_HD, ress[i], 0.0), axis=1, keepdims=True)
        houts.append(ress[i] * (1.0 / jnp.maximum(jnp.abs(den), jnp.exp(-mts[i]))))
        last = 0 if revs[i] else size - 1
        m_new = mts[i][last:last + 1]
        b_last = b_cols[i][last:last + 1]
        decay = jnp.exp(b_last + ms[i] - m_new)
        wk = jnp.exp(b_last + g_cols[i] - m_new)
        s_news.append(decay * s_exts[i] + _dot_tn((wk * ks[i].astype(F32)).astype(BF16), vs[i]))
        m_news.append(m_new)
    return houts, s_news, m_news


def _mlstm_kernel(q_ref, k_ref, v_ref, o_ref, gt_ref, gain_ref, y_ref, hf_ref, hb_ref, *, nc_ctx, nc_all, size, nh):
    hd0 = pl.program_id(1) * nh
    lane = lax.broadcasted_iota(jnp.int32, (size, LANE), 1)
    nc_lat = nc_all - nc_ctx

    def step(c, carry):
        cb = jnp.where(c < nc_ctx, nc_ctx - 1 - c, nc_ctx + nc_lat - 1 - (c - nc_ctx))
        sf = pl.ds(pl.multiple_of(c * size, size), size)
        sb = pl.ds(pl.multiple_of(cb * size, size), size)
        qs, ks, vs, i_rows, lf_rows, revs = [], [], [], [], [], []
        for hh in range(nh):
            ln = slice(hh * LANE, (hh + 1) * LANE)
            for sl, cc, off in ((sf, c, 0), (sb, cb, HEADS)):
                v = v_ref[0, sl, ln]
                qs.append(q_ref[0, sl, ln])
                ks.append(k_ref[0, sl, ln])
                vs.append(jnp.where(lane == ML_HD, jnp.ones_like(v), v))
                i_rows.append(gt_ref[0, off + hd0 + hh, pl.ds(cc, 1), :])
                lf_rows.append(gt_ref[0, 2 * HEADS + off + hd0 + hh, pl.ds(cc, 1), :])
                revs.append(off > 0)
        houts, s_news, m_news = _ml_chunks(qs, ks, vs, i_rows, lf_rows, list(carry[0]), list(carry[1]), revs)
        for hh in range(nh):
            ln = slice(hh * LANE, (hh + 1) * LANE)
            hf_ref[sf, ln] = houts[2 * hh]
            hb_ref[sb, ln] = houts[2 * hh + 1]
        return tuple(s_news), tuple(m_news)

    s0 = jnp.zeros((LANE, LANE), F32)
    m0 = jnp.full((1, 1), M_INIT, F32)
    lax.fori_loop(0, nc_all, step, ((s0,) * (2 * nh), (m0,) * (2 * nh)))

    def finish(c, _):
        sl = pl.ds(pl.multiple_of(c * size, size), size)
        for hh in range(nh):
            ln = slice(hh * LANE, (hh + 1) * LANE)
            h = jnp.where(lane < ML_HD, hf_ref[sl, ln] + hb_ref[sl, ln], 0.0)
            y = h * lax.rsqrt(jnp.sum(h * h, axis=1, keepdims=True) * (1.0 / ML_HD) + EPS) * gain_ref[0, :, ln]
            y_ref[0, sl, ln] = (y * o_ref[0, sl, ln].astype(F32)).astype(BF16)
        return 0

    lax.fori_loop(0, nc_all, finish, 0)


def _mlstm(mlq, mlk, mlv, mlo, gates_t, gain, layer, ctx_len):
    nb, n_all, _ = mlq.shape
    size, nh = ML_CHUNK, ML_HPS
    nc_all = n_all // size
    width = nh * LANE
    ng = HEADS // nh
    kern = functools.partial(_mlstm_kernel, nc_ctx=ctx_len // size, nc_all=nc_all, size=size, nh=nh)
    slab = pl.BlockSpec((1, n_all, width), lambda b, h: (b, 0, h))
    return pl.pallas_call(
        kern,
        grid=(nb, ng),
        in_specs=[slab, slab, slab, slab,
                  pl.BlockSpec((1, 4 * HEADS, nc_all, size), lambda b, h: (b, 0, 0, 0)),
                  pl.BlockSpec((1, 1, width), lambda b, h: (layer * ng + h, 0, 0))],
        out_specs=slab,
        out_shape=jax.ShapeDtypeStruct((nb, n_all, SLAB), BF16),
        scratch_shapes=[pltpu.VMEM((n_all, width), F32), pltpu.VMEM((n_all, width), F32)],
        compiler_params=_cparams(("parallel", "arbitrary")),
        name="mlstm",
    )(mlq, mlk, mlv, mlo, gates_t, gain.reshape(-1, 1, width))


def _hg_tables(size):
    t = np.arange(size)
    tri = (t[None, :] <= t[:, None]).astype(np.float32)
    x = t[:, None] ^ t[None, :]
    lvl = np.where(x < HG_DIAG, 0, np.floor(np.log2(np.maximum(x, 1))).astype(np.int64) - 2)
    code = np.where(t[None, :] <= t[:, None], lvl, -1).astype(np.int32)
    j = np.arange(HG_DIAG * LANE) // LANE
    emat = (np.arange(size)[None, :] % HG_DIAG == j[:, None]).astype(np.float32)
    return (jnp.asarray(np.stack([tri, tri.T]), BF16), jnp.asarray(np.stack([code, code.T])),
            jnp.asarray(emat, BF16))


def _hg_chunks(qs, lfs, vs, sts, tris, codes, emat, revs):
    n = len(qs)
    size = qs[0].shape[0]
    bc2s, kks = [], []
    for i in range(n):
        lf2 = lfs[i] * LOG2E
        hi = lf2.astype(BF16)
        r1 = lf2 - hi.astype(F32)
        mid = r1.astype(BF16)
        lo = (r1 - mid.astype(F32)).astype(BF16)
        cs = _dot(tris[i], jnp.concatenate([hi, mid, lo], axis=1))
        bc2s.append((cs[:, 0:LANE] + cs[:, LANE:2 * LANE]) + cs[:, 2 * LANE:3 * LANE])
        kks.append(jnp.maximum(1.0 - jnp.exp2(lf2), 0.0))

    atts = [jnp.zeros((size, size), F32) for _ in range(n)]
    c, lvl = HG_DIAG, 1
    while c < size:
        blk = 2 * c
        for i in range(n):
            ridx = c if revs[i] else c - 1
            b3 = bc2s[i].reshape(size // blk, blk, LANE)
            ref = jnp.broadcast_to(b3[:, ridx:ridx + 1, :], b3.shape).reshape(size, LANE)
            e = jnp.exp2(-jnp.abs(bc2s[i] - ref))
            a = _dot_nt((qs[i] * e).astype(BF16), (kks[i] * e).astype(BF16))
            atts[i] = jnp.where(codes[i] == lvl, a, atts[i])
        c, lvl = blk, lvl + 1

    for i in range(n):
        w3 = (bc2s[i] - jnp.log2(kks[i])).reshape(size // HG_DIAG, HG_DIAG, LANE)
        ps = []
        for j in range(HG_DIAG):
            wj = jnp.broadcast_to(w3[:, j:j + 1, :], w3.shape).reshape(size, LANE)
            ps.append((qs[i] * jnp.exp2(jnp.minimum(bc2s[i] - wj, 0.0))).astype(BF16))
        atts[i] = jnp.where(codes[i] == 0, _dot(jnp.concatenate(ps, axis=1), emat), atts[i])

    outs, new_sts = [], []
    for i in range(n):
        last = 0 if revs[i] else size - 1
        bl = bc2s[i][last:last + 1]
        outs.append(_dot(atts[i].astype(BF16), vs[i])
                    + _dot_nt((qs[i] * jnp.exp2(bc2s[i])).astype(BF16), sts[i].astype(BF16)))
        new_sts.append(sts[i] * jnp.exp2(bl) + _dot_tn(vs[i], (kks[i] * jnp.exp2(bl - bc2s[i])).astype(BF16)))
    return outs, new_sts


def _hgrn_kernel(q_ref, lf0_ref, lf1_ref, v_ref, g_ref, gain_ref, tri_ref, code_ref, emat_ref,
                 y_ref, of_ref, ob_ref, *, nc_ctx, nc_all, size):
    nc_lat = nc_all - nc_ctx
    low = lax.broadcasted_iota(jnp.int32, (size, LANE), 1) < HG_DV

    def step(c, carry):
        cb = jnp.where(c < nc_ctx, nc_ctx - 1 - c, nc_ctx + nc_lat - 1 - (c - nc_ctx))
        sf = pl.ds(pl.multiple_of(c * size, size), size)
        sb = pl.ds(pl.multiple_of(cb * size, size), size)
        qs, lfs, vs, tris, codes, revs = [], [], [], [], [], []
        for hh in range(2):
            ln = slice(hh * LANE, (hh + 1) * LANE)
            qs += [q_ref[0, sf, ln].astype(F32), q_ref[0, sb, ln].astype(F32)]
            lfs += [lf0_ref[0, sf, ln], lf1_ref[0, sb, ln]]
            vs += [v_ref[0, sf, :], v_ref[0, sb, :]]
            tris += [tri_ref[0], tri_ref[1]]
            codes += [code_ref[0], code_ref[1]]
            revs += [False, True]
        outs, sts = _hg_chunks(qs, lfs, vs, list(carry), tris, codes, emat_ref[...], revs)
        of_ref[sf, :] = jnp.where(low, outs[0], outs[2])
        ob_ref[sb, :] = jnp.where(low, outs[1], outs[3])
        return tuple(sts)

    st0 = jnp.zeros((LANE, LANE), F32)
    lax.fori_loop(0, nc_all, step, (st0,) * 4)

    def finish(c, _):
        sl = pl.ds(pl.multiple_of(c * size, size), size)
        o = of_ref[sl, :] + ob_ref[sl, :]
        sq = o * o
        ms0 = jnp.sum(jnp.where(low, sq, 0.0), axis=1, keepdims=True) * (1.0 / HG_DV)
        ms1 = jnp.sum(jnp.where(low, 0.0, sq), axis=1, keepdims=True) * (1.0 / HG_DV)
        y = o * lax.rsqrt(jnp.where(low, ms0, ms1) + EPS) * gain_ref[0]
        y_ref[0, sl, :] = (y * g_ref[0, sl, :].astype(F32)).astype(BF16)
        return 0

    lax.fori_loop(0, nc_all, finish, 0)


def _hgrn(hgq, hgf, hgv, hgg, gain, layer, ctx_len):
    nb, n_all, _ = hgq.shape
    size = HG_CHUNK
    nc_all = n_all // size
    ng = HEADS // 2
    kern = functools.partial(_hgrn_kernel, nc_ctx=ctx_len // size, nc_all=nc_all, size=size)
    wide = pl.BlockSpec((1, n_all, 2 * LANE), lambda b, h: (b, 0, h))
    slab = pl.BlockSpec((1, n_all, LANE), lambda b, h: (b, 0, h))
    tri, code, emat = _hg_tables(size)
    return pl.pallas_call(
        kern,
        grid=(nb, ng),
        in_specs=[wide, wide,
                  pl.BlockSpec((1, n_all, 2 * LANE), lambda b, h: (b, 0, ng + h)),
                  slab, slab,
                  pl.BlockSpec((1, 1, LANE), lambda b, h: (layer * ng + h, 0, 0)),
                  pl.BlockSpec(tri.shape, lambda b, h: (0, 0, 0)),
                  pl.BlockSpec(code.shape, lambda b, h: (0, 0, 0)),
                  pl.BlockSpec(emat.shape, lambda b, h: (0, 0))],
        out_specs=slab,
        out_shape=jax.ShapeDtypeStruct((nb, n_all, HG_W), BF16),
        scratch_shapes=[pltpu.VMEM((n_all, LANE), F32), pltpu.VMEM((n_all, LANE), F32)],
        compiler_params=_cparams(("parallel", "arbitrary")),
        name="hgrn2",
    )(hgq, hgf, hgf, hgv, hgg, gain, tri, code, emat)


def _attn_kernel(q_ref, k_ref, vt_ref, lam_ref, gain_ref, y_ref, s0_ref, s1_ref, m0_ref, m1_ref,
                 *, tq, kc, ctx_len, lam_init):
    n_all = k_ref.shape[1]
    n_tiles = (n_all - ctx_len) // tq
    s_refs, m_refs = (s0_ref, s1_ref), (m0_ref, m1_ref)
    lv = lam_ref[0]
    lam = (jnp.exp(jnp.sum(lv[0:1] * lv[1:2], axis=1, keepdims=True))
           - jnp.exp(jnp.sum(lv[2:3] * lv[3:4], axis=1, keepdims=True)) + lam_init)
    lane = lax.broadcasted_iota(jnp.int32, (tq, LANE), 1)
    vrow = lax.broadcasted_iota(jnp.int32, (LANE, kc), 0)

    def rows(t):
        return pl.ds(pl.multiple_of(ctx_len + t * tq, tq), tq)

    def load_qq(sl):
        q = q_ref[0, sl, :]
        zero = jnp.zeros_like(q)
        return jnp.concatenate([jnp.where(lane < DA_HD, q, zero), jnp.where(lane < DA_HD, zero, q)], axis=0)

    def stage_a_chunk(qq, slot, c, m):
        s = _dot_nt(k_ref[0, c * kc:(c + 1) * kc, :], qq)
        s_refs[slot][c * kc:(c + 1) * kc, :] = s
        cm = jnp.max(s, axis=0, keepdims=True)
        return cm if m is None else jnp.maximum(m, cm)

    def stage_b_chunk(slot, c, m, acc):
        p = jnp.exp2(s_refs[slot][c * kc:(c + 1) * kc, :] - m).astype(BF16)
        vt = vt_ref[0, :, c * kc:(c + 1) * kc]
        pv = _dot(jnp.where(vrow == DA_VD, jnp.ones_like(vt), vt), p)
        return pv if acc is None else acc + pv

    def finish(acc, sl):
        r0 = 1.0 / acc[DA_VD:DA_VD + 1, 0:tq]
        r1 = lam / acc[DA_VD:DA_VD + 1, tq:2 * tq]
        o = (acc[:, 0:tq] * r0 - acc[:, tq:2 * tq] * r1).T
        o = jnp.where(lane < DA_VD, o, 0.0)
        y = o * lax.rsqrt(jnp.sum(o * o, axis=1, keepdims=True) * (1.0 / DA_VD) + EPS) * gain_ref[0]
        y_ref[0, sl, :] = (y * (1.0 - lam_init)).astype(BF16)

    def stages(sl_a, sl_b, slot_a, nk_a, nk_b):
        slot_b = 1 - slot_a
        if sl_a is not None:
            qq = load_qq(sl_a)
        if sl_b is not None:
            m_b = m_refs[slot_b][0:1, :]
        m = acc = None
        for c in range(max(nk_a, nk_b) // kc):
            if sl_a is not None and c < nk_a // kc:
                m = stage_a_chunk(qq, slot_a, c, m)
            if sl_b is not None and c < nk_b // kc:
                acc = stage_b_chunk(slot_b, c, m_b, acc)
        if sl_a is not None:
            m_refs[slot_a][...] = jnp.broadcast_to(m, (8, 2 * tq))
        if sl_b is not None:
            finish(acc, sl_b)

    ctx_rows = pl.ds(0, tq)
    stages(ctx_rows, None, 1, ctx_len, 0)
    stages(rows(0), ctx_rows, 0, n_all, ctx_len)

    def body(i, _):
        t = 1 + 2 * i
        stages(rows(t), rows(t - 1), 1, n_all, n_all)
        stages(rows(t + 1), rows(t), 0, n_all, n_all)
        return 0

    lax.fori_loop(0, (n_tiles - 2) // 2, body, 0)
    stages(rows(n_tiles - 1), rows(n_tiles - 2), 1, n_all, n_all)
    stages(None, rows(n_tiles - 1), 0, 0, n_all)


def _attn(daq, dak, dav_t, lam_pad, gain, layer, ctx_len, lam_init):
    nb, n_all, _ = daq.shape
    tq, kc = ATT_TQ, ATT_KC
    n_tiles = (n_all - ctx_len) // tq
    assert n_tiles >= 2 and n_tiles % 2 == 0 and ctx_len == tq and ctx_len % kc == 0 and n_all % kc == 0
    kern = functools.partial(_attn_kernel, tq=tq, kc=kc, ctx_len=ctx_len, lam_init=lam_init)
    slab = pl.BlockSpec((1, n_all, LANE), lambda b, h: (b, 0, h))
    sbuf, mbuf = pltpu.VMEM((n_all, 2 * tq), F32), pltpu.VMEM((8, 2 * tq), F32)
    return pl.pallas_call(
        kern,
        grid=(nb, HEADS),
        in_specs=[slab, slab,
                  pl.BlockSpec((1, LANE, n_all), lambda b, h: (b, h, 0)),
                  pl.BlockSpec((1, 4, LANE), lambda b, h: (layer, 0, 0)),
                  pl.BlockSpec((1, 1, LANE), lambda b, h: (layer, 0, 0))],
        out_specs=slab,
        out_shape=jax.ShapeDtypeStruct((nb, n_all, SLAB), BF16),
        scratch_shapes=[sbuf, sbuf, mbuf, mbuf],
        compiler_params=_cparams(("parallel", "arbitrary")),
        name="diff_attn",
    )(daq, dak, dav_t, lam_pad, gain)


def _resident(shape, index_map):
    return pl.BlockSpec(shape, index_map, pipeline_mode=pl.Buffered(1))


def _mix_ffn_kernel(yml_ref, yhg_ref, yda_ref, x_ref, g1x_ref, g1c_ref, a2x_ref, a2c_ref, s2x_ref, s2c_ref,
                    g2x_ref, g2c_ref, wo_ref, gpm_ref, gpf_ref, wg_ref, wu_ref, wd_ref, gqf_ref, o_ref, *, tm, ctx_len, t0):
    is_ctx = _row_is_ctx(tm, ctx_len, t0)
    pick = lambda c_ref, x_ref_: jnp.where(is_ctx, c_ref[0, 0], x_ref_[0, 0])
    y = (_dot(yml_ref[0], wo_ref[0, 0:SLAB, :]) + _dot(yhg_ref[0], wo_ref[0, SLAB:SLAB + HG_W, :])
         + _dot(yda_ref[0], wo_ref[0, SLAB + HG_W:2 * SLAB + HG_W, :]))
    x1 = x_ref[0] + pick(g1c_ref, g1x_ref) * _rms(y, gpm_ref[0])
    h2 = (_rms(x1, gpf_ref[0]) * (1.0 + pick(a2c_ref, a2x_ref)) + pick(s2c_ref, s2x_ref)).astype(BF16)
    act = _silu(_dot(h2, wg_ref[0])) * _dot(h2, wu_ref[0])
    f = _dot(act.astype(BF16), wd_ref[0])
    o_ref[0] = x1 + pick(g2c_ref, g2x_ref) * _rms(f, gqf_ref[0])


def _mix_ffn(yml, yhg, yda, xs, mod4, w_out_pad, g_post_mix, g_pre_ffn, wg, wu, wd, g_post_ffn, layer, ctx_len, tm, t0):
    nb, n_all, d = xs.shape
    dff = wg.shape[-1]
    kern = functools.partial(_mix_ffn_kernel, tm=tm, ctx_len=ctx_len, t0=t0)
    lsel = lambda b, t: (layer, 0, 0)
    tile = lambda width: pl.BlockSpec((1, tm, width), lambda b, t: (b, t + t0, 0))
    return pl.pallas_call(
        kern,
        grid=(nb, n_all // tm - t0),
        in_specs=[tile(SLAB), tile(HG_W), tile(SLAB), tile(d)] + _mod_specs(layer, nb, (2, 4, 3, 5), d) + [
            _resident((1, 2 * SLAB + HG_W, d), lsel),
            pl.BlockSpec((1, 1, d), lsel),
            pl.BlockSpec((1, 1, d), lsel),
            _resident((1, d, dff), lsel),
            _resident((1, d, dff), lsel),
            _resident((1, dff, d), lsel),
            pl.BlockSpec((1, 1, d), lsel)],
        out_specs=pl.BlockSpec((1, tm, d), lambda b, t: (b, t, 0)),
        out_shape=jax.ShapeDtypeStruct((nb, n_all - t0 * tm, d), F32),
        compiler_params=_cparams(("parallel", "arbitrary")),
        name="mix_ffn",
    )(yml, yhg, yda, xs, *([mod4] * 8), w_out_pad, g_post_mix, g_pre_ffn, wg, wu, wd, g_post_ffn)


def _pad_heads(w, hd):
    lead = w.shape[:-1]
    w = w.reshape(*lead, HEADS, hd)
    w = jnp.pad(w, [(0, 0)] * len(lead) + [(0, 0), (0, LANE - hd)])
    return w.reshape(*lead, SLAB)


def _pad_in_cols(w, extra_f=None):
    sizes = (4 * ML_HD,) * 4 + (2 * HEADS, 2 * HEADS, HEADS * HG_DK, 2 * HEADS * HG_DK, HEADS * HG_DV, HEADS * HG_DV,
             2 * HEADS * DA_HD, 2 * HEADS * DA_HD, HEADS * DA_VD)
    parts = []
    off = 0
    for s in sizes:
        parts.append(w[..., off:off + s])
        off += s
    gi, gf = parts[4], parts[5]
    if extra_f is not None:
        gf = gf + extra_f
    gate = jnp.concatenate([gi, gf], axis=-1)
    gate = jnp.pad(gate, [(0, 0)] * (gate.ndim - 1) + [(0, LANE - 4 * HEADS)])
    return jnp.concatenate([
        _pad_heads(parts[0], ML_HD), _pad_heads(parts[1], ML_HD), _pad_heads(parts[2], ML_HD), _pad_heads(parts[3], ML_HD),
        gate, parts[6], parts[7], parts[8], parts[9],
        _pad_heads(parts[10], DA_VD), _pad_heads(parts[11], DA_VD), _pad_heads(parts[12], DA_VD)], axis=-1)


def _pad_out_rows(w_out):
    depth, _, d = w_out.shape
    wt = jnp.swapaxes(w_out, 1, 2)
    ml, hg, da = wt[..., :4 * ML_HD], wt[..., 4 * ML_HD:4 * ML_HD + HEADS * HG_DV], wt[..., 4 * ML_HD + HEADS * HG_DV:]
    wp = jnp.concatenate([_pad_heads(ml, ML_HD), hg, _pad_heads(da, DA_VD)], axis=-1)
    return jnp.swapaxes(wp, 1, 2)


def _rope_tables(n_lat, ctx_len):
    rows = n_lat // GRID_W
    row = jnp.repeat(jnp.arange(rows), GRID_W).astype(F32)
    col = jnp.tile(jnp.arange(GRID_W), rows).astype(F32)
    half = DA_HD // 2
    inv = ROPE_BASE ** (-jnp.arange(0, half, 2, dtype=F32) / half)
    ang_r, ang_c = row[:, None] * inv, col[:, None] * inv
    zero = jnp.zeros_like(ang_r)

    def one_map(fr, fc, sel):
        r, c = fr(ang_r), fc(ang_c)
        if sel == "cos":
            return jnp.concatenate([r, r, c, c], axis=1)
        if sel == "a":
            return jnp.concatenate([-r, zero, -c, zero], axis=1)
        return jnp.concatenate([zero, r, zero, c], axis=1)

    tabs = []
    for sel, fn in (("cos", jnp.cos), ("a", jnp.sin), ("b", jnp.sin)):
        m = one_map(fn, fn, sel)
        lat = jnp.pad(jnp.concatenate([m, m], axis=1), ((0, 0), (0, LANE - 2 * DA_HD)))
        fill = 1.0 if sel == "cos" else 0.0
        ctx_rows = jnp.pad(jnp.full((ctx_len, 2 * DA_HD), fill, F32), ((0, 0), (0, LANE - 2 * DA_HD)))
        tabs.append(jnp.concatenate([ctx_rows, lat], axis=0))
    return tabs


def kernel(x, c, ctx, c_ctx, w_ada, b_ada, g_pre_mix, g_post_mix, g_pre_ffn, g_post_ffn, w_in, b_in, w_out,
           ml_f_bias, ml_norm, hg_lb, hg_norm, da_lambda, da_norm, w_ffn_gate, w_ffn_up, w_ffn_down):
    nb, n_lat, d = x.shape
    ctx_len = ctx.shape[1]
    depth = w_ada.shape[0]
    n_all = ctx_len + n_lat

    w_pad = _pad_in_cols(w_in).astype(BF16)
    b_pad = _pad_in_cols(b_in, extra_f=ml_f_bias)[:, None, :]
    w_out_pad = _pad_out_rows(w_out).astype(BF16)
    wg, wu, wd = w_ffn_gate.astype(BF16), w_ffn_up.astype(BF16), w_ffn_down.astype(BF16)
    ml_gain = _pad_heads(ml_norm, ML_HD).reshape(depth * HEADS, 1, LANE)
    hg_gain = hg_norm.reshape(-1, 1, LANE)
    da_gain = jnp.pad(da_norm, ((0, 0), (0, LANE - DA_VD)))[:, None, :]
    lam_pad = jnp.pad(da_lambda.astype(F32), ((0, 0), (0, 0), (0, LANE - DA_HD)))
    sm = jax.nn.softmax(hg_lb.astype(F32), axis=0)
    lbs = jnp.cumsum(sm, axis=0) - sm[0:1]
    llb = jnp.log(jnp.maximum(lbs, LB_FLOOR))[:, None, :]
    l1m = jnp.log1p(-lbs)[:, None, :]
    rope_tabs = _rope_tables(n_lat, ctx_len)
    r3 = lambda g: g[:, None, :]

    mp = -(-(nb + 1) // 8) * 8
    cc = jnp.concatenate([c, c_ctx[None, :], jnp.zeros((mp - nb - 1, d), F32)], axis=0)
    mod4 = _ada(cc, w_ada, b_ada).reshape(depth, mp, 1, 6 * d)

    xs = jnp.concatenate([ctx, x], axis=1)
    for l in range(depth):
        lam_init = 0.8 - 0.6 * math.exp(-0.3 * l)
        (mlq, mlk, mlv, mlo, gates, hgq, hgf, hgv, hgg, daq, dak, dav_t) = _inproj(
            xs, mod4, r3(g_pre_mix), w_pad, b_pad, rope_tabs, llb, l1m, l, ctx_len, 256)
        gates_t = jnp.swapaxes(gates[:, :, :4 * HEADS], 1, 2).reshape(nb, 4 * HEADS, n_all // ML_CHUNK, ML_CHUNK)
        yml = _mlstm(mlq, mlk, mlv, mlo, gates_t, ml_gain, l, ctx_len)
        yhg = _hgrn(hgq, hgf, hgv, hgg, hg_gain, l, ctx_len)
        yda = _attn(daq, dak, dav_t, lam_pad, da_gain, l, ctx_len, lam_init)
        last = l == depth - 1
        tm = MIX_TM_LAST if last else (MIX_TM if n_all % MIX_TM == 0 else MIX_TM_LAST)
        xs = _mix_ffn(yml, yhg, yda, xs, mod4, w_out_pad, r3(g_post_mix), r3(g_pre_ffn), wg, wu, wd,
                      r3(g_post_ffn), l, ctx_len, tm, ctx_len // tm if last else 0)
    return xs
```

```python
import functools
import math

import numpy as np
import jax
import jax.numpy as jnp
from jax import lax
from jax.experimental import pallas as pl
from jax.experimental.pallas import tpu as pltpu

F32 = jnp.float32
BF16 = jnp.bfloat16
HIGHEST = lax.Precision.HIGHEST

LANE = 128
VMEM_LIMIT = 52 * 1024 * 1024

EPS = 1e-6
NEG = -1e30
LB_FLOOR = 1e-30
M_INIT = -1e30
GRID_W = 64
ROPE_BASE = 10000.0

HEADS = 4
ML_HD = 96
HG_DV = 64
HG_DK = 128
DA_HD = 48
DA_VD = 2 * DA_HD
SLAB = HEADS * LANE
HG_W = HEADS * HG_DV

ML_CHUNK = 256
ML_HPS = 2
HG_CHUNK = 128
HG_DIAG = 8
ATT_TQ = 256
IN_CW = 256
IN_TM = 768
IN_SUB = 3
MIX_TM = 256
MIX_SUB = 2
MIX_TM_LAST = 256
ATT_KC = 256
LOG2E = 1.4426950408889634

OFF_MLQ, OFF_MLK, OFF_MLV, OFF_MLO = 0, SLAB, 2 * SLAB, 3 * SLAB
OFF_GATE = 4 * SLAB
OFF_HGQ = OFF_GATE + LANE
OFF_HGF = OFF_HGQ + SLAB
OFF_HGV = OFF_HGF + 2 * SLAB
OFF_HGG = OFF_HGV + HG_W
OFF_DAQ = OFF_HGG + HG_W
OFF_DAK = OFF_DAQ + SLAB
OFF_DAV = OFF_DAK + SLAB
NP_IN = OFF_DAV + SLAB


def _cparams(sem):
    return pltpu.CompilerParams(dimension_semantics=sem, vmem_limit_bytes=VMEM_LIMIT)


def _silu(x):
    return x * jax.nn.sigmoid(x)


def _log_sigmoid(z):
    return jnp.minimum(z, 0.0) - jnp.log1p(jnp.exp(-jnp.abs(z)))


def _rms(x, g):
    return x * lax.rsqrt(jnp.mean(x * x, axis=-1, keepdims=True) + EPS) * g


def _dot(a, b):
    return jnp.dot(a, b, preferred_element_type=F32)


def _dot_nt(a, b):
    return lax.dot_general(a, b, (((1,), (1,)), ((), ())), preferred_element_type=F32)


def _dot_tn(a, b):
    return lax.dot_general(a, b, (((0,), (0,)), ((), ())), preferred_element_type=F32)


def _ada_kernel(s_ref, w_ref, b_ref, o_ref):
    s = _silu(s_ref[...])
    o_ref[0] = jnp.dot(s, w_ref[0], precision=HIGHEST, preferred_element_type=F32) + b_ref[0]


def _ada(cc, w_ada, b_ada):
    depth, d, d6 = w_ada.shape
    mp = cc.shape[0]
    tn = 1024
    return pl.pallas_call(
        _ada_kernel,
        grid=(depth, d6 // tn),
        in_specs=[pl.BlockSpec((mp, d), lambda l, j: (0, 0)),
                  pl.BlockSpec((1, d, tn), lambda l, j: (l, 0, j)),
                  pl.BlockSpec((1, 1, tn), lambda l, j: (l, 0, j))],
        out_specs=pl.BlockSpec((1, mp, tn), lambda l, j: (l, 0, j)),
        out_shape=jax.ShapeDtypeStruct((depth, mp, d6), F32),
        compiler_params=_cparams(("arbitrary", "arbitrary")),
        name="ada_mod",
    )(cc, w_ada, b_ada.reshape(depth, 1, d6))


def _mod_specs(layer, nb, idxs, d):
    specs = []
    for j in idxs:
        specs.append(pl.BlockSpec((1, 1, 1, d), lambda b, t, j=j: (layer, b, 0, j)))
        specs.append(pl.BlockSpec((1, 1, 1, d), lambda b, t, j=j: (layer, nb, 0, j)))
    return specs


def _inproj_kernel(x_ref, sx_ref, sc_ref, ax_ref, ac_ref, g_ref, w_ref, b_ref,
                   cos_ref, sna_ref, snb_ref, lbf_ref, oml_ref,
                   mlq_ref, mlk_ref, mlv_ref, mlo_ref, gate_ref,
                   hgq_ref, hgf_ref, hgv_ref, hgg_ref, daq_ref, dak_ref, dav_ref,
                   *, tm, ctx_len):
    split = min(ctx_len % tm if ctx_len % tm else tm, tm)
    top_is_ctx = pl.program_id(1) * tm < ctx_len
    sub = tm // IN_SUB
    half = DA_HD // 4
    lane = lax.broadcasted_iota(jnp.int32, (sub, LANE), 1)

    def rope(p, rs):
        cos, sna, snb = cos_ref[rs, :], sna_ref[rs, :], snb_ref[rs, :]
        outs = []
        for i in range(p.shape[1] // LANE):
            xh = p[:, i * LANE:(i + 1) * LANE]
            outs.append(xh * cos + pltpu.roll(xh, LANE - half, 1) * sna + pltpu.roll(xh, half, 1) * snb)
        return jnp.concatenate(outs, axis=1)

    def log2_f(p, c0, rs):
        w = p.shape[1]
        return jnp.log2(lbf_ref[0, :, c0:c0 + w] + oml_ref[0, :, c0:c0 + w] * jax.nn.sigmoid(p))

    def gates(p, c0, rs):
        return jnp.where((lane >= 2 * HEADS) & (lane < 4 * HEADS), _log_sigmoid(p), p)

    plain = lambda p, c0, rs: p
    groups = {
        "mlq": (mlq_ref, OFF_MLQ, SLAB, plain),
        "mlk": (mlk_ref, OFF_MLK, SLAB, lambda p, c0, rs: p * (ML_HD ** -0.5)),
        "mlv": (mlv_ref, OFF_MLV, SLAB, plain),
        "mlo": (mlo_ref, OFF_MLO, SLAB, lambda p, c0, rs: jax.nn.sigmoid(p)),
        "gate": (gate_ref, OFF_GATE, LANE, gates),
        "hgq": (hgq_ref, OFF_HGQ, SLAB, lambda p, c0, rs: _silu(p)),
        "hgf": (hgf_ref, OFF_HGF, 2 * SLAB, log2_f),
        "hgv": (hgv_ref, OFF_HGV, HG_W, plain),
        "hgg": (hgg_ref, OFF_HGG, HG_W, lambda p, c0, rs: _silu(p)),
        "daq": (daq_ref, OFF_DAQ, SLAB, lambda p, c0, rs: rope(p, rs) * (LOG2E * DA_HD ** -0.5)),
        "dak": (dak_ref, OFF_DAK, SLAB, lambda p, c0, rs: rope(p, rs)),
        "dav": (dav_ref, OFF_DAV, SLAB, plain),
    }
    order = [("hgf", 0), ("mlq", 0), ("hgf", 1), ("mlq", 1), ("hgf", 2), ("mlk", 0), ("hgf", 3), ("mlk", 1),
             ("mlo", 0), ("mlv", 0), ("mlo", 1), ("mlv", 1), ("hgq", 0), ("dav", 0), ("hgq", 1), ("dav", 1),
             ("daq", 0), ("hgv", 0), ("daq", 1), ("gate", 0), ("dak", 0), ("hgg", 0), ("dak", 1)]
    for i in range(IN_SUB):
        rs = slice(i * sub, (i + 1) * sub)
        ctx_rows = i * sub < split
        shift = jnp.where(top_is_ctx, sc_ref[0, 0], sx_ref[0, 0]) if ctx_rows else sx_ref[0, 0]
        scale = 1.0 + (jnp.where(top_is_ctx, ac_ref[0, 0], ax_ref[0, 0]) if ctx_rows else ax_ref[0, 0])
        h = (_rms(x_ref[0, rs, :], g_ref[0] * scale) + shift).astype(BF16)
        for name, ci in order:
            ref, off, width, fn = groups[name]
            cw = min(width, IN_CW)
            c0 = ci * cw
            p = _dot(h, w_ref[0, :, off + c0:off + c0 + cw]) + b_ref[0, :, off + c0:off + c0 + cw]
            if name == "dav":
                ref[0, c0:c0 + cw, rs] = p.T.astype(ref.dtype)
            else:
                ref[0, rs, c0:c0 + cw] = fn(p, c0, rs).astype(ref.dtype)


def _inproj(xs, mod4, g_pre, w_pad, b_pad, rope_tabs, lbf, oml, layer, ctx_len, tm):
    nb, n_all, d = xs.shape
    assert tm % IN_SUB == 0 and (ctx_len % tm) % (tm // IN_SUB) == 0
    kern = functools.partial(_inproj_kernel, tm=tm, ctx_len=ctx_len)
    lsel = lambda b, t: (layer, 0, 0)
    tile = lambda width: pl.BlockSpec((1, tm, width), lambda b, t: (b, t, 0))
    tab = pl.BlockSpec((tm, LANE), lambda b, t: (t, 0))
    bf = lambda width: jax.ShapeDtypeStruct((nb, n_all, width), BF16)
    f32 = lambda width: jax.ShapeDtypeStruct((nb, n_all, width), F32)
    out_shapes = [bf(SLAB)] * 4 + [f32(LANE)] + [bf(SLAB), f32(2 * SLAB), bf(HG_W), bf(HG_W)] + [bf(SLAB)] * 3
    out_specs = [tile(s.shape[-1]) for s in out_shapes]
    out_shapes[-1] = jax.ShapeDtypeStruct((nb, SLAB, n_all), BF16)
    out_specs[-1] = pl.BlockSpec((1, SLAB, tm), lambda b, t: (b, 0, t))
    return pl.pallas_call(
        kern,
        grid=(nb, n_all // tm),
        in_specs=[tile(d)] + _mod_specs(layer, nb, (0, 1), d) + [
            pl.BlockSpec((1, 1, d), lsel),
            pl.BlockSpec((1, d, NP_IN), lsel, pipeline_mode=pl.Buffered(1)),
            pl.BlockSpec((1, 1, NP_IN), lsel),
            tab, tab, tab,
            pl.BlockSpec((1, 1, 2 * SLAB), lsel),
            pl.BlockSpec((1, 1, 2 * SLAB), lsel)],
        out_specs=out_specs,
        out_shape=out_shapes,
        compiler_params=_cparams(("parallel", "arbitrary")),
        name="in_proj",
    )(xs, mod4, mod4, mod4, mod4, g_pre, w_pad, b_pad, *rope_tabs, lbf, oml)


def _ml_chunks(qs, ks, vs, i_rows, lf_rows, s_exts, ms, revs):
    n = len(qs)
    size = qs[0].shape[0]
    ii = lax.broadcasted_iota(jnp.int32, (size, size), 0)
    jj = lax.broadcasted_iota(jnp.int32, (size, size), 1)
    lane = lax.broadcasted_iota(jnp.int32, (size, LANE), 1)
    cums = {rev: jnp.where((ii >= jj) if rev else (ii <= jj), 1.0, 0.0).astype(F32) for rev in set(revs)}
    seens = {rev: (jj >= ii) if rev else (jj <= ii) for rev in set(revs)}

    b_cols, g_cols, g_rows = [], [], []
    for i in range(n):
        b_row = jnp.dot(jnp.broadcast_to(lf_rows[i], (8, size)), cums[revs[i]], precision=HIGHEST,
                        preferred_element_type=F32)[0:1]
        g_row = i_rows[i] - b_row
        b_cols.append(jnp.sum(jnp.where(ii == jj, b_row, 0.0), axis=1, keepdims=True))
        g_cols.append(jnp.sum(jnp.where(ii == jj, g_row, 0.0), axis=1, keepdims=True))
        g_rows.append(g_row)

    mts, dws, w_inters = [], [], []
    for i in range(n):
        dmat = jnp.where(seens[revs[i]], b_cols[i] + g_rows[i], NEG)
        a_col = b_cols[i] + ms[i]
        mt = jnp.maximum(a_col, jnp.max(dmat, axis=1, keepdims=True))
        mts.append(mt)
        w_inters.append(jnp.exp(a_col - mt))
        dws.append(jnp.exp(dmat - mt))

    ress = []
    for i in range(n):
        qk = _dot_nt(qs[i], ks[i]) * dws[i]
        ress.append(_dot(qk.astype(BF16), vs[i]) + w_inters[i] * _dot(qs[i], s_exts[i].astype(BF16)))

    houts, s_news, m_news = [], [], []
    for i in range(n):
        den = jnp.sum(jnp.where(lane == ML_HD, ress[i], 0.0), axis=1, keepdims=True)
        houts.append(ress[i] * (1.0 / jnp.maximum(jnp.abs(den), jnp.exp(-mts[i]))))
        last = 0 if revs[i] else size - 1
        m_new = mts[i][last:last + 1]
        b_last = b_cols[i][last:last + 1]
        decay = jnp.exp(b_last + ms[i] - m_new)
        wk = jnp.exp(b_last + g_cols[i] - m_new)
        s_news.append(decay * s_exts[i] + _dot_tn((wk * ks[i].astype(F32)).astype(BF16), vs[i]))
        m_news.append(m_new)
    return houts, s_news, m_news


def _mlstm_kernel(q_ref, k_ref, v_ref, o_ref, gt_ref, gain_ref, y_ref, hf_ref, hb_ref, *, nc_ctx, nc_all, size, nh):
    hd0 = pl.program_id(1) * nh
    lane = lax.broadcasted_iota(jnp.int32, (size, LANE), 1)
    nc_lat = nc_all - nc_ctx

    def step(c, carry):
        cb = jnp.where(c < nc_ctx, nc_ctx - 1 - c, nc_ctx + nc_lat - 1 - (c - nc_ctx))
        sf = pl.ds(pl.multiple_of(c * size, size), size)
        sb = pl.ds(pl.multiple_of(cb * size, size), size)
        qs, ks, vs, i_rows, lf_rows, revs = [], [], [], [], [], []
        for hh in range(nh):
            ln = slice(hh * LANE, (hh + 1) * LANE)
            for sl, cc, off in ((sf, c, 0), (sb, cb, HEADS)):
                v = v_ref[0, sl, ln]
                qs.append(q_ref[0, sl, ln])
                ks.append(k_ref[0, sl, ln])
                vs.append(jnp.where(lane == ML_HD, jnp.ones_like(v), v))
                i_rows.append(gt_ref[0, off + hd0 + hh, pl.ds(cc, 1), :])
                lf_rows.append(gt_ref[0, 2 * HEADS + off + hd0 + hh, pl.ds(cc, 1), :])
                revs.append(off > 0)
        houts, s_news, m_news = _ml_chunks(qs, ks, vs, i_rows, lf_rows, list(carry[0]), list(carry[1]), revs)
        for hh in range(nh):
            ln = slice(hh * LANE, (hh + 1) * LANE)
            hf_ref[sf, ln] = houts[2 * hh]
            hb_ref[sb, ln] = houts[2 * hh + 1]
        return tuple(s_news), tuple(m_news)

    s0 = jnp.zeros((LANE, LANE), F32)
    m0 = jnp.full((1, 1), M_INIT, F32)
    lax.fori_loop(0, nc_all, step, ((s0,) * (2 * nh), (m0,) * (2 * nh)))

    def finish(c, _):
        sl = pl.ds(pl.multiple_of(c * size, size), size)
        for hh in range(nh):
            ln = slice(hh * LANE, (hh + 1) * LANE)
            h = jnp.where(lane < ML_HD, hf_ref[sl, ln] + hb_ref[sl, ln], 0.0)
            y = h * lax.rsqrt(jnp.sum(h * h, axis=1, keepdims=True) * (1.0 / ML_HD) + EPS) * gain_ref[0, :, ln]
            y_ref[0, sl, ln] = (y * o_ref[0, sl, ln].astype(F32)).astype(BF16)
        return 0

    lax.fori_loop(0, nc_all, finish, 0)


def _mlstm(mlq, mlk, mlv, mlo, gates_t, gain, layer, ctx_len):
    nb, n_all, _ = mlq.shape
    size, nh = ML_CHUNK, ML_HPS
    nc_all = n_all // size
    width = nh * LANE
    ng = HEADS // nh
    kern = functools.partial(_mlstm_kernel, nc_ctx=ctx_len // size, nc_all=nc_all, size=size, nh=nh)
    slab = pl.BlockSpec((1, n_all, width), lambda b, h: (b, 0, h))
    return pl.pallas_call(
        kern,
        grid=(nb, ng),
        in_specs=[slab, slab, slab, slab,
                  pl.BlockSpec((1, 4 * HEADS, nc_all, size), lambda b, h: (b, 0, 0, 0)),
                  pl.BlockSpec((1, 1, width), lambda b, h: (layer * ng + h, 0, 0))],
        out_specs=slab,
        out_shape=jax.ShapeDtypeStruct((nb, n_all, SLAB), BF16),
        scratch_shapes=[pltpu.VMEM((n_all, width), F32), pltpu.VMEM((n_all, width), F32)],
        compiler_params=_cparams(("parallel", "arbitrary")),
        name="mlstm",
    )(mlq, mlk, mlv, mlo, gates_t, gain.reshape(-1, 1, width))


def _hg_tables(size):
    t = np.arange(size)
    tri = (t[None, :] <= t[:, None]).astype(np.float32)
    x = t[:, None] ^ t[None, :]
    lvl = np.where(x < HG_DIAG, 0, np.floor(np.log2(np.maximum(x, 1))).astype(np.int64) - 2)
    code = np.where(t[None, :] <= t[:, None], lvl, -1).astype(np.int32)
    j = np.arange(HG_DIAG * LANE) // LANE
    emat = (np.arange(size)[None, :] % HG_DIAG == j[:, None]).astype(np.float32)
    return (jnp.asarray(np.stack([tri, tri.T]), BF16), jnp.asarray(np.stack([code, code.T])),
            jnp.asarray(emat, BF16))


def _hg_chunks(qs, lfs, vs, sts, tris, codes, emat, revs):
    n = len(qs)
    size = qs[0].shape[0]
    bc2s, kks = [], []
    for i in range(n):
        lf2 = lfs[i]
        hi = lf2.astype(BF16)
        r1 = lf2 - hi.astype(F32)
        mid = r1.astype(BF16)
        lo = (r1 - mid.astype(F32)).astype(BF16)
        cs = _dot(tris[i], jnp.concatenate([hi, mid, lo], axis=1))
        bc2s.append((cs[:, 0:LANE] + cs[:, LANE:2 * LANE]) + cs[:, 2 * LANE:3 * LANE])
        kks.append(jnp.maximum(1.0 - jnp.exp2(lf2), 0.0))

    atts = [jnp.zeros((size, size), F32) for _ in range(n)]
    c, lvl = HG_DIAG, 1
    while c < size:
        blk = 2 * c
        for i in range(n):
            ridx = c if revs[i] else c - 1
            b3 = bc2s[i].reshape(size // blk, blk, LANE)
            ref = jnp.broadcast_to(b3[:, ridx:ridx + 1, :], b3.shape).reshape(size, LANE)
            e = jnp.exp2(-jnp.abs(bc2s[i] - ref))
            a = _dot_nt((qs[i] * e).astype(BF16), (kks[i] * e).astype(BF16))
            atts[i] = jnp.where(codes[i] == lvl, a, atts[i])
        c, lvl = blk, lvl + 1

    for i in range(n):
        w3 = (bc2s[i] - jnp.log2(kks[i])).reshape(size // HG_DIAG, HG_DIAG, LANE)
        ps = []
        for j in range(HG_DIAG):
            wj = jnp.broadcast_to(w3[:, j:j + 1, :], w3.shape).reshape(size, LANE)
            ps.append((qs[i] * jnp.exp2(jnp.minimum(bc2s[i] - wj, 0.0))).astype(BF16))
        atts[i] = jnp.where(codes[i] == 0, _dot(jnp.concatenate(ps, axis=1), emat), atts[i])

    outs, new_sts = [], []
    for i in range(n):
        last = 0 if revs[i] else size - 1
        bl = bc2s[i][last:last + 1]
        outs.append(_dot(atts[i].astype(BF16), vs[i])
                    + _dot_nt((qs[i] * jnp.exp2(bc2s[i])).astype(BF16), sts[i].astype(BF16)))
        new_sts.append(sts[i] * jnp.exp2(bl) + _dot_tn(vs[i], (kks[i] * jnp.exp2(bl - bc2s[i])).astype(BF16)))
    return outs, new_sts


def _hgrn_kernel(q_ref, lf0_ref, lf1_ref, v_ref, g_ref, gain_ref, tri_ref, code_ref, emat_ref,
                 y_ref, of_ref, ob_ref, *, nc_ctx, nc_all, size):
    nc_lat = nc_all - nc_ctx
    low = lax.broadcasted_iota(jnp.int32, (size, LANE), 1) < HG_DV

    def step(c, carry):
        cb = jnp.where(c < nc_ctx, nc_ctx - 1 - c, nc_ctx + nc_lat - 1 - (c - nc_ctx))
        sf = pl.ds(pl.multiple_of(c * size, size), size)
        sb = pl.ds(pl.multiple_of(cb * size, size), size)
        qs, lfs, vs, tris, codes, revs = [], [], [], [], [], []
        for hh in range(2):
            ln = slice(hh * LANE, (hh + 1) * LANE)
            qs += [q_ref[0, sf, ln].astype(F32), q_ref[0, sb, ln].astype(F32)]
            lfs += [lf0_ref[0, sf, ln], lf1_ref[0, sb, ln]]
            vs += [v_ref[0, sf, :], v_ref[0, sb, :]]
            tris += [tri_ref[0], tri_ref[1]]
            codes += [code_ref[0], code_ref[1]]
            revs += [False, True]
        outs, sts = _hg_chunks(qs, lfs, vs, list(carry), tris, codes, emat_ref[...], revs)
        of_ref[sf, :] = jnp.where(low, outs[0], outs[2])
        ob_ref[sb, :] = jnp.where(low, outs[1], outs[3])
        return tuple(sts)

    st0 = jnp.zeros((LANE, LANE), F32)
    lax.fori_loop(0, nc_all, step, (st0,) * 4)

    def finish(c, _):
        sl = pl.ds(pl.multiple_of(c * size, size), size)
        o = of_ref[sl, :] + ob_ref[sl, :]
        sq = o * o
        ms0 = jnp.sum(jnp.where(low, sq, 0.0), axis=1, keepdims=True) * (1.0 / HG_DV)
        ms1 = jnp.sum(jnp.where(low, 0.0, sq), axis=1, keepdims=True) * (1.0 / HG_DV)
        y = o * lax.rsqrt(jnp.where(low, ms0, ms1) + EPS) * gain_ref[0]
        y_ref[0, sl, :] = (y * g_ref[0, sl, :].astype(F32)).astype(BF16)
        return 0

    lax.fori_loop(0, nc_all, finish, 0)


def _hgrn(hgq, hgf, hgv, hgg, gain, layer, ctx_len):
    nb, n_all, _ = hgq.shape
    size = HG_CHUNK
    nc_all = n_all // size
    ng = HEADS // 2
    kern = functools.partial(_hgrn_kernel, nc_ctx=ctx_len // size, nc_all=nc_all, size=size)
    wide = pl.BlockSpec((1, n_all, 2 * LANE), lambda b, h: (b, 0, h))
    slab = pl.BlockSpec((1, n_all, LANE), lambda b, h: (b, 0, h))
    tri, code, emat = _hg_tables(size)
    return pl.pallas_call(
        kern,
        grid=(nb, ng),
        in_specs=[wide, wide,
                  pl.BlockSpec((1, n_all, 2 * LANE), lambda b, h: (b, 0, ng + h)),
                  slab, slab,
                  pl.BlockSpec((1, 1, LANE), lambda b, h: (layer * ng + h, 0, 0)),
                  pl.BlockSpec(tri.shape, lambda b, h: (0, 0, 0)),
                  pl.BlockSpec(code.shape, lambda b, h: (0, 0, 0)),
                  pl.BlockSpec(emat.shape, lambda b, h: (0, 0))],
        out_specs=slab,
        out_shape=jax.ShapeDtypeStruct((nb, n_all, HG_W), BF16),
        scratch_shapes=[pltpu.VMEM((n_all, LANE), F32), pltpu.VMEM((n_all, LANE), F32)],
        compiler_params=_cparams(("parallel", "arbitrary")),
        name="hgrn2",
    )(hgq, hgf, hgf, hgv, hgg, gain, tri, code, emat)


def _attn_kernel(q_ref, k_ref, vt_ref, lam_ref, gain_ref, y_ref, s0_ref, s1_ref, m0_ref, m1_ref,
                 *, tq, kc, ctx_len, lam_init):
    n_all = k_ref.shape[1]
    n_tiles = (n_all - ctx_len) // tq
    s_refs, m_refs = (s0_ref, s1_ref), (m0_ref, m1_ref)
    lv = lam_ref[0]
    lam = (jnp.exp(jnp.sum(lv[0:1] * lv[1:2], axis=1, keepdims=True))
           - jnp.exp(jnp.sum(lv[2:3] * lv[3:4], axis=1, keepdims=True)) + lam_init)
    lane = lax.broadcasted_iota(jnp.int32, (tq, LANE), 1)
    vrow = lax.broadcasted_iota(jnp.int32, (LANE, kc), 0)

    def rows(t):
        return pl.ds(pl.multiple_of(ctx_len + t * tq, tq), tq)

    def load_qq(sl):
        q = q_ref[0, sl, :]
        zero = jnp.zeros_like(q)
        return jnp.concatenate([jnp.where(lane < DA_HD, q, zero), jnp.where(lane < DA_HD, zero, q)], axis=0)

    def stage_a_chunk(qq, slot, c, m):
        s = _dot_nt(k_ref[0, c * kc:(c + 1) * kc, :], qq)
        s_refs[slot][c * kc:(c + 1) * kc, :] = s
        cm = jnp.max(s, axis=0, keepdims=True)
        return cm if m is None else jnp.maximum(m, cm)

    def stage_b_chunk(slot, c, m, acc):
        p = jnp.exp2(s_refs[slot][c * kc:(c + 1) * kc, :] - m).astype(BF16)
        vt = vt_ref[0, :, c * kc:(c + 1) * kc]
        pv = _dot(jnp.where(vrow == DA_VD, jnp.ones_like(vt), vt), p)
        return pv if acc is None else acc + pv

    def finish(acc, sl):
        r0 = 1.0 / acc[DA_VD:DA_VD + 1, 0:tq]
        r1 = lam / acc[DA_VD:DA_VD + 1, tq:2 * tq]
        o = (acc[:, 0:tq] * r0 - acc[:, tq:2 * tq] * r1).T
        o = jnp.where(lane < DA_VD, o, 0.0)
        y = o * lax.rsqrt(jnp.sum(o * o, axis=1, keepdims=True) * (1.0 / DA_VD) + EPS) * gain_ref[0]
        y_ref[0, sl, :] = (y * (1.0 - lam_init)).astype(BF16)

    def stages(sl_a, sl_b, slot_a, nk_a, nk_b):
        slot_b = 1 - slot_a
        if sl_a is not None:
            qq = load_qq(sl_a)
        if sl_b is not None:
            m_b = m_refs[slot_b][0:1, :]
        m = acc = None
        for c in range(max(nk_a, nk_b) // kc):
            if sl_a is not None and c < nk_a // kc:
                m = stage_a_chunk(qq, slot_a, c, m)
            if sl_b is not None and c < nk_b // kc:
                acc = stage_b_chunk(slot_b, c, m_b, acc)
        if sl_a is not None:
            m_refs[slot_a][...] = jnp.broadcast_to(m, (8, 2 * tq))
        if sl_b is not None:
            finish(acc, sl_b)

    ctx_rows = pl.ds(0, tq)
    stages(ctx_rows, None, 1, ctx_len, 0)
    stages(rows(0), ctx_rows, 0, n_all, ctx_len)

    def body(i, _):
        t = 1 + 2 * i
        stages(rows(t), rows(t - 1), 1, n_all, n_all)
        stages(rows(t + 1), rows(t), 0, n_all, n_all)
        return 0

    lax.fori_loop(0, (n_tiles - 2) // 2, body, 0)
    stages(rows(n_tiles - 1), rows(n_tiles - 2), 1, n_all, n_all)
    stages(None, rows(n_tiles - 1), 0, 0, n_all)


def _attn(daq, dak, dav_t, lam_pad, gain, layer, ctx_len, lam_init):
    nb, n_all, _ = daq.shape
    tq, kc = ATT_TQ, ATT_KC
    n_tiles = (n_all - ctx_len) // tq
    assert n_tiles >= 2 and n_tiles % 2 == 0 and ctx_len == tq and ctx_len % kc == 0 and n_all % kc == 0
    kern = functools.partial(_attn_kernel, tq=tq, kc=kc, ctx_len=ctx_len, lam_init=lam_init)
    slab = pl.BlockSpec((1, n_all, LANE), lambda b, h: (b, 0, h))
    sbuf, mbuf = pltpu.VMEM((n_all, 2 * tq), F32), pltpu.VMEM((8, 2 * tq), F32)
    return pl.pallas_call(
        kern,
        grid=(nb, HEADS),
        in_specs=[slab, slab,
                  pl.BlockSpec((1, LANE, n_all), lambda b, h: (b, h, 0)),
                  pl.BlockSpec((1, 4, LANE), lambda b, h: (layer, 0, 0)),
                  pl.BlockSpec((1, 1, LANE), lambda b, h: (layer, 0, 0))],
        out_specs=slab,
        out_shape=jax.ShapeDtypeStruct((nb, n_all, SLAB), BF16),
        scratch_shapes=[sbuf, sbuf, mbuf, mbuf],
        compiler_params=_cparams(("parallel", "arbitrary")),
        name="diff_attn",
    )(daq, dak, dav_t, lam_pad, gain)


def _resident(shape, index_map):
    return pl.BlockSpec(shape, index_map, pipeline_mode=pl.Buffered(1))


def _mix_ffn_kernel(yml_ref, yhg_ref, yda_ref, x_ref, g1x_ref, g1c_ref, a2x_ref, a2c_ref, s2x_ref, s2c_ref,
                    g2x_ref, g2c_ref, wo_ref, gpm_ref, gpf_ref, wg_ref, wu_ref, wd_ref, gqf_ref, o_ref, *, tm, ctx_len, t0):
    row0 = (pl.program_id(1) + t0) * tm
    split = min(ctx_len % tm if ctx_len % tm else tm, tm)
    top_is_ctx = row0 < ctx_len

    def mod(c_ref, x_ref_, lo):
        return jnp.where(top_is_ctx, c_ref[0, 0], x_ref_[0, 0]) if lo < split else x_ref_[0, 0]

    nsub = MIX_SUB
    sub = tm // nsub
    x1s, h2s = [], []
    for i in range(nsub):
        lo, rs = i * sub, slice(i * sub, (i + 1) * sub)
        y = (_dot(yml_ref[0, rs, :], wo_ref[0, 0:SLAB, :]) + _dot(yhg_ref[0, rs, :], wo_ref[0, SLAB:SLAB + HG_W, :])
             + _dot(yda_ref[0, rs, :], wo_ref[0, SLAB + HG_W:2 * SLAB + HG_W, :]))
        x1 = x_ref[0, rs, :] + _rms(y, mod(g1c_ref, g1x_ref, lo) * gpm_ref[0])
        h2s.append((_rms(x1, gpf_ref[0] * (1.0 + mod(a2c_ref, a2x_ref, lo))) + mod(s2c_ref, s2x_ref, lo)).astype(BF16))
        x1s.append(x1)
    for i in range(nsub):
        lo, rs = i * sub, slice(i * sub, (i + 1) * sub)
        act = _silu(_dot(h2s[i], wg_ref[0])) * _dot(h2s[i], wu_ref[0])
        f = _dot(act.astype(BF16), wd_ref[0])
        o_ref[0, rs, :] = x1s[i] + _rms(f, mod(g2c_ref, g2x_ref, lo) * gqf_ref[0])


def _mix_ffn(yml, yhg, yda, xs, mod4, w_out_pad, g_post_mix, g_pre_ffn, wg, wu, wd, g_post_ffn, layer, ctx_len, tm, t0):
    nb, n_all, d = xs.shape
    dff = wg.shape[-1]
    assert tm % MIX_SUB == 0 and (ctx_len % tm) % (tm // MIX_SUB) == 0
    kern = functools.partial(_mix_ffn_kernel, tm=tm, ctx_len=ctx_len, t0=t0)
    lsel = lambda b, t: (layer, 0, 0)
    tile = lambda width: pl.BlockSpec((1, tm, width), lambda b, t: (b, t + t0, 0))
    return pl.pallas_call(
        kern,
        grid=(nb, n_all // tm - t0),
        in_specs=[tile(SLAB), tile(HG_W), tile(SLAB), tile(d)] + _mod_specs(layer, nb, (2, 4, 3, 5), d) + [
            _resident((1, 2 * SLAB + HG_W, d), lsel),
            pl.BlockSpec((1, 1, d), lsel),
            pl.BlockSpec((1, 1, d), lsel),
            _resident((1, d, dff), lsel),
            _resident((1, d, dff), lsel),
            _resident((1, dff, d), lsel),
            pl.BlockSpec((1, 1, d), lsel)],
        out_specs=pl.BlockSpec((1, tm, d), lambda b, t: (b, t, 0)),
        out_shape=jax.ShapeDtypeStruct((nb, n_all - t0 * tm, d), F32),
        compiler_params=_cparams(("parallel", "arbitrary")),
        name="mix_ffn",
    )(yml, yhg, yda, xs, *([mod4] * 8), w_out_pad, g_post_mix, g_pre_ffn, wg, wu, wd, g_post_ffn)


def _pad_heads(w, hd):
    lead = w.shape[:-1]
    w = w.reshape(*lead, HEADS, hd)
    w = jnp.pad(w, [(0, 0)] * len(lead) + [(0, 0), (0, LANE - hd)])
    return w.reshape(*lead, SLAB)


def _pad_in_cols(w, extra_f=None):
    sizes = (4 * ML_HD,) * 4 + (2 * HEADS, 2 * HEADS, HEADS * HG_DK, 2 * HEADS * HG_DK, HEADS * HG_DV, HEADS * HG_DV,
             2 * HEADS * DA_HD, 2 * HEADS * DA_HD, HEADS * DA_VD)
    parts = []
    off = 0
    for s in sizes:
        parts.append(w[..., off:off + s])
        off += s
    gi, gf = parts[4], parts[5]
    if extra_f is not None:
        gf = gf + extra_f
    gate = jnp.concatenate([gi, gf], axis=-1)
    gate = jnp.pad(gate, [(0, 0)] * (gate.ndim - 1) + [(0, LANE - 4 * HEADS)])
    return jnp.concatenate([
        _pad_heads(parts[0], ML_HD), _pad_heads(parts[1], ML_HD), _pad_heads(parts[2], ML_HD), _pad_heads(parts[3], ML_HD),
        gate, parts[6], parts[7], parts[8], parts[9],
        _pad_heads(parts[10], DA_VD), _pad_heads(parts[11], DA_VD), _pad_heads(parts[12], DA_VD)], axis=-1)


def _pad_out_rows(w_out):
    depth, _, d = w_out.shape
    wt = jnp.swapaxes(w_out, 1, 2)
    ml, hg, da = wt[..., :4 * ML_HD], wt[..., 4 * ML_HD:4 * ML_HD + HEADS * HG_DV], wt[..., 4 * ML_HD + HEADS * HG_DV:]
    wp = jnp.concatenate([_pad_heads(ml, ML_HD), hg, _pad_heads(da, DA_VD)], axis=-1)
    return jnp.swapaxes(wp, 1, 2)


def _rope_tables(n_lat, ctx_len):
    rows = n_lat // GRID_W
    row = jnp.repeat(jnp.arange(rows), GRID_W).astype(F32)
    col = jnp.tile(jnp.arange(GRID_W), rows).astype(F32)
    half = DA_HD // 2
    inv = ROPE_BASE ** (-jnp.arange(0, half, 2, dtype=F32) / half)
    ang_r, ang_c = row[:, None] * inv, col[:, None] * inv
    zero = jnp.zeros_like(ang_r)

    def one_map(fr, fc, sel):
        r, c = fr(ang_r), fc(ang_c)
        if sel == "cos":
            return jnp.concatenate([r, r, c, c], axis=1)
        if sel == "a":
            return jnp.concatenate([-r, zero, -c, zero], axis=1)
        return jnp.concatenate([zero, r, zero, c], axis=1)

    tabs = []
    for sel, fn in (("cos", jnp.cos), ("a", jnp.sin), ("b", jnp.sin)):
        m = one_map(fn, fn, sel)
        lat = jnp.pad(jnp.concatenate([m, m], axis=1), ((0, 0), (0, LANE - 2 * DA_HD)))
        fill = 1.0 if sel == "cos" else 0.0
        ctx_rows = jnp.pad(jnp.full((ctx_len, 2 * DA_HD), fill, F32), ((0, 0), (0, LANE - 2 * DA_HD)))
        tabs.append(jnp.concatenate([ctx_rows, lat], axis=0))
    return tabs


def kernel(x, c, ctx, c_ctx, w_ada, b_ada, g_pre_mix, g_post_mix, g_pre_ffn, g_post_ffn, w_in, b_in, w_out,
           ml_f_bias, ml_norm, hg_lb, hg_norm, da_lambda, da_norm, w_ffn_gate, w_ffn_up, w_ffn_down):
    nb, n_lat, d = x.shape
    ctx_len = ctx.shape[1]
    depth = w_ada.shape[0]
    n_all = ctx_len + n_lat

    w_pad = _pad_in_cols(w_in).astype(BF16)
    b_pad = _pad_in_cols(b_in, extra_f=ml_f_bias)[:, None, :]
    w_out_pad = _pad_out_rows(w_out).astype(BF16)
    wg, wu, wd = w_ffn_gate.astype(BF16), w_ffn_up.astype(BF16), w_ffn_down.astype(BF16)
    ml_gain = _pad_heads(ml_norm, ML_HD).reshape(depth * HEADS, 1, LANE)
    hg_gain = hg_norm.reshape(-1, 1, LANE)
    da_gain = jnp.pad(da_norm, ((0, 0), (0, LANE - DA_VD)))[:, None, :]
    lam_pad = jnp.pad(da_lambda.astype(F32), ((0, 0), (0, 0), (0, LANE - DA_HD)))
    sm = jax.nn.softmax(hg_lb.astype(F32), axis=0)
    lbs = jnp.cumsum(sm, axis=0) - sm[0:1]
    lbf = jnp.maximum(lbs, LB_FLOOR)[:, None, :]
    oml = (1.0 - lbs)[:, None, :]
    rope_tabs = _rope_tables(n_lat, ctx_len)
    r3 = lambda g: g[:, None, :]

    mp = -(-(nb + 1) // 8) * 8
    cc = jnp.concatenate([c, c_ctx[None, :], jnp.zeros((mp - nb - 1, d), F32)], axis=0)
    mod4 = _ada(cc, w_ada, b_ada).reshape(depth, mp, 1, 6 * d)

    xs = jnp.concatenate([ctx, x], axis=1)
    for l in range(depth):
        lam_init = 0.8 - 0.6 * math.exp(-0.3 * l)
        (mlq, mlk, mlv, mlo, gates, hgq, hgf, hgv, hgg, daq, dak, dav_t) = _inproj(
            xs, mod4, r3(g_pre_mix), w_pad, b_pad, rope_tabs, lbf, oml, l, ctx_len, IN_TM if n_all % IN_TM == 0 else 256)
        gates_t = jnp.swapaxes(gates[:, :, :4 * HEADS], 1, 2).reshape(nb, 4 * HEADS, n_all // ML_CHUNK, ML_CHUNK)
        yml = _mlstm(mlq, mlk, mlv, mlo, gates_t, ml_gain, l, ctx_len)
        yhg = _hgrn(hgq, hgf, hgv, hgg, hg_gain, l, ctx_len)
        yda = _attn(daq, dak, dav_t, lam_pad, da_gain, l, ctx_len, lam_init)
        last = l == depth - 1
        tm = MIX_TM_LAST if last else (MIX_TM if n_all % MIX_TM == 0 else MIX_TM_LAST)
        xs = _mix_ffn(yml, yhg, yda, xs, mod4, w_out_pad, r3(g_post_mix), r3(g_pre_ffn), wg, wu, wd,
                      r3(g_post_ffn), l, ctx_len, tm, ctx_len // tm if last else 0)
    return xs
```

```python
import functools
import math

import numpy as np
import jax
import jax.numpy as jnp
from jax import lax
from jax.experimental import pallas as pl
from jax.experimental.pallas import tpu as pltpu

F32 = jnp.float32
BF16 = jnp.bfloat16
HIGHEST = lax.Precision.HIGHEST

LANE = 128
VMEM_LIMIT = 52 * 1024 * 1024

EPS = 1e-6
NEG = -1e30
LB_FLOOR = 1e-30
M_INIT = -1e30
GRID_W = 64
ROPE_BASE = 10000.0

HEADS = 4
ML_HD = 96
HG_DV = 64
HG_DK = 128
DA_HD = 48
DA_VD = 2 * DA_HD
SLAB = HEADS * LANE
HG_W = HEADS * HG_DV

ML_CHUNK = 256
ML_HPS = 2
HG_CHUNK = 128
HG_DIAG = 8
ATT_TQ = 256
IN_CW = 256
IN_TM = 768
IN_SUB = 3
MIX_TM = 256
MIX_SUB = 2
MIX_TM_LAST = 256
ATT_KC = 256
LOG2E = 1.4426950408889634

OFF_MLQ, OFF_MLK, OFF_MLV, OFF_MLO = 0, SLAB, 2 * SLAB, 3 * SLAB
OFF_GATE = 4 * SLAB
OFF_HGQ = OFF_GATE + LANE
OFF_HGF = OFF_HGQ + SLAB
OFF_HGV = OFF_HGF + 2 * SLAB
OFF_HGG = OFF_HGV + HG_W
OFF_DAQ = OFF_HGG + HG_W
OFF_DAK = OFF_DAQ + SLAB
OFF_DAV = OFF_DAK + SLAB
NP_IN = OFF_DAV + SLAB


def _cparams(sem):
    return pltpu.CompilerParams(dimension_semantics=sem, vmem_limit_bytes=VMEM_LIMIT)


def _silu(x):
    return x * jax.nn.sigmoid(x)


def _log_sigmoid(z):
    return jnp.minimum(z, 0.0) - jnp.log1p(jnp.exp(-jnp.abs(z)))


def _neg_abs(x):
    bits = lax.bitcast_convert_type(x, jnp.uint32) | jnp.uint32(0x80000000)
    return lax.bitcast_convert_type(bits, F32)


def _rms(x, g):
    return x * lax.rsqrt(jnp.mean(x * x, axis=-1, keepdims=True) + EPS) * g


def _dot(a, b):
    return jnp.dot(a, b, preferred_element_type=F32)


def _dot_nt(a, b):
    return lax.dot_general(a, b, (((1,), (1,)), ((), ())), preferred_element_type=F32)


def _dot_tn(a, b):
    return lax.dot_general(a, b, (((0,), (0,)), ((), ())), preferred_element_type=F32)


def _ada_kernel(s_ref, w_ref, b_ref, o_ref):
    s = _silu(s_ref[...])
    o_ref[0] = jnp.dot(s, w_ref[0], precision=HIGHEST, preferred_element_type=F32) + b_ref[0]


def _ada(cc, w_ada, b_ada):
    depth, d, d6 = w_ada.shape
    mp = cc.shape[0]
    tn = 1024
    return pl.pallas_call(
        _ada_kernel,
        grid=(depth, d6 // tn),
        in_specs=[pl.BlockSpec((mp, d), lambda l, j: (0, 0)),
                  pl.BlockSpec((1, d, tn), lambda l, j: (l, 0, j)),
                  pl.BlockSpec((1, 1, tn), lambda l, j: (l, 0, j))],
        out_specs=pl.BlockSpec((1, mp, tn), lambda l, j: (l, 0, j)),
        out_shape=jax.ShapeDtypeStruct((depth, mp, d6), F32),
        compiler_params=_cparams(("arbitrary", "arbitrary")),
        name="ada_mod",
    )(cc, w_ada, b_ada.reshape(depth, 1, d6))


def _mod_specs(layer, nb, idxs, d):
    specs = []
    for j in idxs:
        specs.append(pl.BlockSpec((1, 1, 1, d), lambda b, t, j=j: (layer, b, 0, j)))
        specs.append(pl.BlockSpec((1, 1, 1, d), lambda b, t, j=j: (layer, nb, 0, j)))
    return specs


def _inproj_kernel(x_ref, sx_ref, sc_ref, ax_ref, ac_ref, g_ref, w_ref, b_ref,
                   cos_ref, sna_ref, snb_ref, lbf_ref, oml_ref,
                   mlq_ref, mlk_ref, mlv_ref, mlo_ref, gate_ref,
                   hgq_ref, hgf_ref, hgv_ref, hgg_ref, daq_ref, dak_ref, dav_ref,
                   *, tm, ctx_len):
    split = min(ctx_len % tm if ctx_len % tm else tm, tm)
    top_is_ctx = pl.program_id(1) * tm < ctx_len
    sub = tm // IN_SUB
    half = DA_HD // 4
    lane = lax.broadcasted_iota(jnp.int32, (sub, LANE), 1)

    def rope(p, rs):
        cos, sna, snb = cos_ref[rs, :], sna_ref[rs, :], snb_ref[rs, :]
        outs = []
        for i in range(p.shape[1] // LANE):
            xh = p[:, i * LANE:(i + 1) * LANE]
            outs.append(xh * cos + pltpu.roll(xh, LANE - half, 1) * sna + pltpu.roll(xh, half, 1) * snb)
        return jnp.concatenate(outs, axis=1)

    def log2_f(p, c0, rs):
        w = p.shape[1]
        return jnp.log2(lbf_ref[0, :, c0:c0 + w] + oml_ref[0, :, c0:c0 + w] * jax.nn.sigmoid(p))

    def gates(p, c0, rs):
        return jnp.where((lane >= 2 * HEADS) & (lane < 4 * HEADS), _log_sigmoid(p), p)

    plain = lambda p, c0, rs: p
    groups = {
        "mlq": (mlq_ref, OFF_MLQ, SLAB, plain),
        "mlk": (mlk_ref, OFF_MLK, SLAB, lambda p, c0, rs: p * (ML_HD ** -0.5)),
        "mlv": (mlv_ref, OFF_MLV, SLAB, plain),
        "mlo": (mlo_ref, OFF_MLO, SLAB, lambda p, c0, rs: jax.nn.sigmoid(p)),
        "gate": (gate_ref, OFF_GATE, LANE, gates),
        "hgq": (hgq_ref, OFF_HGQ, SLAB, lambda p, c0, rs: _silu(p)),
        "hgf": (hgf_ref, OFF_HGF, 2 * SLAB, log2_f),
        "hgv": (hgv_ref, OFF_HGV, HG_W, plain),
        "hgg": (hgg_ref, OFF_HGG, HG_W, lambda p, c0, rs: _silu(p)),
        "daq": (daq_ref, OFF_DAQ, SLAB, lambda p, c0, rs: rope(p, rs) * (LOG2E * DA_HD ** -0.5)),
        "dak": (dak_ref, OFF_DAK, SLAB, lambda p, c0, rs: rope(p, rs)),
        "dav": (dav_ref, OFF_DAV, SLAB, plain),
    }
    order = [("hgf", 0), ("mlq", 0), ("hgf", 1), ("mlq", 1), ("hgf", 2), ("mlk", 0), ("hgf", 3), ("mlk", 1),
             ("mlo", 0), ("mlv", 0), ("mlo", 1), ("mlv", 1), ("hgq", 0), ("dav", 0), ("hgq", 1), ("dav", 1),
             ("daq", 0), ("hgv", 0), ("daq", 1), ("gate", 0), ("dak", 0), ("hgg", 0), ("dak", 1)]
    for i in range(IN_SUB):
        rs = slice(i * sub, (i + 1) * sub)
        ctx_rows = i * sub < split
        shift = jnp.where(top_is_ctx, sc_ref[0, 0], sx_ref[0, 0]) if ctx_rows else sx_ref[0, 0]
        scale = 1.0 + (jnp.where(top_is_ctx, ac_ref[0, 0], ax_ref[0, 0]) if ctx_rows else ax_ref[0, 0])
        h = (_rms(x_ref[0, rs, :], g_ref[0] * scale) + shift).astype(BF16)
        for name, ci in order:
            ref, off, width, fn = groups[name]
            cw = min(width, IN_CW)
            c0 = ci * cw
            p = _dot(h, w_ref[0, :, off + c0:off + c0 + cw]) + b_ref[0, :, off + c0:off + c0 + cw]
            if name == "dav":
                ref[0, c0:c0 + cw, rs] = p.T.astype(ref.dtype)
            else:
                ref[0, rs, c0:c0 + cw] = fn(p, c0, rs).astype(ref.dtype)


def _inproj(xs, mod4, g_pre, w_pad, b_pad, rope_tabs, lbf, oml, layer, ctx_len, tm):
    nb, n_all, d = xs.shape
    assert tm % IN_SUB == 0 and (ctx_len % tm) % (tm // IN_SUB) == 0
    kern = functools.partial(_inproj_kernel, tm=tm, ctx_len=ctx_len)
    lsel = lambda b, t: (layer, 0, 0)
    tile = lambda width: pl.BlockSpec((1, tm, width), lambda b, t: (b, t, 0))
    tab = pl.BlockSpec((tm, LANE), lambda b, t: (t, 0))
    bf = lambda width: jax.ShapeDtypeStruct((nb, n_all, width), BF16)
    f32 = lambda width: jax.ShapeDtypeStruct((nb, n_all, width), F32)
    out_shapes = [bf(SLAB)] * 4 + [f32(LANE)] + [bf(SLAB), f32(2 * SLAB), bf(HG_W), bf(HG_W)] + [bf(SLAB)] * 3
    out_specs = [tile(s.shape[-1]) for s in out_shapes]
    out_shapes[-1] = jax.ShapeDtypeStruct((nb, SLAB, n_all), BF16)
    out_specs[-1] = pl.BlockSpec((1, SLAB, tm), lambda b, t: (b, 0, t))
    return pl.pallas_call(
        kern,
        grid=(nb, n_all // tm),
        in_specs=[tile(d)] + _mod_specs(layer, nb, (0, 1), d) + [
            pl.BlockSpec((1, 1, d), lsel),
            pl.BlockSpec((1, d, NP_IN), lsel, pipeline_mode=pl.Buffered(1)),
            pl.BlockSpec((1, 1, NP_IN), lsel),
            tab, tab, tab,
            pl.BlockSpec((1, 1, 2 * SLAB), lsel),
            pl.BlockSpec((1, 1, 2 * SLAB), lsel)],
        out_specs=out_specs,
        out_shape=out_shapes,
        compiler_params=_cparams(("parallel", "arbitrary")),
        name="in_proj",
    )(xs, mod4, mod4, mod4, mod4, g_pre, w_pad, b_pad, *rope_tabs, lbf, oml)


def _ml_chunks(qs, ks, vs, i_rows, lf_rows, s_exts, ms, revs):
    n = len(qs)
    size = qs[0].shape[0]
    ii = lax.broadcasted_iota(jnp.int32, (size, size), 0)
    jj = lax.broadcasted_iota(jnp.int32, (size, size), 1)
    lane = lax.broadcasted_iota(jnp.int32, (size, LANE), 1)
    cums = {rev: jnp.where((ii >= jj) if rev else (ii <= jj), 1.0, 0.0).astype(BF16) for rev in set(revs)}
    seens = {rev: (jj >= ii) if rev else (jj <= ii) for rev in set(revs)}
    row16 = lax.broadcasted_iota(jnp.int32, (16, size), 0)

    b_cols, g_cols, g_rows = [], [], []
    for i in range(n):
        lf2 = lf_rows[i] * LOG2E
        hi = lf2.astype(BF16).astype(F32)
        mid = (lf2 - hi).astype(BF16).astype(F32)
        lo = (lf2 - hi) - mid
        parts = jnp.where(row16 == 0, hi, jnp.where(row16 == 1, mid, jnp.where(row16 == 2, lo, 0.0))).astype(BF16)
        cs = _dot(parts, cums[revs[i]])
        b_row = (cs[0:1] + cs[1:2]) + cs[2:3]
        g_row = i_rows[i] * LOG2E - b_row
        b_cols.append(jnp.sum(jnp.where(ii == jj, b_row, 0.0), axis=1, keepdims=True))
        g_cols.append(jnp.sum(jnp.where(ii == jj, g_row, 0.0), axis=1, keepdims=True))
        g_rows.append(g_row)

    mts, dws, w_inters = [], [], []
    for i in range(n):
        dmat = jnp.where(seens[revs[i]], b_cols[i] + g_rows[i], NEG)
        a_col = b_cols[i] + ms[i]
        mt = jnp.maximum(a_col, jnp.max(dmat, axis=1, keepdims=True))
        mts.append(mt)
        w_inters.append(jnp.exp2(a_col - mt))
        dws.append(jnp.exp2(dmat - mt))

    ress = []
    for i in range(n):
        qk = _dot_nt(qs[i], ks[i]) * dws[i]
        ress.append(_dot(qk.astype(BF16), vs[i]) + w_inters[i] * _dot(qs[i], s_exts[i].astype(BF16)))

    houts, s_news, m_news = [], [], []
    for i in range(n):
        den = jnp.sum(jnp.where(lane == ML_HD, ress[i], 0.0), axis=1, keepdims=True)
        houts.append(ress[i] * (1.0 / jnp.maximum(jnp.abs(den), jnp.exp2(-mts[i]))))
        last = 0 if revs[i] else size - 1
        m_new = mts[i][last:last + 1]
        b_last = b_cols[i][last:last + 1]
        decay = jnp.exp2(b_last + ms[i] - m_new)
        wk = jnp.exp2(b_last + g_cols[i] - m_new)
        s_news.append(decay * s_exts[i] + _dot_tn((wk * ks[i].astype(F32)).astype(BF16), vs[i]))
        m_news.append(m_new)
    return houts, s_news, m_news


def _mlstm_kernel(q_ref, k_ref, v_ref, o_ref, gt_ref, gain_ref, y_ref, hf_ref, hb_ref, *, nc_ctx, nc_all, size, nh):
    hd0 = pl.program_id(1) * nh
    lane = lax.broadcasted_iota(jnp.int32, (size, LANE), 1)
    nc_lat = nc_all - nc_ctx

    def step(c, carry):
        cb = jnp.where(c < nc_ctx, nc_ctx - 1 - c, nc_ctx + nc_lat - 1 - (c - nc_ctx))
        sf = pl.ds(pl.multiple_of(c * size, size), size)
        sb = pl.ds(pl.multiple_of(cb * size, size), size)
        qs, ks, vs, i_rows, lf_rows, revs = [], [], [], [], [], []
        for hh in range(nh):
            ln = slice(hh * LANE, (hh + 1) * LANE)
            for sl, cc, off in ((sf, c, 0), (sb, cb, HEADS)):
                v = v_ref[0, sl, ln]
                qs.append(q_ref[0, sl, ln])
                ks.append(k_ref[0, sl, ln])
                vs.append(jnp.where(lane == ML_HD, jnp.ones_like(v), v))
                i_rows.append(gt_ref[0, off + hd0 + hh, pl.ds(cc, 1), :])
                lf_rows.append(gt_ref[0, 2 * HEADS + off + hd0 + hh, pl.ds(cc, 1), :])
                revs.append(off > 0)
        houts, s_news, m_news = _ml_chunks(qs, ks, vs, i_rows, lf_rows, list(carry[0]), list(carry[1]), revs)
        for hh in range(nh):
            ln = slice(hh * LANE, (hh + 1) * LANE)
            hf_ref[sf, ln] = houts[2 * hh]
            hb_ref[sb, ln] = houts[2 * hh + 1]
        return tuple(s_news), tuple(m_news)

    s0 = jnp.zeros((LANE, LANE), F32)
    m0 = jnp.full((1, 1), M_INIT, F32)
    lax.fori_loop(0, nc_all, step, ((s0,) * (2 * nh), (m0,) * (2 * nh)))

    def finish(c, _):
        sl = pl.ds(pl.multiple_of(c * size, size), size)
        for hh in range(nh):
            ln = slice(hh * LANE, (hh + 1) * LANE)
            h = jnp.where(lane < ML_HD, hf_ref[sl, ln] + hb_ref[sl, ln], 0.0)
            y = h * lax.rsqrt(jnp.sum(h * h, axis=1, keepdims=True) * (1.0 / ML_HD) + EPS) * gain_ref[0, :, ln]
            y_ref[0, sl, ln] = (y * o_ref[0, sl, ln].astype(F32)).astype(BF16)
        return 0

    lax.fori_loop(0, nc_all, finish, 0)


def _mlstm(mlq, mlk, mlv, mlo, gates_t, gain, layer, ctx_len):
    nb, n_all, _ = mlq.shape
    size, nh = ML_CHUNK, ML_HPS
    nc_all = n_all // size
    width = nh * LANE
    ng = HEADS // nh
    kern = functools.partial(_mlstm_kernel, nc_ctx=ctx_len // size, nc_all=nc_all, size=size, nh=nh)
    slab = pl.BlockSpec((1, n_all, width), lambda b, h: (b, 0, h))
    return pl.pallas_call(
        kern,
        grid=(nb, ng),
        in_specs=[slab, slab, slab, slab,
                  pl.BlockSpec((1, 4 * HEADS, nc_all, size), lambda b, h: (b, 0, 0, 0)),
                  pl.BlockSpec((1, 1, width), lambda b, h: (layer * ng + h, 0, 0))],
        out_specs=slab,
        out_shape=jax.ShapeDtypeStruct((nb, n_all, SLAB), BF16),
        scratch_shapes=[pltpu.VMEM((n_all, width), F32), pltpu.VMEM((n_all, width), F32)],
        compiler_params=_cparams(("parallel", "arbitrary")),
        name="mlstm",
    )(mlq, mlk, mlv, mlo, gates_t, gain.reshape(-1, 1, width))


def _hg_tables(size):
    t = np.arange(size)
    tri = (t[None, :] <= t[:, None]).astype(np.float32)
    x = t[:, None] ^ t[None, :]
    lvl = np.where(x < HG_DIAG, 0, np.floor(np.log2(np.maximum(x, 1))).astype(np.int64) - 2)
    code = np.where(t[None, :] <= t[:, None], lvl, -1).astype(np.int32)
    j = np.arange(HG_DIAG * LANE) // LANE
    emat = (np.arange(size)[None, :] % HG_DIAG == j[:, None]).astype(np.float32)
    return (jnp.asarray(np.stack([tri, tri.T]), BF16), jnp.asarray(np.stack([code, code.T])),
            jnp.asarray(emat, BF16))


def _hg_chunks(qs, lfs, vs, sts, tris, codes, emat, revs):
    n = len(qs)
    size = qs[0].shape[0]
    bc2s, kks = [], []
    for i in range(n):
        lf2 = lfs[i]
        hi = lf2.astype(BF16)
        r1 = lf2 - hi.astype(F32)
        mid = r1.astype(BF16)
        lo = (r1 - mid.astype(F32)).astype(BF16)
        cs = _dot(tris[i], jnp.concatenate([hi, mid, lo], axis=1))
        bc2s.append((cs[:, 0:LANE] + cs[:, LANE:2 * LANE]) + cs[:, 2 * LANE:3 * LANE])
        kks.append(jnp.maximum(1.0 - jnp.exp2(lf2), 0.0))

    atts = [jnp.zeros((size, size), F32) for _ in range(n)]
    c, lvl = HG_DIAG, 1
    while c < size:
        blk = 2 * c
        for i in range(n):
            ridx = c if revs[i] else c - 1
            b3 = bc2s[i].reshape(size // blk, blk, LANE)
            ref = jnp.broadcast_to(b3[:, ridx:ridx + 1, :], b3.shape).reshape(size, LANE)
            e = jnp.exp2(_neg_abs(bc2s[i] - ref))
            a = _dot_nt((qs[i] * e).astype(BF16), (kks[i] * e).astype(BF16))
            atts[i] = jnp.where(codes[i] == lvl, a, atts[i])
        c, lvl = blk, lvl + 1

    for i in range(n):
        w3 = (bc2s[i] - jnp.log2(kks[i])).reshape(size // HG_DIAG, HG_DIAG, LANE)
        ps = []
        for j in range(HG_DIAG):
            wj = jnp.broadcast_to(w3[:, j:j + 1, :], w3.shape).reshape(size, LANE)
            ps.append((qs[i] * jnp.exp2(jnp.minimum(bc2s[i] - wj, 0.0))).astype(BF16))
        atts[i] = jnp.where(codes[i] == 0, _dot(jnp.concatenate(ps, axis=1), emat), atts[i])

    outs, new_sts = [], []
    for i in range(n):
        last = 0 if revs[i] else size - 1
        bl = bc2s[i][last:last + 1]
        outs.append(_dot(atts[i].astype(BF16), vs[i])
                    + _dot_nt((qs[i] * jnp.exp2(bc2s[i])).astype(BF16), sts[i].astype(BF16)))
        new_sts.append(sts[i] * jnp.exp2(bl) + _dot_tn(vs[i], (kks[i] * jnp.exp2(bl - bc2s[i])).astype(BF16)))
    return outs, new_sts


def _hgrn_kernel(q_ref, lf0_ref, lf1_ref, v_ref, g_ref, gain_ref, tri_ref, code_ref, emat_ref,
                 y_ref, of_ref, ob_ref, *, nc_ctx, nc_all, size):
    nc_lat = nc_all - nc_ctx
    low = lax.broadcasted_iota(jnp.int32, (size, LANE), 1) < HG_DV

    def step(c, carry):
        cb = jnp.where(c < nc_ctx, nc_ctx - 1 - c, nc_ctx + nc_lat - 1 - (c - nc_ctx))
        sf = pl.ds(pl.multiple_of(c * size, size), size)
        sb = pl.ds(pl.multiple_of(cb * size, size), size)
        qs, lfs, vs, tris, codes, revs = [], [], [], [], [], []
        for hh in range(2):
            ln = slice(hh * LANE, (hh + 1) * LANE)
            qs += [q_ref[0, sf, ln].astype(F32), q_ref[0, sb, ln].astype(F32)]
            lfs += [lf0_ref[0, sf, ln], lf1_ref[0, sb, ln]]
            vs += [v_ref[0, sf, :], v_ref[0, sb, :]]
            tris += [tri_ref[0], tri_ref[1]]
            codes += [code_ref[0], code_ref[1]]
            revs += [False, True]
        outs, sts = _hg_chunks(qs, lfs, vs, list(carry), tris, codes, emat_ref[...], revs)
        of_ref[sf, :] = jnp.where(low, outs[0], outs[2])
        ob_ref[sb, :] = jnp.where(low, outs[1], outs[3])
        return tuple(sts)

    st0 = jnp.zeros((LANE, LANE), F32)
    lax.fori_loop(0, nc_all, step, (st0,) * 4)

    def finish(c, _):
        sl = pl.ds(pl.multiple_of(c * size, size), size)
        o = of_ref[sl, :] + ob_ref[sl, :]
        sq = o * o
        ms0 = jnp.sum(jnp.where(low, sq, 0.0), axis=1, keepdims=True) * (1.0 / HG_DV)
        ms1 = jnp.sum(jnp.where(low, 0.0, sq), axis=1, keepdims=True) * (1.0 / HG_DV)
        y = o * lax.rsqrt(jnp.where(low, ms0, ms1) + EPS) * gain_ref[0]
        y_ref[0, sl, :] = (y * g_ref[0, sl, :].astype(F32)).astype(BF16)
        return 0

    lax.fori_loop(0, nc_all, finish, 0)


def _hgrn(hgq, hgf, hgv, hgg, gain, layer, ctx_len):
    nb, n_all, _ = hgq.shape
    size = HG_CHUNK
    nc_all = n_all // size
    ng = HEADS // 2
    kern = functools.partial(_hgrn_kernel, nc_ctx=ctx_len // size, nc_all=nc_all, size=size)
    wide = pl.BlockSpec((1, n_all, 2 * LANE), lambda b, h: (b, 0, h))
    slab = pl.BlockSpec((1, n_all, LANE), lambda b, h: (b, 0, h))
    tri, code, emat = _hg_tables(size)
    return pl.pallas_call(
        kern,
        grid=(nb, ng),
        in_specs=[wide, wide,
                  pl.BlockSpec((1, n_all, 2 * LANE), lambda b, h: (b, 0, ng + h)),
                  slab, slab,
                  pl.BlockSpec((1, 1, LANE), lambda b, h: (layer * ng + h, 0, 0)),
                  pl.BlockSpec(tri.shape, lambda b, h: (0, 0, 0)),
                  pl.BlockSpec(code.shape, lambda b, h: (0, 0, 0)),
                  pl.BlockSpec(emat.shape, lambda b, h: (0, 0))],
        out_specs=slab,
        out_shape=jax.ShapeDtypeStruct((nb, n_all, HG_W), BF16),
        scratch_shapes=[pltpu.VMEM((n_all, LANE), F32), pltpu.VMEM((n_all, LANE), F32)],
        compiler_params=_cparams(("parallel", "arbitrary")),
        name="hgrn2",
    )(hgq, hgf, hgf, hgv, hgg, gain, tri, code, emat)


def _attn_kernel(q_ref, k_ref, vt_ref, lam_ref, gain_ref, y_ref, s0_ref, s1_ref, m0_ref, m1_ref,
                 *, tq, kc, ctx_len, lam_init):
    n_all = k_ref.shape[1]
    n_tiles = (n_all - ctx_len) // tq
    s_refs, m_refs = (s0_ref, s1_ref), (m0_ref, m1_ref)
    lv = lam_ref[0]
    lam = (jnp.exp(jnp.sum(lv[0:1] * lv[1:2], axis=1, keepdims=True))
           - jnp.exp(jnp.sum(lv[2:3] * lv[3:4], axis=1, keepdims=True)) + lam_init)
    lane = lax.broadcasted_iota(jnp.int32, (tq, LANE), 1)
    vrow = lax.broadcasted_iota(jnp.int32, (LANE, kc), 0)

    def rows(t):
        return pl.ds(pl.multiple_of(ctx_len + t * tq, tq), tq)

    def load_qq(sl):
        q = q_ref[0, sl, :]
        zero = jnp.zeros_like(q)
        return jnp.concatenate([jnp.where(lane < DA_HD, q, zero), jnp.where(lane < DA_HD, zero, q)], axis=0)

    def stage_a_chunk(qq, slot, c, m):
        s = _dot_nt(k_ref[0, c * kc:(c + 1) * kc, :], qq)
        s_refs[slot][c * kc:(c + 1) * kc, :] = s
        cm = jnp.max(s, axis=0, keepdims=True)
        return cm if m is None else jnp.maximum(m, cm)

    def stage_b_chunk(slot, c, m, acc):
        p = jnp.exp2(s_refs[slot][c * kc:(c + 1) * kc, :] - m).astype(BF16)
        vt = vt_ref[0, :, c * kc:(c + 1) * kc]
        pv = _dot(jnp.where(vrow == DA_VD, jnp.ones_like(vt), vt), p)
        return pv if acc is None else acc + pv

    def finish(acc, sl):
        r0 = 1.0 / acc[DA_VD:DA_VD + 1, 0:tq]
        r1 = lam / acc[DA_VD:DA_VD + 1, tq:2 * tq]
        o = (acc[:, 0:tq] * r0 - acc[:, tq:2 * tq] * r1).T
        o = jnp.where(lane < DA_VD, o, 0.0)
        y = o * lax.rsqrt(jnp.sum(o * o, axis=1, keepdims=True) * (1.0 / DA_VD) + EPS) * gain_ref[0]
        y_ref[0, sl, :] = (y * (1.0 - lam_init)).astype(BF16)

    def stages(sl_a, sl_b, slot_a, nk_a, nk_b):
        slot_b = 1 - slot_a
        if sl_a is not None:
            qq = load_qq(sl_a)
        if sl_b is not None:
            m_b = m_refs[slot_b][0:1, :]
        m = acc = None
        for c in range(max(nk_a, nk_b) // kc):
            if sl_a is not None and c < nk_a // kc:
                m = stage_a_chunk(qq, slot_a, c, m)
            if sl_b is not None and c < nk_b // kc:
                acc = stage_b_chunk(slot_b, c, m_b, acc)
        if sl_a is not None:
            m_refs[slot_a][...] = jnp.broadcast_to(m, (8, 2 * tq))
        if sl_b is not None:
            finish(acc, sl_b)

    ctx_rows = pl.ds(0, tq)
    stages(ctx_rows, None, 1, ctx_len, 0)
    stages(rows(0), ctx_rows, 0, n_all, ctx_len)

    def body(i, _):
        t = 1 + 2 * i
        stages(rows(t), rows(t - 1), 1, n_all, n_all)
        stages(rows(t + 1), rows(t), 0, n_all, n_all)
        return 0

    lax.fori_loop(0, (n_tiles - 2) // 2, body, 0)
    stages(rows(n_tiles - 1), rows(n_tiles - 2), 1, n_all, n_all)
    stages(None, rows(n_tiles - 1), 0, 0, n_all)


def _attn(daq, dak, dav_t, lam_pad, gain, layer, ctx_len, lam_init):
    nb, n_all, _ = daq.shape
    tq, kc = ATT_TQ, ATT_KC
    n_tiles = (n_all - ctx_len) // tq
    assert n_tiles >= 2 and n_tiles % 2 == 0 and ctx_len == tq and ctx_len % kc == 0 and n_all % kc == 0
    kern = functools.partial(_attn_kernel, tq=tq, kc=kc, ctx_len=ctx_len, lam_init=lam_init)
    slab = pl.BlockSpec((1, n_all, LANE), lambda b, h: (b, 0, h))
    sbuf, mbuf = pltpu.VMEM((n_all, 2 * tq), F32), pltpu.VMEM((8, 2 * tq), F32)
    return pl.pallas_call(
        kern,
        grid=(nb, HEADS),
        in_specs=[slab, slab,
                  pl.BlockSpec((1, LANE, n_all), lambda b, h: (b, h, 0)),
                  pl.BlockSpec((1, 4, LANE), lambda b, h: (layer, 0, 0)),
                  pl.BlockSpec((1, 1, LANE), lambda b, h: (layer, 0, 0))],
        out_specs=slab,
        out_shape=jax.ShapeDtypeStruct((nb, n_all, SLAB), BF16),
        scratch_shapes=[sbuf, sbuf, mbuf, mbuf],
        compiler_params=_cparams(("parallel", "arbitrary")),
        name="diff_attn",
    )(daq, dak, dav_t, lam_pad, gain)


def _resident(shape, index_map):
    return pl.BlockSpec(shape, index_map, pipeline_mode=pl.Buffered(1))


def _mix_ffn_kernel(yml_ref, yhg_ref, yda_ref, x_ref, g1x_ref, g1c_ref, a2x_ref, a2c_ref, s2x_ref, s2c_ref,
                    g2x_ref, g2c_ref, wo_ref, gpm_ref, gpf_ref, wg_ref, wu_ref, wd_ref, gqf_ref, o_ref, *, tm, ctx_len, t0):
    row0 = (pl.program_id(1) + t0) * tm
    split = min(ctx_len % tm if ctx_len % tm else tm, tm)
    top_is_ctx = row0 < ctx_len

    def mod(c_ref, x_ref_, lo):
        return jnp.where(top_is_ctx, c_ref[0, 0], x_ref_[0, 0]) if lo < split else x_ref_[0, 0]

    nsub = MIX_SUB
    sub = tm // nsub
    x1s, h2s = [], []
    for i in range(nsub):
        lo, rs = i * sub, slice(i * sub, (i + 1) * sub)
        y = (_dot(yml_ref[0, rs, :], wo_ref[0, 0:SLAB, :]) + _dot(yhg_ref[0, rs, :], wo_ref[0, SLAB:SLAB + HG_W, :])
             + _dot(yda_ref[0, rs, :], wo_ref[0, SLAB + HG_W:2 * SLAB + HG_W, :]))
        x1 = x_ref[0, rs, :] + _rms(y, mod(g1c_ref, g1x_ref, lo) * gpm_ref[0])
        h2s.append((_rms(x1, gpf_ref[0] * (1.0 + mod(a2c_ref, a2x_ref, lo))) + mod(s2c_ref, s2x_ref, lo)).astype(BF16))
        x1s.append(x1)
    for i in range(nsub):
        lo, rs = i * sub, slice(i * sub, (i + 1) * sub)
        act = _silu(_dot(h2s[i], wg_ref[0])) * _dot(h2s[i], wu_ref[0])
        f = _dot(act.astype(BF16), wd_ref[0])
        o_ref[0, rs, :] = x1s[i] + _rms(f, mod(g2c_ref, g2x_ref, lo) * gqf_ref[0])


def _mix_ffn(yml, yhg, yda, xs, mod4, w_out_pad, g_post_mix, g_pre_ffn, wg, wu, wd, g_post_ffn, layer, ctx_len, tm, t0):
    nb, n_all, d = xs.shape
    dff = wg.shape[-1]
    assert tm % MIX_SUB == 0 and (ctx_len % tm) % (tm // MIX_SUB) == 0
    kern = functools.partial(_mix_ffn_kernel, tm=tm, ctx_len=ctx_len, t0=t0)
    lsel = lambda b, t: (layer, 0, 0)
    tile = lambda width: pl.BlockSpec((1, tm, width), lambda b, t: (b, t + t0, 0))
    return pl.pallas_call(
        kern,
        grid=(nb, n_all // tm - t0),
        in_specs=[tile(SLAB), tile(HG_W), tile(SLAB), tile(d)] + _mod_specs(layer, nb, (2, 4, 3, 5), d) + [
            _resident((1, 2 * SLAB + HG_W, d), lsel),
            pl.BlockSpec((1, 1, d), lsel),
            pl.BlockSpec((1, 1, d), lsel),
            _resident((1, d, dff), lsel),
            _resident((1, d, dff), lsel),
            _resident((1, dff, d), lsel),
            pl.BlockSpec((1, 1, d), lsel)],
        out_specs=pl.BlockSpec((1, tm, d), lambda b, t: (b, t, 0)),
        out_shape=jax.ShapeDtypeStruct((nb, n_all - t0 * tm, d), F32),
        compiler_params=_cparams(("parallel", "arbitrary")),
        name="mix_ffn",
    )(yml, yhg, yda, xs, *([mod4] * 8), w_out_pad, g_post_mix, g_pre_ffn, wg, wu, wd, g_post_ffn)


def _pad_heads(w, hd):
    lead = w.shape[:-1]
    w = w.reshape(*lead, HEADS, hd)
    w = jnp.pad(w, [(0, 0)] * len(lead) + [(0, 0), (0, LANE - hd)])
    return w.reshape(*lead, SLAB)


def _pad_in_cols(w, extra_f=None):
    sizes = (4 * ML_HD,) * 4 + (2 * HEADS, 2 * HEADS, HEADS * HG_DK, 2 * HEADS * HG_DK, HEADS * HG_DV, HEADS * HG_DV,
             2 * HEADS * DA_HD, 2 * HEADS * DA_HD, HEADS * DA_VD)
    parts = []
    off = 0
    for s in sizes:
        parts.append(w[..., off:off + s])
        off += s
    gi, gf = parts[4], parts[5]
    if extra_f is not None:
        gf = gf + extra_f
    gate = jnp.concatenate([gi, gf], axis=-1)
    gate = jnp.pad(gate, [(0, 0)] * (gate.ndim - 1) + [(0, LANE - 4 * HEADS)])
    return jnp.concatenate([
        _pad_heads(parts[0], ML_HD), _pad_heads(parts[1], ML_HD), _pad_heads(parts[2], ML_HD), _pad_heads(parts[3], ML_HD),
        gate, parts[6], parts[7], parts[8], parts[9],
        _pad_heads(parts[10], DA_VD), _pad_heads(parts[11], DA_VD), _pad_heads(parts[12], DA_VD)], axis=-1)


def _pad_out_rows(w_out):
    depth, _, d = w_out.shape
    wt = jnp.swapaxes(w_out, 1, 2)
    ml, hg, da = wt[..., :4 * ML_HD], wt[..., 4 * ML_HD:4 * ML_HD + HEADS * HG_DV], wt[..., 4 * ML_HD + HEADS * HG_DV:]
    wp = jnp.concatenate([_pad_heads(ml, ML_HD), hg, _pad_heads(da, DA_VD)], axis=-1)
    return jnp.swapaxes(wp, 1, 2)


def _rope_tables(n_lat, ctx_len):
    rows = n_lat // GRID_W
    row = jnp.repeat(jnp.arange(rows), GRID_W).astype(F32)
    col = jnp.tile(jnp.arange(GRID_W), rows).astype(F32)
    half = DA_HD // 2
    inv = ROPE_BASE ** (-jnp.arange(0, half, 2, dtype=F32) / half)
    ang_r, ang_c = row[:, None] * inv, col[:, None] * inv
    zero = jnp.zeros_like(ang_r)

    def one_map(fr, fc, sel):
        r, c = fr(ang_r), fc(ang_c)
        if sel == "cos":
            return jnp.concatenate([r, r, c, c], axis=1)
        if sel == "a":
            return jnp.concatenate([-r, zero, -c, zero], axis=1)
        return jnp.concatenate([zero, r, zero, c], axis=1)

    tabs = []
    for sel, fn in (("cos", jnp.cos), ("a", jnp.sin), ("b", jnp.sin)):
        m = one_map(fn, fn, sel)
        lat = jnp.pad(jnp.concatenate([m, m], axis=1), ((0, 0), (0, LANE - 2 * DA_HD)))
        fill = 1.0 if sel == "cos" else 0.0
        ctx_rows = jnp.pad(jnp.full((ctx_len, 2 * DA_HD), fill, F32), ((0, 0), (0, LANE - 2 * DA_HD)))
        tabs.append(jnp.concatenate([ctx_rows, lat], axis=0))
    return tabs


def kernel(x, c, ctx, c_ctx, w_ada, b_ada, g_pre_mix, g_post_mix, g_pre_ffn, g_post_ffn, w_in, b_in, w_out,
           ml_f_bias, ml_norm, hg_lb, hg_norm, da_lambda, da_norm, w_ffn_gate, w_ffn_up, w_ffn_down):
    nb, n_lat, d = x.shape
    ctx_len = ctx.shape[1]
    depth = w_ada.shape[0]
    n_all = ctx_len + n_lat

    w_pad = _pad_in_cols(w_in).astype(BF16)
    b_pad = _pad_in_cols(b_in, extra_f=ml_f_bias)[:, None, :]
    w_out_pad = _pad_out_rows(w_out).astype(BF16)
    wg, wu, wd = w_ffn_gate.astype(BF16), w_ffn_up.astype(BF16), w_ffn_down.astype(BF16)
    ml_gain = _pad_heads(ml_norm, ML_HD).reshape(depth * HEADS, 1, LANE)
    hg_gain = hg_norm.reshape(-1, 1, LANE)
    da_gain = jnp.pad(da_norm, ((0, 0), (0, LANE - DA_VD)))[:, None, :]
    lam_pad = jnp.pad(da_lambda.astype(F32), ((0, 0), (0, 0), (0, LANE - DA_HD)))
    sm = jax.nn.softmax(hg_lb.astype(F32), axis=0)
    lbs = jnp.cumsum(sm, axis=0) - sm[0:1]
    lbf = jnp.maximum(lbs, LB_FLOOR)[:, None, :]
    oml = (1.0 - lbs)[:, None, :]
    rope_tabs = _rope_tables(n_lat, ctx_len)
    r3 = lambda g: g[:, None, :]

    mp = -(-(nb + 1) // 8) * 8
    cc = jnp.concatenate([c, c_ctx[None, :], jnp.zeros((mp - nb - 1, d), F32)], axis=0)
    mod4 = _ada(cc, w_ada, b_ada).reshape(depth, mp, 1, 6 * d)

    xs = jnp.concatenate([ctx, x], axis=1)
    for l in range(depth):
        lam_init = 0.8 - 0.6 * math.exp(-0.3 * l)
        (mlq, mlk, mlv, mlo, gates, hgq, hgf, hgv, hgg, daq, dak, dav_t) = _inproj(
            xs, mod4, r3(g_pre_mix), w_pad, b_pad, rope_tabs, lbf, oml, l, ctx_len, IN_TM if n_all % IN_TM == 0 else 256)
        gates_t = jnp.swapaxes(gates[:, :, :4 * HEADS], 1, 2).reshape(nb, 4 * HEADS, n_all // ML_CHUNK, ML_CHUNK)
        yml = _mlstm(mlq, mlk, mlv, mlo, gates_t, ml_gain, l, ctx_len)
        yhg = _hgrn(hgq, hgf, hgv, hgg, hg_gain, l, ctx_len)
        yda = _attn(daq, dak, dav_t, lam_pad, da_gain, l, ctx_len, lam_init)
        last = l == depth - 1
        tm = MIX_TM_LAST if last else (MIX_TM if n_all % MIX_TM == 0 else MIX_TM_LAST)
        xs = _mix_ffn(yml, yhg, yda, xs, mod4, w_out_pad, r3(g_post_mix), r3(g_pre_ffn), wg, wu, wd,
                      r3(g_post_ffn), l, ctx_len, tm, ctx_len // tm if last else 0)
    return xs
```

```python
import functools
import math

import numpy as np
import jax
import jax.numpy as jnp
from jax import lax
from jax.experimental import pallas as pl
from jax.experimental.pallas import tpu as pltpu

F32 = jnp.float32
BF16 = jnp.bfloat16
HIGHEST = lax.Precision.HIGHEST

LANE = 128
VMEM_LIMIT = 52 * 1024 * 1024

EPS = 1e-6
NEG = -1e30
LB_FLOOR = 1e-30
M_INIT = -1e30
GRID_W = 64
ROPE_BASE = 10000.0

HEADS = 4
ML_HD = 96
HG_DV = 64
HG_DK = 128
DA_HD = 48
DA_VD = 2 * DA_HD
SLAB = HEADS * LANE
HG_W = HEADS * HG_DV

ML_CHUNK = 256
ML_HPS = 2
HG_CHUNK = 128
HG_DIAG = 8
ATT_TQ = 256
IN_CW = 256
IN_TM = 768
IN_SUB = 3
MIX_TM, MIX_SUB = 768, 3
MIX_TM_LAST, MIX_SUB_LAST = 256, 2
ATT_KC = 256
LOG2E = 1.4426950408889634

OFF_MLQ, OFF_MLK, OFF_MLV, OFF_MLO = 0, SLAB, 2 * SLAB, 3 * SLAB
OFF_GATE = 4 * SLAB
OFF_HGQ = OFF_GATE + LANE
OFF_HGF = OFF_HGQ + SLAB
OFF_HGV = OFF_HGF + 2 * SLAB
OFF_HGG = OFF_HGV + HG_W
OFF_DAQ = OFF_HGG + HG_W
OFF_DAK = OFF_DAQ + SLAB
OFF_DAV = OFF_DAK + SLAB
NP_IN = OFF_DAV + SLAB


def _cparams(sem):
    return pltpu.CompilerParams(dimension_semantics=sem, vmem_limit_bytes=VMEM_LIMIT)


def _silu(x):
    return x * jax.nn.sigmoid(x)


def _log_sigmoid(z):
    return jnp.minimum(z, 0.0) - jnp.log1p(jnp.exp(-jnp.abs(z)))


def _neg_abs(x):
    bits = lax.bitcast_convert_type(x, jnp.uint32) | jnp.uint32(0x80000000)
    return lax.bitcast_convert_type(bits, F32)


def _rms(x, g):
    return x * lax.rsqrt(jnp.mean(x * x, axis=-1, keepdims=True) + EPS) * g


def _dot(a, b):
    return jnp.dot(a, b, preferred_element_type=F32)


def _dot_nt(a, b):
    return lax.dot_general(a, b, (((1,), (1,)), ((), ())), preferred_element_type=F32)


def _dot_tn(a, b):
    return lax.dot_general(a, b, (((0,), (0,)), ((), ())), preferred_element_type=F32)


def _ada_kernel(s_ref, w_ref, b_ref, o_ref):
    s = _silu(s_ref[...])
    o_ref[0] = jnp.dot(s, w_ref[0], precision=HIGHEST, preferred_element_type=F32) + b_ref[0]


def _ada(cc, w_ada, b_ada):
    depth, d, d6 = w_ada.shape
    mp = cc.shape[0]
    tn = 1024
    return pl.pallas_call(
        _ada_kernel,
        grid=(depth, d6 // tn),
        in_specs=[pl.BlockSpec((mp, d), lambda l, j: (0, 0)),
                  pl.BlockSpec((1, d, tn), lambda l, j: (l, 0, j)),
                  pl.BlockSpec((1, 1, tn), lambda l, j: (l, 0, j))],
        out_specs=pl.BlockSpec((1, mp, tn), lambda l, j: (l, 0, j)),
        out_shape=jax.ShapeDtypeStruct((depth, mp, d6), F32),
        compiler_params=_cparams(("arbitrary", "arbitrary")),
        name="ada_mod",
    )(cc, w_ada, b_ada.reshape(depth, 1, d6))


def _mod_specs(layer, nb, idxs, d):
    specs = []
    for j in idxs:
        specs.append(pl.BlockSpec((1, 1, 1, d), lambda b, t, j=j: (layer, b, 0, j)))
        specs.append(pl.BlockSpec((1, 1, 1, d), lambda b, t, j=j: (layer, nb, 0, j)))
    return specs


def _inproj_kernel(x_ref, sx_ref, sc_ref, ax_ref, ac_ref, g_ref, w_ref, b_ref,
                   cos_ref, sna_ref, snb_ref, lbf_ref, oml_ref,
                   mlq_ref, mlk_ref, mlv_ref, mlo_ref, gate_ref,
                   hgq_ref, hgf_ref, hgv_ref, hgg_ref, daq_ref, dak_ref, dav_ref,
                   *, tm, ctx_len):
    split = min(ctx_len % tm if ctx_len % tm else tm, tm)
    top_is_ctx = pl.program_id(1) * tm < ctx_len
    sub = tm // IN_SUB
    half = DA_HD // 4
    lane = lax.broadcasted_iota(jnp.int32, (sub, LANE), 1)

    def rope(p, rs):
        cos, sna, snb = cos_ref[rs, :], sna_ref[rs, :], snb_ref[rs, :]
        outs = []
        for i in range(p.shape[1] // LANE):
            xh = p[:, i * LANE:(i + 1) * LANE]
            outs.append(xh * cos + pltpu.roll(xh, LANE - half, 1) * sna + pltpu.roll(xh, half, 1) * snb)
        return jnp.concatenate(outs, axis=1)

    def log2_f(p, c0, rs):
        w = p.shape[1]
        return jnp.log2(lbf_ref[0, :, c0:c0 + w] + oml_ref[0, :, c0:c0 + w] * jax.nn.sigmoid(p))

    def gates(p, c0, rs):
        return jnp.where((lane >= 2 * HEADS) & (lane < 4 * HEADS), _log_sigmoid(p), p)

    plain = lambda p, c0, rs: p
    groups = {
        "mlq": (mlq_ref, OFF_MLQ, SLAB, plain),
        "mlk": (mlk_ref, OFF_MLK, SLAB, lambda p, c0, rs: p * (ML_HD ** -0.5)),
        "mlv": (mlv_ref, OFF_MLV, SLAB, plain),
        "mlo": (mlo_ref, OFF_MLO, SLAB, lambda p, c0, rs: jax.nn.sigmoid(p)),
        "gate": (gate_ref, OFF_GATE, LANE, gates),
        "hgq": (hgq_ref, OFF_HGQ, SLAB, lambda p, c0, rs: _silu(p)),
        "hgf": (hgf_ref, OFF_HGF, 2 * SLAB, log2_f),
        "hgv": (hgv_ref, OFF_HGV, HG_W, plain),
        "hgg": (hgg_ref, OFF_HGG, HG_W, lambda p, c0, rs: _silu(p)),
        "daq": (daq_ref, OFF_DAQ, SLAB, lambda p, c0, rs: rope(p, rs) * (LOG2E * DA_HD ** -0.5)),
        "dak": (dak_ref, OFF_DAK, SLAB, lambda p, c0, rs: rope(p, rs)),
        "dav": (dav_ref, OFF_DAV, SLAB, plain),
    }
    order = [("hgf", 0), ("mlq", 0), ("hgf", 1), ("mlq", 1), ("hgf", 2), ("mlk", 0), ("hgf", 3), ("mlk", 1),
             ("mlo", 0), ("mlv", 0), ("mlo", 1), ("mlv", 1), ("hgq", 0), ("dav", 0), ("hgq", 1), ("dav", 1),
             ("daq", 0), ("hgv", 0), ("daq", 1), ("gate", 0), ("dak", 0), ("hgg", 0), ("dak", 1)]
    for i in range(IN_SUB):
        rs = slice(i * sub, (i + 1) * sub)
        ctx_rows = i * sub < split
        shift = jnp.where(top_is_ctx, sc_ref[0, 0], sx_ref[0, 0]) if ctx_rows else sx_ref[0, 0]
        scale = 1.0 + (jnp.where(top_is_ctx, ac_ref[0, 0], ax_ref[0, 0]) if ctx_rows else ax_ref[0, 0])
        h = (_rms(x_ref[0, rs, :], g_ref[0] * scale) + shift).astype(BF16)
        for name, ci in order:
            ref, off, width, fn = groups[name]
            cw = min(width, IN_CW)
            c0 = ci * cw
            p = _dot(h, w_ref[0, :, off + c0:off + c0 + cw]) + b_ref[0, :, off + c0:off + c0 + cw]
            if name == "dav":
                ref[0, c0:c0 + cw, rs] = p.T.astype(ref.dtype)
            else:
                ref[0, rs, c0:c0 + cw] = fn(p, c0, rs).astype(ref.dtype)


def _inproj(xs, mod4, g_pre, w_pad, b_pad, rope_tabs, lbf, oml, layer, ctx_len, tm):
    nb, n_all, d = xs.shape
    assert tm % IN_SUB == 0 and (ctx_len % tm) % (tm // IN_SUB) == 0
    kern = functools.partial(_inproj_kernel, tm=tm, ctx_len=ctx_len)
    lsel = lambda b, t: (layer, 0, 0)
    tile = lambda width: pl.BlockSpec((1, tm, width), lambda b, t: (b, t, 0))
    tab = pl.BlockSpec((tm, LANE), lambda b, t: (t, 0))
    bf = lambda width: jax.ShapeDtypeStruct((nb, n_all, width), BF16)
    f32 = lambda width: jax.ShapeDtypeStruct((nb, n_all, width), F32)
    out_shapes = [bf(SLAB)] * 4 + [f32(LANE)] + [bf(SLAB), f32(2 * SLAB), bf(HG_W), bf(HG_W)] + [bf(SLAB)] * 3
    out_specs = [tile(s.shape[-1]) for s in out_shapes]
    out_shapes[-1] = jax.ShapeDtypeStruct((nb, SLAB, n_all), BF16)
    out_specs[-1] = pl.BlockSpec((1, SLAB, tm), lambda b, t: (b, 0, t))
    return pl.pallas_call(
        kern,
        grid=(nb, n_all // tm),
        in_specs=[tile(d)] + _mod_specs(layer, nb, (0, 1), d) + [
            pl.BlockSpec((1, 1, d), lsel),
            pl.BlockSpec((1, d, NP_IN), lsel, pipeline_mode=pl.Buffered(1)),
            pl.BlockSpec((1, 1, NP_IN), lsel),
            tab, tab, tab,
            pl.BlockSpec((1, 1, 2 * SLAB), lsel),
            pl.BlockSpec((1, 1, 2 * SLAB), lsel)],
        out_specs=out_specs,
        out_shape=out_shapes,
        compiler_params=_cparams(("parallel", "arbitrary")),
        name="in_proj",
    )(xs, mod4, mod4, mod4, mod4, g_pre, w_pad, b_pad, *rope_tabs, lbf, oml)


def _ml_chunks(qs, ks, vs, i_rows, lf_rows, s_exts, ms, revs):
    n = len(qs)
    size = qs[0].shape[0]
    ii = lax.broadcasted_iota(jnp.int32, (size, size), 0)
    jj = lax.broadcasted_iota(jnp.int32, (size, size), 1)
    lane = lax.broadcasted_iota(jnp.int32, (size, LANE), 1)
    cums = {rev: jnp.where((ii >= jj) if rev else (ii <= jj), 1.0, 0.0).astype(BF16) for rev in set(revs)}
    seens = {rev: (jj >= ii) if rev else (jj <= ii) for rev in set(revs)}
    row16 = lax.broadcasted_iota(jnp.int32, (16, size), 0)

    b_cols, g_cols, g_rows = [], [], []
    for i in range(n):
        lf2 = lf_rows[i] * LOG2E
        hi = lf2.astype(BF16).astype(F32)
        mid = (lf2 - hi).astype(BF16).astype(F32)
        lo = (lf2 - hi) - mid
        parts = jnp.where(row16 == 0, hi, jnp.where(row16 == 1, mid, jnp.where(row16 == 2, lo, 0.0))).astype(BF16)
        cs = _dot(parts, cums[revs[i]])
        b_row = (cs[0:1] + cs[1:2]) + cs[2:3]
        g_row = i_rows[i] * LOG2E - b_row
        b_cols.append(jnp.sum(jnp.where(ii == jj, b_row, 0.0), axis=1, keepdims=True))
        g_cols.append(jnp.sum(jnp.where(ii == jj, g_row, 0.0), axis=1, keepdims=True))
        g_rows.append(g_row)

    mts, dws, w_inters = [], [], []
    for i in range(n):
        dmat = jnp.where(seens[revs[i]], b_cols[i] + g_rows[i], NEG)
        a_col = b_cols[i] + ms[i]
        mt = jnp.maximum(a_col, jnp.max(dmat, axis=1, keepdims=True))
        mts.append(mt)
        w_inters.append(jnp.exp2(a_col - mt))
        dws.append(jnp.exp2(dmat - mt))

    ress = []
    for i in range(n):
        qk = _dot_nt(qs[i], ks[i]) * dws[i]
        ress.append(_dot(qk.astype(BF16), vs[i]) + w_inters[i] * _dot(qs[i], s_exts[i].astype(BF16)))

    houts, s_news, m_news = [], [], []
    for i in range(n):
        den = jnp.sum(jnp.where(lane == ML_HD, ress[i], 0.0), axis=1, keepdims=True)
        houts.append(ress[i] * (1.0 / jnp.maximum(jnp.abs(den), jnp.exp2(-mts[i]))))
        last = 0 if revs[i] else size - 1
        m_new = mts[i][last:last + 1]
        b_last = b_cols[i][last:last + 1]
        decay = jnp.exp2(b_last + ms[i] - m_new)
        wk = jnp.exp2(b_last + g_cols[i] - m_new)
        s_news.append(decay * s_exts[i] + _dot_tn((wk * ks[i].astype(F32)).astype(BF16), vs[i]))
        m_news.append(m_new)
    return houts, s_news, m_news


def _mlstm_kernel(q_ref, k_ref, v_ref, o_ref, gt_ref, gain_ref, y_ref, hf_ref, hb_ref, *, nc_ctx, nc_all, size, nh):
    hd0 = pl.program_id(1) * nh
    lane = lax.broadcasted_iota(jnp.int32, (size, LANE), 1)
    nc_lat = nc_all - nc_ctx

    def step(c, carry):
        cb = jnp.where(c < nc_ctx, nc_ctx - 1 - c, nc_ctx + nc_lat - 1 - (c - nc_ctx))
        sf = pl.ds(pl.multiple_of(c * size, size), size)
        sb = pl.ds(pl.multiple_of(cb * size, size), size)
        qs, ks, vs, i_rows, lf_rows, revs = [], [], [], [], [], []
        for hh in range(nh):
            ln = slice(hh * LANE, (hh + 1) * LANE)
            for sl, cc, off in ((sf, c, 0), (sb, cb, HEADS)):
                v = v_ref[0, sl, ln]
                qs.append(q_ref[0, sl, ln])
                ks.append(k_ref[0, sl, ln])
                vs.append(jnp.where(lane == ML_HD, jnp.ones_like(v), v))
                i_rows.append(gt_ref[0, off + hd0 + hh, pl.ds(cc, 1), :])
                lf_rows.append(gt_ref[0, 2 * HEADS + off + hd0 + hh, pl.ds(cc, 1), :])
                revs.append(off > 0)
        houts, s_news, m_news = _ml_chunks(qs, ks, vs, i_rows, lf_rows, list(carry[0]), list(carry[1]), revs)
        for hh in range(nh):
            ln = slice(hh * LANE, (hh + 1) * LANE)
            hf_ref[sf, ln] = houts[2 * hh]
            hb_ref[sb, ln] = houts[2 * hh + 1]
        return tuple(s_news), tuple(m_news)

    s0 = jnp.zeros((LANE, LANE), F32)
    m0 = jnp.full((1, 1), M_INIT, F32)
    lax.fori_loop(0, nc_all, step, ((s0,) * (2 * nh), (m0,) * (2 * nh)))

    def finish(c, _):
        sl = pl.ds(pl.multiple_of(c * size, size), size)
        for hh in range(nh):
            ln = slice(hh * LANE, (hh + 1) * LANE)
            h = jnp.where(lane < ML_HD, hf_ref[sl, ln] + hb_ref[sl, ln], 0.0)
            y = h * lax.rsqrt(jnp.sum(h * h, axis=1, keepdims=True) * (1.0 / ML_HD) + EPS) * gain_ref[0, :, ln]
            y_ref[0, sl, ln] = (y * o_ref[0, sl, ln].astype(F32)).astype(BF16)
        return 0

    lax.fori_loop(0, nc_all, finish, 0)


def _mlstm(mlq, mlk, mlv, mlo, gates_t, gain, layer, ctx_len):
    nb, n_all, _ = mlq.shape
    size, nh = ML_CHUNK, ML_HPS
    nc_all = n_all // size
    width = nh * LANE
    ng = HEADS // nh
    kern = functools.partial(_mlstm_kernel, nc_ctx=ctx_len // size, nc_all=nc_all, size=size, nh=nh)
    slab = pl.BlockSpec((1, n_all, width), lambda b, h: (b, 0, h))
    return pl.pallas_call(
        kern,
        grid=(nb, ng),
        in_specs=[slab, slab, slab, slab,
                  pl.BlockSpec((1, 4 * HEADS, nc_all, size), lambda b, h: (b, 0, 0, 0)),
                  pl.BlockSpec((1, 1, width), lambda b, h: (layer * ng + h, 0, 0))],
        out_specs=slab,
        out_shape=jax.ShapeDtypeStruct((nb, n_all, SLAB), BF16),
        scratch_shapes=[pltpu.VMEM((n_all, width), F32), pltpu.VMEM((n_all, width), F32)],
        compiler_params=_cparams(("parallel", "arbitrary")),
        name="mlstm",
    )(mlq, mlk, mlv, mlo, gates_t, gain.reshape(-1, 1, width))


def _hg_tables(size):
    t = np.arange(size)
    tri = (t[None, :] <= t[:, None]).astype(np.float32)
    x = t[:, None] ^ t[None, :]
    lvl = np.where(x < HG_DIAG, 0, np.floor(np.log2(np.maximum(x, 1))).astype(np.int64) - 2)
    code = np.where(t[None, :] <= t[:, None], lvl, -1).astype(np.int32)
    j = np.arange(HG_DIAG * LANE) // LANE
    emat = (np.arange(size)[None, :] % HG_DIAG == j[:, None]).astype(np.float32)
    return (jnp.asarray(np.stack([tri, tri.T]), BF16), jnp.asarray(np.stack([code, code.T])),
            jnp.asarray(emat, BF16))


def _hg_chunks(qs, lfs, vs, sts, tris, codes, emat, revs):
    n = len(qs)
    size = qs[0].shape[0]
    bc2s, kks = [], []
    for i in range(n):
        lf2 = lfs[i]
        hi = lf2.astype(BF16)
        r1 = lf2 - hi.astype(F32)
        mid = r1.astype(BF16)
        lo = (r1 - mid.astype(F32)).astype(BF16)
        cs = _dot(tris[i], jnp.concatenate([hi, mid, lo], axis=1))
        bc2s.append((cs[:, 0:LANE] + cs[:, LANE:2 * LANE]) + cs[:, 2 * LANE:3 * LANE])
        kks.append(jnp.maximum(1.0 - jnp.exp2(lf2), 0.0))

    atts = [jnp.zeros((size, size), F32) for _ in range(n)]
    c, lvl = HG_DIAG, 1
    while c < size:
        blk = 2 * c
        for i in range(n):
            ridx = c if revs[i] else c - 1
            b3 = bc2s[i].reshape(size // blk, blk, LANE)
            ref = jnp.broadcast_to(b3[:, ridx:ridx + 1, :], b3.shape).reshape(size, LANE)
            e = jnp.exp2(_neg_abs(bc2s[i] - ref))
            a = _dot_nt((qs[i] * e).astype(BF16), (kks[i] * e).astype(BF16))
            atts[i] = jnp.where(codes[i] == lvl, a, atts[i])
        c, lvl = blk, lvl + 1

    for i in range(n):
        w3 = (bc2s[i] - jnp.log2(kks[i])).reshape(size // HG_DIAG, HG_DIAG, LANE)
        ps = []
        for j in range(HG_DIAG):
            wj = jnp.broadcast_to(w3[:, j:j + 1, :], w3.shape).reshape(size, LANE)
            ps.append((qs[i] * jnp.exp2(jnp.minimum(bc2s[i] - wj, 0.0))).astype(BF16))
        atts[i] = jnp.where(codes[i] == 0, _dot(jnp.concatenate(ps, axis=1), emat), atts[i])

    outs, new_sts = [], []
    for i in range(n):
        last = 0 if revs[i] else size - 1
        bl = bc2s[i][last:last + 1]
        outs.append(_dot(atts[i].astype(BF16), vs[i])
                    + _dot_nt((qs[i] * jnp.exp2(bc2s[i])).astype(BF16), sts[i].astype(BF16)))
        new_sts.append(sts[i] * jnp.exp2(bl) + _dot_tn(vs[i], (kks[i] * jnp.exp2(bl - bc2s[i])).astype(BF16)))
    return outs, new_sts


def _hgrn_kernel(q_ref, lf0_ref, lf1_ref, v_ref, g_ref, gain_ref, tri_ref, code_ref, emat_ref,
                 y_ref, of_ref, ob_ref, *, nc_ctx, nc_all, size):
    nc_lat = nc_all - nc_ctx
    low = lax.broadcasted_iota(jnp.int32, (size, LANE), 1) < HG_DV

    def step(c, carry):
        cb = jnp.where(c < nc_ctx, nc_ctx - 1 - c, nc_ctx + nc_lat - 1 - (c - nc_ctx))
        sf = pl.ds(pl.multiple_of(c * size, size), size)
        sb = pl.ds(pl.multiple_of(cb * size, size), size)
        qs, lfs, vs, tris, codes, revs = [], [], [], [], [], []
        for hh in range(2):
            ln = slice(hh * LANE, (hh + 1) * LANE)
            qs += [q_ref[0, sf, ln].astype(F32), q_ref[0, sb, ln].astype(F32)]
            lfs += [lf0_ref[0, sf, ln], lf1_ref[0, sb, ln]]
            vs += [v_ref[0, sf, :], v_ref[0, sb, :]]
            tris += [tri_ref[0], tri_ref[1]]
            codes += [code_ref[0], code_ref[1]]
            revs += [False, True]
        outs, sts = _hg_chunks(qs, lfs, vs, list(carry), tris, codes, emat_ref[...], revs)
        of_ref[sf, :] = jnp.where(low, outs[0], outs[2])
        ob_ref[sb, :] = jnp.where(low, outs[1], outs[3])
        return tuple(sts)

    st0 = jnp.zeros((LANE, LANE), F32)
    lax.fori_loop(0, nc_all, step, (st0,) * 4)

    def finish(c, _):
        sl = pl.ds(pl.multiple_of(c * size, size), size)
        o = of_ref[sl, :] + ob_ref[sl, :]
        sq = o * o
        ms0 = jnp.sum(jnp.where(low, sq, 0.0), axis=1, keepdims=True) * (1.0 / HG_DV)
        ms1 = jnp.sum(jnp.where(low, 0.0, sq), axis=1, keepdims=True) * (1.0 / HG_DV)
        y = o * lax.rsqrt(jnp.where(low, ms0, ms1) + EPS) * gain_ref[0]
        y_ref[0, sl, :] = (y * g_ref[0, sl, :].astype(F32)).astype(BF16)
        return 0

    lax.fori_loop(0, nc_all, finish, 0)


def _hgrn(hgq, hgf, hgv, hgg, gain, layer, ctx_len):
    nb, n_all, _ = hgq.shape
    size = HG_CHUNK
    nc_all = n_all // size
    ng = HEADS // 2
    kern = functools.partial(_hgrn_kernel, nc_ctx=ctx_len // size, nc_all=nc_all, size=size)
    wide = pl.BlockSpec((1, n_all, 2 * LANE), lambda b, h: (b, 0, h))
    slab = pl.BlockSpec((1, n_all, LANE), lambda b, h: (b, 0, h))
    tri, code, emat = _hg_tables(size)
    return pl.pallas_call(
        kern,
        grid=(nb, ng),
        in_specs=[wide, wide,
                  pl.BlockSpec((1, n_all, 2 * LANE), lambda b, h: (b, 0, ng + h)),
                  slab, slab,
                  pl.BlockSpec((1, 1, LANE), lambda b, h: (layer * ng + h, 0, 0)),
                  pl.BlockSpec(tri.shape, lambda b, h: (0, 0, 0)),
                  pl.BlockSpec(code.shape, lambda b, h: (0, 0, 0)),
                  pl.BlockSpec(emat.shape, lambda b, h: (0, 0))],
        out_specs=slab,
        out_shape=jax.ShapeDtypeStruct((nb, n_all, HG_W), BF16),
        scratch_shapes=[pltpu.VMEM((n_all, LANE), F32), pltpu.VMEM((n_all, LANE), F32)],
        compiler_params=_cparams(("parallel", "arbitrary")),
        name="hgrn2",
    )(hgq, hgf, hgf, hgv, hgg, gain, tri, code, emat)


def _attn_kernel(q_ref, k_ref, vt_ref, lam_ref, gain_ref, y_ref, s0_ref, s1_ref, m0_ref, m1_ref,
                 *, tq, kc, ctx_len, lam_init):
    n_all = k_ref.shape[1]
    n_tiles = (n_all - ctx_len) // tq
    s_refs, m_refs = (s0_ref, s1_ref), (m0_ref, m1_ref)
    lv = lam_ref[0]
    lam = (jnp.exp(jnp.sum(lv[0:1] * lv[1:2], axis=1, keepdims=True))
           - jnp.exp(jnp.sum(lv[2:3] * lv[3:4], axis=1, keepdims=True)) + lam_init)
    lane = lax.broadcasted_iota(jnp.int32, (tq, LANE), 1)
    vrow = lax.broadcasted_iota(jnp.int32, (LANE, kc), 0)

    def rows(t):
        return pl.ds(pl.multiple_of(ctx_len + t * tq, tq), tq)

    def load_qq(sl):
        q = q_ref[0, sl, :]
        zero = jnp.zeros_like(q)
        return jnp.concatenate([jnp.where(lane < DA_HD, q, zero), jnp.where(lane < DA_HD, zero, q)], axis=0)

    def stage_a_chunk(qq, slot, c, m):
        s = _dot_nt(k_ref[0, c * kc:(c + 1) * kc, :], qq)
        s_refs[slot][c * kc:(c + 1) * kc, :] = s
        cm = jnp.max(s, axis=0, keepdims=True)
        return cm if m is None else jnp.maximum(m, cm)

    def stage_b_chunk(slot, c, m, acc):
        p = jnp.exp2(s_refs[slot][c * kc:(c + 1) * kc, :] - m).astype(BF16)
        vt = vt_ref[0, :, c * kc:(c + 1) * kc]
        pv = _dot(jnp.where(vrow == DA_VD, jnp.ones_like(vt), vt), p)
        return pv if acc is None else acc + pv

    def finish(acc, sl):
        r0 = 1.0 / acc[DA_VD:DA_VD + 1, 0:tq]
        r1 = lam / acc[DA_VD:DA_VD + 1, tq:2 * tq]
        o = (acc[:, 0:tq] * r0 - acc[:, tq:2 * tq] * r1).T
        o = jnp.where(lane < DA_VD, o, 0.0)
        y = o * lax.rsqrt(jnp.sum(o * o, axis=1, keepdims=True) * (1.0 / DA_VD) + EPS) * gain_ref[0]
        y_ref[0, sl, :] = (y * (1.0 - lam_init)).astype(BF16)

    def stages(sl_a, sl_b, slot_a, nk_a, nk_b):
        slot_b = 1 - slot_a
        if sl_a is not None:
            qq = load_qq(sl_a)
        if sl_b is not None:
            m_b = m_refs[slot_b][0:1, :]
        m = acc = None
        for c in range(max(nk_a, nk_b) // kc):
            if sl_a is not None and c < nk_a // kc:
                m = stage_a_chunk(qq, slot_a, c, m)
            if sl_b is not None and c < nk_b // kc:
                acc = stage_b_chunk(slot_b, c, m_b, acc)
        if sl_a is not None:
            m_refs[slot_a][...] = jnp.broadcast_to(m, (8, 2 * tq))
        if sl_b is not None:
            finish(acc, sl_b)

    ctx_rows = pl.ds(0, tq)
    stages(ctx_rows, None, 1, ctx_len, 0)
    stages(rows(0), ctx_rows, 0, n_all, ctx_len)

    def body(i, _):
        t = 1 + 2 * i
        stages(rows(t), rows(t - 1), 1, n_all, n_all)
        stages(rows(t + 1), rows(t), 0, n_all, n_all)
        return 0

    lax.fori_loop(0, (n_tiles - 2) // 2, body, 0)
    stages(rows(n_tiles - 1), rows(n_tiles - 2), 1, n_all, n_all)
    stages(None, rows(n_tiles - 1), 0, 0, n_all)


def _attn(daq, dak, dav_t, lam_pad, gain, layer, ctx_len, lam_init):
    nb, n_all, _ = daq.shape
    tq, kc = ATT_TQ, ATT_KC
    n_tiles = (n_all - ctx_len) // tq
    assert n_tiles >= 2 and n_tiles % 2 == 0 and ctx_len == tq and ctx_len % kc == 0 and n_all % kc == 0
    kern = functools.partial(_attn_kernel, tq=tq, kc=kc, ctx_len=ctx_len, lam_init=lam_init)
    slab = pl.BlockSpec((1, n_all, LANE), lambda b, h: (b, 0, h))
    sbuf, mbuf = pltpu.VMEM((n_all, 2 * tq), F32), pltpu.VMEM((8, 2 * tq), F32)
    return pl.pallas_call(
        kern,
        grid=(nb, HEADS),
        in_specs=[slab, slab,
                  pl.BlockSpec((1, LANE, n_all), lambda b, h: (b, h, 0)),
                  pl.BlockSpec((1, 4, LANE), lambda b, h: (layer, 0, 0)),
                  pl.BlockSpec((1, 1, LANE), lambda b, h: (layer, 0, 0))],
        out_specs=slab,
        out_shape=jax.ShapeDtypeStruct((nb, n_all, SLAB), BF16),
        scratch_shapes=[sbuf, sbuf, mbuf, mbuf],
        compiler_params=_cparams(("parallel", "arbitrary")),
        name="diff_attn",
    )(daq, dak, dav_t, lam_pad, gain)


def _resident(shape, index_map):
    return pl.BlockSpec(shape, index_map, pipeline_mode=pl.Buffered(1))


def _mix_ffn_kernel(yml_ref, yhg_ref, yda_ref, x_ref, g1x_ref, g1c_ref, a2x_ref, a2c_ref, s2x_ref, s2c_ref,
                    g2x_ref, g2c_ref, wo_ref, gpm_ref, gpf_ref, wg_ref, wu_ref, wd_ref, gqf_ref, o_ref, *, tm, nsub, ctx_len, t0):
    row0 = (pl.program_id(1) + t0) * tm
    split = min(ctx_len % tm if ctx_len % tm else tm, tm)
    top_is_ctx = row0 < ctx_len

    def mod(c_ref, x_ref_, lo):
        return jnp.where(top_is_ctx, c_ref[0, 0], x_ref_[0, 0]) if lo < split else x_ref_[0, 0]

    sub = tm // nsub
    x1s, h2s = [], []
    for i in range(nsub):
        lo, rs = i * sub, slice(i * sub, (i + 1) * sub)
        y = (_dot(yml_ref[0, rs, :], wo_ref[0, 0:SLAB, :]) + _dot(yhg_ref[0, rs, :], wo_ref[0, SLAB:SLAB + HG_W, :])
             + _dot(yda_ref[0, rs, :], wo_ref[0, SLAB + HG_W:2 * SLAB + HG_W, :]))
        x1 = x_ref[0, rs, :] + _rms(y, mod(g1c_ref, g1x_ref, lo) * gpm_ref[0])
        h2s.append((_rms(x1, gpf_ref[0] * (1.0 + mod(a2c_ref, a2x_ref, lo))) + mod(s2c_ref, s2x_ref, lo)).astype(BF16))
        x1s.append(x1)
    for i in range(nsub):
        lo, rs = i * sub, slice(i * sub, (i + 1) * sub)
        act = _silu(_dot(h2s[i], wg_ref[0])) * _dot(h2s[i], wu_ref[0])
        f = _dot(act.astype(BF16), wd_ref[0])
        o_ref[0, rs, :] = x1s[i] + _rms(f, mod(g2c_ref, g2x_ref, lo) * gqf_ref[0])


def _mix_ffn(yml, yhg, yda, xs, mod4, w_out_pad, g_post_mix, g_pre_ffn, wg, wu, wd, g_post_ffn, layer, ctx_len,
             tm, nsub, t0):
    nb, n_all, d = xs.shape
    dff = wg.shape[-1]
    assert tm % nsub == 0 and (ctx_len % tm) % (tm // nsub) == 0
    kern = functools.partial(_mix_ffn_kernel, tm=tm, nsub=nsub, ctx_len=ctx_len, t0=t0)
    lsel = lambda b, t: (layer, 0, 0)
    tile = lambda width: pl.BlockSpec((1, tm, width), lambda b, t: (b, t + t0, 0))
    return pl.pallas_call(
        kern,
        grid=(nb, n_all // tm - t0),
        in_specs=[tile(SLAB), tile(HG_W), tile(SLAB), tile(d)] + _mod_specs(layer, nb, (2, 4, 3, 5), d) + [
            _resident((1, 2 * SLAB + HG_W, d), lsel),
            pl.BlockSpec((1, 1, d), lsel),
            pl.BlockSpec((1, 1, d), lsel),
            _resident((1, d, dff), lsel),
            _resident((1, d, dff), lsel),
            _resident((1, dff, d), lsel),
            pl.BlockSpec((1, 1, d), lsel)],
        out_specs=pl.BlockSpec((1, tm, d), lambda b, t: (b, t, 0)),
        out_shape=jax.ShapeDtypeStruct((nb, n_all - t0 * tm, d), F32),
        compiler_params=_cparams(("parallel", "arbitrary")),
        name="mix_ffn",
    )(yml, yhg, yda, xs, *([mod4] * 8), w_out_pad, g_post_mix, g_pre_ffn, wg, wu, wd, g_post_ffn)


def _pad_heads(w, hd):
    lead = w.shape[:-1]
    w = w.reshape(*lead, HEADS, hd)
    w = jnp.pad(w, [(0, 0)] * len(lead) + [(0, 0), (0, LANE - hd)])
    return w.reshape(*lead, SLAB)


def _pad_in_cols(w, extra_f=None):
    sizes = (4 * ML_HD,) * 4 + (2 * HEADS, 2 * HEADS, HEADS * HG_DK, 2 * HEADS * HG_DK, HEADS * HG_DV, HEADS * HG_DV,
             2 * HEADS * DA_HD, 2 * HEADS * DA_HD, HEADS * DA_VD)
    parts = []
    off = 0
    for s in sizes:
        parts.append(w[..., off:off + s])
        off += s
    gi, gf = parts[4], parts[5]
    if extra_f is not None:
        gf = gf + extra_f
    gate = jnp.concatenate([gi, gf], axis=-1)
    gate = jnp.pad(gate, [(0, 0)] * (gate.ndim - 1) + [(0, LANE - 4 * HEADS)])
    return jnp.concatenate([
        _pad_heads(parts[0], ML_HD), _pad_heads(parts[1], ML_HD), _pad_heads(parts[2], ML_HD), _pad_heads(parts[3], ML_HD),
        gate, parts[6], parts[7], parts[8], parts[9],
        _pad_heads(parts[10], DA_VD), _pad_heads(parts[11], DA_VD), _pad_heads(parts[12], DA_VD)], axis=-1)


def _pad_out_rows(w_out):
    depth, _, d = w_out.shape
    wt = jnp.swapaxes(w_out, 1, 2)
    ml, hg, da = wt[..., :4 * ML_HD], wt[..., 4 * ML_HD:4 * ML_HD + HEADS * HG_DV], wt[..., 4 * ML_HD + HEADS * HG_DV:]
    wp = jnp.concatenate([_pad_heads(ml, ML_HD), hg, _pad_heads(da, DA_VD)], axis=-1)
    return jnp.swapaxes(wp, 1, 2)


def _rope_tables(n_lat, ctx_len):
    rows = n_lat // GRID_W
    row = jnp.repeat(jnp.arange(rows), GRID_W).astype(F32)
    col = jnp.tile(jnp.arange(GRID_W), rows).astype(F32)
    half = DA_HD // 2
    inv = ROPE_BASE ** (-jnp.arange(0, half, 2, dtype=F32) / half)
    ang_r, ang_c = row[:, None] * inv, col[:, None] * inv
    zero = jnp.zeros_like(ang_r)

    def one_map(fr, fc, sel):
        r, c = fr(ang_r), fc(ang_c)
        if sel == "cos":
            return jnp.concatenate([r, r, c, c], axis=1)
        if sel == "a":
            return jnp.concatenate([-r, zero, -c, zero], axis=1)
        return jnp.concatenate([zero, r, zero, c], axis=1)

    tabs = []
    for sel, fn in (("cos", jnp.cos), ("a", jnp.sin), ("b", jnp.sin)):
        m = one_map(fn, fn, sel)
        lat = jnp.pad(jnp.concatenate([m, m], axis=1), ((0, 0), (0, LANE - 2 * DA_HD)))
        fill = 1.0 if sel == "cos" else 0.0
        ctx_rows = jnp.pad(jnp.full((ctx_len, 2 * DA_HD), fill, F32), ((0, 0), (0, LANE - 2 * DA_HD)))
        tabs.append(jnp.concatenate([ctx_rows, lat], axis=0))
    return tabs


def kernel(x, c, ctx, c_ctx, w_ada, b_ada, g_pre_mix, g_post_mix, g_pre_ffn, g_post_ffn, w_in, b_in, w_out,
           ml_f_bias, ml_norm, hg_lb, hg_norm, da_lambda, da_norm, w_ffn_gate, w_ffn_up, w_ffn_down):
    nb, n_lat, d = x.shape
    ctx_len = ctx.shape[1]
    depth = w_ada.shape[0]
    n_all = ctx_len + n_lat

    w_pad = _pad_in_cols(w_in).astype(BF16)
    b_pad = _pad_in_cols(b_in, extra_f=ml_f_bias)[:, None, :]
    w_out_pad = _pad_out_rows(w_out).astype(BF16)
    wg, wu, wd = w_ffn_gate.astype(BF16), w_ffn_up.astype(BF16), w_ffn_down.astype(BF16)
    ml_gain = _pad_heads(ml_norm, ML_HD).reshape(depth * HEADS, 1, LANE)
    hg_gain = hg_norm.reshape(-1, 1, LANE)
    da_gain = jnp.pad(da_norm, ((0, 0), (0, LANE - DA_VD)))[:, None, :]
    lam_pad = jnp.pad(da_lambda.astype(F32), ((0, 0), (0, 0), (0, LANE - DA_HD)))
    sm = jax.nn.softmax(hg_lb.astype(F32), axis=0)
    lbs = jnp.cumsum(sm, axis=0) - sm[0:1]
    lbf = jnp.maximum(lbs, LB_FLOOR)[:, None, :]
    oml = (1.0 - lbs)[:, None, :]
    rope_tabs = _rope_tables(n_lat, ctx_len)
    r3 = lambda g: g[:, None, :]

    mp = -(-(nb + 1) // 8) * 8
    cc = jnp.concatenate([c, c_ctx[None, :], jnp.zeros((mp - nb - 1, d), F32)], axis=0)
    mod4 = _ada(cc, w_ada, b_ada).reshape(depth, mp, 1, 6 * d)

    xs = jnp.concatenate([ctx, x], axis=1)
    for l in range(depth):
        lam_init = 0.8 - 0.6 * math.exp(-0.3 * l)
        (mlq, mlk, mlv, mlo, gates, hgq, hgf, hgv, hgg, daq, dak, dav_t) = _inproj(
            xs, mod4, r3(g_pre_mix), w_pad, b_pad, rope_tabs, lbf, oml, l, ctx_len, IN_TM if n_all % IN_TM == 0 else 256)
        gates_t = jnp.swapaxes(gates[:, :, :4 * HEADS], 1, 2).reshape(nb, 4 * HEADS, n_all // ML_CHUNK, ML_CHUNK)
        yml = _mlstm(mlq, mlk, mlv, mlo, gates_t, ml_gain, l, ctx_len)
        yhg = _hgrn(hgq, hgf, hgv, hgg, hg_gain, l, ctx_len)
        yda = _attn(daq, dak, dav_t, lam_pad, da_gain, l, ctx_len, lam_init)
        last = l == depth - 1
        tm, nsub = (MIX_TM, MIX_SUB) if not last and n_all % MIX_TM == 0 else (MIX_TM_LAST, MIX_SUB_LAST)
        xs = _mix_ffn(yml, yhg, yda, xs, mod4, w_out_pad, r3(g_post_mix), r3(g_pre_ffn), wg, wu, wd,
                      r3(g_post_ffn), l, ctx_len, tm, nsub, ctx_len // tm if last else 0)
    return xs
```

```python
import functools
import math

import numpy as np
import jax
import jax.numpy as jnp
from jax import lax
from jax.experimental import pallas as pl
from jax.experimental.pallas import tpu as pltpu

F32 = jnp.float32
BF16 = jnp.bfloat16
HIGHEST = lax.Precision.HIGHEST

LANE = 128
VMEM_LIMIT = 52 * 1024 * 1024

EPS = 1e-6
NEG = -1e30
LB_FLOOR = 1e-30
M_INIT = -1e30
GRID_W = 64
ROPE_BASE = 10000.0

HEADS = 4
ML_HD = 96
HG_DV = 64
HG_DK = 128
DA_HD = 48
DA_VD = 2 * DA_HD
SLAB = HEADS * LANE
HG_W = HEADS * HG_DV

ML_CHUNK = 256
ML_HPS = 2
HG_CHUNK = 128
HG_DIAG = 8
ATT_TQ = 256
IN_CW = 256
IN_TM, IN_SUB = 768, 3
IN_TM_ALT, IN_SUB_ALT = 256, 2
MIX_TM, MIX_SUB = 768, 3
MIX_TM_LAST, MIX_SUB_LAST = 256, 2
ATT_KC = 256
LOG2E = 1.4426950408889634

OFF_MLQ, OFF_MLK, OFF_MLV, OFF_MLO = 0, SLAB, 2 * SLAB, 3 * SLAB
OFF_GATE = 4 * SLAB
OFF_HGQ = OFF_GATE + LANE
OFF_HGF = OFF_HGQ + SLAB
OFF_HGV = OFF_HGF + 2 * SLAB
OFF_HGG = OFF_HGV + HG_W
OFF_DAQ = OFF_HGG + HG_W
OFF_DAK = OFF_DAQ + SLAB
OFF_DAV = OFF_DAK + SLAB
NP_IN = OFF_DAV + SLAB


def _cparams(sem):
    return pltpu.CompilerParams(dimension_semantics=sem, vmem_limit_bytes=VMEM_LIMIT)


def _silu(x):
    return x * jax.nn.sigmoid(x)


def _log_sigmoid(z):
    return jnp.minimum(z, 0.0) - jnp.log1p(jnp.exp(-jnp.abs(z)))


def _neg_abs(x):
    bits = lax.bitcast_convert_type(x, jnp.uint32) | jnp.uint32(0x80000000)
    return lax.bitcast_convert_type(bits, F32)


def _rms(x, g):
    return x * lax.rsqrt(jnp.mean(x * x, axis=-1, keepdims=True) + EPS) * g


def _dot(a, b):
    return jnp.dot(a, b, preferred_element_type=F32)


def _dot_nt(a, b):
    return lax.dot_general(a, b, (((1,), (1,)), ((), ())), preferred_element_type=F32)


def _dot_tn(a, b):
    return lax.dot_general(a, b, (((0,), (0,)), ((), ())), preferred_element_type=F32)


def _ada_kernel(s_ref, w_ref, b_ref, o_ref):
    s = _silu(s_ref[...])
    o_ref[0] = jnp.dot(s, w_ref[0], precision=HIGHEST, preferred_element_type=F32) + b_ref[0]


def _ada(cc, w_ada, b_ada):
    depth, d, d6 = w_ada.shape
    mp = cc.shape[0]
    tn = 1024
    return pl.pallas_call(
        _ada_kernel,
        grid=(depth, d6 // tn),
        in_specs=[pl.BlockSpec((mp, d), lambda l, j: (0, 0)),
                  pl.BlockSpec((1, d, tn), lambda l, j: (l, 0, j)),
                  pl.BlockSpec((1, 1, tn), lambda l, j: (l, 0, j))],
        out_specs=pl.BlockSpec((1, mp, tn), lambda l, j: (l, 0, j)),
        out_shape=jax.ShapeDtypeStruct((depth, mp, d6), F32),
        compiler_params=_cparams(("arbitrary", "arbitrary")),
        name="ada_mod",
    )(cc, w_ada, b_ada.reshape(depth, 1, d6))


def _mod_specs(layer, nb, idxs, d):
    specs = []
    for j in idxs:
        specs.append(pl.BlockSpec((1, 1, 1, d), lambda b, t, j=j: (layer, b, 0, j)))
        specs.append(pl.BlockSpec((1, 1, 1, d), lambda b, t, j=j: (layer, nb, 0, j)))
    return specs


def _inproj_kernel(x_ref, sx_ref, sc_ref, ax_ref, ac_ref, g_ref, w_ref, b_ref,
                   cos_ref, sna_ref, snb_ref, lbf_ref, oml_ref,
                   mlq_ref, mlk_ref, mlv_ref, mlo_ref, gate_ref,
                   hgq_ref, hgf_ref, hgv_ref, hgg_ref, daq_ref, dak_ref, dav_ref,
                   *, tm, nsub, ctx_len):
    split = min(ctx_len % tm if ctx_len % tm else tm, tm)
    top_is_ctx = pl.program_id(1) * tm < ctx_len
    sub = tm // nsub
    half = DA_HD // 4
    lane = lax.broadcasted_iota(jnp.int32, (sub, LANE), 1)

    def rope(p, rs):
        cos, sna, snb = cos_ref[rs, :], sna_ref[rs, :], snb_ref[rs, :]
        outs = []
        for i in range(p.shape[1] // LANE):
            xh = p[:, i * LANE:(i + 1) * LANE]
            outs.append(xh * cos + pltpu.roll(xh, LANE - half, 1) * sna + pltpu.roll(xh, half, 1) * snb)
        return jnp.concatenate(outs, axis=1)

    def log2_f(p, c0, rs):
        w = p.shape[1]
        return jnp.log2(lbf_ref[0, :, c0:c0 + w] + oml_ref[0, :, c0:c0 + w] * jax.nn.sigmoid(p))

    def gates(p, c0, rs):
        return jnp.where((lane >= 2 * HEADS) & (lane < 4 * HEADS), _log_sigmoid(p), p)

    plain = lambda p, c0, rs: p
    groups = {
        "mlq": (mlq_ref, OFF_MLQ, SLAB, plain),
        "mlk": (mlk_ref, OFF_MLK, SLAB, lambda p, c0, rs: p * (ML_HD ** -0.5)),
        "mlv": (mlv_ref, OFF_MLV, SLAB, plain),
        "mlo": (mlo_ref, OFF_MLO, SLAB, lambda p, c0, rs: jax.nn.sigmoid(p)),
        "gate": (gate_ref, OFF_GATE, LANE, gates),
        "hgq": (hgq_ref, OFF_HGQ, SLAB, lambda p, c0, rs: _silu(p)),
        "hgf": (hgf_ref, OFF_HGF, 2 * SLAB, log2_f),
        "hgv": (hgv_ref, OFF_HGV, HG_W, plain),
        "hgg": (hgg_ref, OFF_HGG, HG_W, lambda p, c0, rs: _silu(p)),
        "daq": (daq_ref, OFF_DAQ, SLAB, lambda p, c0, rs: rope(p, rs) * (LOG2E * DA_HD ** -0.5)),
        "dak": (dak_ref, OFF_DAK, SLAB, lambda p, c0, rs: rope(p, rs)),
        "dav": (dav_ref, OFF_DAV, SLAB, plain),
    }
    order = [("hgf", 0), ("mlq", 0), ("hgf", 1), ("mlq", 1), ("hgf", 2), ("mlk", 0), ("hgf", 3), ("mlk", 1),
             ("mlo", 0), ("mlv", 0), ("mlo", 1), ("mlv", 1), ("hgq", 0), ("dav", 0), ("hgq", 1), ("dav", 1),
             ("daq", 0), ("hgv", 0), ("daq", 1), ("gate", 0), ("dak", 0), ("hgg", 0), ("dak", 1)]
    for i in range(nsub):
        rs = slice(i * sub, (i + 1) * sub)
        ctx_rows = i * sub < split
        shift = jnp.where(top_is_ctx, sc_ref[0, 0], sx_ref[0, 0]) if ctx_rows else sx_ref[0, 0]
        scale = 1.0 + (jnp.where(top_is_ctx, ac_ref[0, 0], ax_ref[0, 0]) if ctx_rows else ax_ref[0, 0])
        h = (_rms(x_ref[0, rs, :], g_ref[0] * scale) + shift).astype(BF16)
        for name, ci in order:
            ref, off, width, fn = groups[name]
            cw = min(width, IN_CW)
            c0 = ci * cw
            p = _dot(h, w_ref[0, :, off + c0:off + c0 + cw]) + b_ref[0, :, off + c0:off + c0 + cw]
            if name == "dav":
                ref[0, c0:c0 + cw, rs] = p.T.astype(ref.dtype)
            else:
                ref[0, rs, c0:c0 + cw] = fn(p, c0, rs).astype(ref.dtype)


def _inproj(xs, mod4, g_pre, w_pad, b_pad, rope_tabs, lbf, oml, layer, ctx_len, tm, nsub):
    nb, n_all, d = xs.shape
    assert tm % nsub == 0 and (ctx_len % tm) % (tm // nsub) == 0
    kern = functools.partial(_inproj_kernel, tm=tm, nsub=nsub, ctx_len=ctx_len)
    lsel = lambda b, t: (layer, 0, 0)
    tile = lambda width: pl.BlockSpec((1, tm, width), lambda b, t: (b, t, 0))
    tab = pl.BlockSpec((tm, LANE), lambda b, t: (t, 0))
    bf = lambda width: jax.ShapeDtypeStruct((nb, n_all, width), BF16)
    f32 = lambda width: jax.ShapeDtypeStruct((nb, n_all, width), F32)
    out_shapes = [bf(SLAB)] * 4 + [f32(LANE)] + [bf(SLAB), f32(2 * SLAB), bf(HG_W), bf(HG_W)] + [bf(SLAB)] * 3
    out_specs = [tile(s.shape[-1]) for s in out_shapes]
    out_shapes[-1] = jax.ShapeDtypeStruct((nb, SLAB, n_all), BF16)
    out_specs[-1] = pl.BlockSpec((1, SLAB, tm), lambda b, t: (b, 0, t))
    return pl.pallas_call(
        kern,
        grid=(nb, n_all // tm),
        in_specs=[tile(d)] + _mod_specs(layer, nb, (0, 1), d) + [
            pl.BlockSpec((1, 1, d), lsel),
            pl.BlockSpec((1, d, NP_IN), lsel, pipeline_mode=pl.Buffered(1)),
            pl.BlockSpec((1, 1, NP_IN), lsel),
            tab, tab, tab,
            pl.BlockSpec((1, 1, 2 * SLAB), lsel),
            pl.BlockSpec((1, 1, 2 * SLAB), lsel)],
        out_specs=out_specs,
        out_shape=out_shapes,
        compiler_params=_cparams(("parallel", "arbitrary")),
        name="in_proj",
    )(xs, mod4, mod4, mod4, mod4, g_pre, w_pad, b_pad, *rope_tabs, lbf, oml)


def _ml_chunks(qs, ks, vs, i_rows, lf_rows, s_exts, ms, revs):
    n = len(qs)
    size = qs[0].shape[0]
    ii = lax.broadcasted_iota(jnp.int32, (size, size), 0)
    jj = lax.broadcasted_iota(jnp.int32, (size, size), 1)
    lane = lax.broadcasted_iota(jnp.int32, (size, LANE), 1)
    cums = {rev: jnp.where((ii >= jj) if rev else (ii <= jj), 1.0, 0.0).astype(BF16) for rev in set(revs)}
    seens = {rev: (jj >= ii) if rev else (jj <= ii) for rev in set(revs)}
    row16 = lax.broadcasted_iota(jnp.int32, (16, size), 0)

    b_cols, g_cols, g_rows = [], [], []
    for i in range(n):
        lf2 = lf_rows[i] * LOG2E
        hi = lf2.astype(BF16).astype(F32)
        mid = (lf2 - hi).astype(BF16).astype(F32)
        lo = (lf2 - hi) - mid
        parts = jnp.where(row16 == 0, hi, jnp.where(row16 == 1, mid, jnp.where(row16 == 2, lo, 0.0))).astype(BF16)
        cs = _dot(parts, cums[revs[i]])
        b_row = (cs[0:1] + cs[1:2]) + cs[2:3]
        g_row = i_rows[i] * LOG2E - b_row
        b_cols.append(jnp.sum(jnp.where(ii == jj, b_row, 0.0), axis=1, keepdims=True))
        g_cols.append(jnp.sum(jnp.where(ii == jj, g_row, 0.0), axis=1, keepdims=True))
        g_rows.append(g_row)

    mts, dws, w_inters = [], [], []
    for i in range(n):
        dmat = jnp.where(seens[revs[i]], b_cols[i] + g_rows[i], NEG)
        a_col = b_cols[i] + ms[i]
        mt = jnp.maximum(a_col, jnp.max(dmat, axis=1, keepdims=True))
        mts.append(mt)
        w_inters.append(jnp.exp2(a_col - mt))
        dws.append(jnp.exp2(dmat - mt))

    ress = []
    for i in range(n):
        qk = _dot_nt(qs[i], ks[i]) * dws[i]
        ress.append(_dot(qk.astype(BF16), vs[i]) + w_inters[i] * _dot(qs[i], s_exts[i].astype(BF16)))

    houts, s_news, m_news = [], [], []
    for i in range(n):
        den = jnp.sum(jnp.where(lane == ML_HD, ress[i], 0.0), axis=1, keepdims=True)
        houts.append(ress[i] * (1.0 / jnp.maximum(jnp.abs(den), jnp.exp2(-mts[i]))))
        last = 0 if revs[i] else size - 1
        m_new = mts[i][last:last + 1]
        b_last = b_cols[i][last:last + 1]
        decay = jnp.exp2(b_last + ms[i] - m_new)
        wk = jnp.exp2(b_last + g_cols[i] - m_new)
        s_news.append(decay * s_exts[i] + _dot_tn((wk * ks[i].astype(F32)).astype(BF16), vs[i]))
        m_news.append(m_new)
    return houts, s_news, m_news


def _mlstm_kernel(q_ref, k_ref, v_ref, o_ref, gt_ref, gain_ref, y_ref, hf_ref, hb_ref, *, nc_ctx, nc_all, size, nh):
    hd0 = pl.program_id(1) * nh
    lane = lax.broadcasted_iota(jnp.int32, (size, LANE), 1)
    nc_lat = nc_all - nc_ctx

    def step(c, carry):
        cb = jnp.where(c < nc_ctx, nc_ctx - 1 - c, nc_ctx + nc_lat - 1 - (c - nc_ctx))
        sf = pl.ds(pl.multiple_of(c * size, size), size)
        sb = pl.ds(pl.multiple_of(cb * size, size), size)
        qs, ks, vs, i_rows, lf_rows, revs = [], [], [], [], [], []
        for hh in range(nh):
            ln = slice(hh * LANE, (hh + 1) * LANE)
            for sl, cc, off in ((sf, c, 0), (sb, cb, HEADS)):
                v = v_ref[0, sl, ln]
                qs.append(q_ref[0, sl, ln])
                ks.append(k_ref[0, sl, ln])
                vs.append(jnp.where(lane == ML_HD, jnp.ones_like(v), v))
                i_rows.append(gt_ref[0, off + hd0 + hh, pl.ds(cc, 1), :])
                lf_rows.append(gt_ref[0, 2 * HEADS + off + hd0 + hh, pl.ds(cc, 1), :])
                revs.append(off > 0)
        houts, s_news, m_news = _ml_chunks(qs, ks, vs, i_rows, lf_rows, list(carry[0]), list(carry[1]), revs)
        for hh in range(nh):
            ln = slice(hh * LANE, (hh + 1) * LANE)
            hf_ref[sf, ln] = houts[2 * hh]
            hb_ref[sb, ln] = houts[2 * hh + 1]
        return tuple(s_news), tuple(m_news)

    s0 = jnp.zeros((LANE, LANE), F32)
    m0 = jnp.full((1, 1), M_INIT, F32)
    lax.fori_loop(0, nc_all, step, ((s0,) * (2 * nh), (m0,) * (2 * nh)), unroll=3)

    def finish(c, _):
        sl = pl.ds(pl.multiple_of(c * size, size), size)
        for hh in range(nh):
            ln = slice(hh * LANE, (hh + 1) * LANE)
            h = jnp.where(lane < ML_HD, hf_ref[sl, ln] + hb_ref[sl, ln], 0.0)
            y = h * lax.rsqrt(jnp.sum(h * h, axis=1, keepdims=True) * (1.0 / ML_HD) + EPS) * gain_ref[0, :, ln]
            y_ref[0, sl, ln] = (y * o_ref[0, sl, ln].astype(F32)).astype(BF16)
        return 0

    lax.fori_loop(0, nc_all, finish, 0, unroll=3)


def _mlstm(mlq, mlk, mlv, mlo, gates_t, gain, layer, ctx_len):
    nb, n_all, _ = mlq.shape
    size, nh = ML_CHUNK, ML_HPS
    nc_all = n_all // size
    width = nh * LANE
    ng = HEADS // nh
    kern = functools.partial(_mlstm_kernel, nc_ctx=ctx_len // size, nc_all=nc_all, size=size, nh=nh)
    slab = pl.BlockSpec((1, n_all, width), lambda b, h: (b, 0, h))
    return pl.pallas_call(
        kern,
        grid=(nb, ng),
        in_specs=[slab, slab, slab, slab,
                  pl.BlockSpec((1, 4 * HEADS, nc_all, size), lambda b, h: (b, 0, 0, 0)),
                  pl.BlockSpec((1, 1, width), lambda b, h: (layer * ng + h, 0, 0))],
        out_specs=slab,
        out_shape=jax.ShapeDtypeStruct((nb, n_all, SLAB), BF16),
        scratch_shapes=[pltpu.VMEM((n_all, width), F32), pltpu.VMEM((n_all, width), F32)],
        compiler_params=_cparams(("parallel", "arbitrary")),
        name="mlstm",
    )(mlq, mlk, mlv, mlo, gates_t, gain.reshape(-1, 1, width))


def _hg_tables(size):
    t = np.arange(size)
    tri = (t[None, :] <= t[:, None]).astype(np.float32)
    x = t[:, None] ^ t[None, :]
    lvl = np.where(x < HG_DIAG, 0, np.floor(np.log2(np.maximum(x, 1))).astype(np.int64) - 2)
    code = np.where(t[None, :] <= t[:, None], lvl, -1).astype(np.int32)
    j = np.arange(HG_DIAG * LANE) // LANE
    emat = (np.arange(size)[None, :] % HG_DIAG == j[:, None]).astype(np.float32)
    return (jnp.asarray(np.stack([tri, tri.T]), BF16), jnp.asarray(np.stack([code, code.T])),
            jnp.asarray(emat, BF16))


def _hg_chunks(qs, lfs, vs, sts, tris, codes, emat, revs):
    n = len(qs)
    size = qs[0].shape[0]
    bc2s, kks = [], []
    for i in range(n):
        lf2 = lfs[i]
        hi = lf2.astype(BF16)
        r1 = lf2 - hi.astype(F32)
        mid = r1.astype(BF16)
        lo = (r1 - mid.astype(F32)).astype(BF16)
        cs = _dot(tris[i], jnp.concatenate([hi, mid, lo], axis=1))
        bc2s.append((cs[:, 0:LANE] + cs[:, LANE:2 * LANE]) + cs[:, 2 * LANE:3 * LANE])
        kks.append(jnp.maximum(1.0 - jnp.exp2(lf2), 0.0))

    atts = [jnp.zeros((size, size), F32) for _ in range(n)]
    c, lvl = HG_DIAG, 1
    while c < size:
        blk = 2 * c
        for i in range(n):
            ridx = c if revs[i] else c - 1
            b3 = bc2s[i].reshape(size // blk, blk, LANE)
            ref = jnp.broadcast_to(b3[:, ridx:ridx + 1, :], b3.shape).reshape(size, LANE)
            e = jnp.exp2(_neg_abs(bc2s[i] - ref))
            a = _dot_nt((qs[i] * e).astype(BF16), (kks[i] * e).astype(BF16))
            atts[i] = jnp.where(codes[i] == lvl, a, atts[i])
        c, lvl = blk, lvl + 1

    for i in range(n):
        w3 = (bc2s[i] - jnp.log2(kks[i])).reshape(size // HG_DIAG, HG_DIAG, LANE)
        ps = []
        for j in range(HG_DIAG):
            wj = jnp.broadcast_to(w3[:, j:j + 1, :], w3.shape).reshape(size, LANE)
            ps.append((qs[i] * jnp.exp2(jnp.minimum(bc2s[i] - wj, 0.0))).astype(BF16))
        atts[i] = jnp.where(codes[i] == 0, _dot(jnp.concatenate(ps, axis=1), emat), atts[i])

    outs, new_sts = [], []
    for i in range(n):
        last = 0 if revs[i] else size - 1
        bl = bc2s[i][last:last + 1]
        outs.append(_dot(atts[i].astype(BF16), vs[i])
                    + _dot_nt((qs[i] * jnp.exp2(bc2s[i])).astype(BF16), sts[i].astype(BF16)))
        new_sts.append(sts[i] * jnp.exp2(bl) + _dot_tn(vs[i], (kks[i] * jnp.exp2(bl - bc2s[i])).astype(BF16)))
    return outs, new_sts


def _hgrn_kernel(q_ref, lf0_ref, lf1_ref, v_ref, g_ref, gain_ref, tri_ref, code_ref, emat_ref,
                 y_ref, of_ref, ob_ref, *, nc_ctx, nc_all, size):
    nc_lat = nc_all - nc_ctx
    low = lax.broadcasted_iota(jnp.int32, (size, LANE), 1) < HG_DV

    def step(c, carry):
        cb = jnp.where(c < nc_ctx, nc_ctx - 1 - c, nc_ctx + nc_lat - 1 - (c - nc_ctx))
        sf = pl.ds(pl.multiple_of(c * size, size), size)
        sb = pl.ds(pl.multiple_of(cb * size, size), size)
        qs, lfs, vs, tris, codes, revs = [], [], [], [], [], []
        for hh in range(2):
            ln = slice(hh * LANE, (hh + 1) * LANE)
            qs += [q_ref[0, sf, ln].astype(F32), q_ref[0, sb, ln].astype(F32)]
            lfs += [lf0_ref[0, sf, ln], lf1_ref[0, sb, ln]]
            vs += [v_ref[0, sf, :], v_ref[0, sb, :]]
            tris += [tri_ref[0], tri_ref[1]]
            codes += [code_ref[0], code_ref[1]]
            revs += [False, True]
        outs, sts = _hg_chunks(qs, lfs, vs, list(carry), tris, codes, emat_ref[...], revs)
        of_ref[sf, :] = jnp.where(low, outs[0], outs[2])
        ob_ref[sb, :] = jnp.where(low, outs[1], outs[3])
        return tuple(sts)

    st0 = jnp.zeros((LANE, LANE), F32)
    lax.fori_loop(0, nc_all, step, (st0,) * 4, unroll=2)

    def finish(c, _):
        sl = pl.ds(pl.multiple_of(c * size, size), size)
        o = of_ref[sl, :] + ob_ref[sl, :]
        sq = o * o
        ms0 = jnp.sum(jnp.where(low, sq, 0.0), axis=1, keepdims=True) * (1.0 / HG_DV)
        ms1 = jnp.sum(jnp.where(low, 0.0, sq), axis=1, keepdims=True) * (1.0 / HG_DV)
        y = o * lax.rsqrt(jnp.where(low, ms0, ms1) + EPS) * gain_ref[0]
        y_ref[0, sl, :] = (y * g_ref[0, sl, :].astype(F32)).astype(BF16)
        return 0

    lax.fori_loop(0, nc_all, finish, 0, unroll=6)


def _hgrn(hgq, hgf, hgv, hgg, gain, layer, ctx_len):
    nb, n_all, _ = hgq.shape
    size = HG_CHUNK
    nc_all = n_all // size
    ng = HEADS // 2
    kern = functools.partial(_hgrn_kernel, nc_ctx=ctx_len // size, nc_all=nc_all, size=size)
    wide = pl.BlockSpec((1, n_all, 2 * LANE), lambda b, h: (b, 0, h))
    slab = pl.BlockSpec((1, n_all, LANE), lambda b, h: (b, 0, h))
    tri, code, emat = _hg_tables(size)
    return pl.pallas_call(
        kern,
        grid=(nb, ng),
        in_specs=[wide, wide,
                  pl.BlockSpec((1, n_all, 2 * LANE), lambda b, h: (b, 0, ng + h)),
                  slab, slab,
                  pl.BlockSpec((1, 1, LANE), lambda b, h: (layer * ng + h, 0, 0)),
                  pl.BlockSpec(tri.shape, lambda b, h: (0, 0, 0)),
                  pl.BlockSpec(code.shape, lambda b, h: (0, 0, 0)),
                  pl.BlockSpec(emat.shape, lambda b, h: (0, 0))],
        out_specs=slab,
        out_shape=jax.ShapeDtypeStruct((nb, n_all, HG_W), BF16),
        scratch_shapes=[pltpu.VMEM((n_all, LANE), F32), pltpu.VMEM((n_all, LANE), F32)],
        compiler_params=_cparams(("parallel", "arbitrary")),
        name="hgrn2",
    )(hgq, hgf, hgf, hgv, hgg, gain, tri, code, emat)


def _attn_kernel(q_ref, k_ref, vt_ref, lam_ref, gain_ref, y_ref, s0_ref, s1_ref, m0_ref, m1_ref,
                 *, tq, kc, ctx_len, lam_init):
    n_all = k_ref.shape[1]
    n_tiles = (n_all - ctx_len) // tq
    s_refs, m_refs = (s0_ref, s1_ref), (m0_ref, m1_ref)
    lv = lam_ref[0]
    lam = (jnp.exp(jnp.sum(lv[0:1] * lv[1:2], axis=1, keepdims=True))
           - jnp.exp(jnp.sum(lv[2:3] * lv[3:4], axis=1, keepdims=True)) + lam_init)
    lane = lax.broadcasted_iota(jnp.int32, (tq, LANE), 1)
    vrow = lax.broadcasted_iota(jnp.int32, (LANE, kc), 0)

    def rows(t):
        return pl.ds(pl.multiple_of(ctx_len + t * tq, tq), tq)

    def load_qq(sl):
        q = q_ref[0, sl, :]
        zero = jnp.zeros_like(q)
        return jnp.concatenate([jnp.where(lane < DA_HD, q, zero), jnp.where(lane < DA_HD, zero, q)], axis=0)

    def stage_a_chunk(qq, slot, c, m):
        s = _dot_nt(k_ref[0, c * kc:(c + 1) * kc, :], qq)
        s_refs[slot][c * kc:(c + 1) * kc, :] = s
        cm = jnp.max(s, axis=0, keepdims=True)
        return cm if m is None else jnp.maximum(m, cm)

    def stage_b_chunk(slot, c, m, acc):
        p = jnp.exp2(s_refs[slot][c * kc:(c + 1) * kc, :] - m).astype(BF16)
        vt = vt_ref[0, :, c * kc:(c + 1) * kc]
        pv = _dot(jnp.where(vrow == DA_VD, jnp.ones_like(vt), vt), p)
        return pv if acc is None else acc + pv

    def finish(acc, sl):
        r0 = 1.0 / acc[DA_VD:DA_VD + 1, 0:tq]
        r1 = lam / acc[DA_VD:DA_VD + 1, tq:2 * tq]
        o = (acc[:, 0:tq] * r0 - acc[:, tq:2 * tq] * r1).T
        o = jnp.where(lane < DA_VD, o, 0.0)
        y = o * lax.rsqrt(jnp.sum(o * o, axis=1, keepdims=True) * (1.0 / DA_VD) + EPS) * gain_ref[0]
        y_ref[0, sl, :] = (y * (1.0 - lam_init)).astype(BF16)

    def stages(sl_a, sl_b, slot_a, nk_a, nk_b):
        slot_b = 1 - slot_a
        if sl_a is not None:
            qq = load_qq(sl_a)
        if sl_b is not None:
            m_b = m_refs[slot_b][0:1, :]
        m = acc = None
        for c in range(max(nk_a, nk_b) // kc):
            if sl_a is not None and c < nk_a // kc:
                m = stage_a_chunk(qq, slot_a, c, m)
            if sl_b is not None and c < nk_b // kc:
                acc = stage_b_chunk(slot_b, c, m_b, acc)
        if sl_a is not None:
            m_refs[slot_a][...] = jnp.broadcast_to(m, (8, 2 * tq))
        if sl_b is not None:
            finish(acc, sl_b)

    ctx_rows = pl.ds(0, tq)
    stages(ctx_rows, None, 1, ctx_len, 0)
    stages(rows(0), ctx_rows, 0, n_all, ctx_len)

    def body(i, _):
        t = 1 + 2 * i
        stages(rows(t), rows(t - 1), 1, n_all, n_all)
        stages(rows(t + 1), rows(t), 0, n_all, n_all)
        return 0

    lax.fori_loop(0, (n_tiles - 2) // 2, body, 0)
    stages(rows(n_tiles - 1), rows(n_tiles - 2), 1, n_all, n_all)
    stages(None, rows(n_tiles - 1), 0, 0, n_all)


def _attn(daq, dak, dav_t, lam_pad, gain, layer, ctx_len, lam_init):
    nb, n_all, _ = daq.shape
    tq, kc = ATT_TQ, ATT_KC
    n_tiles = (n_all - ctx_len) // tq
    assert n_tiles >= 2 and n_tiles % 2 == 0 and ctx_len == tq and ctx_len % kc == 0 and n_all % kc == 0
    kern = functools.partial(_attn_kernel, tq=tq, kc=kc, ctx_len=ctx_len, lam_init=lam_init)
    slab = pl.BlockSpec((1, n_all, LANE), lambda b, h: (b, 0, h))
    sbuf, mbuf = pltpu.VMEM((n_all, 2 * tq), F32), pltpu.VMEM((8, 2 * tq), F32)
    return pl.pallas_call(
        kern,
        grid=(nb, HEADS),
        in_specs=[slab, slab,
                  pl.BlockSpec((1, LANE, n_all), lambda b, h: (b, h, 0)),
                  pl.BlockSpec((1, 4, LANE), lambda b, h: (layer, 0, 0)),
                  pl.BlockSpec((1, 1, LANE), lambda b, h: (layer, 0, 0))],
        out_specs=slab,
        out_shape=jax.ShapeDtypeStruct((nb, n_all, SLAB), BF16),
        scratch_shapes=[sbuf, sbuf, mbuf, mbuf],
        compiler_params=_cparams(("parallel", "arbitrary")),
        name="diff_attn",
    )(daq, dak, dav_t, lam_pad, gain)


def _resident(shape, index_map):
    return pl.BlockSpec(shape, index_map, pipeline_mode=pl.Buffered(1))


def _mix_ffn_kernel(yml_ref, yhg_ref, yda_ref, x_ref, g1x_ref, g1c_ref, a2x_ref, a2c_ref, s2x_ref, s2c_ref,
                    g2x_ref, g2c_ref, wo_ref, gpm_ref, gpf_ref, wg_ref, wu_ref, wd_ref, gqf_ref, o_ref, *, tm, nsub, ctx_len, t0):
    row0 = (pl.program_id(1) + t0) * tm
    split = min(ctx_len % tm if ctx_len % tm else tm, tm)
    top_is_ctx = row0 < ctx_len

    def mod(c_ref, x_ref_, lo):
        return jnp.where(top_is_ctx, c_ref[0, 0], x_ref_[0, 0]) if lo < split else x_ref_[0, 0]

    sub = tm // nsub
    x1s, h2s = [], []
    for i in range(nsub):
        lo, rs = i * sub, slice(i * sub, (i + 1) * sub)
        y = (_dot(yml_ref[0, rs, :], wo_ref[0, 0:SLAB, :]) + _dot(yhg_ref[0, rs, :], wo_ref[0, SLAB:SLAB + HG_W, :])
             + _dot(yda_ref[0, rs, :], wo_ref[0, SLAB + HG_W:2 * SLAB + HG_W, :]))
        x1 = x_ref[0, rs, :] + _rms(y, mod(g1c_ref, g1x_ref, lo) * gpm_ref[0])
        h2s.append((_rms(x1, gpf_ref[0] * (1.0 + mod(a2c_ref, a2x_ref, lo))) + mod(s2c_ref, s2x_ref, lo)).astype(BF16))
        x1s.append(x1)
    for i in range(nsub):
        lo, rs = i * sub, slice(i * sub, (i + 1) * sub)
        act = _silu(_dot(h2s[i], wg_ref[0])) * _dot(h2s[i], wu_ref[0])
        f = _dot(act.astype(BF16), wd_ref[0])
        o_ref[0, rs, :] = x1s[i] + _rms(f, mod(g2c_ref, g2x_ref, lo) * gqf_ref[0])


def _mix_ffn(yml, yhg, yda, xs, mod4, w_out_pad, g_post_mix, g_pre_ffn, wg, wu, wd, g_post_ffn, layer, ctx_len,
             tm, nsub, t0):
    nb, n_all, d = xs.shape
    dff = wg.shape[-1]
    assert tm % nsub == 0 and (ctx_len % tm) % (tm // nsub) == 0
    kern = functools.partial(_mix_ffn_kernel, tm=tm, nsub=nsub, ctx_len=ctx_len, t0=t0)
    lsel = lambda b, t: (layer, 0, 0)
    tile = lambda width: pl.BlockSpec((1, tm, width), lambda b, t: (b, t + t0, 0))
    return pl.pallas_call(
        kern,
        grid=(nb, n_all // tm - t0),
        in_specs=[tile(SLAB), tile(HG_W), tile(SLAB), tile(d)] + _mod_specs(layer, nb, (2, 4, 3, 5), d) + [
            _resident((1, 2 * SLAB + HG_W, d), lsel),
            pl.BlockSpec((1, 1, d), lsel),
            pl.BlockSpec((1, 1, d), lsel),
            _resident((1, d, dff), lsel),
            _resident((1, d, dff), lsel),
            _resident((1, dff, d), lsel),
            pl.BlockSpec((1, 1, d), lsel)],
        out_specs=pl.BlockSpec((1, tm, d), lambda b, t: (b, t, 0)),
        out_shape=jax.ShapeDtypeStruct((nb, n_all - t0 * tm, d), F32),
        compiler_params=_cparams(("parallel", "arbitrary")),
        name="mix_ffn",
    )(yml, yhg, yda, xs, *([mod4] * 8), w_out_pad, g_post_mix, g_pre_ffn, wg, wu, wd, g_post_ffn)


def _pad_heads(w, hd):
    lead = w.shape[:-1]
    w = w.reshape(*lead, HEADS, hd)
    w = jnp.pad(w, [(0, 0)] * len(lead) + [(0, 0), (0, LANE - hd)])
    return w.reshape(*lead, SLAB)


def _pad_in_cols(w, extra_f=None):
    sizes = (4 * ML_HD,) * 4 + (2 * HEADS, 2 * HEADS, HEADS * HG_DK, 2 * HEADS * HG_DK, HEADS * HG_DV, HEADS * HG_DV,
             2 * HEADS * DA_HD, 2 * HEADS * DA_HD, HEADS * DA_VD)
    parts = []
    off = 0
    for s in sizes:
        parts.append(w[..., off:off + s])
        off += s
    gi, gf = parts[4], parts[5]
    if extra_f is not None:
        gf = gf + extra_f
    gate = jnp.concatenate([gi, gf], axis=-1)
    gate = jnp.pad(gate, [(0, 0)] * (gate.ndim - 1) + [(0, LANE - 4 * HEADS)])
    return jnp.concatenate([
        _pad_heads(parts[0], ML_HD), _pad_heads(parts[1], ML_HD), _pad_heads(parts[2], ML_HD), _pad_heads(parts[3], ML_HD),
        gate, parts[6], parts[7], parts[8], parts[9],
        _pad_heads(parts[10], DA_VD), _pad_heads(parts[11], DA_VD), _pad_heads(parts[12], DA_VD)], axis=-1)


def _pad_out_rows(w_out):
    depth, _, d = w_out.shape
    wt = jnp.swapaxes(w_out, 1, 2)
    ml, hg, da = wt[..., :4 * ML_HD], wt[..., 4 * ML_HD:4 * ML_HD + HEADS * HG_DV], wt[..., 4 * ML_HD + HEADS * HG_DV:]
    wp = jnp.concatenate([_pad_heads(ml, ML_HD), hg, _pad_heads(da, DA_VD)], axis=-1)
    return jnp.swapaxes(wp, 1, 2)


def _rope_tables(n_lat, ctx_len):
    rows = n_lat // GRID_W
    row = jnp.repeat(jnp.arange(rows), GRID_W).astype(F32)
    col = jnp.tile(jnp.arange(GRID_W), rows).astype(F32)
    half = DA_HD // 2
    inv = ROPE_BASE ** (-jnp.arange(0, half, 2, dtype=F32) / half)
    ang_r, ang_c = row[:, None] * inv, col[:, None] * inv
    zero = jnp.zeros_like(ang_r)

    def one_map(fr, fc, sel):
        r, c = fr(ang_r), fc(ang_c)
        if sel == "cos":
            return jnp.concatenate([r, r, c, c], axis=1)
        if sel == "a":
            return jnp.concatenate([-r, zero, -c, zero], axis=1)
        return jnp.concatenate([zero, r, zero, c], axis=1)

    tabs = []
    for sel, fn in (("cos", jnp.cos), ("a", jnp.sin), ("b", jnp.sin)):
        m = one_map(fn, fn, sel)
        lat = jnp.pad(jnp.concatenate([m, m], axis=1), ((0, 0), (0, LANE - 2 * DA_HD)))
        fill = 1.0 if sel == "cos" else 0.0
        ctx_rows = jnp.pad(jnp.full((ctx_len, 2 * DA_HD), fill, F32), ((0, 0), (0, LANE - 2 * DA_HD)))
        tabs.append(jnp.concatenate([ctx_rows, lat], axis=0))
    return tabs


def kernel(x, c, ctx, c_ctx, w_ada, b_ada, g_pre_mix, g_post_mix, g_pre_ffn, g_post_ffn, w_in, b_in, w_out,
           ml_f_bias, ml_norm, hg_lb, hg_norm, da_lambda, da_norm, w_ffn_gate, w_ffn_up, w_ffn_down):
    nb, n_lat, d = x.shape
    ctx_len = ctx.shape[1]
    depth = w_ada.shape[0]
    n_all = ctx_len + n_lat

    w_pad = _pad_in_cols(w_in).astype(BF16)
    b_pad = _pad_in_cols(b_in, extra_f=ml_f_bias)[:, None, :]
    w_out_pad = _pad_out_rows(w_out).astype(BF16)
    wg, wu, wd = w_ffn_gate.astype(BF16), w_ffn_up.astype(BF16), w_ffn_down.astype(BF16)
    ml_gain = _pad_heads(ml_norm, ML_HD).reshape(depth * HEADS, 1, LANE)
    hg_gain = hg_norm.reshape(-1, 1, LANE)
    da_gain = jnp.pad(da_norm, ((0, 0), (0, LANE - DA_VD)))[:, None, :]
    lam_pad = jnp.pad(da_lambda.astype(F32), ((0, 0), (0, 0), (0, LANE - DA_HD)))
    sm = jax.nn.softmax(hg_lb.astype(F32), axis=0)
    lbs = jnp.cumsum(sm, axis=0) - sm[0:1]
    lbf = jnp.maximum(lbs, LB_FLOOR)[:, None, :]
    oml = (1.0 - lbs)[:, None, :]
    rope_tabs = _rope_tables(n_lat, ctx_len)
    r3 = lambda g: g[:, None, :]

    mp = -(-(nb + 1) // 8) * 8
    cc = jnp.concatenate([c, c_ctx[None, :], jnp.zeros((mp - nb - 1, d), F32)], axis=0)
    mod4 = _ada(cc, w_ada, b_ada).reshape(depth, mp, 1, 6 * d)

    xs = jnp.concatenate([ctx, x], axis=1)
    for l in range(depth):
        lam_init = 0.8 - 0.6 * math.exp(-0.3 * l)
        (mlq, mlk, mlv, mlo, gates, hgq, hgf, hgv, hgg, daq, dak, dav_t) = _inproj(
            xs, mod4, r3(g_pre_mix), w_pad, b_pad, rope_tabs, lbf, oml, l, ctx_len,
            *((IN_TM, IN_SUB) if n_all % IN_TM == 0 else (IN_TM_ALT, IN_SUB_ALT)))
        gates_t = jnp.swapaxes(gates[:, :, :4 * HEADS], 1, 2).reshape(nb, 4 * HEADS, n_all // ML_CHUNK, ML_CHUNK)
        yml = _mlstm(mlq, mlk, mlv, mlo, gates_t, ml_gain, l, ctx_len)
        yhg = _hgrn(hgq, hgf, hgv, hgg, hg_gain, l, ctx_len)
        yda = _attn(daq, dak, dav_t, lam_pad, da_gain, l, ctx_len, lam_init)
        last = l == depth - 1
        tm, nsub = (MIX_TM, MIX_SUB) if not last and n_all % MIX_TM == 0 else (MIX_TM_LAST, MIX_SUB_LAST)
        xs = _mix_ffn(yml, yhg, yda, xs, mod4, w_out_pad, r3(g_post_mix), r3(g_pre_ffn), wg, wu, wd,
                      r3(g_post_ffn), l, ctx_len, tm, nsub, ctx_len // tm if last else 0)
    return xs
```

```python
import functools
import math

import numpy as np
import jax
import jax.numpy as jnp
from jax import lax
from jax.experimental import pallas as pl
from jax.experimental.pallas import tpu as pltpu

F32 = jnp.float32
BF16 = jnp.bfloat16
HIGHEST = lax.Precision.HIGHEST

LANE = 128
VMEM_LIMIT = 52 * 1024 * 1024

EPS = 1e-6
NEG = -1e30
LB_FLOOR = 1e-30
M_INIT = -1e30
GRID_W = 64
ROPE_BASE = 10000.0

HEADS = 4
ML_HD = 96
HG_DV = 64
HG_DK = 128
DA_HD = 48
DA_VD = 2 * DA_HD
SLAB = HEADS * LANE
HG_W = HEADS * HG_DV

ML_CHUNK = 256
ML_HPS = 2
HG_CHUNK = 128
HG_DIAG = 8
ATT_TQ = 256
IN_CW = 256
IN_TM, IN_SUB = 768, 3
IN_TM_ALT, IN_SUB_ALT = 256, 2
MIX_TM, MIX_SUB = 768, 3
MIX_TM_LAST, MIX_SUB_LAST = 256, 2
ATT_KC = 256
ATT_HPS = 2
LOG2E = 1.4426950408889634

OFF_MLQ, OFF_MLK, OFF_MLV, OFF_MLO = 0, SLAB, 2 * SLAB, 3 * SLAB
OFF_GATE = 4 * SLAB
OFF_HGQ = OFF_GATE + LANE
OFF_HGF = OFF_HGQ + SLAB
OFF_HGV = OFF_HGF + 2 * SLAB
OFF_HGG = OFF_HGV + HG_W
OFF_DAQ = OFF_HGG + HG_W
OFF_DAK = OFF_DAQ + SLAB
OFF_DAV = OFF_DAK + SLAB
NP_IN = OFF_DAV + SLAB


def _cparams(sem):
    return pltpu.CompilerParams(dimension_semantics=sem, vmem_limit_bytes=VMEM_LIMIT)


def _silu(x):
    return x * jax.nn.sigmoid(x)


def _log_sigmoid(z):
    return jnp.minimum(z, 0.0) - jnp.log1p(jnp.exp(-jnp.abs(z)))


def _neg_abs(x):
    bits = lax.bitcast_convert_type(x, jnp.uint32) | jnp.uint32(0x80000000)
    return lax.bitcast_convert_type(bits, F32)


def _rms(x, g):
    return x * lax.rsqrt(jnp.mean(x * x, axis=-1, keepdims=True) + EPS) * g


def _dot(a, b):
    return jnp.dot(a, b, preferred_element_type=F32)


def _dot_nt(a, b):
    return lax.dot_general(a, b, (((1,), (1,)), ((), ())), preferred_element_type=F32)


def _dot_tn(a, b):
    return lax.dot_general(a, b, (((0,), (0,)), ((), ())), preferred_element_type=F32)


def _ada_kernel(s_ref, w_ref, b_ref, o_ref):
    s = _silu(s_ref[...])
    o_ref[0] = jnp.dot(s, w_ref[0], precision=HIGHEST, preferred_element_type=F32) + b_ref[0]


def _ada(cc, w_ada, b_ada):
    depth, d, d6 = w_ada.shape
    mp = cc.shape[0]
    tn = 1024
    return pl.pallas_call(
        _ada_kernel,
        grid=(depth, d6 // tn),
        in_specs=[pl.BlockSpec((mp, d), lambda l, j: (0, 0)),
                  pl.BlockSpec((1, d, tn), lambda l, j: (l, 0, j)),
                  pl.BlockSpec((1, 1, tn), lambda l, j: (l, 0, j))],
        out_specs=pl.BlockSpec((1, mp, tn), lambda l, j: (l, 0, j)),
        out_shape=jax.ShapeDtypeStruct((depth, mp, d6), F32),
        compiler_params=_cparams(("arbitrary", "arbitrary")),
        name="ada_mod",
    )(cc, w_ada, b_ada.reshape(depth, 1, d6))


def _mod_specs(layer, nb, idxs, d):
    specs = []
    for j in idxs:
        specs.append(pl.BlockSpec((1, 1, 1, d), lambda b, t, j=j: (layer, b, 0, j)))
        specs.append(pl.BlockSpec((1, 1, 1, d), lambda b, t, j=j: (layer, nb, 0, j)))
    return specs


def _inproj_kernel(x_ref, sx_ref, sc_ref, ax_ref, ac_ref, g_ref, w_ref, b_ref,
                   cos_ref, sna_ref, snb_ref, lbf_ref, oml_ref,
                   mlq_ref, mlk_ref, mlv_ref, mlo_ref, gate_ref,
                   hgq_ref, hgf_ref, hgv_ref, hgg_ref, daq_ref, dak_ref, dav_ref,
                   *, tm, nsub, ctx_len):
    split = min(ctx_len % tm if ctx_len % tm else tm, tm)
    top_is_ctx = pl.program_id(1) * tm < ctx_len
    sub = tm // nsub
    half = DA_HD // 4
    lane = lax.broadcasted_iota(jnp.int32, (sub, LANE), 1)

    def rope(p, rs):
        cos, sna, snb = cos_ref[rs, :], sna_ref[rs, :], snb_ref[rs, :]
        outs = []
        for i in range(p.shape[1] // LANE):
            xh = p[:, i * LANE:(i + 1) * LANE]
            outs.append(xh * cos + pltpu.roll(xh, LANE - half, 1) * sna + pltpu.roll(xh, half, 1) * snb)
        return jnp.concatenate(outs, axis=1)

    def log2_f(p, c0, rs):
        w = p.shape[1]
        return jnp.log2(lbf_ref[0, :, c0:c0 + w] + oml_ref[0, :, c0:c0 + w] * jax.nn.sigmoid(p))

    def gates(p, c0, rs):
        return jnp.where((lane >= 2 * HEADS) & (lane < 4 * HEADS), _log_sigmoid(p), p)

    plain = lambda p, c0, rs: p
    groups = {
        "mlq": (mlq_ref, OFF_MLQ, SLAB, plain),
        "mlk": (mlk_ref, OFF_MLK, SLAB, lambda p, c0, rs: p * (ML_HD ** -0.5)),
        "mlv": (mlv_ref, OFF_MLV, SLAB, plain),
        "mlo": (mlo_ref, OFF_MLO, SLAB, lambda p, c0, rs: jax.nn.sigmoid(p)),
        "gate": (gate_ref, OFF_GATE, LANE, gates),
        "hgq": (hgq_ref, OFF_HGQ, SLAB, lambda p, c0, rs: _silu(p)),
        "hgf": (hgf_ref, OFF_HGF, 2 * SLAB, log2_f),
        "hgv": (hgv_ref, OFF_HGV, HG_W, plain),
        "hgg": (hgg_ref, OFF_HGG, HG_W, lambda p, c0, rs: _silu(p)),
        "daq": (daq_ref, OFF_DAQ, SLAB, lambda p, c0, rs: rope(p, rs) * (LOG2E * DA_HD ** -0.5)),
        "dak": (dak_ref, OFF_DAK, SLAB, lambda p, c0, rs: rope(p, rs)),
        "dav": (dav_ref, OFF_DAV, SLAB, plain),
    }
    order = [("hgf", 0), ("mlq", 0), ("hgf", 1), ("mlq", 1), ("hgf", 2), ("mlk", 0), ("hgf", 3), ("mlk", 1),
             ("mlo", 0), ("mlv", 0), ("mlo", 1), ("mlv", 1), ("hgq", 0), ("dav", 0), ("hgq", 1), ("dav", 1),
             ("daq", 0), ("hgv", 0), ("daq", 1), ("gate", 0), ("dak", 0), ("hgg", 0), ("dak", 1)]
    for i in range(nsub):
        rs = slice(i * sub, (i + 1) * sub)
        ctx_rows = i * sub < split
        shift = jnp.where(top_is_ctx, sc_ref[0, 0], sx_ref[0, 0]) if ctx_rows else sx_ref[0, 0]
        scale = 1.0 + (jnp.where(top_is_ctx, ac_ref[0, 0], ax_ref[0, 0]) if ctx_rows else ax_ref[0, 0])
        h = (_rms(x_ref[0, rs, :], g_ref[0] * scale) + shift).astype(BF16)
        for name, ci in order:
            ref, off, width, fn = groups[name]
            cw = min(width, IN_CW)
            c0 = ci * cw
            p = _dot(h, w_ref[0, :, off + c0:off + c0 + cw]) + b_ref[0, :, off + c0:off + c0 + cw]
            if name == "dav":
                ref[0, c0:c0 + cw, rs] = p.T.astype(ref.dtype)
            else:
                ref[0, rs, c0:c0 + cw] = fn(p, c0, rs).astype(ref.dtype)


def _inproj(xs, mod4, g_pre, w_pad, b_pad, rope_tabs, lbf, oml, layer, ctx_len, tm, nsub):
    nb, n_all, d = xs.shape
    assert tm % nsub == 0 and (ctx_len % tm) % (tm // nsub) == 0
    kern = functools.partial(_inproj_kernel, tm=tm, nsub=nsub, ctx_len=ctx_len)
    lsel = lambda b, t: (layer, 0, 0)
    tile = lambda width: pl.BlockSpec((1, tm, width), lambda b, t: (b, t, 0))
    tab = pl.BlockSpec((tm, LANE), lambda b, t: (t, 0))
    bf = lambda width: jax.ShapeDtypeStruct((nb, n_all, width), BF16)
    f32 = lambda width: jax.ShapeDtypeStruct((nb, n_all, width), F32)
    out_shapes = [bf(SLAB)] * 4 + [f32(LANE)] + [bf(SLAB), f32(2 * SLAB), bf(HG_W), bf(HG_W)] + [bf(SLAB)] * 3
    out_specs = [tile(s.shape[-1]) for s in out_shapes]
    out_shapes[-1] = jax.ShapeDtypeStruct((nb, SLAB, n_all), BF16)
    out_specs[-1] = pl.BlockSpec((1, SLAB, tm), lambda b, t: (b, 0, t))
    return pl.pallas_call(
        kern,
        grid=(nb, n_all // tm),
        in_specs=[tile(d)] + _mod_specs(layer, nb, (0, 1), d) + [
            pl.BlockSpec((1, 1, d), lsel),
            pl.BlockSpec((1, d, NP_IN), lsel, pipeline_mode=pl.Buffered(1)),
            pl.BlockSpec((1, 1, NP_IN), lsel),
            tab, tab, tab,
            pl.BlockSpec((1, 1, 2 * SLAB), lsel),
            pl.BlockSpec((1, 1, 2 * SLAB), lsel)],
        out_specs=out_specs,
        out_shape=out_shapes,
        compiler_params=_cparams(("parallel", "arbitrary")),
        name="in_proj",
    )(xs, mod4, mod4, mod4, mod4, g_pre, w_pad, b_pad, *rope_tabs, lbf, oml)


def _ml_chunks(qs, ks, vs, i_rows, lf_rows, s_exts, ms, revs):
    n = len(qs)
    size = qs[0].shape[0]
    ii = lax.broadcasted_iota(jnp.int32, (size, size), 0)
    jj = lax.broadcasted_iota(jnp.int32, (size, size), 1)
    lane = lax.broadcasted_iota(jnp.int32, (size, LANE), 1)
    cums = {rev: jnp.where((ii >= jj) if rev else (ii <= jj), 1.0, 0.0).astype(BF16) for rev in set(revs)}
    seens = {rev: (jj >= ii) if rev else (jj <= ii) for rev in set(revs)}
    row16 = lax.broadcasted_iota(jnp.int32, (16, size), 0)

    b_cols, g_cols, g_rows = [], [], []
    for i in range(n):
        lf2 = lf_rows[i] * LOG2E
        hi = lf2.astype(BF16).astype(F32)
        mid = (lf2 - hi).astype(BF16).astype(F32)
        lo = (lf2 - hi) - mid
        parts = jnp.where(row16 == 0, hi, jnp.where(row16 == 1, mid, jnp.where(row16 == 2, lo, 0.0))).astype(BF16)
        cs = _dot(parts, cums[revs[i]])
        b_row = (cs[0:1] + cs[1:2]) + cs[2:3]
        g_row = i_rows[i] * LOG2E - b_row
        b_cols.append(jnp.sum(jnp.where(ii == jj, b_row, 0.0), axis=1, keepdims=True))
        g_cols.append(jnp.sum(jnp.where(ii == jj, g_row, 0.0), axis=1, keepdims=True))
        g_rows.append(g_row)

    mts, dws, w_inters = [], [], []
    for i in range(n):
        dmat = jnp.where(seens[revs[i]], b_cols[i] + g_rows[i], NEG)
        a_col = b_cols[i] + ms[i]
        mt = jnp.maximum(a_col, jnp.max(dmat, axis=1, keepdims=True))
        mts.append(mt)
        w_inters.append(jnp.exp2(a_col - mt))
        dws.append(jnp.exp2(dmat - mt))

    ress = []
    for i in range(n):
        qk = _dot_nt(qs[i], ks[i]) * dws[i]
        ress.append(_dot(qk.astype(BF16), vs[i]) + w_inters[i] * _dot(qs[i], s_exts[i].astype(BF16)))

    houts, s_news, m_news = [], [], []
    for i in range(n):
        den = jnp.sum(jnp.where(lane == ML_HD, ress[i], 0.0), axis=1, keepdims=True)
        houts.append(ress[i] * (1.0 / jnp.maximum(jnp.abs(den), jnp.exp2(-mts[i]))))
        last = 0 if revs[i] else size - 1
        m_new = mts[i][last:last + 1]
        b_last = b_cols[i][last:last + 1]
        decay = jnp.exp2(b_last + ms[i] - m_new)
        wk = jnp.exp2(b_last + g_cols[i] - m_new)
        s_news.append(decay * s_exts[i] + _dot_tn((wk * ks[i].astype(F32)).astype(BF16), vs[i]))
        m_news.append(m_new)
    return houts, s_news, m_news


def _mlstm_kernel(q_ref, k_ref, v_ref, o_ref, gt_ref, gain_ref, y_ref, hf_ref, hb_ref, *, nc_ctx, nc_all, size, nh):
    hd0 = pl.program_id(1) * nh
    lane = lax.broadcasted_iota(jnp.int32, (size, LANE), 1)
    nc_lat = nc_all - nc_ctx

    def step(c, carry):
        cb = jnp.where(c < nc_ctx, nc_ctx - 1 - c, nc_ctx + nc_lat - 1 - (c - nc_ctx))
        sf = pl.ds(pl.multiple_of(c * size, size), size)
        sb = pl.ds(pl.multiple_of(cb * size, size), size)
        qs, ks, vs, i_rows, lf_rows, revs = [], [], [], [], [], []
        for hh in range(nh):
            ln = slice(hh * LANE, (hh + 1) * LANE)
            for sl, cc, off in ((sf, c, 0), (sb, cb, HEADS)):
                v = v_ref[0, sl, ln]
                qs.append(q_ref[0, sl, ln])
                ks.append(k_ref[0, sl, ln])
                vs.append(jnp.where(lane == ML_HD, jnp.ones_like(v), v))
                i_rows.append(gt_ref[0, off + hd0 + hh, pl.ds(cc, 1), :])
                lf_rows.append(gt_ref[0, 2 * HEADS + off + hd0 + hh, pl.ds(cc, 1), :])
                revs.append(off > 0)
        houts, s_news, m_news = _ml_chunks(qs, ks, vs, i_rows, lf_rows, list(carry[0]), list(carry[1]), revs)
        for hh in range(nh):
            ln = slice(hh * LANE, (hh + 1) * LANE)
            hf_ref[sf, ln] = houts[2 * hh]
            hb_ref[sb, ln] = houts[2 * hh + 1]
        return tuple(s_news), tuple(m_news)

    s0 = jnp.zeros((LANE, LANE), F32)
    m0 = jnp.full((1, 1), M_INIT, F32)
    lax.fori_loop(0, nc_all, step, ((s0,) * (2 * nh), (m0,) * (2 * nh)), unroll=3)

    def finish(c, _):
        sl = pl.ds(pl.multiple_of(c * size, size), size)
        for hh in range(nh):
            ln = slice(hh * LANE, (hh + 1) * LANE)
            h = jnp.where(lane < ML_HD, hf_ref[sl, ln] + hb_ref[sl, ln], 0.0)
            y = h * lax.rsqrt(jnp.sum(h * h, axis=1, keepdims=True) * (1.0 / ML_HD) + EPS) * gain_ref[0, :, ln]
            y_ref[0, sl, ln] = (y * o_ref[0, sl, ln].astype(F32)).astype(BF16)
        return 0

    lax.fori_loop(0, nc_all, finish, 0, unroll=3)


def _mlstm(mlq, mlk, mlv, mlo, gates_t, gain, layer, ctx_len):
    nb, n_all, _ = mlq.shape
    size, nh = ML_CHUNK, ML_HPS
    nc_all = n_all // size
    width = nh * LANE
    ng = HEADS // nh
    kern = functools.partial(_mlstm_kernel, nc_ctx=ctx_len // size, nc_all=nc_all, size=size, nh=nh)
    slab = pl.BlockSpec((1, n_all, width), lambda b, h: (b, 0, h))
    return pl.pallas_call(
        kern,
        grid=(nb, ng),
        in_specs=[slab, slab, slab, slab,
                  pl.BlockSpec((1, 4 * HEADS, nc_all, size), lambda b, h: (b, 0, 0, 0)),
                  pl.BlockSpec((1, 1, width), lambda b, h: (layer * ng + h, 0, 0))],
        out_specs=slab,
        out_shape=jax.ShapeDtypeStruct((nb, n_all, SLAB), BF16),
        scratch_shapes=[pltpu.VMEM((n_all, width), F32), pltpu.VMEM((n_all, width), F32)],
        compiler_params=_cparams(("parallel", "arbitrary")),
        name="mlstm",
    )(mlq, mlk, mlv, mlo, gates_t, gain.reshape(-1, 1, width))


def _hg_tables(size):
    t = np.arange(size)
    tri = (t[None, :] <= t[:, None]).astype(np.float32)
    x = t[:, None] ^ t[None, :]
    lvl = np.where(x < HG_DIAG, 0, np.floor(np.log2(np.maximum(x, 1))).astype(np.int64) - 2)
    code = np.where(t[None, :] <= t[:, None], lvl, -1).astype(np.int32)
    j = np.arange(HG_DIAG * LANE) // LANE
    emat = (np.arange(size)[None, :] % HG_DIAG == j[:, None]).astype(np.float32)
    return (jnp.asarray(np.stack([tri, tri.T]), BF16), jnp.asarray(np.stack([code, code.T])),
            jnp.asarray(emat, BF16))


def _hg_chunks(qs, lfs, vs, sts, tris, codes, emat, revs):
    n = len(qs)
    size = qs[0].shape[0]
    bc2s, kks = [], []
    for i in range(n):
        lf2 = lfs[i]
        hi = lf2.astype(BF16)
        r1 = lf2 - hi.astype(F32)
        mid = r1.astype(BF16)
        lo = (r1 - mid.astype(F32)).astype(BF16)
        cs = _dot(tris[i], jnp.concatenate([hi, mid, lo], axis=1))
        bc2s.append((cs[:, 0:LANE] + cs[:, LANE:2 * LANE]) + cs[:, 2 * LANE:3 * LANE])
        kks.append(jnp.maximum(1.0 - jnp.exp2(lf2), 0.0))

    atts = [jnp.zeros((size, size), F32) for _ in range(n)]
    c, lvl = HG_DIAG, 1
    while c < size:
        blk = 2 * c
        for i in range(n):
            ridx = c if revs[i] else c - 1
            b3 = bc2s[i].reshape(size // blk, blk, LANE)
            ref = jnp.broadcast_to(b3[:, ridx:ridx + 1, :], b3.shape).reshape(size, LANE)
            e = jnp.exp2(_neg_abs(bc2s[i] - ref))
            a = _dot_nt((qs[i] * e).astype(BF16), (kks[i] * e).astype(BF16))
            atts[i] = jnp.where(codes[i] == lvl, a, atts[i])
        c, lvl = blk, lvl + 1

    for i in range(n):
        w3 = (bc2s[i] - jnp.log2(kks[i])).reshape(size // HG_DIAG, HG_DIAG, LANE)
        ps = []
        for j in range(HG_DIAG):
            wj = jnp.broadcast_to(w3[:, j:j + 1, :], w3.shape).reshape(size, LANE)
            ps.append((qs[i] * jnp.exp2(jnp.minimum(bc2s[i] - wj, 0.0))).astype(BF16))
        atts[i] = jnp.where(codes[i] == 0, _dot(jnp.concatenate(ps, axis=1), emat), atts[i])

    outs, new_sts = [], []
    for i in range(n):
        last = 0 if revs[i] else size - 1
        bl = bc2s[i][last:last + 1]
        outs.append(_dot(atts[i].astype(BF16), vs[i])
                    + _dot_nt((qs[i] * jnp.exp2(bc2s[i])).astype(BF16), sts[i].astype(BF16)))
        new_sts.append(sts[i] * jnp.exp2(bl) + _dot_tn(vs[i], (kks[i] * jnp.exp2(bl - bc2s[i])).astype(BF16)))
    return outs, new_sts


def _hgrn_kernel(q_ref, lf0_ref, lf1_ref, v_ref, g_ref, gain_ref, tri_ref, code_ref, emat_ref,
                 y_ref, of_ref, ob_ref, *, nc_ctx, nc_all, size):
    nc_lat = nc_all - nc_ctx
    low = lax.broadcasted_iota(jnp.int32, (size, LANE), 1) < HG_DV

    def step(c, carry):
        cb = jnp.where(c < nc_ctx, nc_ctx - 1 - c, nc_ctx + nc_lat - 1 - (c - nc_ctx))
        sf = pl.ds(pl.multiple_of(c * size, size), size)
        sb = pl.ds(pl.multiple_of(cb * size, size), size)
        qs, lfs, vs, tris, codes, revs = [], [], [], [], [], []
        for hh in range(2):
            ln = slice(hh * LANE, (hh + 1) * LANE)
            qs += [q_ref[0, sf, ln].astype(F32), q_ref[0, sb, ln].astype(F32)]
            lfs += [lf0_ref[0, sf, ln], lf1_ref[0, sb, ln]]
            vs += [v_ref[0, sf, :], v_ref[0, sb, :]]
            tris += [tri_ref[0], tri_ref[1]]
            codes += [code_ref[0], code_ref[1]]
            revs += [False, True]
        outs, sts = _hg_chunks(qs, lfs, vs, list(carry), tris, codes, emat_ref[...], revs)
        of_ref[sf, :] = jnp.where(low, outs[0], outs[2])
        ob_ref[sb, :] = jnp.where(low, outs[1], outs[3])
        return tuple(sts)

    st0 = jnp.zeros((LANE, LANE), F32)
    lax.fori_loop(0, nc_all, step, (st0,) * 4, unroll=2)

    def finish(c, _):
        sl = pl.ds(pl.multiple_of(c * size, size), size)
        o = of_ref[sl, :] + ob_ref[sl, :]
        sq = o * o
        ms0 = jnp.sum(jnp.where(low, sq, 0.0), axis=1, keepdims=True) * (1.0 / HG_DV)
        ms1 = jnp.sum(jnp.where(low, 0.0, sq), axis=1, keepdims=True) * (1.0 / HG_DV)
        y = o * lax.rsqrt(jnp.where(low, ms0, ms1) + EPS) * gain_ref[0]
        y_ref[0, sl, :] = (y * g_ref[0, sl, :].astype(F32)).astype(BF16)
        return 0

    lax.fori_loop(0, nc_all, finish, 0, unroll=6)


def _hgrn(hgq, hgf, hgv, hgg, gain, layer, ctx_len):
    nb, n_all, _ = hgq.shape
    size = HG_CHUNK
    nc_all = n_all // size
    ng = HEADS // 2
    kern = functools.partial(_hgrn_kernel, nc_ctx=ctx_len // size, nc_all=nc_all, size=size)
    wide = pl.BlockSpec((1, n_all, 2 * LANE), lambda b, h: (b, 0, h))
    slab = pl.BlockSpec((1, n_all, LANE), lambda b, h: (b, 0, h))
    tri, code, emat = _hg_tables(size)
    return pl.pallas_call(
        kern,
        grid=(nb, ng),
        in_specs=[wide, wide,
                  pl.BlockSpec((1, n_all, 2 * LANE), lambda b, h: (b, 0, ng + h)),
                  slab, slab,
                  pl.BlockSpec((1, 1, LANE), lambda b, h: (layer * ng + h, 0, 0)),
                  pl.BlockSpec(tri.shape, lambda b, h: (0, 0, 0)),
                  pl.BlockSpec(code.shape, lambda b, h: (0, 0, 0)),
                  pl.BlockSpec(emat.shape, lambda b, h: (0, 0))],
        out_specs=slab,
        out_shape=jax.ShapeDtypeStruct((nb, n_all, HG_W), BF16),
        scratch_shapes=[pltpu.VMEM((n_all, LANE), F32), pltpu.VMEM((n_all, LANE), F32)],
        compiler_params=_cparams(("parallel", "arbitrary")),
        name="hgrn2",
    )(hgq, hgf, hgf, hgv, hgg, gain, tri, code, emat)


def _attn_kernel(q_ref, k_ref, vt_ref, lam_ref, gain_ref, y_ref, *scratch, tq, kc, nh, ctx_len, lam_init):
    n_all = k_ref.shape[1]
    n_tiles = (n_all - ctx_len) // tq
    s_refs = [scratch[2 * hh:2 * hh + 2] for hh in range(nh)]
    m_refs = [scratch[2 * nh + 2 * hh:2 * nh + 2 * hh + 2] for hh in range(nh)]
    lv = lam_ref[0]
    lam = (jnp.exp(jnp.sum(lv[0:1] * lv[1:2], axis=1, keepdims=True))
           - jnp.exp(jnp.sum(lv[2:3] * lv[3:4], axis=1, keepdims=True)) + lam_init)
    lane = lax.broadcasted_iota(jnp.int32, (tq, LANE), 1)
    vrow = lax.broadcasted_iota(jnp.int32, (LANE, kc), 0)
    lanes = [slice(hh * LANE, (hh + 1) * LANE) for hh in range(nh)]

    def rows(t):
        return pl.ds(pl.multiple_of(ctx_len + t * tq, tq), tq)

    def load_qq(sl, hh):
        q = q_ref[0, sl, lanes[hh]]
        zero = jnp.zeros_like(q)
        return jnp.concatenate([jnp.where(lane < DA_HD, q, zero), jnp.where(lane < DA_HD, zero, q)], axis=0)

    def stage_a_chunk(qq, hh, slot, c, m):
        s = _dot_nt(k_ref[0, c * kc:(c + 1) * kc, lanes[hh]], qq)
        s_refs[hh][slot][c * kc:(c + 1) * kc, :] = s
        cm = jnp.max(s, axis=0, keepdims=True)
        return cm if m is None else jnp.maximum(m, cm)

    def stage_b_chunk(hh, slot, c, m, acc):
        p = jnp.exp2(s_refs[hh][slot][c * kc:(c + 1) * kc, :] - m).astype(BF16)
        vt = vt_ref[0, lanes[hh], c * kc:(c + 1) * kc]
        pv = _dot(jnp.where(vrow == DA_VD, jnp.ones_like(vt), vt), p)
        return pv if acc is None else acc + pv

    def finish(acc, sl, hh):
        r0 = 1.0 / acc[DA_VD:DA_VD + 1, 0:tq]
        r1 = lam / acc[DA_VD:DA_VD + 1, tq:2 * tq]
        o = (acc[:, 0:tq] * r0 - acc[:, tq:2 * tq] * r1).T
        o = jnp.where(lane < DA_VD, o, 0.0)
        y = o * lax.rsqrt(jnp.sum(o * o, axis=1, keepdims=True) * (1.0 / DA_VD) + EPS) * gain_ref[0]
        y_ref[0, sl, lanes[hh]] = (y * (1.0 - lam_init)).astype(BF16)

    def stages(sl_a, sl_b, slot_a, nk_a, nk_b):
        slot_b = 1 - slot_a
        qqs = [load_qq(sl_a, hh) for hh in range(nh)] if sl_a is not None else None
        m_bs = [m_refs[hh][slot_b][0:1, :] for hh in range(nh)] if sl_b is not None else None
        ms, accs = [None] * nh, [None] * nh
        for c in range(max(nk_a, nk_b) // kc):
            for hh in range(nh):
                if sl_a is not None and c < nk_a // kc:
                    ms[hh] = stage_a_chunk(qqs[hh], hh, slot_a, c, ms[hh])
                if sl_b is not None and c < nk_b // kc:
                    accs[hh] = stage_b_chunk(hh, slot_b, c, m_bs[hh], accs[hh])
        for hh in range(nh):
            if sl_a is not None:
                m_refs[hh][slot_a][...] = jnp.broadcast_to(ms[hh], (8, 2 * tq))
            if sl_b is not None:
                finish(accs[hh], sl_b, hh)

    ctx_rows = pl.ds(0, tq)
    stages(ctx_rows, None, 1, ctx_len, 0)
    stages(rows(0), ctx_rows, 0, n_all, ctx_len)

    def body(i, _):
        t = 1 + 2 * i
        stages(rows(t), rows(t - 1), 1, n_all, n_all)
        stages(rows(t + 1), rows(t), 0, n_all, n_all)
        return 0

    lax.fori_loop(0, (n_tiles - 2) // 2, body, 0)
    stages(rows(n_tiles - 1), rows(n_tiles - 2), 1, n_all, n_all)
    stages(None, rows(n_tiles - 1), 0, 0, n_all)


def _attn(daq, dak, dav_t, lam_pad, gain, layer, ctx_len, lam_init):
    nb, n_all, _ = daq.shape
    tq, kc, nh = ATT_TQ, ATT_KC, ATT_HPS
    n_tiles = (n_all - ctx_len) // tq
    assert n_tiles >= 2 and n_tiles % 2 == 0 and ctx_len == tq and ctx_len % kc == 0 and n_all % kc == 0
    kern = functools.partial(_attn_kernel, tq=tq, kc=kc, nh=nh, ctx_len=ctx_len, lam_init=lam_init)
    slab = pl.BlockSpec((1, n_all, nh * LANE), lambda b, h: (b, 0, h))
    sbuf, mbuf = pltpu.VMEM((n_all, 2 * tq), F32), pltpu.VMEM((8, 2 * tq), F32)
    return pl.pallas_call(
        kern,
        grid=(nb, HEADS // nh),
        in_specs=[slab, slab,
                  pl.BlockSpec((1, nh * LANE, n_all), lambda b, h: (b, h, 0)),
                  pl.BlockSpec((1, 4, LANE), lambda b, h: (layer, 0, 0)),
                  pl.BlockSpec((1, 1, LANE), lambda b, h: (layer, 0, 0))],
        out_specs=slab,
        out_shape=jax.ShapeDtypeStruct((nb, n_all, SLAB), BF16),
        scratch_shapes=[sbuf] * (2 * nh) + [mbuf] * (2 * nh),
        compiler_params=_cparams(("parallel", "arbitrary")),
        name="diff_attn",
    )(daq, dak, dav_t, lam_pad, gain)


def _resident(shape, index_map):
    return pl.BlockSpec(shape, index_map, pipeline_mode=pl.Buffered(1))


def _mix_ffn_kernel(yml_ref, yhg_ref, yda_ref, x_ref, g1x_ref, g1c_ref, a2x_ref, a2c_ref, s2x_ref, s2c_ref,
                    g2x_ref, g2c_ref, wo_ref, gpm_ref, gpf_ref, wg_ref, wu_ref, wd_ref, gqf_ref, o_ref, *, tm, nsub, ctx_len, t0):
    row0 = (pl.program_id(1) + t0) * tm
    split = min(ctx_len % tm if ctx_len % tm else tm, tm)
    top_is_ctx = row0 < ctx_len

    def mod(c_ref, x_ref_, lo):
        return jnp.where(top_is_ctx, c_ref[0, 0], x_ref_[0, 0]) if lo < split else x_ref_[0, 0]

    sub = tm // nsub
    x1s, h2s = [], []
    for i in range(nsub):
        lo, rs = i * sub, slice(i * sub, (i + 1) * sub)
        y = (_dot(yml_ref[0, rs, :], wo_ref[0, 0:SLAB, :]) + _dot(yhg_ref[0, rs, :], wo_ref[0, SLAB:SLAB + HG_W, :])
             + _dot(yda_ref[0, rs, :], wo_ref[0, SLAB + HG_W:2 * SLAB + HG_W, :]))
        x1 = x_ref[0, rs, :] + _rms(y, mod(g1c_ref, g1x_ref, lo) * gpm_ref[0])
        h2s.append((_rms(x1, gpf_ref[0] * (1.0 + mod(a2c_ref, a2x_ref, lo))) + mod(s2c_ref, s2x_ref, lo)).astype(BF16))
        x1s.append(x1)
    for i in range(nsub):
        lo, rs = i * sub, slice(i * sub, (i + 1) * sub)
        act = _silu(_dot(h2s[i], wg_ref[0])) * _dot(h2s[i], wu_ref[0])
        f = _dot(act.astype(BF16), wd_ref[0])
        o_ref[0, rs, :] = x1s[i] + _rms(f, mod(g2c_ref, g2x_ref, lo) * gqf_ref[0])


def _mix_ffn(yml, yhg, yda, xs, mod4, w_out_pad, g_post_mix, g_pre_ffn, wg, wu, wd, g_post_ffn, layer, ctx_len,
             tm, nsub, t0):
    nb, n_all, d = xs.shape
    dff = wg.shape[-1]
    assert tm % nsub == 0 and (ctx_len % tm) % (tm // nsub) == 0
    kern = functools.partial(_mix_ffn_kernel, tm=tm, nsub=nsub, ctx_len=ctx_len, t0=t0)
    lsel = lambda b, t: (layer, 0, 0)
    tile = lambda width: pl.BlockSpec((1, tm, width), lambda b, t: (b, t + t0, 0))
    return pl.pallas_call(
        kern,
        grid=(nb, n_all // tm - t0),
        in_specs=[tile(SLAB), tile(HG_W), tile(SLAB), tile(d)] + _mod_specs(layer, nb, (2, 4, 3, 5), d) + [
            _resident((1, 2 * SLAB + HG_W, d), lsel),
            pl.BlockSpec((1, 1, d), lsel),
            pl.BlockSpec((1, 1, d), lsel),
            _resident((1, d, dff), lsel),
            _resident((1, d, dff), lsel),
            _resident((1, dff, d), lsel),
            pl.BlockSpec((1, 1, d), lsel)],
        out_specs=pl.BlockSpec((1, tm, d), lambda b, t: (b, t, 0)),
        out_shape=jax.ShapeDtypeStruct((nb, n_all - t0 * tm, d), F32),
        compiler_params=_cparams(("parallel", "arbitrary")),
        name="mix_ffn",
    )(yml, yhg, yda, xs, *([mod4] * 8), w_out_pad, g_post_mix, g_pre_ffn, wg, wu, wd, g_post_ffn)


def _pad_heads(w, hd):
    lead = w.shape[:-1]
    w = w.reshape(*lead, HEADS, hd)
    w = jnp.pad(w, [(0, 0)] * len(lead) + [(0, 0), (0, LANE - hd)])
    return w.reshape(*lead, SLAB)


def _pad_in_cols(w, extra_f=None):
    sizes = (4 * ML_HD,) * 4 + (2 * HEADS, 2 * HEADS, HEADS * HG_DK, 2 * HEADS * HG_DK, HEADS * HG_DV, HEADS * HG_DV,
             2 * HEADS * DA_HD, 2 * HEADS * DA_HD, HEADS * DA_VD)
    parts = []
    off = 0
    for s in sizes:
        parts.append(w[..., off:off + s])
        off += s
    gi, gf = parts[4], parts[5]
    if extra_f is not None:
        gf = gf + extra_f
    gate = jnp.concatenate([gi, gf], axis=-1)
    gate = jnp.pad(gate, [(0, 0)] * (gate.ndim - 1) + [(0, LANE - 4 * HEADS)])
    return jnp.concatenate([
        _pad_heads(parts[0], ML_HD), _pad_heads(parts[1], ML_HD), _pad_heads(parts[2], ML_HD), _pad_heads(parts[3], ML_HD),
        gate, parts[6], parts[7], parts[8], parts[9],
        _pad_heads(parts[10], DA_VD), _pad_heads(parts[11], DA_VD), _pad_heads(parts[12], DA_VD)], axis=-1)


def _pad_out_rows(w_out):
    depth, _, d = w_out.shape
    wt = jnp.swapaxes(w_out, 1, 2)
    ml, hg, da = wt[..., :4 * ML_HD], wt[..., 4 * ML_HD:4 * ML_HD + HEADS * HG_DV], wt[..., 4 * ML_HD + HEADS * HG_DV:]
    wp = jnp.concatenate([_pad_heads(ml, ML_HD), hg, _pad_heads(da, DA_VD)], axis=-1)
    return jnp.swapaxes(wp, 1, 2)


def _rope_tables(n_lat, ctx_len):
    rows = n_lat // GRID_W
    row = jnp.repeat(jnp.arange(rows), GRID_W).astype(F32)
    col = jnp.tile(jnp.arange(GRID_W), rows).astype(F32)
    half = DA_HD // 2
    inv = ROPE_BASE ** (-jnp.arange(0, half, 2, dtype=F32) / half)
    ang_r, ang_c = row[:, None] * inv, col[:, None] * inv
    zero = jnp.zeros_like(ang_r)

    def one_map(fr, fc, sel):
        r, c = fr(ang_r), fc(ang_c)
        if sel == "cos":
            return jnp.concatenate([r, r, c, c], axis=1)
        if sel == "a":
            return jnp.concatenate([-r, zero, -c, zero], axis=1)
        return jnp.concatenate([zero, r, zero, c], axis=1)

    tabs = []
    for sel, fn in (("cos", jnp.cos), ("a", jnp.sin), ("b", jnp.sin)):
        m = one_map(fn, fn, sel)
        lat = jnp.pad(jnp.concatenate([m, m], axis=1), ((0, 0), (0, LANE - 2 * DA_HD)))
        fill = 1.0 if sel == "cos" else 0.0
        ctx_rows = jnp.pad(jnp.full((ctx_len, 2 * DA_HD), fill, F32), ((0, 0), (0, LANE - 2 * DA_HD)))
        tabs.append(jnp.concatenate([ctx_rows, lat], axis=0))
    return tabs


def kernel(x, c, ctx, c_ctx, w_ada, b_ada, g_pre_mix, g_post_mix, g_pre_ffn, g_post_ffn, w_in, b_in, w_out,
           ml_f_bias, ml_norm, hg_lb, hg_norm, da_lambda, da_norm, w_ffn_gate, w_ffn_up, w_ffn_down):
    nb, n_lat, d = x.shape
    ctx_len = ctx.shape[1]
    depth = w_ada.shape[0]
    n_all = ctx_len + n_lat

    w_pad = _pad_in_cols(w_in).astype(BF16)
    b_pad = _pad_in_cols(b_in, extra_f=ml_f_bias)[:, None, :]
    w_out_pad = _pad_out_rows(w_out).astype(BF16)
    wg, wu, wd = w_ffn_gate.astype(BF16), w_ffn_up.astype(BF16), w_ffn_down.astype(BF16)
    ml_gain = _pad_heads(ml_norm, ML_HD).reshape(depth * HEADS, 1, LANE)
    hg_gain = hg_norm.reshape(-1, 1, LANE)
    da_gain = jnp.pad(da_norm, ((0, 0), (0, LANE - DA_VD)))[:, None, :]
    lam_pad = jnp.pad(da_lambda.astype(F32), ((0, 0), (0, 0), (0, LANE - DA_HD)))
    sm = jax.nn.softmax(hg_lb.astype(F32), axis=0)
    lbs = jnp.cumsum(sm, axis=0) - sm[0:1]
    lbf = jnp.maximum(lbs, LB_FLOOR)[:, None, :]
    oml = (1.0 - lbs)[:, None, :]
    rope_tabs = _rope_tables(n_lat, ctx_len)
    r3 = lambda g: g[:, None, :]

    mp = -(-(nb + 1) // 8) * 8
    cc = jnp.concatenate([c, c_ctx[None, :], jnp.zeros((mp - nb - 1, d), F32)], axis=0)
    mod4 = _ada(cc, w_ada, b_ada).reshape(depth, mp, 1, 6 * d)

    xs = jnp.concatenate([ctx, x], axis=1)
    for l in range(depth):
        lam_init = 0.8 - 0.6 * math.exp(-0.3 * l)
        (mlq, mlk, mlv, mlo, gates, hgq, hgf, hgv, hgg, daq, dak, dav_t) = _inproj(
            xs, mod4, r3(g_pre_mix), w_pad, b_pad, rope_tabs, lbf, oml, l, ctx_len,
            *((IN_TM, IN_SUB) if n_all % IN_TM == 0 else (IN_TM_ALT, IN_SUB_ALT)))
        gates_t = jnp.swapaxes(gates[:, :, :4 * HEADS], 1, 2).reshape(nb, 4 * HEADS, n_all // ML_CHUNK, ML_CHUNK)
        yml = _mlstm(mlq, mlk, mlv, mlo, gates_t, ml_gain, l, ctx_len)
        yhg = _hgrn(hgq, hgf, hgv, hgg, hg_gain, l, ctx_len)
        yda = _attn(daq, dak, dav_t, lam_pad, da_gain, l, ctx_len, lam_init)
        last = l == depth - 1
        tm, nsub = (MIX_TM, MIX_SUB) if not last and n_all % MIX_TM == 0 else (MIX_TM_LAST, MIX_SUB_LAST)
        xs = _mix_ffn(yml, yhg, yda, xs, mod4, w_out_pad, r3(g_post_mix), r3(g_pre_ffn), wg, wu, wd,
                      r3(g_post_ffn), l, ctx_len, tm, nsub, ctx_len // tm if last else 0)
    return xs
```

```python
import functools
import math

import numpy as np
import jax
import jax.numpy as jnp
from jax import lax
from jax.experimental import pallas as pl
from jax.experimental.pallas import tpu as pltpu

F32 = jnp.float32
BF16 = jnp.bfloat16
HIGHEST = lax.Precision.HIGHEST

LANE = 128
VMEM_LIMIT = 52 * 1024 * 1024

EPS = 1e-6
NEG = -1e30
LB_FLOOR = 1e-30
M_INIT = -1e30
GRID_W = 64
ROPE_BASE = 10000.0

HEADS = 4
ML_HD = 96
HG_DV = 64
HG_DK = 128
DA_HD = 48
DA_VD = 2 * DA_HD
SLAB = HEADS * LANE
HG_W = HEADS * HG_DV

ML_CHUNK = 256
ML_HPS = 2
HG_CHUNK = 128
HG_DIAG = 8
ATT_TQ = 256
IN_CW = 256
IN_TM, IN_SUB = 768, 3
IN_TM_ALT, IN_SUB_ALT = 256, 2
MIX_TM, MIX_SUB = 768, 3
MIX_TM_LAST, MIX_SUB_LAST = 256, 2
ATT_KC = 256
ATT_HPS = 2
LOG2E = 1.4426950408889634

OFF_MLQ, OFF_MLK, OFF_MLV, OFF_MLO = 0, SLAB, 2 * SLAB, 3 * SLAB
OFF_GATE = 4 * SLAB
OFF_HGQ = OFF_GATE + LANE
OFF_HGF = OFF_HGQ + SLAB
OFF_HGV = OFF_HGF + 2 * SLAB
OFF_HGG = OFF_HGV + HG_W
OFF_DAQ = OFF_HGG + HG_W
OFF_DAK = OFF_DAQ + SLAB
OFF_DAV = OFF_DAK + SLAB
NP_IN = OFF_DAV + SLAB


def _cparams(sem):
    return pltpu.CompilerParams(dimension_semantics=sem, vmem_limit_bytes=VMEM_LIMIT)


def _silu(x):
    return x * jax.nn.sigmoid(x)


def _log_sigmoid(z):
    return jnp.minimum(z, 0.0) - jnp.log1p(jnp.exp(-jnp.abs(z)))


def _neg_abs(x):
    bits = lax.bitcast_convert_type(x, jnp.uint32) | jnp.uint32(0x80000000)
    return lax.bitcast_convert_type(bits, F32)


def _rms(x, g):
    return x * lax.rsqrt(jnp.mean(x * x, axis=-1, keepdims=True) + EPS) * g


def _dot(a, b):
    return jnp.dot(a, b, preferred_element_type=F32)


def _dot_nt(a, b):
    return lax.dot_general(a, b, (((1,), (1,)), ((), ())), preferred_element_type=F32)


def _dot_tn(a, b):
    return lax.dot_general(a, b, (((0,), (0,)), ((), ())), preferred_element_type=F32)


def _ada_kernel(s_ref, w_ref, b_ref, o_ref):
    s = _silu(s_ref[...])
    o_ref[0] = jnp.dot(s, w_ref[0], precision=HIGHEST, preferred_element_type=F32) + b_ref[0]


def _ada(cc, w_ada, b_ada):
    depth, d, d6 = w_ada.shape
    mp = cc.shape[0]
    tn = 1024
    return pl.pallas_call(
        _ada_kernel,
        grid=(depth, d6 // tn),
        in_specs=[pl.BlockSpec((mp, d), lambda l, j: (0, 0)),
                  pl.BlockSpec((1, d, tn), lambda l, j: (l, 0, j)),
                  pl.BlockSpec((1, 1, tn), lambda l, j: (l, 0, j))],
        out_specs=pl.BlockSpec((1, mp, tn), lambda l, j: (l, 0, j)),
        out_shape=jax.ShapeDtypeStruct((depth, mp, d6), F32),
        compiler_params=_cparams(("arbitrary", "arbitrary")),
        name="ada_mod",
    )(cc, w_ada, b_ada.reshape(depth, 1, d6))


def _mod_specs(layer, nb, idxs, d):
    specs = []
    for j in idxs:
        specs.append(pl.BlockSpec((1, 1, 1, d), lambda b, t, j=j: (layer, b, 0, j)))
        specs.append(pl.BlockSpec((1, 1, 1, d), lambda b, t, j=j: (layer, nb, 0, j)))
    return specs


def _inproj_kernel(x_ref, sx_ref, sc_ref, ax_ref, ac_ref, g_ref, w_ref, b_ref,
                   cos_ref, sna_ref, snb_ref, lbf_ref, oml_ref,
                   mlq_ref, mlk_ref, mlv_ref, mlo_ref, gate_ref,
                   hgq_ref, hgf_ref, hgv_ref, hgg_ref, daq_ref, dak_ref, dav_ref,
                   *, tm, nsub, ctx_len):
    split = min(ctx_len % tm if ctx_len % tm else tm, tm)
    top_is_ctx = pl.program_id(1) * tm < ctx_len
    sub = tm // nsub
    half = DA_HD // 4
    lane = lax.broadcasted_iota(jnp.int32, (sub, LANE), 1)

    def rope(p, rs):
        cos, sna, snb = cos_ref[rs, :], sna_ref[rs, :], snb_ref[rs, :]
        outs = []
        for i in range(p.shape[1] // LANE):
            xh = p[:, i * LANE:(i + 1) * LANE]
            outs.append(xh * cos + pltpu.roll(xh, LANE - half, 1) * sna + pltpu.roll(xh, half, 1) * snb)
        return jnp.concatenate(outs, axis=1)

    def log2_f(p, c0, rs):
        w = p.shape[1]
        return jnp.log2(lbf_ref[0, :, c0:c0 + w] + oml_ref[0, :, c0:c0 + w] * jax.nn.sigmoid(p))

    def gates(p, c0, rs):
        return jnp.where((lane >= 2 * HEADS) & (lane < 4 * HEADS), _log_sigmoid(p), p)

    plain = lambda p, c0, rs: p
    groups = {
        "mlq": (mlq_ref, OFF_MLQ, SLAB, plain),
        "mlk": (mlk_ref, OFF_MLK, SLAB, lambda p, c0, rs: p * (ML_HD ** -0.5)),
        "mlv": (mlv_ref, OFF_MLV, SLAB, plain),
        "mlo": (mlo_ref, OFF_MLO, SLAB, lambda p, c0, rs: jax.nn.sigmoid(p)),
        "gate": (gate_ref, OFF_GATE, LANE, gates),
        "hgq": (hgq_ref, OFF_HGQ, SLAB, lambda p, c0, rs: _silu(p)),
        "hgf": (hgf_ref, OFF_HGF, 2 * SLAB, log2_f),
        "hgv": (hgv_ref, OFF_HGV, HG_W, plain),
        "hgg": (hgg_ref, OFF_HGG, HG_W, lambda p, c0, rs: _silu(p)),
        "daq": (daq_ref, OFF_DAQ, SLAB, lambda p, c0, rs: rope(p, rs) * (LOG2E * DA_HD ** -0.5)),
        "dak": (dak_ref, OFF_DAK, SLAB, lambda p, c0, rs: rope(p, rs)),
        "dav": (dav_ref, OFF_DAV, SLAB, plain),
    }
    order = [("hgf", 0), ("mlq", 0), ("hgf", 1), ("mlq", 1), ("hgf", 2), ("mlk", 0), ("hgf", 3), ("mlk", 1),
             ("mlo", 0), ("mlv", 0), ("mlo", 1), ("mlv", 1), ("hgq", 0), ("dav", 0), ("hgq", 1), ("dav", 1),
             ("daq", 0), ("hgv", 0), ("daq", 1), ("gate", 0), ("dak", 0), ("hgg", 0), ("dak", 1)]
    for i in range(nsub):
        rs = slice(i * sub, (i + 1) * sub)
        ctx_rows = i * sub < split
        shift = jnp.where(top_is_ctx, sc_ref[0, 0], sx_ref[0, 0]) if ctx_rows else sx_ref[0, 0]
        scale = 1.0 + (jnp.where(top_is_ctx, ac_ref[0, 0], ax_ref[0, 0]) if ctx_rows else ax_ref[0, 0])
        h = (_rms(x_ref[0, rs, :], g_ref[0] * scale) + shift).astype(BF16)
        for name, ci in order:
            ref, off, width, fn = groups[name]
            cw = min(width, IN_CW)
            c0 = ci * cw
            p = _dot(h, w_ref[0, :, off + c0:off + c0 + cw]) + b_ref[0, :, off + c0:off + c0 + cw]
            if name == "dav":
                ref[0, c0:c0 + cw, rs] = p.T.astype(ref.dtype)
            else:
                ref[0, rs, c0:c0 + cw] = fn(p, c0, rs).astype(ref.dtype)


def _inproj(xs, mod4, g_pre, w_pad, b_pad, rope_tabs, lbf, oml, layer, ctx_len, tm, nsub):
    nb, n_all, d = xs.shape
    assert tm % nsub == 0 and (ctx_len % tm) % (tm // nsub) == 0
    kern = functools.partial(_inproj_kernel, tm=tm, nsub=nsub, ctx_len=ctx_len)
    lsel = lambda b, t: (layer, 0, 0)
    tile = lambda width: pl.BlockSpec((1, tm, width), lambda b, t: (b, t, 0))
    tab = pl.BlockSpec((tm, LANE), lambda b, t: (t, 0))
    bf = lambda width: jax.ShapeDtypeStruct((nb, n_all, width), BF16)
    f32 = lambda width: jax.ShapeDtypeStruct((nb, n_all, width), F32)
    out_shapes = [bf(SLAB)] * 4 + [f32(LANE)] + [bf(SLAB), f32(2 * SLAB), bf(HG_W), bf(HG_W)] + [bf(SLAB)] * 3
    out_specs = [tile(s.shape[-1]) for s in out_shapes]
    out_shapes[-1] = jax.ShapeDtypeStruct((nb, SLAB, n_all), BF16)
    out_specs[-1] = pl.BlockSpec((1, SLAB, tm), lambda b, t: (b, 0, t))
    return pl.pallas_call(
        kern,
        grid=(nb, n_all // tm),
        in_specs=[tile(d)] + _mod_specs(layer, nb, (0, 1), d) + [
            pl.BlockSpec((1, 1, d), lsel),
            pl.BlockSpec((1, d, NP_IN), lsel, pipeline_mode=pl.Buffered(1)),
            pl.BlockSpec((1, 1, NP_IN), lsel),
            tab, tab, tab,
            pl.BlockSpec((1, 1, 2 * SLAB), lsel),
            pl.BlockSpec((1, 1, 2 * SLAB), lsel)],
        out_specs=out_specs,
        out_shape=out_shapes,
        compiler_params=_cparams(("parallel", "arbitrary")),
        name="in_proj",
    )(xs, mod4, mod4, mod4, mod4, g_pre, w_pad, b_pad, *rope_tabs, lbf, oml)


def _ml_chunks(qs, ks, vs, i_rows, lf_rows, s_exts, ms, revs):
    n = len(qs)
    size = qs[0].shape[0]
    ii = lax.broadcasted_iota(jnp.int32, (size, size), 0)
    jj = lax.broadcasted_iota(jnp.int32, (size, size), 1)
    lane = lax.broadcasted_iota(jnp.int32, (size, LANE), 1)
    cums = {rev: jnp.where((ii >= jj) if rev else (ii <= jj), 1.0, 0.0).astype(BF16) for rev in set(revs)}
    seens = {rev: (jj >= ii) if rev else (jj <= ii) for rev in set(revs)}
    row16 = lax.broadcasted_iota(jnp.int32, (16, size), 0)

    b_cols, g_cols, g_rows = [], [], []
    for i in range(n):
        lf2 = lf_rows[i] * LOG2E
        hi = lf2.astype(BF16).astype(F32)
        mid = (lf2 - hi).astype(BF16).astype(F32)
        lo = (lf2 - hi) - mid
        parts = jnp.where(row16 == 0, hi, jnp.where(row16 == 1, mid, jnp.where(row16 == 2, lo, 0.0))).astype(BF16)
        cs = _dot(parts, cums[revs[i]])
        b_row = (cs[0:1] + cs[1:2]) + cs[2:3]
        g_row = i_rows[i] * LOG2E - b_row
        b_cols.append(jnp.sum(jnp.where(ii == jj, b_row, 0.0), axis=1, keepdims=True))
        g_cols.append(jnp.sum(jnp.where(ii == jj, g_row, 0.0), axis=1, keepdims=True))
        g_rows.append(g_row)

    mts, dws, w_inters = [], [], []
    for i in range(n):
        dmat = jnp.where(seens[revs[i]], b_cols[i] + g_rows[i], NEG)
        a_col = b_cols[i] + ms[i]
        mt = jnp.maximum(a_col, jnp.max(dmat, axis=1, keepdims=True))
        mts.append(mt)
        w_inters.append(jnp.exp2(a_col - mt))
        dws.append(jnp.exp2(dmat - mt))

    ress = []
    for i in range(n):
        qk = _dot_nt(qs[i], ks[i]) * dws[i]
        ress.append(_dot(qk.astype(BF16), vs[i]) + w_inters[i] * _dot(qs[i], s_exts[i].astype(BF16)))

    houts, s_news, m_news = [], [], []
    for i in range(n):
        den = jnp.sum(jnp.where(lane == ML_HD, ress[i], 0.0), axis=1, keepdims=True)
        houts.append(ress[i] * (1.0 / jnp.maximum(jnp.abs(den), jnp.exp2(-mts[i]))))
        last = 0 if revs[i] else size - 1
        m_new = mts[i][last:last + 1]
        b_last = b_cols[i][last:last + 1]
        decay = jnp.exp2(b_last + ms[i] - m_new)
        wk = jnp.exp2(b_last + g_cols[i] - m_new)
        s_news.append(decay * s_exts[i] + _dot_tn((wk * ks[i].astype(F32)).astype(BF16), vs[i]))
        m_news.append(m_new)
    return houts, s_news, m_news


def _mlstm_kernel(q_ref, k_ref, v_ref, o_ref, gt_ref, gain_ref, y_ref, hf_ref, hb_ref, *, nc_ctx, nc_all, size, nh):
    hd0 = pl.program_id(1) * nh
    lane = lax.broadcasted_iota(jnp.int32, (size, LANE), 1)
    nc_lat = nc_all - nc_ctx

    def step(c, carry):
        cb = jnp.where(c < nc_ctx, nc_ctx - 1 - c, nc_ctx + nc_lat - 1 - (c - nc_ctx))
        sf = pl.ds(pl.multiple_of(c * size, size), size)
        sb = pl.ds(pl.multiple_of(cb * size, size), size)
        qs, ks, vs, i_rows, lf_rows, revs = [], [], [], [], [], []
        for hh in range(nh):
            ln = slice(hh * LANE, (hh + 1) * LANE)
            for sl, cc, off in ((sf, c, 0), (sb, cb, HEADS)):
                v = v_ref[0, sl, ln]
                qs.append(q_ref[0, sl, ln])
                ks.append(k_ref[0, sl, ln])
                vs.append(jnp.where(lane == ML_HD, jnp.ones_like(v), v))
                i_rows.append(gt_ref[0, off + hd0 + hh, pl.ds(cc, 1), :])
                lf_rows.append(gt_ref[0, 2 * HEADS + off + hd0 + hh, pl.ds(cc, 1), :])
                revs.append(off > 0)
        houts, s_news, m_news = _ml_chunks(qs, ks, vs, i_rows, lf_rows, list(carry[0]), list(carry[1]), revs)
        for hh in range(nh):
            ln = slice(hh * LANE, (hh + 1) * LANE)
            hf_ref[sf, ln] = houts[2 * hh]
            hb_ref[sb, ln] = houts[2 * hh + 1]
        return tuple(s_news), tuple(m_news)

    s0 = jnp.zeros((LANE, LANE), F32)
    m0 = jnp.full((1, 1), M_INIT, F32)
    lax.fori_loop(0, nc_all, step, ((s0,) * (2 * nh), (m0,) * (2 * nh)), unroll=3)

    def finish(c, _):
        sl = pl.ds(pl.multiple_of(c * size, size), size)
        for hh in range(nh):
            ln = slice(hh * LANE, (hh + 1) * LANE)
            h = jnp.where(lane < ML_HD, hf_ref[sl, ln] + hb_ref[sl, ln], 0.0)
            y = h * lax.rsqrt(jnp.sum(h * h, axis=1, keepdims=True) * (1.0 / ML_HD) + EPS) * gain_ref[0, :, ln]
            y_ref[0, sl, ln] = (y * o_ref[0, sl, ln].astype(F32)).astype(BF16)
        return 0

    lax.fori_loop(0, nc_all, finish, 0, unroll=3)


def _mlstm(mlq, mlk, mlv, mlo, gates_t, gain, layer, ctx_len):
    nb, n_all, _ = mlq.shape
    size, nh = ML_CHUNK, ML_HPS
    nc_all = n_all // size
    width = nh * LANE
    ng = HEADS // nh
    kern = functools.partial(_mlstm_kernel, nc_ctx=ctx_len // size, nc_all=nc_all, size=size, nh=nh)
    slab = pl.BlockSpec((1, n_all, width), lambda b, h: (b, 0, h))
    return pl.pallas_call(
        kern,
        grid=(nb, ng),
        in_specs=[slab, slab, slab, slab,
                  pl.BlockSpec((1, 4 * HEADS, nc_all, size), lambda b, h: (b, 0, 0, 0)),
                  pl.BlockSpec((1, 1, width), lambda b, h: (layer * ng + h, 0, 0))],
        out_specs=slab,
        out_shape=jax.ShapeDtypeStruct((nb, n_all, SLAB), BF16),
        scratch_shapes=[pltpu.VMEM((n_all, width), F32), pltpu.VMEM((n_all, width), F32)],
        compiler_params=_cparams(("parallel", "arbitrary")),
        name="mlstm",
    )(mlq, mlk, mlv, mlo, gates_t, gain.reshape(-1, 1, width))


def _hg_tables(size):
    t = np.arange(size)
    tri = (t[None, :] <= t[:, None]).astype(np.float32)
    x = t[:, None] ^ t[None, :]
    lvl = np.where(x < HG_DIAG, 0, np.floor(np.log2(np.maximum(x, 1))).astype(np.int64) - 2)
    code = np.where(t[None, :] <= t[:, None], lvl, -1).astype(np.int32)
    j = np.arange(HG_DIAG * LANE) // LANE
    emat = (np.arange(size)[None, :] % HG_DIAG == j[:, None]).astype(np.float32)
    return (jnp.asarray(np.stack([tri, tri.T]), BF16), jnp.asarray(np.stack([code, code.T])),
            jnp.asarray(emat, BF16))


def _hg_chunks(qs, lfs, vs, sts, tris, codes, emat, revs, w_refs):
    n = len(qs)
    size = qs[0].shape[0]
    bc2s, kks = [], []
    for i in range(n):
        lf2 = lfs[i]
        hi = lf2.astype(BF16)
        r1 = lf2 - hi.astype(F32)
        mid = r1.astype(BF16)
        lo = (r1 - mid.astype(F32)).astype(BF16)
        cs = _dot(tris[i], jnp.concatenate([hi, mid, lo], axis=1))
        bc2s.append((cs[:, 0:LANE] + cs[:, LANE:2 * LANE]) + cs[:, 2 * LANE:3 * LANE])
        kks.append(jnp.maximum(1.0 - jnp.exp2(lf2), 0.0))

    atts = [jnp.zeros((size, size), F32) for _ in range(n)]
    c, lvl = HG_DIAG, 1
    while c < size:
        blk = 2 * c
        for i in range(n):
            ridx = c if revs[i] else c - 1
            b3 = bc2s[i].reshape(size // blk, blk, LANE)
            ref = jnp.broadcast_to(b3[:, ridx:ridx + 1, :], b3.shape).reshape(size, LANE)
            e = jnp.exp2(_neg_abs(bc2s[i] - ref))
            a = _dot_nt((qs[i] * e).astype(BF16), (kks[i] * e).astype(BF16))
            atts[i] = jnp.where(codes[i] == lvl, a, atts[i])
        c, lvl = blk, lvl + 1

    for i in range(n):
        w_refs[i][...] = bc2s[i] - jnp.log2(kks[i])
        ps = []
        for j in range(HG_DIAG):
            wj = jnp.concatenate([jnp.broadcast_to(w_refs[i][pl.ds(HG_DIAG * blk_i + j, 1), :], (HG_DIAG, LANE))
                                  for blk_i in range(size // HG_DIAG)], axis=0)
            ps.append((qs[i] * jnp.exp2(jnp.minimum(bc2s[i] - wj, 0.0))).astype(BF16))
        atts[i] = jnp.where(codes[i] == 0, _dot(jnp.concatenate(ps, axis=1), emat), atts[i])

    outs, new_sts = [], []
    for i in range(n):
        last = 0 if revs[i] else size - 1
        bl = bc2s[i][last:last + 1]
        outs.append(_dot(atts[i].astype(BF16), vs[i])
                    + _dot_nt((qs[i] * jnp.exp2(bc2s[i])).astype(BF16), sts[i].astype(BF16)))
        new_sts.append(sts[i] * jnp.exp2(bl) + _dot_tn(vs[i], (kks[i] * jnp.exp2(bl - bc2s[i])).astype(BF16)))
    return outs, new_sts


def _hgrn_kernel(q_ref, lf0_ref, lf1_ref, v_ref, g_ref, gain_ref, tri_ref, code_ref, emat_ref,
                 y_ref, of_ref, ob_ref, *w_refs, nc_ctx, nc_all, size):
    nc_lat = nc_all - nc_ctx
    low = lax.broadcasted_iota(jnp.int32, (size, LANE), 1) < HG_DV

    def step(c, carry):
        cb = jnp.where(c < nc_ctx, nc_ctx - 1 - c, nc_ctx + nc_lat - 1 - (c - nc_ctx))
        sf = pl.ds(pl.multiple_of(c * size, size), size)
        sb = pl.ds(pl.multiple_of(cb * size, size), size)
        qs, lfs, vs, tris, codes, revs = [], [], [], [], [], []
        for hh in range(2):
            ln = slice(hh * LANE, (hh + 1) * LANE)
            qs += [q_ref[0, sf, ln].astype(F32), q_ref[0, sb, ln].astype(F32)]
            lfs += [lf0_ref[0, sf, ln], lf1_ref[0, sb, ln]]
            vs += [v_ref[0, sf, :], v_ref[0, sb, :]]
            tris += [tri_ref[0], tri_ref[1]]
            codes += [code_ref[0], code_ref[1]]
            revs += [False, True]
        outs, sts = _hg_chunks(qs, lfs, vs, list(carry), tris, codes, emat_ref[...], revs, w_refs)
        of_ref[sf, :] = jnp.where(low, outs[0], outs[2])
        ob_ref[sb, :] = jnp.where(low, outs[1], outs[3])
        return tuple(sts)

    st0 = jnp.zeros((LANE, LANE), F32)
    lax.fori_loop(0, nc_all, step, (st0,) * 4, unroll=2)

    def finish(c, _):
        sl = pl.ds(pl.multiple_of(c * size, size), size)
        o = of_ref[sl, :] + ob_ref[sl, :]
        sq = o * o
        ms0 = jnp.sum(jnp.where(low, sq, 0.0), axis=1, keepdims=True) * (1.0 / HG_DV)
        ms1 = jnp.sum(jnp.where(low, 0.0, sq), axis=1, keepdims=True) * (1.0 / HG_DV)
        y = o * lax.rsqrt(jnp.where(low, ms0, ms1) + EPS) * gain_ref[0]
        y_ref[0, sl, :] = (y * g_ref[0, sl, :].astype(F32)).astype(BF16)
        return 0

    lax.fori_loop(0, nc_all, finish, 0, unroll=6)


def _hgrn(hgq, hgf, hgv, hgg, gain, layer, ctx_len):
    nb, n_all, _ = hgq.shape
    size = HG_CHUNK
    nc_all = n_all // size
    ng = HEADS // 2
    kern = functools.partial(_hgrn_kernel, nc_ctx=ctx_len // size, nc_all=nc_all, size=size)
    wide = pl.BlockSpec((1, n_all, 2 * LANE), lambda b, h: (b, 0, h))
    slab = pl.BlockSpec((1, n_all, LANE), lambda b, h: (b, 0, h))
    tri, code, emat = _hg_tables(size)
    return pl.pallas_call(
        kern,
        grid=(nb, ng),
        in_specs=[wide, wide,
                  pl.BlockSpec((1, n_all, 2 * LANE), lambda b, h: (b, 0, ng + h)),
                  slab, slab,
                  pl.BlockSpec((1, 1, LANE), lambda b, h: (layer * ng + h, 0, 0)),
                  pl.BlockSpec(tri.shape, lambda b, h: (0, 0, 0)),
                  pl.BlockSpec(code.shape, lambda b, h: (0, 0, 0)),
                  pl.BlockSpec(emat.shape, lambda b, h: (0, 0))],
        out_specs=slab,
        out_shape=jax.ShapeDtypeStruct((nb, n_all, HG_W), BF16),
        scratch_shapes=[pltpu.VMEM((n_all, LANE), F32), pltpu.VMEM((n_all, LANE), F32)] + [pltpu.VMEM((size, LANE), F32)] * 4,
        compiler_params=_cparams(("parallel", "arbitrary")),
        name="hgrn2",
    )(hgq, hgf, hgf, hgv, hgg, gain, tri, code, emat)


def _attn_kernel(q_ref, k_ref, vt_ref, lam_ref, gain_ref, y_ref, *scratch, tq, kc, nh, ctx_len, lam_init):
    n_all = k_ref.shape[1]
    n_tiles = (n_all - ctx_len) // tq
    s_refs = [scratch[2 * hh:2 * hh + 2] for hh in range(nh)]
    m_refs = [scratch[2 * nh + 2 * hh:2 * nh + 2 * hh + 2] for hh in range(nh)]
    lv = lam_ref[0]
    lam = (jnp.exp(jnp.sum(lv[0:1] * lv[1:2], axis=1, keepdims=True))
           - jnp.exp(jnp.sum(lv[2:3] * lv[3:4], axis=1, keepdims=True)) + lam_init)
    lane = lax.broadcasted_iota(jnp.int32, (tq, LANE), 1)
    vrow = lax.broadcasted_iota(jnp.int32, (LANE, kc), 0)
    lanes = [slice(hh * LANE, (hh + 1) * LANE) for hh in range(nh)]

    def rows(t):
        return pl.ds(pl.multiple_of(ctx_len + t * tq, tq), tq)

    def load_qq(sl, hh):
        q = q_ref[0, sl, lanes[hh]]
        zero = jnp.zeros_like(q)
        return jnp.concatenate([jnp.where(lane < DA_HD, q, zero), jnp.where(lane < DA_HD, zero, q)], axis=0)

    def stage_a_chunk(qq, hh, slot, c, m):
        s = _dot_nt(k_ref[0, c * kc:(c + 1) * kc, lanes[hh]], qq)
        s_refs[hh][slot][c * kc:(c + 1) * kc, :] = s
        cm = jnp.max(s, axis=0, keepdims=True)
        return cm if m is None else jnp.maximum(m, cm)

    def stage_b_chunk(hh, slot, c, m, acc):
        p = jnp.exp2(s_refs[hh][slot][c * kc:(c + 1) * kc, :] - m).astype(BF16)
        vt = vt_ref[0, lanes[hh], c * kc:(c + 1) * kc]
        pv = _dot(jnp.where(vrow == DA_VD, jnp.ones_like(vt), vt), p)
        return pv if acc is None else acc + pv

    def finish(acc, sl, hh):
        r0 = 1.0 / acc[DA_VD:DA_VD + 1, 0:tq]
        r1 = lam / acc[DA_VD:DA_VD + 1, tq:2 * tq]
        o = (acc[:, 0:tq] * r0 - acc[:, tq:2 * tq] * r1).T
        o = jnp.where(lane < DA_VD, o, 0.0)
        y = o * lax.rsqrt(jnp.sum(o * o, axis=1, keepdims=True) * (1.0 / DA_VD) + EPS) * gain_ref[0]
        y_ref[0, sl, lanes[hh]] = (y * (1.0 - lam_init)).astype(BF16)

    def stages(sl_a, sl_b, slot_a, nk_a, nk_b):
        slot_b = 1 - slot_a
        qqs = [load_qq(sl_a, hh) for hh in range(nh)] if sl_a is not None else None
        m_bs = [m_refs[hh][slot_b][0:1, :] for hh in range(nh)] if sl_b is not None else None
        ms, accs = [None] * nh, [None] * nh
        for c in range(max(nk_a, nk_b) // kc):
            for hh in range(nh):
                if sl_a is not None and c < nk_a // kc:
                    ms[hh] = stage_a_chunk(qqs[hh], hh, slot_a, c, ms[hh])
                if sl_b is not None and c < nk_b // kc:
                    accs[hh] = stage_b_chunk(hh, slot_b, c, m_bs[hh], accs[hh])
        for hh in range(nh):
            if sl_a is not None:
                m_refs[hh][slot_a][...] = jnp.broadcast_to(ms[hh], (8, 2 * tq))
            if sl_b is not None:
                finish(accs[hh], sl_b, hh)

    ctx_rows = pl.ds(0, tq)
    stages(ctx_rows, None, 1, ctx_len, 0)
    stages(rows(0), ctx_rows, 0, n_all, ctx_len)

    def body(i, _):
        t = 1 + 2 * i
        stages(rows(t), rows(t - 1), 1, n_all, n_all)
        stages(rows(t + 1), rows(t), 0, n_all, n_all)
        return 0

    lax.fori_loop(0, (n_tiles - 2) // 2, body, 0)
    stages(rows(n_tiles - 1), rows(n_tiles - 2), 1, n_all, n_all)
    stages(None, rows(n_tiles - 1), 0, 0, n_all)


def _attn(daq, dak, dav_t, lam_pad, gain, layer, ctx_len, lam_init):
    nb, n_all, _ = daq.shape
    tq, kc, nh = ATT_TQ, ATT_KC, ATT_HPS
    n_tiles = (n_all - ctx_len) // tq
    assert n_tiles >= 2 and n_tiles % 2 == 0 and ctx_len == tq and ctx_len % kc == 0 and n_all % kc == 0
    kern = functools.partial(_attn_kernel, tq=tq, kc=kc, nh=nh, ctx_len=ctx_len, lam_init=lam_init)
    slab = pl.BlockSpec((1, n_all, nh * LANE), lambda b, h: (b, 0, h))
    sbuf, mbuf = pltpu.VMEM((n_all, 2 * tq), F32), pltpu.VMEM((8, 2 * tq), F32)
    return pl.pallas_call(
        kern,
        grid=(nb, HEADS // nh),
        in_specs=[slab, slab,
                  pl.BlockSpec((1, nh * LANE, n_all), lambda b, h: (b, h, 0)),
                  pl.BlockSpec((1, 4, LANE), lambda b, h: (layer, 0, 0)),
                  pl.BlockSpec((1, 1, LANE), lambda b, h: (layer, 0, 0))],
        out_specs=slab,
        out_shape=jax.ShapeDtypeStruct((nb, n_all, SLAB), BF16),
        scratch_shapes=[sbuf] * (2 * nh) + [mbuf] * (2 * nh),
        compiler_params=_cparams(("parallel", "arbitrary")),
        name="diff_attn",
    )(daq, dak, dav_t, lam_pad, gain)


def _resident(shape, index_map):
    return pl.BlockSpec(shape, index_map, pipeline_mode=pl.Buffered(1))


def _mix_ffn_kernel(yml_ref, yhg_ref, yda_ref, x_ref, g1x_ref, g1c_ref, a2x_ref, a2c_ref, s2x_ref, s2c_ref,
                    g2x_ref, g2c_ref, wo_ref, gpm_ref, gpf_ref, wg_ref, wu_ref, wd_ref, gqf_ref, o_ref, *, tm, nsub, ctx_len, t0):
    row0 = (pl.program_id(1) + t0) * tm
    split = min(ctx_len % tm if ctx_len % tm else tm, tm)
    top_is_ctx = row0 < ctx_len

    def mod(c_ref, x_ref_, lo):
        return jnp.where(top_is_ctx, c_ref[0, 0], x_ref_[0, 0]) if lo < split else x_ref_[0, 0]

    sub = tm // nsub
    x1s, h2s = [], []
    for i in range(nsub):
        lo, rs = i * sub, slice(i * sub, (i + 1) * sub)
        y = (_dot(yml_ref[0, rs, :], wo_ref[0, 0:SLAB, :]) + _dot(yhg_ref[0, rs, :], wo_ref[0, SLAB:SLAB + HG_W, :])
             + _dot(yda_ref[0, rs, :], wo_ref[0, SLAB + HG_W:2 * SLAB + HG_W, :]))
        x1 = x_ref[0, rs, :] + _rms(y, mod(g1c_ref, g1x_ref, lo) * gpm_ref[0])
        h2s.append((_rms(x1, gpf_ref[0] * (1.0 + mod(a2c_ref, a2x_ref, lo))) + mod(s2c_ref, s2x_ref, lo)).astype(BF16))
        x1s.append(x1)
    for i in range(nsub):
        lo, rs = i * sub, slice(i * sub, (i + 1) * sub)
        act = _silu(_dot(h2s[i], wg_ref[0])) * _dot(h2s[i], wu_ref[0])
        f = _dot(act.astype(BF16), wd_ref[0])
        o_ref[0, rs, :] = x1s[i] + _rms(f, mod(g2c_ref, g2x_ref, lo) * gqf_ref[0])


def _mix_ffn(yml, yhg, yda, xs, mod4, w_out_pad, g_post_mix, g_pre_ffn, wg, wu, wd, g_post_ffn, layer, ctx_len,
             tm, nsub, t0):
    nb, n_all, d = xs.shape
    dff = wg.shape[-1]
    assert tm % nsub == 0 and (ctx_len % tm) % (tm // nsub) == 0
    kern = functools.partial(_mix_ffn_kernel, tm=tm, nsub=nsub, ctx_len=ctx_len, t0=t0)
    lsel = lambda b, t: (layer, 0, 0)
    tile = lambda width: pl.BlockSpec((1, tm, width), lambda b, t: (b, t + t0, 0))
    return pl.pallas_call(
        kern,
        grid=(nb, n_all // tm - t0),
        in_specs=[tile(SLAB), tile(HG_W), tile(SLAB), tile(d)] + _mod_specs(layer, nb, (2, 4, 3, 5), d) + [
            _resident((1, 2 * SLAB + HG_W, d), lsel),
            pl.BlockSpec((1, 1, d), lsel),
            pl.BlockSpec((1, 1, d), lsel),
            _resident((1, d, dff), lsel),
            _resident((1, d, dff), lsel),
            _resident((1, dff, d), lsel),
            pl.BlockSpec((1, 1, d), lsel)],
        out_specs=pl.BlockSpec((1, tm, d), lambda b, t: (b, t, 0)),
        out_shape=jax.ShapeDtypeStruct((nb, n_all - t0 * tm, d), F32),
        compiler_params=_cparams(("parallel", "arbitrary")),
        name="mix_ffn",
    )(yml, yhg, yda, xs, *([mod4] * 8), w_out_pad, g_post_mix, g_pre_ffn, wg, wu, wd, g_post_ffn)


def _pad_heads(w, hd):
    lead = w.shape[:-1]
    w = w.reshape(*lead, HEADS, hd)
    w = jnp.pad(w, [(0, 0)] * len(lead) + [(0, 0), (0, LANE - hd)])
    return w.reshape(*lead, SLAB)


def _pad_in_cols(w, extra_f=None):
    sizes = (4 * ML_HD,) * 4 + (2 * HEADS, 2 * HEADS, HEADS * HG_DK, 2 * HEADS * HG_DK, HEADS * HG_DV, HEADS * HG_DV,
             2 * HEADS * DA_HD, 2 * HEADS * DA_HD, HEADS * DA_VD)
    parts = []
    off = 0
    for s in sizes:
        parts.append(w[..., off:off + s])
        off += s
    gi, gf = parts[4], parts[5]
    if extra_f is not None:
        gf = gf + extra_f
    gate = jnp.concatenate([gi, gf], axis=-1)
    gate = jnp.pad(gate, [(0, 0)] * (gate.ndim - 1) + [(0, LANE - 4 * HEADS)])
    return jnp.concatenate([
        _pad_heads(parts[0], ML_HD), _pad_heads(parts[1], ML_HD), _pad_heads(parts[2], ML_HD), _pad_heads(parts[3], ML_HD),
        gate, parts[6], parts[7], parts[8], parts[9],
        _pad_heads(parts[10], DA_VD), _pad_heads(parts[11], DA_VD), _pad_heads(parts[12], DA_VD)], axis=-1)


def _pad_out_rows(w_out):
    depth, _, d = w_out.shape
    wt = jnp.swapaxes(w_out, 1, 2)
    ml, hg, da = wt[..., :4 * ML_HD], wt[..., 4 * ML_HD:4 * ML_HD + HEADS * HG_DV], wt[..., 4 * ML_HD + HEADS * HG_DV:]
    wp = jnp.concatenate([_pad_heads(ml, ML_HD), hg, _pad_heads(da, DA_VD)], axis=-1)
    return jnp.swapaxes(wp, 1, 2)


def _rope_tables(n_lat, ctx_len):
    rows = n_lat // GRID_W
    row = jnp.repeat(jnp.arange(rows), GRID_W).astype(F32)
    col = jnp.tile(jnp.arange(GRID_W), rows).astype(F32)
    half = DA_HD // 2
    inv = ROPE_BASE ** (-jnp.arange(0, half, 2, dtype=F32) / half)
    ang_r, ang_c = row[:, None] * inv, col[:, None] * inv
    zero = jnp.zeros_like(ang_r)

    def one_map(fr, fc, sel):
        r, c = fr(ang_r), fc(ang_c)
        if sel == "cos":
            return jnp.concatenate([r, r, c, c], axis=1)
        if sel == "a":
            return jnp.concatenate([-r, zero, -c, zero], axis=1)
        return jnp.concatenate([zero, r, zero, c], axis=1)

    tabs = []
    for sel, fn in (("cos", jnp.cos), ("a", jnp.sin), ("b", jnp.sin)):
        m = one_map(fn, fn, sel)
        lat = jnp.pad(jnp.concatenate([m, m], axis=1), ((0, 0), (0, LANE - 2 * DA_HD)))
        fill = 1.0 if sel == "cos" else 0.0
        ctx_rows = jnp.pad(jnp.full((ctx_len, 2 * DA_HD), fill, F32), ((0, 0), (0, LANE - 2 * DA_HD)))
        tabs.append(jnp.concatenate([ctx_rows, lat], axis=0))
    return tabs


def kernel(x, c, ctx, c_ctx, w_ada, b_ada, g_pre_mix, g_post_mix, g_pre_ffn, g_post_ffn, w_in, b_in, w_out,
           ml_f_bias, ml_norm, hg_lb, hg_norm, da_lambda, da_norm, w_ffn_gate, w_ffn_up, w_ffn_down):
    nb, n_lat, d = x.shape
    ctx_len = ctx.shape[1]
    depth = w_ada.shape[0]
    n_all = ctx_len + n_lat

    w_pad = _pad_in_cols(w_in).astype(BF16)
    b_pad = _pad_in_cols(b_in, extra_f=ml_f_bias)[:, None, :]
    w_out_pad = _pad_out_rows(w_out).astype(BF16)
    wg, wu, wd = w_ffn_gate.astype(BF16), w_ffn_up.astype(BF16), w_ffn_down.astype(BF16)
    ml_gain = _pad_heads(ml_norm, ML_HD).reshape(depth * HEADS, 1, LANE)
    hg_gain = hg_norm.reshape(-1, 1, LANE)
    da_gain = jnp.pad(da_norm, ((0, 0), (0, LANE - DA_VD)))[:, None, :]
    lam_pad = jnp.pad(da_lambda.astype(F32), ((0, 0), (0, 0), (0, LANE - DA_HD)))
    sm = jax.nn.softmax(hg_lb.astype(F32), axis=0)
    lbs = jnp.cumsum(sm, axis=0) - sm[0:1]
    lbf = jnp.maximum(lbs, LB_FLOOR)[:, None, :]
    oml = (1.0 - lbs)[:, None, :]
    rope_tabs = _rope_tables(n_lat, ctx_len)
    r3 = lambda g: g[:, None, :]

    mp = -(-(nb + 1) // 8) * 8
    cc = jnp.concatenate([c, c_ctx[None, :], jnp.zeros((mp - nb - 1, d), F32)], axis=0)
    mod4 = _ada(cc, w_ada, b_ada).reshape(depth, mp, 1, 6 * d)

    xs = jnp.concatenate([ctx, x], axis=1)
    for l in range(depth):
        lam_init = 0.8 - 0.6 * math.exp(-0.3 * l)
        (mlq, mlk, mlv, mlo, gates, hgq, hgf, hgv, hgg, daq, dak, dav_t) = _inproj(
            xs, mod4, r3(g_pre_mix), w_pad, b_pad, rope_tabs, lbf, oml, l, ctx_len,
            *((IN_TM, IN_SUB) if n_all % IN_TM == 0 else (IN_TM_ALT, IN_SUB_ALT)))
        gates_t = jnp.swapaxes(gates[:, :, :4 * HEADS], 1, 2).reshape(nb, 4 * HEADS, n_all // ML_CHUNK, ML_CHUNK)
        yml = _mlstm(mlq, mlk, mlv, mlo, gates_t, ml_gain, l, ctx_len)
        yhg = _hgrn(hgq, hgf, hgv, hgg, hg_gain, l, ctx_len)
        yda = _attn(daq, dak, dav_t, lam_pad, da_gain, l, ctx_len, lam_init)
        last = l == depth - 1
        tm, nsub = (MIX_TM, MIX_SUB) if not last and n_all % MIX_TM == 0 else (MIX_TM_LAST, MIX_SUB_LAST)
        xs = _mix_ffn(yml, yhg, yda, xs, mod4, w_out_pad, r3(g_post_mix), r3(g_pre_ffn), wg, wu, wd,
                      r3(g_post_ffn), l, ctx_len, tm, nsub, ctx_len // tm if last else 0)
    return xs
```

```python
import functools
import math

import numpy as np
import jax
import jax.numpy as jnp
from jax import lax
from jax.experimental import pallas as pl
from jax.experimental.pallas import tpu as pltpu

F32 = jnp.float32
BF16 = jnp.bfloat16
HIGHEST = lax.Precision.HIGHEST

LANE = 128
VMEM_LIMIT = 52 * 1024 * 1024

EPS = 1e-6
NEG = -1e30
LB_FLOOR = 1e-30
M_INIT = -1e30
GRID_W = 64
ROPE_BASE = 10000.0

HEADS = 4
ML_HD = 96
HG_DV = 64
HG_DK = 128
DA_HD = 48
DA_VD = 2 * DA_HD
SLAB = HEADS * LANE
HG_W = HEADS * HG_DV

ML_CHUNK = 256
ML_HPS = 2
HG_CHUNK = 128
HG_DIAG = 8
ATT_TQ = 256
IN_CW = 256
IN_TM, IN_SUB = 768, 3
IN_TM_ALT, IN_SUB_ALT = 256, 2
MIX_TM, MIX_SUB = 768, 3
MIX_TM_LAST, MIX_SUB_LAST = 256, 2
ATT_KC = 256
ATT_HPS = 2
LOG2E = 1.4426950408889634

OFF_MLQ, OFF_MLK, OFF_MLV, OFF_MLO = 0, SLAB, 2 * SLAB, 3 * SLAB
OFF_GATE = 4 * SLAB
OFF_HGQ = OFF_GATE + LANE
OFF_HGF = OFF_HGQ + SLAB
OFF_HGV = OFF_HGF + 2 * SLAB
OFF_HGG = OFF_HGV + HG_W
OFF_DAQ = OFF_HGG + HG_W
OFF_DAK = OFF_DAQ + SLAB
OFF_DAV = OFF_DAK + SLAB
NP_IN = OFF_DAV + SLAB


def _cparams(sem):
    return pltpu.CompilerParams(dimension_semantics=sem, vmem_limit_bytes=VMEM_LIMIT)


def _silu(x):
    return x * jax.nn.sigmoid(x)


def _log_sigmoid(z):
    return jnp.minimum(z, 0.0) - jnp.log1p(jnp.exp(-jnp.abs(z)))


def _neg_abs(x):
    bits = lax.bitcast_convert_type(x, jnp.uint32) | jnp.uint32(0x80000000)
    return lax.bitcast_convert_type(bits, F32)


def _rms(x, g):
    return x * lax.rsqrt(jnp.mean(x * x, axis=-1, keepdims=True) + EPS) * g


def _dot(a, b):
    return jnp.dot(a, b, preferred_element_type=F32)


def _dot_nt(a, b):
    return lax.dot_general(a, b, (((1,), (1,)), ((), ())), preferred_element_type=F32)


def _dot_tn(a, b):
    return lax.dot_general(a, b, (((0,), (0,)), ((), ())), preferred_element_type=F32)


def _ada_kernel(s_ref, w_ref, b_ref, o_ref):
    s = _silu(s_ref[...])
    o_ref[0] = jnp.dot(s, w_ref[0], precision=HIGHEST, preferred_element_type=F32) + b_ref[0]


def _ada(cc, w_ada, b_ada):
    depth, d, d6 = w_ada.shape
    mp = cc.shape[0]
    tn = 2048
    return pl.pallas_call(
        _ada_kernel,
        grid=(depth, d6 // tn),
        in_specs=[pl.BlockSpec((mp, d), lambda l, j: (0, 0)),
                  pl.BlockSpec((1, d, tn), lambda l, j: (l, 0, j)),
                  pl.BlockSpec((1, 1, tn), lambda l, j: (l, 0, j))],
        out_specs=pl.BlockSpec((1, mp, tn), lambda l, j: (l, 0, j)),
        out_shape=jax.ShapeDtypeStruct((depth, mp, d6), F32),
        compiler_params=_cparams(("arbitrary", "arbitrary")),
        name="ada_mod",
    )(cc, w_ada, b_ada.reshape(depth, 1, d6))


def _mod_specs(layer, nb, idxs, d):
    specs = []
    for j in idxs:
        specs.append(pl.BlockSpec((1, 1, 1, d), lambda b, t, j=j: (layer, b, 0, j)))
        specs.append(pl.BlockSpec((1, 1, 1, d), lambda b, t, j=j: (layer, nb, 0, j)))
    return specs


def _inproj_kernel(x_ref, sx_ref, sc_ref, ax_ref, ac_ref, g_ref, w_ref, b_ref,
                   cos_ref, sna_ref, snb_ref, lbf_ref, oml_ref,
                   mlq_ref, mlk_ref, mlv_ref, mlo_ref, gate_ref,
                   hgq_ref, hgf_ref, hgv_ref, hgg_ref, daq_ref, dak_ref, dav_ref,
                   *, tm, nsub, ctx_len):
    split = min(ctx_len % tm if ctx_len % tm else tm, tm)
    top_is_ctx = pl.program_id(1) * tm < ctx_len
    sub = tm // nsub
    half = DA_HD // 4
    lane = lax.broadcasted_iota(jnp.int32, (sub, LANE), 1)

    def rope(p, rs):
        cos, sna, snb = cos_ref[rs, :], sna_ref[rs, :], snb_ref[rs, :]
        outs = []
        for i in range(p.shape[1] // LANE):
            xh = p[:, i * LANE:(i + 1) * LANE]
            outs.append(xh * cos + pltpu.roll(xh, LANE - half, 1) * sna + pltpu.roll(xh, half, 1) * snb)
        return jnp.concatenate(outs, axis=1)

    def log2_f(p, c0, rs):
        w = p.shape[1]
        return jnp.log2(lbf_ref[0, :, c0:c0 + w] + oml_ref[0, :, c0:c0 + w] * jax.nn.sigmoid(p))

    def gates(p, c0, rs):
        return jnp.where((lane >= 2 * HEADS) & (lane < 4 * HEADS), _log_sigmoid(p), p)

    plain = lambda p, c0, rs: p
    groups = {
        "mlq": (mlq_ref, OFF_MLQ, SLAB, plain),
        "mlk": (mlk_ref, OFF_MLK, SLAB, lambda p, c0, rs: p * (ML_HD ** -0.5)),
        "mlv": (mlv_ref, OFF_MLV, SLAB, plain),
        "mlo": (mlo_ref, OFF_MLO, SLAB, lambda p, c0, rs: jax.nn.sigmoid(p)),
        "gate": (gate_ref, OFF_GATE, LANE, gates),
        "hgq": (hgq_ref, OFF_HGQ, SLAB, lambda p, c0, rs: _silu(p)),
        "hgf": (hgf_ref, OFF_HGF, 2 * SLAB, log2_f),
        "hgv": (hgv_ref, OFF_HGV, HG_W, plain),
        "hgg": (hgg_ref, OFF_HGG, HG_W, lambda p, c0, rs: _silu(p)),
        "daq": (daq_ref, OFF_DAQ, SLAB, lambda p, c0, rs: rope(p, rs) * (LOG2E * DA_HD ** -0.5)),
        "dak": (dak_ref, OFF_DAK, SLAB, lambda p, c0, rs: rope(p, rs)),
        "dav": (dav_ref, OFF_DAV, SLAB, plain),
    }
    order = [("hgf", 0), ("mlq", 0), ("hgf", 1), ("mlq", 1), ("hgf", 2), ("mlk", 0), ("hgf", 3), ("mlk", 1),
             ("mlo", 0), ("mlv", 0), ("mlo", 1), ("mlv", 1), ("hgq", 0), ("dav", 0), ("hgq", 1), ("dav", 1),
             ("daq", 0), ("hgv", 0), ("daq", 1), ("gate", 0), ("dak", 0), ("hgg", 0), ("dak", 1)]
    for i in range(nsub):
        rs = slice(i * sub, (i + 1) * sub)
        ctx_rows = i * sub < split
        shift = jnp.where(top_is_ctx, sc_ref[0, 0], sx_ref[0, 0]) if ctx_rows else sx_ref[0, 0]
        scale = 1.0 + (jnp.where(top_is_ctx, ac_ref[0, 0], ax_ref[0, 0]) if ctx_rows else ax_ref[0, 0])
        h = (_rms(x_ref[0, rs, :], g_ref[0] * scale) + shift).astype(BF16)
        for name, ci in order:
            ref, off, width, fn = groups[name]
            cw = min(width, IN_CW)
            c0 = ci * cw
            p = _dot(h, w_ref[0, :, off + c0:off + c0 + cw]) + b_ref[0, :, off + c0:off + c0 + cw]
            if name == "dav":
                ref[0, c0:c0 + cw, rs] = p.T.astype(ref.dtype)
            elif name == "gate":
                ref[0, :, rs] = fn(p, c0, rs).T[0:4 * HEADS, :]
            else:
                ref[0, rs, c0:c0 + cw] = fn(p, c0, rs).astype(ref.dtype)


def _inproj(xs, mod4, g_pre, w_pad, b_pad, rope_tabs, lbf, oml, layer, ctx_len, tm, nsub):
    nb, n_all, d = xs.shape
    assert tm % nsub == 0 and (ctx_len % tm) % (tm // nsub) == 0
    kern = functools.partial(_inproj_kernel, tm=tm, nsub=nsub, ctx_len=ctx_len)
    lsel = lambda b, t: (layer, 0, 0)
    tile = lambda width: pl.BlockSpec((1, tm, width), lambda b, t: (b, t, 0))
    tab = pl.BlockSpec((tm, LANE), lambda b, t: (t, 0))
    bf = lambda width: jax.ShapeDtypeStruct((nb, n_all, width), BF16)
    f32 = lambda width: jax.ShapeDtypeStruct((nb, n_all, width), F32)
    out_shapes = [bf(SLAB)] * 4 + [f32(LANE)] + [bf(SLAB), f32(2 * SLAB), bf(HG_W), bf(HG_W)] + [bf(SLAB)] * 3
    out_specs = [tile(s.shape[-1]) for s in out_shapes]
    out_shapes[-1] = jax.ShapeDtypeStruct((nb, SLAB, n_all), BF16)
    out_specs[-1] = pl.BlockSpec((1, SLAB, tm), lambda b, t: (b, 0, t))
    out_shapes[4] = jax.ShapeDtypeStruct((nb, 4 * HEADS, n_all), F32)
    out_specs[4] = pl.BlockSpec((1, 4 * HEADS, tm), lambda b, t: (b, 0, t))
    return pl.pallas_call(
        kern,
        grid=(nb, n_all // tm),
        in_specs=[tile(d)] + _mod_specs(layer, nb, (0, 1), d) + [
            pl.BlockSpec((1, 1, d), lsel),
            pl.BlockSpec((1, d, NP_IN), lsel, pipeline_mode=pl.Buffered(1)),
            pl.BlockSpec((1, 1, NP_IN), lsel),
            tab, tab, tab,
            pl.BlockSpec((1, 1, 2 * SLAB), lsel),
            pl.BlockSpec((1, 1, 2 * SLAB), lsel)],
        out_specs=out_specs,
        out_shape=out_shapes,
        compiler_params=_cparams(("parallel", "arbitrary")),
        name="in_proj",
    )(xs, mod4, mod4, mod4, mod4, g_pre, w_pad, b_pad, *rope_tabs, lbf, oml)


def _ml_chunks(qs, ks, vs, i_rows, lf_rows, s_exts, ms, revs):
    n = len(qs)
    size = qs[0].shape[0]
    ii = lax.broadcasted_iota(jnp.int32, (size, size), 0)
    jj = lax.broadcasted_iota(jnp.int32, (size, size), 1)
    lane = lax.broadcasted_iota(jnp.int32, (size, LANE), 1)
    cums = {rev: jnp.where((ii >= jj) if rev else (ii <= jj), 1.0, 0.0).astype(BF16) for rev in set(revs)}
    seens = {rev: (jj >= ii) if rev else (jj <= ii) for rev in set(revs)}
    row16 = lax.broadcasted_iota(jnp.int32, (16, size), 0)

    b_cols, g_cols, g_rows = [], [], []
    for i in range(n):
        lf2 = lf_rows[i] * LOG2E
        hi = lf2.astype(BF16).astype(F32)
        mid = (lf2 - hi).astype(BF16).astype(F32)
        lo = (lf2 - hi) - mid
        parts = jnp.where(row16 == 0, hi, jnp.where(row16 == 1, mid, jnp.where(row16 == 2, lo, 0.0))).astype(BF16)
        cs = _dot(parts, cums[revs[i]])
        b_row = (cs[0:1] + cs[1:2]) + cs[2:3]
        g_row = i_rows[i] * LOG2E - b_row
        b_cols.append(jnp.sum(jnp.where(ii == jj, b_row, 0.0), axis=1, keepdims=True))
        g_cols.append(jnp.sum(jnp.where(ii == jj, g_row, 0.0), axis=1, keepdims=True))
        g_rows.append(g_row)

    mts, dws, w_inters = [], [], []
    for i in range(n):
        dmat = jnp.where(seens[revs[i]], b_cols[i] + g_rows[i], NEG)
        a_col = b_cols[i] + ms[i]
        mt = jnp.maximum(a_col, jnp.max(dmat, axis=1, keepdims=True))
        mts.append(mt)
        w_inters.append(jnp.exp2(a_col - mt))
        dws.append(jnp.exp2(dmat - mt))

    ress = []
    for i in range(n):
        qk = _dot_nt(qs[i], ks[i]) * dws[i]
        ress.append(_dot(qk.astype(BF16), vs[i]) + w_inters[i] * _dot(qs[i], s_exts[i].astype(BF16)))

    houts, s_news, m_news = [], [], []
    for i in range(n):
        den = jnp.sum(jnp.where(lane == ML_HD, ress[i], 0.0), axis=1, keepdims=True)
        houts.append(ress[i] * (1.0 / jnp.maximum(jnp.abs(den), jnp.exp2(-mts[i]))))
        last = 0 if revs[i] else size - 1
        m_new = mts[i][last:last + 1]
        b_last = b_cols[i][last:last + 1]
        decay = jnp.exp2(b_last + ms[i] - m_new)
        wk = jnp.exp2(b_last + g_cols[i] - m_new)
        s_news.append(decay * s_exts[i] + _dot_tn((wk * ks[i].astype(F32)).astype(BF16), vs[i]))
        m_news.append(m_new)
    return houts, s_news, m_news


def _mlstm_kernel(q_ref, k_ref, v_ref, o_ref, gt_ref, gain_ref, y_ref, hf_ref, hb_ref, *, nc_ctx, nc_all, size, nh):
    hd0 = pl.program_id(1) * nh
    lane = lax.broadcasted_iota(jnp.int32, (size, LANE), 1)
    nc_lat = nc_all - nc_ctx

    def step(c, carry):
        cb = jnp.where(c < nc_ctx, nc_ctx - 1 - c, nc_ctx + nc_lat - 1 - (c - nc_ctx))
        sf = pl.ds(pl.multiple_of(c * size, size), size)
        sb = pl.ds(pl.multiple_of(cb * size, size), size)
        qs, ks, vs, i_rows, lf_rows, revs = [], [], [], [], [], []
        for hh in range(nh):
            ln = slice(hh * LANE, (hh + 1) * LANE)
            for sl, cc, off in ((sf, c, 0), (sb, cb, HEADS)):
                v = v_ref[0, sl, ln]
                qs.append(q_ref[0, sl, ln])
                ks.append(k_ref[0, sl, ln])
                vs.append(jnp.where(lane == ML_HD, jnp.ones_like(v), v))
                i_rows.append(gt_ref[0, off + hd0 + hh, pl.ds(cc, 1), :])
                lf_rows.append(gt_ref[0, 2 * HEADS + off + hd0 + hh, pl.ds(cc, 1), :])
                revs.append(off > 0)
        houts, s_news, m_news = _ml_chunks(qs, ks, vs, i_rows, lf_rows, list(carry[0]), list(carry[1]), revs)
        for hh in range(nh):
            ln = slice(hh * LANE, (hh + 1) * LANE)
            hf_ref[sf, ln] = houts[2 * hh]
            hb_ref[sb, ln] = houts[2 * hh + 1]
        return tuple(s_news), tuple(m_news)

    s0 = jnp.zeros((LANE, LANE), F32)
    m0 = jnp.full((1, 1), M_INIT, F32)
    lax.fori_loop(0, nc_all, step, ((s0,) * (2 * nh), (m0,) * (2 * nh)), unroll=3)

    def finish(c, _):
        sl = pl.ds(pl.multiple_of(c * size, size), size)
        for hh in range(nh):
            ln = slice(hh * LANE, (hh + 1) * LANE)
            h = jnp.where(lane < ML_HD, hf_ref[sl, ln] + hb_ref[sl, ln], 0.0)
            y = h * lax.rsqrt(jnp.sum(h * h, axis=1, keepdims=True) * (1.0 / ML_HD) + EPS) * gain_ref[0, :, ln]
            y_ref[0, sl, ln] = (y * o_ref[0, sl, ln].astype(F32)).astype(BF16)
        return 0

    lax.fori_loop(0, nc_all, finish, 0, unroll=3)


def _mlstm(mlq, mlk, mlv, mlo, gates_t, gain, layer, ctx_len):
    nb, n_all, _ = mlq.shape
    size, nh = ML_CHUNK, ML_HPS
    nc_all = n_all // size
    width = nh * LANE
    ng = HEADS // nh
    kern = functools.partial(_mlstm_kernel, nc_ctx=ctx_len // size, nc_all=nc_all, size=size, nh=nh)
    slab = pl.BlockSpec((1, n_all, width), lambda b, h: (b, 0, h))
    return pl.pallas_call(
        kern,
        grid=(nb, ng),
        in_specs=[slab, slab, slab, slab,
                  pl.BlockSpec((1, 4 * HEADS, nc_all, size), lambda b, h: (b, 0, 0, 0)),
                  pl.BlockSpec((1, 1, width), lambda b, h: (layer * ng + h, 0, 0))],
        out_specs=slab,
        out_shape=jax.ShapeDtypeStruct((nb, n_all, SLAB), BF16),
        scratch_shapes=[pltpu.VMEM((n_all, width), F32), pltpu.VMEM((n_all, width), F32)],
        compiler_params=_cparams(("parallel", "arbitrary")),
        name="mlstm",
    )(mlq, mlk, mlv, mlo, gates_t, gain.reshape(-1, 1, width))


def _hg_tables(size):
    t = np.arange(size)
    tri = (t[None, :] <= t[:, None]).astype(np.float32)
    x = t[:, None] ^ t[None, :]
    lvl = np.where(x < HG_DIAG, 0, np.floor(np.log2(np.maximum(x, 1))).astype(np.int64) - 2)
    code = np.where(t[None, :] <= t[:, None], lvl, -1).astype(np.int32)
    j = np.arange(HG_DIAG * LANE) // LANE
    emat = (np.arange(size)[None, :] % HG_DIAG == j[:, None]).astype(np.float32)
    return (jnp.asarray(np.stack([tri, tri.T]), BF16), jnp.asarray(np.stack([code, code.T])),
            jnp.asarray(emat, BF16))


def _hg_chunks(qs, lfs, vs, sts, tris, codes, emat, revs, w_refs):
    n = len(qs)
    size = qs[0].shape[0]
    bc2s, kks = [], []
    for i in range(n):
        lf2 = lfs[i]
        hi = lf2.astype(BF16)
        r1 = lf2 - hi.astype(F32)
        mid = r1.astype(BF16)
        lo = (r1 - mid.astype(F32)).astype(BF16)
        cs = _dot(tris[i], jnp.concatenate([hi, mid, lo], axis=1))
        bc2s.append((cs[:, 0:LANE] + cs[:, LANE:2 * LANE]) + cs[:, 2 * LANE:3 * LANE])
        kks.append(jnp.maximum(1.0 - jnp.exp2(lf2), 0.0))

    atts = [jnp.zeros((size, size), F32) for _ in range(n)]
    c, lvl = HG_DIAG, 1
    while c < size:
        blk = 2 * c
        for i in range(n):
            ridx = c if revs[i] else c - 1
            b3 = bc2s[i].reshape(size // blk, blk, LANE)
            ref = jnp.broadcast_to(b3[:, ridx:ridx + 1, :], b3.shape).reshape(size, LANE)
            e = jnp.exp2(_neg_abs(bc2s[i] - ref))
            a = _dot_nt((qs[i] * e).astype(BF16), (kks[i] * e).astype(BF16))
            atts[i] = jnp.where(codes[i] == lvl, a, atts[i])
        c, lvl = blk, lvl + 1

    for i in range(n):
        w_refs[i][...] = bc2s[i] - jnp.log2(kks[i])
        ps = []
        for j in range(HG_DIAG):
            wj = jnp.concatenate([jnp.broadcast_to(w_refs[i][pl.ds(HG_DIAG * blk_i + j, 1), :], (HG_DIAG, LANE))
                                  for blk_i in range(size // HG_DIAG)], axis=0)
            ps.append((qs[i] * jnp.exp2(jnp.minimum(bc2s[i] - wj, 0.0))).astype(BF16))
        atts[i] = jnp.where(codes[i] == 0, _dot(jnp.concatenate(ps, axis=1), emat), atts[i])

    outs, new_sts = [], []
    for i in range(n):
        last = 0 if revs[i] else size - 1
        bl = bc2s[i][last:last + 1]
        outs.append(_dot(atts[i].astype(BF16), vs[i])
                    + _dot_nt((qs[i] * jnp.exp2(bc2s[i])).astype(BF16), sts[i].astype(BF16)))
        new_sts.append(sts[i] * jnp.exp2(bl) + _dot_tn(vs[i], (kks[i] * jnp.exp2(bl - bc2s[i])).astype(BF16)))
    return outs, new_sts


def _hgrn_kernel(q_ref, lf0_ref, lf1_ref, v_ref, g_ref, gain_ref, tri_ref, code_ref, emat_ref,
                 y_ref, of_ref, ob_ref, *w_refs, nc_ctx, nc_all, size):
    nc_lat = nc_all - nc_ctx
    low = lax.broadcasted_iota(jnp.int32, (size, LANE), 1) < HG_DV

    def step(c, carry):
        cb = jnp.where(c < nc_ctx, nc_ctx - 1 - c, nc_ctx + nc_lat - 1 - (c - nc_ctx))
        sf = pl.ds(pl.multiple_of(c * size, size), size)
        sb = pl.ds(pl.multiple_of(cb * size, size), size)
        qs, lfs, vs, tris, codes, revs = [], [], [], [], [], []
        for hh in range(2):
            ln = slice(hh * LANE, (hh + 1) * LANE)
            qs += [q_ref[0, sf, ln].astype(F32), q_ref[0, sb, ln].astype(F32)]
            lfs += [lf0_ref[0, sf, ln], lf1_ref[0, sb, ln]]
            vs += [v_ref[0, sf, :], v_ref[0, sb, :]]
            tris += [tri_ref[0], tri_ref[1]]
            codes += [code_ref[0], code_ref[1]]
            revs += [False, True]
        outs, sts = _hg_chunks(qs, lfs, vs, list(carry), tris, codes, emat_ref[...], revs, w_refs)
        of_ref[sf, :] = jnp.where(low, outs[0], outs[2])
        ob_ref[sb, :] = jnp.where(low, outs[1], outs[3])
        return tuple(sts)

    st0 = jnp.zeros((LANE, LANE), F32)
    lax.fori_loop(0, nc_all, step, (st0,) * 4, unroll=2)

    def finish(c, _):
        sl = pl.ds(pl.multiple_of(c * size, size), size)
        o = of_ref[sl, :] + ob_ref[sl, :]
        sq = o * o
        ms0 = jnp.sum(jnp.where(low, sq, 0.0), axis=1, keepdims=True) * (1.0 / HG_DV)
        ms1 = jnp.sum(jnp.where(low, 0.0, sq), axis=1, keepdims=True) * (1.0 / HG_DV)
        y = o * lax.rsqrt(jnp.where(low, ms0, ms1) + EPS) * gain_ref[0]
        y_ref[0, sl, :] = (y * g_ref[0, sl, :].astype(F32)).astype(BF16)
        return 0

    lax.fori_loop(0, nc_all, finish, 0, unroll=6)


def _hgrn(hgq, hgf, hgv, hgg, gain, layer, ctx_len):
    nb, n_all, _ = hgq.shape
    size = HG_CHUNK
    nc_all = n_all // size
    ng = HEADS // 2
    kern = functools.partial(_hgrn_kernel, nc_ctx=ctx_len // size, nc_all=nc_all, size=size)
    wide = pl.BlockSpec((1, n_all, 2 * LANE), lambda b, h: (b, 0, h))
    slab = pl.BlockSpec((1, n_all, LANE), lambda b, h: (b, 0, h))
    tri, code, emat = _hg_tables(size)
    return pl.pallas_call(
        kern,
        grid=(nb, ng),
        in_specs=[wide, wide,
                  pl.BlockSpec((1, n_all, 2 * LANE), lambda b, h: (b, 0, ng + h)),
                  slab, slab,
                  pl.BlockSpec((1, 1, LANE), lambda b, h: (layer * ng + h, 0, 0)),
                  pl.BlockSpec(tri.shape, lambda b, h: (0, 0, 0)),
                  pl.BlockSpec(code.shape, lambda b, h: (0, 0, 0)),
                  pl.BlockSpec(emat.shape, lambda b, h: (0, 0))],
        out_specs=slab,
        out_shape=jax.ShapeDtypeStruct((nb, n_all, HG_W), BF16),
        scratch_shapes=[pltpu.VMEM((n_all, LANE), F32), pltpu.VMEM((n_all, LANE), F32)] + [pltpu.VMEM((size, LANE), F32)] * 4,
        compiler_params=_cparams(("parallel", "arbitrary")),
        name="hgrn2",
    )(hgq, hgf, hgf, hgv, hgg, gain, tri, code, emat)


def _attn_kernel(q_ref, k_ref, vt_ref, lam_ref, gain_ref, y_ref, *scratch, tq, kc, nh, ctx_len, lam_init):
    n_all = k_ref.shape[1]
    n_tiles = (n_all - ctx_len) // tq
    s_refs = [scratch[2 * hh:2 * hh + 2] for hh in range(nh)]
    m_refs = [scratch[2 * nh + 2 * hh:2 * nh + 2 * hh + 2] for hh in range(nh)]
    lv = lam_ref[0]
    lam = (jnp.exp(jnp.sum(lv[0:1] * lv[1:2], axis=1, keepdims=True))
           - jnp.exp(jnp.sum(lv[2:3] * lv[3:4], axis=1, keepdims=True)) + lam_init)
    lane = lax.broadcasted_iota(jnp.int32, (tq, LANE), 1)
    vrow = lax.broadcasted_iota(jnp.int32, (LANE, kc), 0)
    lanes = [slice(hh * LANE, (hh + 1) * LANE) for hh in range(nh)]

    def rows(t):
        return pl.ds(pl.multiple_of(ctx_len + t * tq, tq), tq)

    def load_qq(sl, hh):
        q = q_ref[0, sl, lanes[hh]]
        zero = jnp.zeros_like(q)
        return jnp.concatenate([jnp.where(lane < DA_HD, q, zero), jnp.where(lane < DA_HD, zero, q)], axis=0)

    def stage_a_chunk(qq, hh, slot, c, m):
        s = _dot_nt(k_ref[0, c * kc:(c + 1) * kc, lanes[hh]], qq)
        s_refs[hh][slot][c * kc:(c + 1) * kc, :] = s
        cm = jnp.max(s, axis=0, keepdims=True)
        return cm if m is None else jnp.maximum(m, cm)

    def stage_b_chunk(hh, slot, c, m, acc):
        p = jnp.exp2(s_refs[hh][slot][c * kc:(c + 1) * kc, :] - m).astype(BF16)
        vt = vt_ref[0, lanes[hh], c * kc:(c + 1) * kc]
        pv = _dot(jnp.where(vrow == DA_VD, jnp.ones_like(vt), vt), p)
        return pv if acc is None else acc + pv

    def finish(acc, sl, hh):
        r0 = 1.0 / acc[DA_VD:DA_VD + 1, 0:tq]
        r1 = lam / acc[DA_VD:DA_VD + 1, tq:2 * tq]
        o = (acc[:, 0:tq] * r0 - acc[:, tq:2 * tq] * r1).T
        o = jnp.where(lane < DA_VD, o, 0.0)
        y = o * lax.rsqrt(jnp.sum(o * o, axis=1, keepdims=True) * (1.0 / DA_VD) + EPS) * gain_ref[0]
        y_ref[0, sl, lanes[hh]] = (y * (1.0 - lam_init)).astype(BF16)

    def stages(sl_a, sl_b, slot_a, nk_a, nk_b):
        slot_b = 1 - slot_a
        qqs = [load_qq(sl_a, hh) for hh in range(nh)] if sl_a is not None else None
        m_bs = [m_refs[hh][slot_b][0:1, :] for hh in range(nh)] if sl_b is not None else None
        ms, accs = [None] * nh, [None] * nh
        for c in range(max(nk_a, nk_b) // kc):
            for hh in range(nh):
                if sl_a is not None and c < nk_a // kc:
                    ms[hh] = stage_a_chunk(qqs[hh], hh, slot_a, c, ms[hh])
                if sl_b is not None and c < nk_b // kc:
                    accs[hh] = stage_b_chunk(hh, slot_b, c, m_bs[hh], accs[hh])
        for hh in range(nh):
            if sl_a is not None:
                m_refs[hh][slot_a][...] = jnp.broadcast_to(ms[hh], (8, 2 * tq))
            if sl_b is not None:
                finish(accs[hh], sl_b, hh)

    ctx_rows = pl.ds(0, tq)
    stages(ctx_rows, None, 1, ctx_len, 0)
    stages(rows(0), ctx_rows, 0, n_all, ctx_len)

    def body(i, _):
        t = 1 + 2 * i
        stages(rows(t), rows(t - 1), 1, n_all, n_all)
        stages(rows(t + 1), rows(t), 0, n_all, n_all)
        return 0

    lax.fori_loop(0, (n_tiles - 2) // 2, body, 0)
    stages(rows(n_tiles - 1), rows(n_tiles - 2), 1, n_all, n_all)
    stages(None, rows(n_tiles - 1), 0, 0, n_all)


def _attn(daq, dak, dav_t, lam_pad, gain, layer, ctx_len, lam_init):
    nb, n_all, _ = daq.shape
    tq, kc, nh = ATT_TQ, ATT_KC, ATT_HPS
    n_tiles = (n_all - ctx_len) // tq
    assert n_tiles >= 2 and n_tiles % 2 == 0 and ctx_len == tq and ctx_len % kc == 0 and n_all % kc == 0
    kern = functools.partial(_attn_kernel, tq=tq, kc=kc, nh=nh, ctx_len=ctx_len, lam_init=lam_init)
    slab = pl.BlockSpec((1, n_all, nh * LANE), lambda b, h: (b, 0, h))
    sbuf, mbuf = pltpu.VMEM((n_all, 2 * tq), F32), pltpu.VMEM((8, 2 * tq), F32)
    return pl.pallas_call(
        kern,
        grid=(nb, HEADS // nh),
        in_specs=[slab, slab,
                  pl.BlockSpec((1, nh * LANE, n_all), lambda b, h: (b, h, 0)),
                  pl.BlockSpec((1, 4, LANE), lambda b, h: (layer, 0, 0)),
                  pl.BlockSpec((1, 1, LANE), lambda b, h: (layer, 0, 0))],
        out_specs=slab,
        out_shape=jax.ShapeDtypeStruct((nb, n_all, SLAB), BF16),
        scratch_shapes=[sbuf] * (2 * nh) + [mbuf] * (2 * nh),
        compiler_params=_cparams(("parallel", "arbitrary")),
        name="diff_attn",
    )(daq, dak, dav_t, lam_pad, gain)


def _resident(shape, index_map):
    return pl.BlockSpec(shape, index_map, pipeline_mode=pl.Buffered(1))


def _mix_ffn_kernel(yml_ref, yhg_ref, yda_ref, x_ref, g1x_ref, g1c_ref, a2x_ref, a2c_ref, s2x_ref, s2c_ref,
                    g2x_ref, g2c_ref, wo_ref, gpm_ref, gpf_ref, wg_ref, wu_ref, wd_ref, gqf_ref, o_ref, *, tm, nsub, ctx_len, t0):
    row0 = (pl.program_id(1) + t0) * tm
    split = min(ctx_len % tm if ctx_len % tm else tm, tm)
    top_is_ctx = row0 < ctx_len

    def mod(c_ref, x_ref_, lo):
        return jnp.where(top_is_ctx, c_ref[0, 0], x_ref_[0, 0]) if lo < split else x_ref_[0, 0]

    sub = tm // nsub
    x1s, h2s = [], []
    for i in range(nsub):
        lo, rs = i * sub, slice(i * sub, (i + 1) * sub)
        y = (_dot(yml_ref[0, rs, :], wo_ref[0, 0:SLAB, :]) + _dot(yhg_ref[0, rs, :], wo_ref[0, SLAB:SLAB + HG_W, :])
             + _dot(yda_ref[0, rs, :], wo_ref[0, SLAB + HG_W:2 * SLAB + HG_W, :]))
        x1 = x_ref[0, rs, :] + _rms(y, mod(g1c_ref, g1x_ref, lo) * gpm_ref[0])
        h2s.append((_rms(x1, gpf_ref[0] * (1.0 + mod(a2c_ref, a2x_ref, lo))) + mod(s2c_ref, s2x_ref, lo)).astype(BF16))
        x1s.append(x1)
    for i in range(nsub):
        lo, rs = i * sub, slice(i * sub, (i + 1) * sub)
        act = _silu(_dot(h2s[i], wg_ref[0])) * _dot(h2s[i], wu_ref[0])
        f = _dot(act.astype(BF16), wd_ref[0])
        o_ref[0, rs, :] = x1s[i] + _rms(f, mod(g2c_ref, g2x_ref, lo) * gqf_ref[0])


def _mix_ffn(yml, yhg, yda, xs, mod4, w_out_pad, g_post_mix, g_pre_ffn, wg, wu, wd, g_post_ffn, layer, ctx_len,
             tm, nsub, t0):
    nb, n_all, d = xs.shape
    dff = wg.shape[-1]
    assert tm % nsub == 0 and (ctx_len % tm) % (tm // nsub) == 0
    kern = functools.partial(_mix_ffn_kernel, tm=tm, nsub=nsub, ctx_len=ctx_len, t0=t0)
    lsel = lambda b, t: (layer, 0, 0)
    tile = lambda width: pl.BlockSpec((1, tm, width), lambda b, t: (b, t + t0, 0))
    return pl.pallas_call(
        kern,
        grid=(nb, n_all // tm - t0),
        in_specs=[tile(SLAB), tile(HG_W), tile(SLAB), tile(d)] + _mod_specs(layer, nb, (2, 4, 3, 5), d) + [
            _resident((1, 2 * SLAB + HG_W, d), lsel),
            pl.BlockSpec((1, 1, d), lsel),
            pl.BlockSpec((1, 1, d), lsel),
            _resident((1, d, dff), lsel),
            _resident((1, d, dff), lsel),
            _resident((1, dff, d), lsel),
            pl.BlockSpec((1, 1, d), lsel)],
        out_specs=pl.BlockSpec((1, tm, d), lambda b, t: (b, t, 0)),
        out_shape=jax.ShapeDtypeStruct((nb, n_all - t0 * tm, d), F32),
        compiler_params=_cparams(("parallel", "arbitrary")),
        name="mix_ffn",
    )(yml, yhg, yda, xs, *([mod4] * 8), w_out_pad, g_post_mix, g_pre_ffn, wg, wu, wd, g_post_ffn)


def _pad_heads(w, hd):
    lead = w.shape[:-1]
    w = w.reshape(*lead, HEADS, hd)
    w = jnp.pad(w, [(0, 0)] * len(lead) + [(0, 0), (0, LANE - hd)])
    return w.reshape(*lead, SLAB)


def _pad_in_cols(w, extra_f=None):
    sizes = (4 * ML_HD,) * 4 + (2 * HEADS, 2 * HEADS, HEADS * HG_DK, 2 * HEADS * HG_DK, HEADS * HG_DV, HEADS * HG_DV,
             2 * HEADS * DA_HD, 2 * HEADS * DA_HD, HEADS * DA_VD)
    parts = []
    off = 0
    for s in sizes:
        parts.append(w[..., off:off + s])
        off += s
    gi, gf = parts[4], parts[5]
    if extra_f is not None:
        gf = gf + extra_f
    gate = jnp.concatenate([gi, gf], axis=-1)
    gate = jnp.pad(gate, [(0, 0)] * (gate.ndim - 1) + [(0, LANE - 4 * HEADS)])
    return jnp.concatenate([
        _pad_heads(parts[0], ML_HD), _pad_heads(parts[1], ML_HD), _pad_heads(parts[2], ML_HD), _pad_heads(parts[3], ML_HD),
        gate, parts[6], parts[7], parts[8], parts[9],
        _pad_heads(parts[10], DA_VD), _pad_heads(parts[11], DA_VD), _pad_heads(parts[12], DA_VD)], axis=-1)


def _pad_out_rows(w_out):
    depth, _, d = w_out.shape

    def pad_rows(w, hd):
        w = jnp.pad(w.reshape(depth, HEADS, hd, d), ((0, 0), (0, 0), (0, LANE - hd), (0, 0)))
        return w.reshape(depth, SLAB, d)

    ml, hg, da = w_out[:, :4 * ML_HD], w_out[:, 4 * ML_HD:4 * ML_HD + HG_W], w_out[:, 4 * ML_HD + HG_W:]
    return jnp.concatenate([pad_rows(ml, ML_HD), hg, pad_rows(da, DA_VD)], axis=1)


def _rope_tables(n_lat, ctx_len):
    rows = n_lat // GRID_W
    row = jnp.repeat(jnp.arange(rows), GRID_W).astype(F32)
    col = jnp.tile(jnp.arange(GRID_W), rows).astype(F32)
    half = DA_HD // 2
    inv = ROPE_BASE ** (-jnp.arange(0, half, 2, dtype=F32) / half)
    ang_r, ang_c = row[:, None] * inv, col[:, None] * inv
    zero = jnp.zeros_like(ang_r)

    def one_map(fr, fc, sel):
        r, c = fr(ang_r), fc(ang_c)
        if sel == "cos":
            return jnp.concatenate([r, r, c, c], axis=1)
        if sel == "a":
            return jnp.concatenate([-r, zero, -c, zero], axis=1)
        return jnp.concatenate([zero, r, zero, c], axis=1)

    tabs = []
    for sel, fn in (("cos", jnp.cos), ("a", jnp.sin), ("b", jnp.sin)):
        m = one_map(fn, fn, sel)
        lat = jnp.pad(jnp.concatenate([m, m], axis=1), ((0, 0), (0, LANE - 2 * DA_HD)))
        fill = 1.0 if sel == "cos" else 0.0
        ctx_rows = jnp.pad(jnp.full((ctx_len, 2 * DA_HD), fill, F32), ((0, 0), (0, LANE - 2 * DA_HD)))
        tabs.append(jnp.concatenate([ctx_rows, lat], axis=0))
    return tabs


def kernel(x, c, ctx, c_ctx, w_ada, b_ada, g_pre_mix, g_post_mix, g_pre_ffn, g_post_ffn, w_in, b_in, w_out,
           ml_f_bias, ml_norm, hg_lb, hg_norm, da_lambda, da_norm, w_ffn_gate, w_ffn_up, w_ffn_down):
    nb, n_lat, d = x.shape
    ctx_len = ctx.shape[1]
    depth = w_ada.shape[0]
    n_all = ctx_len + n_lat

    w_pad = _pad_in_cols(w_in).astype(BF16)
    b_pad = _pad_in_cols(b_in, extra_f=ml_f_bias)[:, None, :]
    w_out_pad = _pad_out_rows(w_out).astype(BF16)
    wg, wu, wd = w_ffn_gate.astype(BF16), w_ffn_up.astype(BF16), w_ffn_down.astype(BF16)
    ml_gain = _pad_heads(ml_norm, ML_HD).reshape(depth * HEADS, 1, LANE)
    hg_gain = hg_norm.reshape(-1, 1, LANE)
    da_gain = jnp.pad(da_norm, ((0, 0), (0, LANE - DA_VD)))[:, None, :]
    lam_pad = jnp.pad(da_lambda.astype(F32), ((0, 0), (0, 0), (0, LANE - DA_HD)))
    sm = jax.nn.softmax(hg_lb.astype(F32), axis=0)
    lbs = jnp.cumsum(sm, axis=0) - sm[0:1]
    lbf = jnp.maximum(lbs, LB_FLOOR)[:, None, :]
    oml = (1.0 - lbs)[:, None, :]
    rope_tabs = _rope_tables(n_lat, ctx_len)
    r3 = lambda g: g[:, None, :]

    mp = -(-(nb + 1) // 8) * 8
    cc = jnp.concatenate([c, c_ctx[None, :], jnp.zeros((mp - nb - 1, d), F32)], axis=0)
    mod4 = _ada(cc, w_ada, b_ada).reshape(depth, mp, 1, 6 * d)

    xs = jnp.concatenate([ctx, x], axis=1)
    for l in range(depth):
        lam_init = 0.8 - 0.6 * math.exp(-0.3 * l)
        (mlq, mlk, mlv, mlo, gates, hgq, hgf, hgv, hgg, daq, dak, dav_t) = _inproj(
            xs, mod4, r3(g_pre_mix), w_pad, b_pad, rope_tabs, lbf, oml, l, ctx_len,
            *((IN_TM, IN_SUB) if n_all % IN_TM == 0 else (IN_TM_ALT, IN_SUB_ALT)))
        gates_t = gates.reshape(nb, 4 * HEADS, n_all // ML_CHUNK, ML_CHUNK)
        yml = _mlstm(mlq, mlk, mlv, mlo, gates_t, ml_gain, l, ctx_len)
        yhg = _hgrn(hgq, hgf, hgv, hgg, hg_gain, l, ctx_len)
        yda = _attn(daq, dak, dav_t, lam_pad, da_gain, l, ctx_len, lam_init)
        last = l == depth - 1
        tm, nsub = (MIX_TM, MIX_SUB) if not last and n_all % MIX_TM == 0 else (MIX_TM_LAST, MIX_SUB_LAST)
        xs = _mix_ffn(yml, yhg, yda, xs, mod4, w_out_pad, r3(g_post_mix), r3(g_pre_ffn), wg, wu, wd,
                      r3(g_post_ffn), l, ctx_len, tm, nsub, ctx_len // tm if last else 0)
    return xs
```

```python
import functools
import math

import numpy as np
import jax
import jax.numpy as jnp
from jax import lax
from jax.experimental import pallas as pl
from jax.experimental.pallas import tpu as pltpu

F32 = jnp.float32
BF16 = jnp.bfloat16
HIGHEST = lax.Precision.HIGHEST

LANE = 128
VMEM_LIMIT = 52 * 1024 * 1024

EPS = 1e-6
NEG = -1e30
LB_FLOOR = 1e-30
M_INIT = -1e30
GRID_W = 64
ROPE_BASE = 10000.0

HEADS = 4
ML_HD = 96
HG_DV = 64
HG_DK = 128
DA_HD = 48
DA_VD = 2 * DA_HD
SLAB = HEADS * LANE
HG_W = HEADS * HG_DV

ML_CHUNK = 256
ML_HPS = 2
HG_CHUNK = 128
HG_DIAG = 8
HG_PAIRS, HG_UNROLL = 2, 2
ATT_TQ = 256
IN_CW = 256
IN_TM, IN_SUB = 768, 3
IN_TM_ALT, IN_SUB_ALT = 256, 2
MIX_TM, MIX_SUB = 768, 3
MIX_TM_LAST, MIX_SUB_LAST = 256, 2
ATT_KC = 256
ATT_HPS = 2
LOG2E = 1.4426950408889634

OFF_MLQ, OFF_MLK, OFF_MLV, OFF_MLO = 0, SLAB, 2 * SLAB, 3 * SLAB
OFF_GATE = 4 * SLAB
OFF_HGQ = OFF_GATE + LANE
OFF_HGF = OFF_HGQ + SLAB
OFF_HGV = OFF_HGF + 2 * SLAB
OFF_HGG = OFF_HGV + HG_W
OFF_DAQ = OFF_HGG + HG_W
OFF_DAK = OFF_DAQ + SLAB
OFF_DAV = OFF_DAK + SLAB
NP_IN = OFF_DAV + SLAB


def _cparams(sem):
    return pltpu.CompilerParams(dimension_semantics=sem, vmem_limit_bytes=VMEM_LIMIT)


def _silu(x):
    return x * jax.nn.sigmoid(x)


def _log_sigmoid(z):
    return jnp.minimum(z, 0.0) - jnp.log1p(jnp.exp(-jnp.abs(z)))


def _neg_abs(x):
    bits = lax.bitcast_convert_type(x, jnp.uint32) | jnp.uint32(0x80000000)
    return lax.bitcast_convert_type(bits, F32)


def _rms(x, g):
    return x * lax.rsqrt(jnp.mean(x * x, axis=-1, keepdims=True) + EPS) * g


def _dot(a, b):
    return jnp.dot(a, b, preferred_element_type=F32)


def _dot_nt(a, b):
    return lax.dot_general(a, b, (((1,), (1,)), ((), ())), preferred_element_type=F32)


def _dot_tn(a, b):
    return lax.dot_general(a, b, (((0,), (0,)), ((), ())), preferred_element_type=F32)


def _ada_kernel(s_ref, w_ref, b_ref, o_ref):
    s = _silu(s_ref[...])
    o_ref[0] = jnp.dot(s, w_ref[0], precision=HIGHEST, preferred_element_type=F32) + b_ref[0]


def _ada(cc, w_ada, b_ada):
    depth, d, d6 = w_ada.shape
    mp = cc.shape[0]
    tn = 2048
    return pl.pallas_call(
        _ada_kernel,
        grid=(depth, d6 // tn),
        in_specs=[pl.BlockSpec((mp, d), lambda l, j: (0, 0)),
                  pl.BlockSpec((1, d, tn), lambda l, j: (l, 0, j)),
                  pl.BlockSpec((1, 1, tn), lambda l, j: (l, 0, j))],
        out_specs=pl.BlockSpec((1, mp, tn), lambda l, j: (l, 0, j)),
        out_shape=jax.ShapeDtypeStruct((depth, mp, d6), F32),
        compiler_params=_cparams(("arbitrary", "arbitrary")),
        name="ada_mod",
    )(cc, w_ada, b_ada.reshape(depth, 1, d6))


def _mod_specs(layer, nb, idxs, d):
    specs = []
    for j in idxs:
        specs.append(pl.BlockSpec((1, 1, 1, d), lambda b, t, j=j: (layer, b, 0, j)))
        specs.append(pl.BlockSpec((1, 1, 1, d), lambda b, t, j=j: (layer, nb, 0, j)))
    return specs


def _inproj_kernel(x_ref, sx_ref, sc_ref, ax_ref, ac_ref, g_ref, w_ref, b_ref,
                   cos_ref, sna_ref, snb_ref, lbf_ref, oml_ref,
                   mlq_ref, mlk_ref, mlv_ref, mlo_ref, gate_ref,
                   hgq_ref, hgf_ref, hgv_ref, hgg_ref, daq_ref, dak_ref, dav_ref,
                   *, tm, nsub, ctx_len):
    split = min(ctx_len % tm if ctx_len % tm else tm, tm)
    top_is_ctx = pl.program_id(1) * tm < ctx_len
    sub = tm // nsub
    half = DA_HD // 4
    lane = lax.broadcasted_iota(jnp.int32, (sub, LANE), 1)

    def rope(p, rs):
        cos, sna, snb = cos_ref[rs, :], sna_ref[rs, :], snb_ref[rs, :]
        outs = []
        for i in range(p.shape[1] // LANE):
            xh = p[:, i * LANE:(i + 1) * LANE]
            outs.append(xh * cos + pltpu.roll(xh, LANE - half, 1) * sna + pltpu.roll(xh, half, 1) * snb)
        return jnp.concatenate(outs, axis=1)

    def log2_f(p, c0, rs):
        w = p.shape[1]
        return jnp.log2(lbf_ref[0, :, c0:c0 + w] + oml_ref[0, :, c0:c0 + w] * jax.nn.sigmoid(p))

    def gates(p, c0, rs):
        return jnp.where((lane >= 2 * HEADS) & (lane < 4 * HEADS), _log_sigmoid(p), p)

    plain = lambda p, c0, rs: p
    groups = {
        "mlq": (mlq_ref, OFF_MLQ, SLAB, plain),
        "mlk": (mlk_ref, OFF_MLK, SLAB, lambda p, c0, rs: p * (ML_HD ** -0.5)),
        "mlv": (mlv_ref, OFF_MLV, SLAB, plain),
        "mlo": (mlo_ref, OFF_MLO, SLAB, lambda p, c0, rs: jax.nn.sigmoid(p)),
        "gate": (gate_ref, OFF_GATE, LANE, gates),
        "hgq": (hgq_ref, OFF_HGQ, SLAB, lambda p, c0, rs: _silu(p)),
        "hgf": (hgf_ref, OFF_HGF, 2 * SLAB, log2_f),
        "hgv": (hgv_ref, OFF_HGV, HG_W, plain),
        "hgg": (hgg_ref, OFF_HGG, HG_W, lambda p, c0, rs: _silu(p)),
        "daq": (daq_ref, OFF_DAQ, SLAB, lambda p, c0, rs: rope(p, rs) * (LOG2E * DA_HD ** -0.5)),
        "dak": (dak_ref, OFF_DAK, SLAB, lambda p, c0, rs: rope(p, rs)),
        "dav": (dav_ref, OFF_DAV, SLAB, plain),
    }
    order = [("hgf", 0), ("mlq", 0), ("hgf", 1), ("mlq", 1), ("hgf", 2), ("mlk", 0), ("hgf", 3), ("mlk", 1),
             ("mlo", 0), ("mlv", 0), ("mlo", 1), ("mlv", 1), ("hgq", 0), ("dav", 0), ("hgq", 1), ("dav", 1),
             ("daq", 0), ("hgv", 0), ("daq", 1), ("gate", 0), ("dak", 0), ("hgg", 0), ("dak", 1)]
    for i in range(nsub):
        rs = slice(i * sub, (i + 1) * sub)
        ctx_rows = i * sub < split
        shift = jnp.where(top_is_ctx, sc_ref[0, 0], sx_ref[0, 0]) if ctx_rows else sx_ref[0, 0]
        scale = 1.0 + (jnp.where(top_is_ctx, ac_ref[0, 0], ax_ref[0, 0]) if ctx_rows else ax_ref[0, 0])
        h = (_rms(x_ref[0, rs, :], g_ref[0] * scale) + shift).astype(BF16)
        for name, ci in order:
            ref, off, width, fn = groups[name]
            cw = min(width, IN_CW)
            c0 = ci * cw
            p = _dot(h, w_ref[0, :, off + c0:off + c0 + cw]) + b_ref[0, :, off + c0:off + c0 + cw]
            if name == "dav":
                ref[0, c0:c0 + cw, rs] = p.T.astype(ref.dtype)
            elif name == "gate":
                ref[0, :, rs] = fn(p, c0, rs).T[0:4 * HEADS, :]
            else:
                ref[0, rs, c0:c0 + cw] = fn(p, c0, rs).astype(ref.dtype)


def _inproj(xs, mod4, g_pre, w_pad, b_pad, rope_tabs, lbf, oml, layer, ctx_len, tm, nsub):
    nb, n_all, d = xs.shape
    assert tm % nsub == 0 and (ctx_len % tm) % (tm // nsub) == 0
    kern = functools.partial(_inproj_kernel, tm=tm, nsub=nsub, ctx_len=ctx_len)
    lsel = lambda b, t: (layer, 0, 0)
    tile = lambda width: pl.BlockSpec((1, tm, width), lambda b, t: (b, t, 0))
    tab = pl.BlockSpec((tm, LANE), lambda b, t: (t, 0))
    bf = lambda width: jax.ShapeDtypeStruct((nb, n_all, width), BF16)
    f32 = lambda width: jax.ShapeDtypeStruct((nb, n_all, width), F32)
    out_shapes = [bf(SLAB)] * 4 + [f32(LANE)] + [bf(SLAB), f32(2 * SLAB), bf(HG_W), bf(HG_W)] + [bf(SLAB)] * 3
    out_specs = [tile(s.shape[-1]) for s in out_shapes]
    out_shapes[-1] = jax.ShapeDtypeStruct((nb, SLAB, n_all), BF16)
    out_specs[-1] = pl.BlockSpec((1, SLAB, tm), lambda b, t: (b, 0, t))
    out_shapes[4] = jax.ShapeDtypeStruct((nb, 4 * HEADS, n_all), F32)
    out_specs[4] = pl.BlockSpec((1, 4 * HEADS, tm), lambda b, t: (b, 0, t))
    return pl.pallas_call(
        kern,
        grid=(nb, n_all // tm),
        in_specs=[tile(d)] + _mod_specs(layer, nb, (0, 1), d) + [
            pl.BlockSpec((1, 1, d), lsel),
            pl.BlockSpec((1, d, NP_IN), lsel, pipeline_mode=pl.Buffered(1)),
            pl.BlockSpec((1, 1, NP_IN), lsel),
            tab, tab, tab,
            pl.BlockSpec((1, 1, 2 * SLAB), lsel),
            pl.BlockSpec((1, 1, 2 * SLAB), lsel)],
        out_specs=out_specs,
        out_shape=out_shapes,
        compiler_params=_cparams(("parallel", "arbitrary")),
        name="in_proj",
    )(xs, mod4, mod4, mod4, mod4, g_pre, w_pad, b_pad, *rope_tabs, lbf, oml)


def _ml_chunks(qs, ks, vs, i_rows, lf_rows, s_exts, ms, revs):
    n = len(qs)
    size = qs[0].shape[0]
    ii = lax.broadcasted_iota(jnp.int32, (size, size), 0)
    jj = lax.broadcasted_iota(jnp.int32, (size, size), 1)
    lane = lax.broadcasted_iota(jnp.int32, (size, LANE), 1)
    cums = {rev: jnp.where((ii >= jj) if rev else (ii <= jj), 1.0, 0.0).astype(BF16) for rev in set(revs)}
    seens = {rev: (jj >= ii) if rev else (jj <= ii) for rev in set(revs)}
    row16 = lax.broadcasted_iota(jnp.int32, (16, size), 0)

    b_cols, g_cols, g_rows = [], [], []
    for i in range(n):
        lf2 = lf_rows[i] * LOG2E
        hi = lf2.astype(BF16).astype(F32)
        mid = (lf2 - hi).astype(BF16).astype(F32)
        lo = (lf2 - hi) - mid
        parts = jnp.where(row16 == 0, hi, jnp.where(row16 == 1, mid, jnp.where(row16 == 2, lo, 0.0))).astype(BF16)
        cs = _dot(parts, cums[revs[i]])
        b_row = (cs[0:1] + cs[1:2]) + cs[2:3]
        g_row = i_rows[i] * LOG2E - b_row
        b_cols.append(jnp.sum(jnp.where(ii == jj, b_row, 0.0), axis=1, keepdims=True))
        g_cols.append(jnp.sum(jnp.where(ii == jj, g_row, 0.0), axis=1, keepdims=True))
        g_rows.append(g_row)

    mts, dws, w_inters = [], [], []
    for i in range(n):
        dmat = jnp.where(seens[revs[i]], b_cols[i] + g_rows[i], NEG)
        a_col = b_cols[i] + ms[i]
        mt = jnp.maximum(a_col, jnp.max(dmat, axis=1, keepdims=True))
        mts.append(mt)
        w_inters.append(jnp.exp2(a_col - mt))
        dws.append(jnp.exp2(dmat - mt))

    ress = []
    for i in range(n):
        qk = _dot_nt(qs[i], ks[i]) * dws[i]
        ress.append(_dot(qk.astype(BF16), vs[i]) + w_inters[i] * _dot(qs[i], s_exts[i].astype(BF16)))

    houts, s_news, m_news = [], [], []
    for i in range(n):
        den = jnp.sum(jnp.where(lane == ML_HD, ress[i], 0.0), axis=1, keepdims=True)
        houts.append(ress[i] * (1.0 / jnp.maximum(jnp.abs(den), jnp.exp2(-mts[i]))))
        last = 0 if revs[i] else size - 1
        m_new = mts[i][last:last + 1]
        b_last = b_cols[i][last:last + 1]
        decay = jnp.exp2(b_last + ms[i] - m_new)
        wk = jnp.exp2(b_last + g_cols[i] - m_new)
        s_news.append(decay * s_exts[i] + _dot_tn((wk * ks[i].astype(F32)).astype(BF16), vs[i]))
        m_news.append(m_new)
    return houts, s_news, m_news


def _mlstm_kernel(q_ref, k_ref, v_ref, o_ref, gt_ref, gain_ref, y_ref, hf_ref, hb_ref, *, nc_ctx, nc_all, size, nh):
    hd0 = pl.program_id(1) * nh
    lane = lax.broadcasted_iota(jnp.int32, (size, LANE), 1)
    nc_lat = nc_all - nc_ctx

    def step(c, carry):
        cb = jnp.where(c < nc_ctx, nc_ctx - 1 - c, nc_ctx + nc_lat - 1 - (c - nc_ctx))
        sf = pl.ds(pl.multiple_of(c * size, size), size)
        sb = pl.ds(pl.multiple_of(cb * size, size), size)
        qs, ks, vs, i_rows, lf_rows, revs = [], [], [], [], [], []
        for hh in range(nh):
            ln = slice(hh * LANE, (hh + 1) * LANE)
            for sl, cc, off in ((sf, c, 0), (sb, cb, HEADS)):
                v = v_ref[0, sl, ln]
                qs.append(q_ref[0, sl, ln])
                ks.append(k_ref[0, sl, ln])
                vs.append(jnp.where(lane == ML_HD, jnp.ones_like(v), v))
                i_rows.append(gt_ref[0, off + hd0 + hh, pl.ds(cc, 1), :])
                lf_rows.append(gt_ref[0, 2 * HEADS + off + hd0 + hh, pl.ds(cc, 1), :])
                revs.append(off > 0)
        houts, s_news, m_news = _ml_chunks(qs, ks, vs, i_rows, lf_rows, list(carry[0]), list(carry[1]), revs)
        for hh in range(nh):
            ln = slice(hh * LANE, (hh + 1) * LANE)
            hf_ref[sf, ln] = houts[2 * hh]
            hb_ref[sb, ln] = houts[2 * hh + 1]
        return tuple(s_news), tuple(m_news)

    s0 = jnp.zeros((LANE, LANE), F32)
    m0 = jnp.full((1, 1), M_INIT, F32)
    lax.fori_loop(0, nc_all, step, ((s0,) * (2 * nh), (m0,) * (2 * nh)), unroll=3)

    def finish(c, _):
        sl = pl.ds(pl.multiple_of(c * size, size), size)
        for hh in range(nh):
            ln = slice(hh * LANE, (hh + 1) * LANE)
            h = jnp.where(lane < ML_HD, hf_ref[sl, ln] + hb_ref[sl, ln], 0.0)
            y = h * lax.rsqrt(jnp.sum(h * h, axis=1, keepdims=True) * (1.0 / ML_HD) + EPS) * gain_ref[0, :, ln]
            y_ref[0, sl, ln] = (y * o_ref[0, sl, ln].astype(F32)).astype(BF16)
        return 0

    lax.fori_loop(0, nc_all, finish, 0, unroll=3)


def _mlstm(mlq, mlk, mlv, mlo, gates_t, gain, layer, ctx_len):
    nb, n_all, _ = mlq.shape
    size, nh = ML_CHUNK, ML_HPS
    nc_all = n_all // size
    width = nh * LANE
    ng = HEADS // nh
    kern = functools.partial(_mlstm_kernel, nc_ctx=ctx_len // size, nc_all=nc_all, size=size, nh=nh)
    slab = pl.BlockSpec((1, n_all, width), lambda b, h: (b, 0, h))
    return pl.pallas_call(
        kern,
        grid=(nb, ng),
        in_specs=[slab, slab, slab, slab,
                  pl.BlockSpec((1, 4 * HEADS, nc_all, size), lambda b, h: (b, 0, 0, 0)),
                  pl.BlockSpec((1, 1, width), lambda b, h: (layer * ng + h, 0, 0))],
        out_specs=slab,
        out_shape=jax.ShapeDtypeStruct((nb, n_all, SLAB), BF16),
        scratch_shapes=[pltpu.VMEM((n_all, width), F32), pltpu.VMEM((n_all, width), F32)],
        compiler_params=_cparams(("parallel", "arbitrary")),
        name="mlstm",
    )(mlq, mlk, mlv, mlo, gates_t, gain.reshape(-1, 1, width))


def _hg_tables(size):
    t = np.arange(size)
    tri = (t[None, :] <= t[:, None]).astype(np.float32)
    x = t[:, None] ^ t[None, :]
    lvl = np.where(x < HG_DIAG, 0, np.floor(np.log2(np.maximum(x, 1))).astype(np.int64) - 2)
    code = np.where(t[None, :] <= t[:, None], lvl, -1).astype(np.int32)
    j = np.arange(HG_DIAG * LANE) // LANE
    emat = (np.arange(size)[None, :] % HG_DIAG == j[:, None]).astype(np.float32)
    return (jnp.asarray(np.stack([tri, tri.T]), BF16), jnp.asarray(np.stack([code, code.T])),
            jnp.asarray(emat, BF16))


def _hg_chunks(qs, lfs, vs, sts, tris, codes, emat, revs, w_refs):
    n = len(qs)
    size = qs[0].shape[0]
    bc2s, kks = [], []
    for i in range(n):
        lf2 = lfs[i]
        hi = lf2.astype(BF16)
        r1 = lf2 - hi.astype(F32)
        mid = r1.astype(BF16)
        lo = (r1 - mid.astype(F32)).astype(BF16)
        cs = _dot(tris[i], jnp.concatenate([hi, mid, lo], axis=1))
        bc2s.append((cs[:, 0:LANE] + cs[:, LANE:2 * LANE]) + cs[:, 2 * LANE:3 * LANE])
        kks.append(jnp.maximum(1.0 - jnp.exp2(lf2), 0.0))

    atts = [jnp.zeros((size, size), F32) for _ in range(n)]
    c, lvl = HG_DIAG, 1
    while c < size:
        blk = 2 * c
        for i in range(n):
            ridx = c if revs[i] else c - 1
            b3 = bc2s[i].reshape(size // blk, blk, LANE)
            ref = jnp.broadcast_to(b3[:, ridx:ridx + 1, :], b3.shape).reshape(size, LANE)
            e = jnp.exp2(_neg_abs(bc2s[i] - ref))
            a = _dot_nt((qs[i] * e).astype(BF16), (kks[i] * e).astype(BF16))
            atts[i] = jnp.where(codes[i] == lvl, a, atts[i])
        c, lvl = blk, lvl + 1

    for i in range(n):
        w_refs[i][...] = bc2s[i] - jnp.log2(kks[i])
        ps = []
        for j in range(HG_DIAG):
            wj = jnp.concatenate([jnp.broadcast_to(w_refs[i][pl.ds(HG_DIAG * blk_i + j, 1), :], (HG_DIAG, LANE))
                                  for blk_i in range(size // HG_DIAG)], axis=0)
            ps.append((qs[i] * jnp.exp2(jnp.minimum(bc2s[i] - wj, 0.0))).astype(BF16))
        atts[i] = jnp.where(codes[i] == 0, _dot(jnp.concatenate(ps, axis=1), emat), atts[i])

    outs, new_sts = [], []
    for i in range(n):
        last = 0 if revs[i] else size - 1
        bl = bc2s[i][last:last + 1]
        outs.append(_dot(atts[i].astype(BF16), vs[i])
                    + _dot_nt((qs[i] * jnp.exp2(bc2s[i])).astype(BF16), sts[i].astype(BF16)))
        new_sts.append(sts[i] * jnp.exp2(bl) + _dot_tn(vs[i], (kks[i] * jnp.exp2(bl - bc2s[i])).astype(BF16)))
    return outs, new_sts


def _hgrn_kernel(q_ref, lf0_ref, lf1_ref, v_ref, g_ref, gain_ref, tri_ref, code_ref, emat_ref,
                 y_ref, of_ref, ob_ref, *w_refs, nc_ctx, nc_all, size, npair):
    nc_lat = nc_all - nc_ctx
    low = lax.broadcasted_iota(jnp.int32, (size, LANE), 1) < HG_DV

    def step(c, carry):
        cb = jnp.where(c < nc_ctx, nc_ctx - 1 - c, nc_ctx + nc_lat - 1 - (c - nc_ctx))
        sf = pl.ds(pl.multiple_of(c * size, size), size)
        sb = pl.ds(pl.multiple_of(cb * size, size), size)
        qs, lfs, vs, tris, codes, revs = [], [], [], [], [], []
        for hh in range(2 * npair):
            ln = slice(hh * LANE, (hh + 1) * LANE)
            pv = slice((hh // 2) * LANE, (hh // 2 + 1) * LANE)
            qs += [q_ref[0, sf, ln].astype(F32), q_ref[0, sb, ln].astype(F32)]
            lfs += [lf0_ref[0, sf, ln], lf1_ref[0, sb, ln]]
            vs += [v_ref[0, sf, pv], v_ref[0, sb, pv]]
            tris += [tri_ref[0], tri_ref[1]]
            codes += [code_ref[0], code_ref[1]]
            revs += [False, True]
        outs, sts = _hg_chunks(qs, lfs, vs, list(carry), tris, codes, emat_ref[...], revs, w_refs)
        for p in range(npair):
            pv = slice(p * LANE, (p + 1) * LANE)
            of_ref[sf, pv] = jnp.where(low, outs[4 * p], outs[4 * p + 2])
            ob_ref[sb, pv] = jnp.where(low, outs[4 * p + 1], outs[4 * p + 3])
        return tuple(sts)

    st0 = jnp.zeros((LANE, LANE), F32)
    lax.fori_loop(0, nc_all, step, (st0,) * (4 * npair), unroll=HG_UNROLL)

    def finish(c, _):
        sl = pl.ds(pl.multiple_of(c * size, size), size)
        for p in range(npair):
            pv = slice(p * LANE, (p + 1) * LANE)
            o = of_ref[sl, pv] + ob_ref[sl, pv]
            sq = o * o
            ms0 = jnp.sum(jnp.where(low, sq, 0.0), axis=1, keepdims=True) * (1.0 / HG_DV)
            ms1 = jnp.sum(jnp.where(low, 0.0, sq), axis=1, keepdims=True) * (1.0 / HG_DV)
            y = o * lax.rsqrt(jnp.where(low, ms0, ms1) + EPS) * gain_ref[0, :, pv]
            y_ref[0, sl, pv] = (y * g_ref[0, sl, pv].astype(F32)).astype(BF16)
        return 0

    lax.fori_loop(0, nc_all, finish, 0, unroll=6 // npair)


def _hgrn(hgq, hgf, hgv, hgg, gain, layer, ctx_len):
    nb, n_all, _ = hgq.shape
    size, npair = HG_CHUNK, HG_PAIRS
    nc_all = n_all // size
    ng = HEADS // (2 * npair)
    kern = functools.partial(_hgrn_kernel, nc_ctx=ctx_len // size, nc_all=nc_all, size=size, npair=npair)
    wide = pl.BlockSpec((1, n_all, 2 * npair * LANE), lambda b, h: (b, 0, h))
    slab = pl.BlockSpec((1, n_all, npair * LANE), lambda b, h: (b, 0, h))
    tri, code, emat = _hg_tables(size)
    return pl.pallas_call(
        kern,
        grid=(nb, ng),
        in_specs=[wide, wide,
                  pl.BlockSpec((1, n_all, 2 * npair * LANE), lambda b, h: (b, 0, ng + h)),
                  slab, slab,
                  pl.BlockSpec((1, 1, npair * LANE), lambda b, h: (layer * ng + h, 0, 0)),
                  pl.BlockSpec(tri.shape, lambda b, h: (0, 0, 0)),
                  pl.BlockSpec(code.shape, lambda b, h: (0, 0, 0)),
                  pl.BlockSpec(emat.shape, lambda b, h: (0, 0))],
        out_specs=slab,
        out_shape=jax.ShapeDtypeStruct((nb, n_all, HG_W), BF16),
        scratch_shapes=[pltpu.VMEM((n_all, npair * LANE), F32), pltpu.VMEM((n_all, npair * LANE), F32)]
        + [pltpu.VMEM((size, LANE), F32)] * (4 * npair),
        compiler_params=_cparams(("parallel", "arbitrary")),
        name="hgrn2",
    )(hgq, hgf, hgf, hgv, hgg, gain.reshape(-1, 1, npair * LANE), tri, code, emat)


def _attn_kernel(q_ref, k_ref, vt_ref, lam_ref, gain_ref, y_ref, *scratch, tq, kc, nh, ctx_len, lam_init):
    n_all = k_ref.shape[1]
    n_tiles = (n_all - ctx_len) // tq
    s_refs = [scratch[2 * hh:2 * hh + 2] for hh in range(nh)]
    m_refs = [scratch[2 * nh + 2 * hh:2 * nh + 2 * hh + 2] for hh in range(nh)]
    lv = lam_ref[0]
    lam = (jnp.exp(jnp.sum(lv[0:1] * lv[1:2], axis=1, keepdims=True))
           - jnp.exp(jnp.sum(lv[2:3] * lv[3:4], axis=1, keepdims=True)) + lam_init)
    lane = lax.broadcasted_iota(jnp.int32, (tq, LANE), 1)
    vrow = lax.broadcasted_iota(jnp.int32, (LANE, kc), 0)
    lanes = [slice(hh * LANE, (hh + 1) * LANE) for hh in range(nh)]

    def rows(t):
        return pl.ds(pl.multiple_of(ctx_len + t * tq, tq), tq)

    def load_qq(sl, hh):
        q = q_ref[0, sl, lanes[hh]]
        zero = jnp.zeros_like(q)
        return jnp.concatenate([jnp.where(lane < DA_HD, q, zero), jnp.where(lane < DA_HD, zero, q)], axis=0)

    def stage_a_chunk(qq, hh, slot, c, m):
        s = _dot_nt(k_ref[0, c * kc:(c + 1) * kc, lanes[hh]], qq)
        s_refs[hh][slot][c * kc:(c + 1) * kc, :] = s
        cm = jnp.max(s, axis=0, keepdims=True)
        return cm if m is None else jnp.maximum(m, cm)

    def stage_b_chunk(hh, slot, c, m, acc):
        p = jnp.exp2(s_refs[hh][slot][c * kc:(c + 1) * kc, :] - m).astype(BF16)
        vt = vt_ref[0, lanes[hh], c * kc:(c + 1) * kc]
        pv = _dot(jnp.where(vrow == DA_VD, jnp.ones_like(vt), vt), p)
        return pv if acc is None else acc + pv

    def finish(acc, sl, hh):
        r0 = 1.0 / acc[DA_VD:DA_VD + 1, 0:tq]
        r1 = lam / acc[DA_VD:DA_VD + 1, tq:2 * tq]
        o = (acc[:, 0:tq] * r0 - acc[:, tq:2 * tq] * r1).T
        o = jnp.where(lane < DA_VD, o, 0.0)
        y = o * lax.rsqrt(jnp.sum(o * o, axis=1, keepdims=True) * (1.0 / DA_VD) + EPS) * gain_ref[0]
        y_ref[0, sl, lanes[hh]] = (y * (1.0 - lam_init)).astype(BF16)

    def stages(sl_a, sl_b, slot_a, nk_a, nk_b):
        slot_b = 1 - slot_a
        qqs = [load_qq(sl_a, hh) for hh in range(nh)] if sl_a is not None else None
        m_bs = [m_refs[hh][slot_b][0:1, :] for hh in range(nh)] if sl_b is not None else None
        ms, accs = [None] * nh, [None] * nh
        for c in range(max(nk_a, nk_b) // kc):
            for hh in range(nh):
                if sl_a is not None and c < nk_a // kc:
                    ms[hh] = stage_a_chunk(qqs[hh], hh, slot_a, c, ms[hh])
                if sl_b is not None and c < nk_b // kc:
                    accs[hh] = stage_b_chunk(hh, slot_b, c, m_bs[hh], accs[hh])
        for hh in range(nh):
            if sl_a is not None:
                m_refs[hh][slot_a][...] = jnp.broadcast_to(ms[hh], (8, 2 * tq))
            if sl_b is not None:
                finish(accs[hh], sl_b, hh)

    ctx_rows = pl.ds(0, tq)
    stages(ctx_rows, None, 1, ctx_len, 0)
    stages(rows(0), ctx_rows, 0, n_all, ctx_len)

    def body(i, _):
        t = 1 + 2 * i
        stages(rows(t), rows(t - 1), 1, n_all, n_all)
        stages(rows(t + 1), rows(t), 0, n_all, n_all)
        return 0

    lax.fori_loop(0, (n_tiles - 2) // 2, body, 0)
    stages(rows(n_tiles - 1), rows(n_tiles - 2), 1, n_all, n_all)
    stages(None, rows(n_tiles - 1), 0, 0, n_all)


def _attn(daq, dak, dav_t, lam_pad, gain, layer, ctx_len, lam_init):
    nb, n_all, _ = daq.shape
    tq, kc, nh = ATT_TQ, ATT_KC, ATT_HPS
    n_tiles = (n_all - ctx_len) // tq
    assert n_tiles >= 2 and n_tiles % 2 == 0 and ctx_len == tq and ctx_len % kc == 0 and n_all % kc == 0
    kern = functools.partial(_attn_kernel, tq=tq, kc=kc, nh=nh, ctx_len=ctx_len, lam_init=lam_init)
    slab = pl.BlockSpec((1, n_all, nh * LANE), lambda b, h: (b, 0, h))
    sbuf, mbuf = pltpu.VMEM((n_all, 2 * tq), F32), pltpu.VMEM((8, 2 * tq), F32)
    return pl.pallas_call(
        kern,
        grid=(nb, HEADS // nh),
        in_specs=[slab, slab,
                  pl.BlockSpec((1, nh * LANE, n_all), lambda b, h: (b, h, 0)),
                  pl.BlockSpec((1, 4, LANE), lambda b, h: (layer, 0, 0)),
                  pl.BlockSpec((1, 1, LANE), lambda b, h: (layer, 0, 0))],
        out_specs=slab,
        out_shape=jax.ShapeDtypeStruct((nb, n_all, SLAB), BF16),
        scratch_shapes=[sbuf] * (2 * nh) + [mbuf] * (2 * nh),
        compiler_params=_cparams(("parallel", "arbitrary")),
        name="diff_attn",
    )(daq, dak, dav_t, lam_pad, gain)


def _resident(shape, index_map):
    return pl.BlockSpec(shape, index_map, pipeline_mode=pl.Buffered(1))


def _mix_ffn_kernel(yml_ref, yhg_ref, yda_ref, x_ref, g1x_ref, g1c_ref, a2x_ref, a2c_ref, s2x_ref, s2c_ref,
                    g2x_ref, g2c_ref, wo_ref, gpm_ref, gpf_ref, wg_ref, wu_ref, wd_ref, gqf_ref, o_ref, *, tm, nsub, ctx_len, t0):
    row0 = (pl.program_id(1) + t0) * tm
    split = min(ctx_len % tm if ctx_len % tm else tm, tm)
    top_is_ctx = row0 < ctx_len

    def mod(c_ref, x_ref_, lo):
        return jnp.where(top_is_ctx, c_ref[0, 0], x_ref_[0, 0]) if lo < split else x_ref_[0, 0]

    sub = tm // nsub
    x1s, h2s = [], []
    for i in range(nsub):
        lo, rs = i * sub, slice(i * sub, (i + 1) * sub)
        y = (_dot(yml_ref[0, rs, :], wo_ref[0, 0:SLAB, :]) + _dot(yhg_ref[0, rs, :], wo_ref[0, SLAB:SLAB + HG_W, :])
             + _dot(yda_ref[0, rs, :], wo_ref[0, SLAB + HG_W:2 * SLAB + HG_W, :]))
        x1 = x_ref[0, rs, :] + _rms(y, mod(g1c_ref, g1x_ref, lo) * gpm_ref[0])
        h2s.append((_rms(x1, gpf_ref[0] * (1.0 + mod(a2c_ref, a2x_ref, lo))) + mod(s2c_ref, s2x_ref, lo)).astype(BF16))
        x1s.append(x1)
    for i in range(nsub):
        lo, rs = i * sub, slice(i * sub, (i + 1) * sub)
        act = _silu(_dot(h2s[i], wg_ref[0])) * _dot(h2s[i], wu_ref[0])
        f = _dot(act.astype(BF16), wd_ref[0])
        o_ref[0, rs, :] = x1s[i] + _rms(f, mod(g2c_ref, g2x_ref, lo) * gqf_ref[0])


def _mix_ffn(yml, yhg, yda, xs, mod4, w_out_pad, g_post_mix, g_pre_ffn, wg, wu, wd, g_post_ffn, layer, ctx_len,
             tm, nsub, t0):
    nb, n_all, d = xs.shape
    dff = wg.shape[-1]
    assert tm % nsub == 0 and (ctx_len % tm) % (tm // nsub) == 0
    kern = functools.partial(_mix_ffn_kernel, tm=tm, nsub=nsub, ctx_len=ctx_len, t0=t0)
    lsel = lambda b, t: (layer, 0, 0)
    tile = lambda width: pl.BlockSpec((1, tm, width), lambda b, t: (b, t + t0, 0))
    return pl.pallas_call(
        kern,
        grid=(nb, n_all // tm - t0),
        in_specs=[tile(SLAB), tile(HG_W), tile(SLAB), tile(d)] + _mod_specs(layer, nb, (2, 4, 3, 5), d) + [
            _resident((1, 2 * SLAB + HG_W, d), lsel),
            pl.BlockSpec((1, 1, d), lsel),
            pl.BlockSpec((1, 1, d), lsel),
            _resident((1, d, dff), lsel),
            _resident((1, d, dff), lsel),
            _resident((1, dff, d), lsel),
            pl.BlockSpec((1, 1, d), lsel)],
        out_specs=pl.BlockSpec((1, tm, d), lambda b, t: (b, t, 0)),
        out_shape=jax.ShapeDtypeStruct((nb, n_all - t0 * tm, d), F32),
        compiler_params=_cparams(("parallel", "arbitrary")),
        name="mix_ffn",
    )(yml, yhg, yda, xs, *([mod4] * 8), w_out_pad, g_post_mix, g_pre_ffn, wg, wu, wd, g_post_ffn)


def _pad_heads(w, hd):
    lead = w.shape[:-1]
    w = w.reshape(*lead, HEADS, hd)
    w = jnp.pad(w, [(0, 0)] * len(lead) + [(0, 0), (0, LANE - hd)])
    return w.reshape(*lead, SLAB)


def _pad_in_cols(w, extra_f=None):
    sizes = (4 * ML_HD,) * 4 + (2 * HEADS, 2 * HEADS, HEADS * HG_DK, 2 * HEADS * HG_DK, HEADS * HG_DV, HEADS * HG_DV,
             2 * HEADS * DA_HD, 2 * HEADS * DA_HD, HEADS * DA_VD)
    parts = []
    off = 0
    for s in sizes:
        parts.append(w[..., off:off + s])
        off += s
    gi, gf = parts[4], parts[5]
    if extra_f is not None:
        gf = gf + extra_f
    gate = jnp.concatenate([gi, gf], axis=-1)
    gate = jnp.pad(gate, [(0, 0)] * (gate.ndim - 1) + [(0, LANE - 4 * HEADS)])
    return jnp.concatenate([
        _pad_heads(parts[0], ML_HD), _pad_heads(parts[1], ML_HD), _pad_heads(parts[2], ML_HD), _pad_heads(parts[3], ML_HD),
        gate, parts[6], parts[7], parts[8], parts[9],
        _pad_heads(parts[10], DA_VD), _pad_heads(parts[11], DA_VD), _pad_heads(parts[12], DA_VD)], axis=-1)


def _pad_out_rows(w_out):
    depth, _, d = w_out.shape

    def pad_rows(w, hd):
        w = jnp.pad(w.reshape(depth, HEADS, hd, d), ((0, 0), (0, 0), (0, LANE - hd), (0, 0)))
        return w.reshape(depth, SLAB, d)

    ml, hg, da = w_out[:, :4 * ML_HD], w_out[:, 4 * ML_HD:4 * ML_HD + HG_W], w_out[:, 4 * ML_HD + HG_W:]
    return jnp.concatenate([pad_rows(ml, ML_HD), hg, pad_rows(da, DA_VD)], axis=1)


def _rope_tables(n_lat, ctx_len):
    rows = n_lat // GRID_W
    row = jnp.repeat(jnp.arange(rows), GRID_W).astype(F32)
    col = jnp.tile(jnp.arange(GRID_W), rows).astype(F32)
    half = DA_HD // 2
    inv = ROPE_BASE ** (-jnp.arange(0, half, 2, dtype=F32) / half)
    ang_r, ang_c = row[:, None] * inv, col[:, None] * inv
    zero = jnp.zeros_like(ang_r)

    def one_map(fr, fc, sel):
        r, c = fr(ang_r), fc(ang_c)
        if sel == "cos":
            return jnp.concatenate([r, r, c, c], axis=1)
        if sel == "a":
            return jnp.concatenate([-r, zero, -c, zero], axis=1)
        return jnp.concatenate([zero, r, zero, c], axis=1)

    tabs = []
    for sel, fn in (("cos", jnp.cos), ("a", jnp.sin), ("b", jnp.sin)):
        m = one_map(fn, fn, sel)
        lat = jnp.pad(jnp.concatenate([m, m], axis=1), ((0, 0), (0, LANE - 2 * DA_HD)))
        fill = 1.0 if sel == "cos" else 0.0
        ctx_rows = jnp.pad(jnp.full((ctx_len, 2 * DA_HD), fill, F32), ((0, 0), (0, LANE - 2 * DA_HD)))
        tabs.append(jnp.concatenate([ctx_rows, lat], axis=0))
    return tabs


def kernel(x, c, ctx, c_ctx, w_ada, b_ada, g_pre_mix, g_post_mix, g_pre_ffn, g_post_ffn, w_in, b_in, w_out,
           ml_f_bias, ml_norm, hg_lb, hg_norm, da_lambda, da_norm, w_ffn_gate, w_ffn_up, w_ffn_down):
    nb, n_lat, d = x.shape
    ctx_len = ctx.shape[1]
    depth = w_ada.shape[0]
    n_all = ctx_len + n_lat

    w_pad = _pad_in_cols(w_in).astype(BF16)
    b_pad = _pad_in_cols(b_in, extra_f=ml_f_bias)[:, None, :]
    w_out_pad = _pad_out_rows(w_out).astype(BF16)
    wg, wu, wd = w_ffn_gate.astype(BF16), w_ffn_up.astype(BF16), w_ffn_down.astype(BF16)
    ml_gain = _pad_heads(ml_norm, ML_HD).reshape(depth * HEADS, 1, LANE)
    hg_gain = hg_norm.reshape(-1, 1, LANE)
    da_gain = jnp.pad(da_norm, ((0, 0), (0, LANE - DA_VD)))[:, None, :]
    lam_pad = jnp.pad(da_lambda.astype(F32), ((0, 0), (0, 0), (0, LANE - DA_HD)))
    sm = jax.nn.softmax(hg_lb.astype(F32), axis=0)
    lbs = jnp.cumsum(sm, axis=0) - sm[0:1]
    lbf = jnp.maximum(lbs, LB_FLOOR)[:, None, :]
    oml = (1.0 - lbs)[:, None, :]
    rope_tabs = _rope_tables(n_lat, ctx_len)
    r3 = lambda g: g[:, None, :]

    mp = -(-(nb + 1) // 8) * 8
    cc = jnp.concatenate([c, c_ctx[None, :], jnp.zeros((mp - nb - 1, d), F32)], axis=0)
    mod4 = _ada(cc, w_ada, b_ada).reshape(depth, mp, 1, 6 * d)

    xs = jnp.concatenate([ctx, x], axis=1)
    for l in range(depth):
        lam_init = 0.8 - 0.6 * math.exp(-0.3 * l)
        (mlq, mlk, mlv, mlo, gates, hgq, hgf, hgv, hgg, daq, dak, dav_t) = _inproj(
            xs, mod4, r3(g_pre_mix), w_pad, b_pad, rope_tabs, lbf, oml, l, ctx_len,
            *((IN_TM, IN_SUB) if n_all % IN_TM == 0 else (IN_TM_ALT, IN_SUB_ALT)))
        gates_t = gates.reshape(nb, 4 * HEADS, n_all // ML_CHUNK, ML_CHUNK)
        yml = _mlstm(mlq, mlk, mlv, mlo, gates_t, ml_gain, l, ctx_len)
        yhg = _hgrn(hgq, hgf, hgv, hgg, hg_gain, l, ctx_len)
        yda = _attn(daq, dak, dav_t, lam_pad, da_gain, l, ctx_len, lam_init)
        last = l == depth - 1
        tm, nsub = (MIX_TM, MIX_SUB) if not last and n_all % MIX_TM == 0 else (MIX_TM_LAST, MIX_SUB_LAST)
        xs = _mix_ffn(yml, yhg, yda, xs, mod4, w_out_pad, r3(g_post_mix), r3(g_pre_ffn), wg, wu, wd,
                      r3(g_post_ffn), l, ctx_len, tm, nsub, ctx_len // tm if last else 0)
    return xs
```

```python
import functools
import math

import numpy as np
import jax
import jax.numpy as jnp
from jax import lax
from jax.experimental import pallas as pl
from jax.experimental.pallas import tpu as pltpu

F32 = jnp.float32
BF16 = jnp.bfloat16
HIGHEST = lax.Precision.HIGHEST

LANE = 128
VMEM_LIMIT = 52 * 1024 * 1024

EPS = 1e-6
NEG = -1e30
LB_FLOOR = 1e-30
M_INIT = -1e30
GRID_W = 64
ROPE_BASE = 10000.0

HEADS = 4
ML_HD = 96
HG_DV = 64
HG_DK = 128
DA_HD = 48
DA_VD = 2 * DA_HD
SLAB = HEADS * LANE
HG_W = HEADS * HG_DV

ML_CHUNK = 256
ML_HPS = 2
HG_CHUNK = 128
HG_DIAG = 8
HG_PAIRS, HG_UNROLL = 2, 3
ATT_TQ = 256
IN_CW = 256
IN_TM, IN_SUB = 768, 3
IN_TM_ALT, IN_SUB_ALT = 256, 2
MIX_TM, MIX_SUB = 768, 3
MIX_TM_LAST, MIX_SUB_LAST = 256, 2
ATT_KC = 256
ATT_HPS = 2
LOG2E = 1.4426950408889634

OFF_MLQ, OFF_MLK, OFF_MLV, OFF_MLO = 0, SLAB, 2 * SLAB, 3 * SLAB
OFF_GATE = 4 * SLAB
OFF_HGQ = OFF_GATE + LANE
OFF_HGF = OFF_HGQ + SLAB
OFF_HGV = OFF_HGF + 2 * SLAB
OFF_HGG = OFF_HGV + HG_W
OFF_DAQ = OFF_HGG + HG_W
OFF_DAK = OFF_DAQ + SLAB
OFF_DAV = OFF_DAK + SLAB
NP_IN = OFF_DAV + SLAB


def _cparams(sem):
    return pltpu.CompilerParams(dimension_semantics=sem, vmem_limit_bytes=VMEM_LIMIT)


def _silu(x):
    return x * jax.nn.sigmoid(x)


def _log_sigmoid(z):
    return jnp.minimum(z, 0.0) - jnp.log1p(jnp.exp(-jnp.abs(z)))


def _neg_abs(x):
    bits = lax.bitcast_convert_type(x, jnp.uint32) | jnp.uint32(0x80000000)
    return lax.bitcast_convert_type(bits, F32)


def _rms(x, g):
    return x * lax.rsqrt(jnp.mean(x * x, axis=-1, keepdims=True) + EPS) * g


def _dot(a, b):
    return jnp.dot(a, b, preferred_element_type=F32)


def _dot_nt(a, b):
    return lax.dot_general(a, b, (((1,), (1,)), ((), ())), preferred_element_type=F32)


def _dot_tn(a, b):
    return lax.dot_general(a, b, (((0,), (0,)), ((), ())), preferred_element_type=F32)


def _ada_kernel(s_ref, w_ref, b_ref, o_ref):
    s = _silu(s_ref[...])
    o_ref[0] = jnp.dot(s, w_ref[0], precision=HIGHEST, preferred_element_type=F32) + b_ref[0]


def _ada(cc, w_ada, b_ada):
    depth, d, d6 = w_ada.shape
    mp = cc.shape[0]
    tn = 2048
    return pl.pallas_call(
        _ada_kernel,
        grid=(depth, d6 // tn),
        in_specs=[pl.BlockSpec((mp, d), lambda l, j: (0, 0)),
                  pl.BlockSpec((1, d, tn), lambda l, j: (l, 0, j)),
                  pl.BlockSpec((1, 1, tn), lambda l, j: (l, 0, j))],
        out_specs=pl.BlockSpec((1, mp, tn), lambda l, j: (l, 0, j)),
        out_shape=jax.ShapeDtypeStruct((depth, mp, d6), F32),
        compiler_params=_cparams(("arbitrary", "arbitrary")),
        name="ada_mod",
    )(cc, w_ada, b_ada.reshape(depth, 1, d6))


def _mod_specs(layer, nb, idxs, d):
    specs = []
    for j in idxs:
        specs.append(pl.BlockSpec((1, 1, 1, d), lambda b, t, j=j: (layer, b, 0, j)))
        specs.append(pl.BlockSpec((1, 1, 1, d), lambda b, t, j=j: (layer, nb, 0, j)))
    return specs


def _inproj_kernel(x_ref, sx_ref, sc_ref, ax_ref, ac_ref, g_ref, w_ref, b_ref,
                   cos_ref, sna_ref, snb_ref, lbf_ref, oml_ref,
                   mlq_ref, mlk_ref, mlv_ref, mlo_ref, gate_ref,
                   hgq_ref, hgf_ref, hgv_ref, hgg_ref, daq_ref, dak_ref, dav_ref,
                   *, tm, nsub, ctx_len):
    split = min(ctx_len % tm if ctx_len % tm else tm, tm)
    top_is_ctx = pl.program_id(1) * tm < ctx_len
    sub = tm // nsub
    half = DA_HD // 4
    lane = lax.broadcasted_iota(jnp.int32, (sub, LANE), 1)

    def rope(p, rs):
        cos, sna, snb = cos_ref[rs, :], sna_ref[rs, :], snb_ref[rs, :]
        outs = []
        for i in range(p.shape[1] // LANE):
            xh = p[:, i * LANE:(i + 1) * LANE]
            outs.append(xh * cos + pltpu.roll(xh, LANE - half, 1) * sna + pltpu.roll(xh, half, 1) * snb)
        return jnp.concatenate(outs, axis=1)

    def log2_f(p, c0, rs):
        w = p.shape[1]
        return jnp.log2(lbf_ref[0, :, c0:c0 + w] + oml_ref[0, :, c0:c0 + w] * jax.nn.sigmoid(p))

    def gates(p, c0, rs):
        return jnp.where((lane >= 2 * HEADS) & (lane < 4 * HEADS), _log_sigmoid(p), p)

    plain = lambda p, c0, rs: p
    groups = {
        "mlq": (mlq_ref, OFF_MLQ, SLAB, plain),
        "mlk": (mlk_ref, OFF_MLK, SLAB, lambda p, c0, rs: p * (ML_HD ** -0.5)),
        "mlv": (mlv_ref, OFF_MLV, SLAB, plain),
        "mlo": (mlo_ref, OFF_MLO, SLAB, lambda p, c0, rs: jax.nn.sigmoid(p)),
        "gate": (gate_ref, OFF_GATE, LANE, gates),
        "hgq": (hgq_ref, OFF_HGQ, SLAB, lambda p, c0, rs: _silu(p)),
        "hgf": (hgf_ref, OFF_HGF, 2 * SLAB, log2_f),
        "hgv": (hgv_ref, OFF_HGV, HG_W, plain),
        "hgg": (hgg_ref, OFF_HGG, HG_W, lambda p, c0, rs: _silu(p)),
        "daq": (daq_ref, OFF_DAQ, SLAB, lambda p, c0, rs: rope(p, rs) * (LOG2E * DA_HD ** -0.5)),
        "dak": (dak_ref, OFF_DAK, SLAB, lambda p, c0, rs: rope(p, rs)),
        "dav": (dav_ref, OFF_DAV, SLAB, plain),
    }
    order = [("hgf", 0), ("mlq", 0), ("hgf", 1), ("mlq", 1), ("hgf", 2), ("mlk", 0), ("hgf", 3), ("mlk", 1),
             ("mlo", 0), ("mlv", 0), ("mlo", 1), ("mlv", 1), ("hgq", 0), ("dav", 0), ("hgq", 1), ("dav", 1),
             ("daq", 0), ("hgv", 0), ("daq", 1), ("gate", 0), ("dak", 0), ("hgg", 0), ("dak", 1)]
    for i in range(nsub):
        rs = slice(i * sub, (i + 1) * sub)
        ctx_rows = i * sub < split
        shift = jnp.where(top_is_ctx, sc_ref[0, 0], sx_ref[0, 0]) if ctx_rows else sx_ref[0, 0]
        scale = 1.0 + (jnp.where(top_is_ctx, ac_ref[0, 0], ax_ref[0, 0]) if ctx_rows else ax_ref[0, 0])
        h = (_rms(x_ref[0, rs, :], g_ref[0] * scale) + shift).astype(BF16)
        for name, ci in order:
            ref, off, width, fn = groups[name]
            cw = min(width, IN_CW)
            c0 = ci * cw
            p = _dot(h, w_ref[0, :, off + c0:off + c0 + cw]) + b_ref[0, :, off + c0:off + c0 + cw]
            if name == "dav":
                ref[0, c0:c0 + cw, rs] = p.T.astype(ref.dtype)
            elif name == "gate":
                ref[0, :, rs] = fn(p, c0, rs).T[0:4 * HEADS, :]
            else:
                ref[0, rs, c0:c0 + cw] = fn(p, c0, rs).astype(ref.dtype)


def _inproj(xs, mod4, g_pre, w_pad, b_pad, rope_tabs, lbf, oml, layer, ctx_len, tm, nsub):
    nb, n_all, d = xs.shape
    assert tm % nsub == 0 and (ctx_len % tm) % (tm // nsub) == 0
    kern = functools.partial(_inproj_kernel, tm=tm, nsub=nsub, ctx_len=ctx_len)
    lsel = lambda b, t: (layer, 0, 0)
    tile = lambda width: pl.BlockSpec((1, tm, width), lambda b, t: (b, t, 0))
    tab = pl.BlockSpec((tm, LANE), lambda b, t: (t, 0))
    bf = lambda width: jax.ShapeDtypeStruct((nb, n_all, width), BF16)
    f32 = lambda width: jax.ShapeDtypeStruct((nb, n_all, width), F32)
    out_shapes = [bf(SLAB)] * 4 + [f32(LANE)] + [bf(SLAB), f32(2 * SLAB), bf(HG_W), bf(HG_W)] + [bf(SLAB)] * 3
    out_specs = [tile(s.shape[-1]) for s in out_shapes]
    out_shapes[-1] = jax.ShapeDtypeStruct((nb, SLAB, n_all), BF16)
    out_specs[-1] = pl.BlockSpec((1, SLAB, tm), lambda b, t: (b, 0, t))
    out_shapes[4] = jax.ShapeDtypeStruct((nb, 4 * HEADS, n_all), F32)
    out_specs[4] = pl.BlockSpec((1, 4 * HEADS, tm), lambda b, t: (b, 0, t))
    return pl.pallas_call(
        kern,
        grid=(nb, n_all // tm),
        in_specs=[tile(d)] + _mod_specs(layer, nb, (0, 1), d) + [
            pl.BlockSpec((1, 1, d), lsel),
            pl.BlockSpec((1, d, NP_IN), lsel, pipeline_mode=pl.Buffered(1)),
            pl.BlockSpec((1, 1, NP_IN), lsel),
            tab, tab, tab,
            pl.BlockSpec((1, 1, 2 * SLAB), lsel),
            pl.BlockSpec((1, 1, 2 * SLAB), lsel)],
        out_specs=out_specs,
        out_shape=out_shapes,
        compiler_params=_cparams(("parallel", "arbitrary")),
        name="in_proj",
    )(xs, mod4, mod4, mod4, mod4, g_pre, w_pad, b_pad, *rope_tabs, lbf, oml)


def _ml_chunks(qs, ks, vs, i_rows, lf_rows, s_exts, ms, revs):
    n = len(qs)
    size = qs[0].shape[0]
    ii = lax.broadcasted_iota(jnp.int32, (size, size), 0)
    jj = lax.broadcasted_iota(jnp.int32, (size, size), 1)
    lane = lax.broadcasted_iota(jnp.int32, (size, LANE), 1)
    cums = {rev: jnp.where((ii >= jj) if rev else (ii <= jj), 1.0, 0.0).astype(BF16) for rev in set(revs)}
    seens = {rev: (jj >= ii) if rev else (jj <= ii) for rev in set(revs)}
    row16 = lax.broadcasted_iota(jnp.int32, (16, size), 0)

    b_cols, g_cols, g_rows = [], [], []
    for i in range(n):
        lf2 = lf_rows[i] * LOG2E
        hi = lf2.astype(BF16).astype(F32)
        mid = (lf2 - hi).astype(BF16).astype(F32)
        lo = (lf2 - hi) - mid
        parts = jnp.where(row16 == 0, hi, jnp.where(row16 == 1, mid, jnp.where(row16 == 2, lo, 0.0))).astype(BF16)
        cs = _dot(parts, cums[revs[i]])
        b_row = (cs[0:1] + cs[1:2]) + cs[2:3]
        g_row = i_rows[i] * LOG2E - b_row
        b_cols.append(jnp.sum(jnp.where(ii == jj, b_row, 0.0), axis=1, keepdims=True))
        g_cols.append(jnp.sum(jnp.where(ii == jj, g_row, 0.0), axis=1, keepdims=True))
        g_rows.append(g_row)

    mts, dws, w_inters = [], [], []
    for i in range(n):
        dmat = jnp.where(seens[revs[i]], b_cols[i] + g_rows[i], NEG)
        a_col = b_cols[i] + ms[i]
        mt = jnp.maximum(a_col, jnp.max(dmat, axis=1, keepdims=True))
        mts.append(mt)
        w_inters.append(jnp.exp2(a_col - mt))
        dws.append(jnp.exp2(dmat - mt))

    ress = []
    for i in range(n):
        qk = _dot_nt(qs[i], ks[i]) * dws[i]
        ress.append(_dot(qk.astype(BF16), vs[i]) + w_inters[i] * _dot(qs[i], s_exts[i].astype(BF16)))

    houts, s_news, m_news = [], [], []
    for i in range(n):
        den = jnp.sum(jnp.where(lane == ML_HD, ress[i], 0.0), axis=1, keepdims=True)
        houts.append(ress[i] * (1.0 / jnp.maximum(jnp.abs(den), jnp.exp2(-mts[i]))))
        last = 0 if revs[i] else size - 1
        m_new = mts[i][last:last + 1]
        b_last = b_cols[i][last:last + 1]
        decay = jnp.exp2(b_last + ms[i] - m_new)
        wk = jnp.exp2(b_last + g_cols[i] - m_new)
        s_news.append(decay * s_exts[i] + _dot_tn((wk * ks[i].astype(F32)).astype(BF16), vs[i]))
        m_news.append(m_new)
    return houts, s_news, m_news


def _mlstm_kernel(q_ref, k_ref, v_ref, o_ref, gt_ref, gain_ref, y_ref, hf_ref, hb_ref, *, nc_ctx, nc_all, size, nh):
    hd0 = pl.program_id(1) * nh
    lane = lax.broadcasted_iota(jnp.int32, (size, LANE), 1)
    nc_lat = nc_all - nc_ctx

    def step(c, carry):
        cb = jnp.where(c < nc_ctx, nc_ctx - 1 - c, nc_ctx + nc_lat - 1 - (c - nc_ctx))
        sf = pl.ds(pl.multiple_of(c * size, size), size)
        sb = pl.ds(pl.multiple_of(cb * size, size), size)
        qs, ks, vs, i_rows, lf_rows, revs = [], [], [], [], [], []
        for hh in range(nh):
            ln = slice(hh * LANE, (hh + 1) * LANE)
            for sl, cc, off in ((sf, c, 0), (sb, cb, HEADS)):
                v = v_ref[0, sl, ln]
                qs.append(q_ref[0, sl, ln])
                ks.append(k_ref[0, sl, ln])
                vs.append(jnp.where(lane == ML_HD, jnp.ones_like(v), v))
                i_rows.append(gt_ref[0, off + hd0 + hh, pl.ds(cc, 1), :])
                lf_rows.append(gt_ref[0, 2 * HEADS + off + hd0 + hh, pl.ds(cc, 1), :])
                revs.append(off > 0)
        houts, s_news, m_news = _ml_chunks(qs, ks, vs, i_rows, lf_rows, list(carry[0]), list(carry[1]), revs)
        for hh in range(nh):
            ln = slice(hh * LANE, (hh + 1) * LANE)
            hf_ref[sf, ln] = houts[2 * hh]
            hb_ref[sb, ln] = houts[2 * hh + 1]
        return tuple(s_news), tuple(m_news)

    s0 = jnp.zeros((LANE, LANE), F32)
    m0 = jnp.full((1, 1), M_INIT, F32)
    lax.fori_loop(0, nc_all, step, ((s0,) * (2 * nh), (m0,) * (2 * nh)), unroll=True)

    def finish(c, _):
        sl = pl.ds(pl.multiple_of(c * size, size), size)
        for hh in range(nh):
            ln = slice(hh * LANE, (hh + 1) * LANE)
            h = jnp.where(lane < ML_HD, hf_ref[sl, ln] + hb_ref[sl, ln], 0.0)
            y = h * lax.rsqrt(jnp.sum(h * h, axis=1, keepdims=True) * (1.0 / ML_HD) + EPS) * gain_ref[0, :, ln]
            y_ref[0, sl, ln] = (y * o_ref[0, sl, ln].astype(F32)).astype(BF16)
        return 0

    lax.fori_loop(0, nc_all, finish, 0, unroll=3)


def _mlstm(mlq, mlk, mlv, mlo, gates_t, gain, layer, ctx_len):
    nb, n_all, _ = mlq.shape
    size, nh = ML_CHUNK, ML_HPS
    nc_all = n_all // size
    width = nh * LANE
    ng = HEADS // nh
    kern = functools.partial(_mlstm_kernel, nc_ctx=ctx_len // size, nc_all=nc_all, size=size, nh=nh)
    slab = pl.BlockSpec((1, n_all, width), lambda b, h: (b, 0, h))
    return pl.pallas_call(
        kern,
        grid=(nb, ng),
        in_specs=[slab, slab, slab, slab,
                  pl.BlockSpec((1, 4 * HEADS, nc_all, size), lambda b, h: (b, 0, 0, 0)),
                  pl.BlockSpec((1, 1, width), lambda b, h: (layer * ng + h, 0, 0))],
        out_specs=slab,
        out_shape=jax.ShapeDtypeStruct((nb, n_all, SLAB), BF16),
        scratch_shapes=[pltpu.VMEM((n_all, width), F32), pltpu.VMEM((n_all, width), F32)],
        compiler_params=_cparams(("parallel", "arbitrary")),
        name="mlstm",
    )(mlq, mlk, mlv, mlo, gates_t, gain.reshape(-1, 1, width))


def _hg_tables(size):
    t = np.arange(size)
    tri = (t[None, :] <= t[:, None]).astype(np.float32)
    x = t[:, None] ^ t[None, :]
    lvl = np.where(x < HG_DIAG, 0, np.floor(np.log2(np.maximum(x, 1))).astype(np.int64) - 2)
    code = np.where(t[None, :] <= t[:, None], lvl, -1).astype(np.int32)
    j = np.arange(HG_DIAG * LANE) // LANE
    emat = (np.arange(size)[None, :] % HG_DIAG == j[:, None]).astype(np.float32)
    return (jnp.asarray(np.stack([tri, tri.T]), BF16), jnp.asarray(np.stack([code, code.T])),
            jnp.asarray(emat, BF16))


def _hg_chunks(qs, lfs, vs, sts, tris, codes, emat, revs, w_refs):
    n = len(qs)
    size = qs[0].shape[0]
    bc2s, kks = [], []
    for i in range(n):
        lf2 = lfs[i]
        hi = lf2.astype(BF16)
        r1 = lf2 - hi.astype(F32)
        mid = r1.astype(BF16)
        lo = (r1 - mid.astype(F32)).astype(BF16)
        cs = _dot(tris[i], jnp.concatenate([hi, mid, lo], axis=1))
        bc2s.append((cs[:, 0:LANE] + cs[:, LANE:2 * LANE]) + cs[:, 2 * LANE:3 * LANE])
        kks.append(jnp.maximum(1.0 - jnp.exp2(lf2), 0.0))

    atts = [jnp.zeros((size, size), F32) for _ in range(n)]
    c, lvl = HG_DIAG, 1
    while c < size:
        blk = 2 * c
        for i in range(n):
            ridx = c if revs[i] else c - 1
            b3 = bc2s[i].reshape(size // blk, blk, LANE)
            ref = jnp.broadcast_to(b3[:, ridx:ridx + 1, :], b3.shape).reshape(size, LANE)
            e = jnp.exp2(_neg_abs(bc2s[i] - ref))
            a = _dot_nt((qs[i] * e).astype(BF16), (kks[i] * e).astype(BF16))
            atts[i] = jnp.where(codes[i] == lvl, a, atts[i])
        c, lvl = blk, lvl + 1

    for i in range(n):
        w_refs[i][...] = bc2s[i] - jnp.log2(kks[i])
        ps = []
        for j in range(HG_DIAG):
            wj = jnp.concatenate([jnp.broadcast_to(w_refs[i][pl.ds(HG_DIAG * blk_i + j, 1), :], (HG_DIAG, LANE))
                                  for blk_i in range(size // HG_DIAG)], axis=0)
            ps.append((qs[i] * jnp.exp2(jnp.minimum(bc2s[i] - wj, 0.0))).astype(BF16))
        atts[i] = jnp.where(codes[i] == 0, _dot(jnp.concatenate(ps, axis=1), emat), atts[i])

    outs, new_sts = [], []
    for i in range(n):
        last = 0 if revs[i] else size - 1
        bl = bc2s[i][last:last + 1]
        outs.append(_dot(atts[i].astype(BF16), vs[i])
                    + _dot_nt((qs[i] * jnp.exp2(bc2s[i])).astype(BF16), sts[i].astype(BF16)))
        new_sts.append(sts[i] * jnp.exp2(bl) + _dot_tn(vs[i], (kks[i] * jnp.exp2(bl - bc2s[i])).astype(BF16)))
    return outs, new_sts


def _hgrn_kernel(q_ref, lf0_ref, lf1_ref, v_ref, g_ref, gain_ref, tri_ref, code_ref, emat_ref,
                 y_ref, of_ref, ob_ref, *w_refs, nc_ctx, nc_all, size, npair):
    nc_lat = nc_all - nc_ctx
    low = lax.broadcasted_iota(jnp.int32, (size, LANE), 1) < HG_DV

    def step(c, carry):
        cb = jnp.where(c < nc_ctx, nc_ctx - 1 - c, nc_ctx + nc_lat - 1 - (c - nc_ctx))
        sf = pl.ds(pl.multiple_of(c * size, size), size)
        sb = pl.ds(pl.multiple_of(cb * size, size), size)
        qs, lfs, vs, tris, codes, revs = [], [], [], [], [], []
        for hh in range(2 * npair):
            ln = slice(hh * LANE, (hh + 1) * LANE)
            pv = slice((hh // 2) * LANE, (hh // 2 + 1) * LANE)
            qs += [q_ref[0, sf, ln].astype(F32), q_ref[0, sb, ln].astype(F32)]
            lfs += [lf0_ref[0, sf, ln], lf1_ref[0, sb, ln]]
            vs += [v_ref[0, sf, pv], v_ref[0, sb, pv]]
            tris += [tri_ref[0], tri_ref[1]]
            codes += [code_ref[0], code_ref[1]]
            revs += [False, True]
        outs, sts = _hg_chunks(qs, lfs, vs, list(carry), tris, codes, emat_ref[...], revs, w_refs)
        for p in range(npair):
            pv = slice(p * LANE, (p + 1) * LANE)
            of_ref[sf, pv] = jnp.where(low, outs[4 * p], outs[4 * p + 2])
            ob_ref[sb, pv] = jnp.where(low, outs[4 * p + 1], outs[4 * p + 3])
        return tuple(sts)

    st0 = jnp.zeros((LANE, LANE), F32)
    lax.fori_loop(0, nc_all, step, (st0,) * (4 * npair), unroll=HG_UNROLL)

    def finish(c, _):
        sl = pl.ds(pl.multiple_of(c * size, size), size)
        for p in range(npair):
            pv = slice(p * LANE, (p + 1) * LANE)
            o = of_ref[sl, pv] + ob_ref[sl, pv]
            sq = o * o
            ms0 = jnp.sum(jnp.where(low, sq, 0.0), axis=1, keepdims=True) * (1.0 / HG_DV)
            ms1 = jnp.sum(jnp.where(low, 0.0, sq), axis=1, keepdims=True) * (1.0 / HG_DV)
            y = o * lax.rsqrt(jnp.where(low, ms0, ms1) + EPS) * gain_ref[0, :, pv]
            y_ref[0, sl, pv] = (y * g_ref[0, sl, pv].astype(F32)).astype(BF16)
        return 0

    lax.fori_loop(0, nc_all, finish, 0, unroll=6 // npair)


def _hgrn(hgq, hgf, hgv, hgg, gain, layer, ctx_len):
    nb, n_all, _ = hgq.shape
    size, npair = HG_CHUNK, HG_PAIRS
    nc_all = n_all // size
    ng = HEADS // (2 * npair)
    kern = functools.partial(_hgrn_kernel, nc_ctx=ctx_len // size, nc_all=nc_all, size=size, npair=npair)
    wide = pl.BlockSpec((1, n_all, 2 * npair * LANE), lambda b, h: (b, 0, h))
    slab = pl.BlockSpec((1, n_all, npair * LANE), lambda b, h: (b, 0, h))
    tri, code, emat = _hg_tables(size)
    return pl.pallas_call(
        kern,
        grid=(nb, ng),
        in_specs=[wide, wide,
                  pl.BlockSpec((1, n_all, 2 * npair * LANE), lambda b, h: (b, 0, ng + h)),
                  slab, slab,
                  pl.BlockSpec((1, 1, npair * LANE), lambda b, h: (layer * ng + h, 0, 0)),
                  pl.BlockSpec(tri.shape, lambda b, h: (0, 0, 0)),
                  pl.BlockSpec(code.shape, lambda b, h: (0, 0, 0)),
                  pl.BlockSpec(emat.shape, lambda b, h: (0, 0))],
        out_specs=slab,
        out_shape=jax.ShapeDtypeStruct((nb, n_all, HG_W), BF16),
        scratch_shapes=[pltpu.VMEM((n_all, npair * LANE), F32), pltpu.VMEM((n_all, npair * LANE), F32)]
        + [pltpu.VMEM((size, LANE), F32)] * (4 * npair),
        compiler_params=_cparams(("parallel", "arbitrary")),
        name="hgrn2",
    )(hgq, hgf, hgf, hgv, hgg, gain.reshape(-1, 1, npair * LANE), tri, code, emat)


def _attn_kernel(q_ref, k_ref, vt_ref, lam_ref, gain_ref, y_ref, *scratch, tq, kc, nh, ctx_len, lam_init):
    n_all = k_ref.shape[1]
    n_tiles = (n_all - ctx_len) // tq
    s_refs = [scratch[2 * hh:2 * hh + 2] for hh in range(nh)]
    m_refs = [scratch[2 * nh + 2 * hh:2 * nh + 2 * hh + 2] for hh in range(nh)]
    lv = lam_ref[0]
    lam = (jnp.exp(jnp.sum(lv[0:1] * lv[1:2], axis=1, keepdims=True))
           - jnp.exp(jnp.sum(lv[2:3] * lv[3:4], axis=1, keepdims=True)) + lam_init)
    lane = lax.broadcasted_iota(jnp.int32, (tq, LANE), 1)
    vrow = lax.broadcasted_iota(jnp.int32, (LANE, kc), 0)
    lanes = [slice(hh * LANE, (hh + 1) * LANE) for hh in range(nh)]

    def rows(t):
        return pl.ds(pl.multiple_of(ctx_len + t * tq, tq), tq)

    def load_qq(sl, hh):
        q = q_ref[0, sl, lanes[hh]]
        zero = jnp.zeros_like(q)
        return jnp.concatenate([jnp.where(lane < DA_HD, q, zero), jnp.where(lane < DA_HD, zero, q)], axis=0)

    def stage_a_chunk(qq, hh, slot, c, m):
        s = _dot_nt(k_ref[0, c * kc:(c + 1) * kc, lanes[hh]], qq)
        s_refs[hh][slot][c * kc:(c + 1) * kc, :] = s
        cm = jnp.max(s, axis=0, keepdims=True)
        return cm if m is None else jnp.maximum(m, cm)

    def stage_b_chunk(hh, slot, c, m, acc):
        p = jnp.exp2(s_refs[hh][slot][c * kc:(c + 1) * kc, :] - m).astype(BF16)
        vt = vt_ref[0, lanes[hh], c * kc:(c + 1) * kc]
        pv = _dot(jnp.where(vrow == DA_VD, jnp.ones_like(vt), vt), p)
        return pv if acc is None else acc + pv

    def finish(acc, sl, hh):
        r0 = 1.0 / acc[DA_VD:DA_VD + 1, 0:tq]
        r1 = lam / acc[DA_VD:DA_VD + 1, tq:2 * tq]
        o = (acc[:, 0:tq] * r0 - acc[:, tq:2 * tq] * r1).T
        o = jnp.where(lane < DA_VD, o, 0.0)
        y = o * lax.rsqrt(jnp.sum(o * o, axis=1, keepdims=True) * (1.0 / DA_VD) + EPS) * gain_ref[0]
        y_ref[0, sl, lanes[hh]] = (y * (1.0 - lam_init)).astype(BF16)

    def stages(sl_a, sl_b, slot_a, nk_a, nk_b):
        slot_b = 1 - slot_a
        qqs = [load_qq(sl_a, hh) for hh in range(nh)] if sl_a is not None else None
        m_bs = [m_refs[hh][slot_b][0:1, :] for hh in range(nh)] if sl_b is not None else None
        ms, accs = [None] * nh, [None] * nh
        for c in range(max(nk_a, nk_b) // kc):
            for hh in range(nh):
                if sl_a is not None and c < nk_a // kc:
                    ms[hh] = stage_a_chunk(qqs[hh], hh, slot_a, c, ms[hh])
                if sl_b is not None and c < nk_b // kc:
                    accs[hh] = stage_b_chunk(hh, slot_b, c, m_bs[hh], accs[hh])
        for hh in range(nh):
            if sl_a is not None:
                m_refs[hh][slot_a][...] = jnp.broadcast_to(ms[hh], (8, 2 * tq))
            if sl_b is not None:
                finish(accs[hh], sl_b, hh)

    ctx_rows = pl.ds(0, tq)
    stages(ctx_rows, None, 1, ctx_len, 0)
    stages(rows(0), ctx_rows, 0, n_all, ctx_len)

    def body(i, _):
        t = 1 + 2 * i
        stages(rows(t), rows(t - 1), 1, n_all, n_all)
        stages(rows(t + 1), rows(t), 0, n_all, n_all)
        return 0

    lax.fori_loop(0, (n_tiles - 2) // 2, body, 0)
    stages(rows(n_tiles - 1), rows(n_tiles - 2), 1, n_all, n_all)
    stages(None, rows(n_tiles - 1), 0, 0, n_all)


def _attn(daq, dak, dav_t, lam_pad, gain, layer, ctx_len, lam_init):
    nb, n_all, _ = daq.shape
    tq, kc, nh = ATT_TQ, ATT_KC, ATT_HPS
    n_tiles = (n_all - ctx_len) // tq
    assert n_tiles >= 2 and n_tiles % 2 == 0 and ctx_len == tq and ctx_len % kc == 0 and n_all % kc == 0
    kern = functools.partial(_attn_kernel, tq=tq, kc=kc, nh=nh, ctx_len=ctx_len, lam_init=lam_init)
    slab = pl.BlockSpec((1, n_all, nh * LANE), lambda b, h: (b, 0, h))
    sbuf, mbuf = pltpu.VMEM((n_all, 2 * tq), F32), pltpu.VMEM((8, 2 * tq), F32)
    return pl.pallas_call(
        kern,
        grid=(nb, HEADS // nh),
        in_specs=[slab, slab,
                  pl.BlockSpec((1, nh * LANE, n_all), lambda b, h: (b, h, 0)),
                  pl.BlockSpec((1, 4, LANE), lambda b, h: (layer, 0, 0)),
                  pl.BlockSpec((1, 1, LANE), lambda b, h: (layer, 0, 0))],
        out_specs=slab,
        out_shape=jax.ShapeDtypeStruct((nb, n_all, SLAB), BF16),
        scratch_shapes=[sbuf] * (2 * nh) + [mbuf] * (2 * nh),
        compiler_params=_cparams(("parallel", "arbitrary")),
        name="diff_attn",
    )(daq, dak, dav_t, lam_pad, gain)


def _resident(shape, index_map):
    return pl.BlockSpec(shape, index_map, pipeline_mode=pl.Buffered(1))


def _mix_ffn_kernel(yml_ref, yhg_ref, yda_ref, x_ref, g1x_ref, g1c_ref, a2x_ref, a2c_ref, s2x_ref, s2c_ref,
                    g2x_ref, g2c_ref, wo_ref, gpm_ref, gpf_ref, wg_ref, wu_ref, wd_ref, gqf_ref, o_ref, *, tm, nsub, ctx_len, t0):
    row0 = (pl.program_id(1) + t0) * tm
    split = min(ctx_len % tm if ctx_len % tm else tm, tm)
    top_is_ctx = row0 < ctx_len

    def mod(c_ref, x_ref_, lo):
        return jnp.where(top_is_ctx, c_ref[0, 0], x_ref_[0, 0]) if lo < split else x_ref_[0, 0]

    sub = tm // nsub
    x1s, h2s = [], []
    for i in range(nsub):
        lo, rs = i * sub, slice(i * sub, (i + 1) * sub)
        y = (_dot(yml_ref[0, rs, :], wo_ref[0, 0:SLAB, :]) + _dot(yhg_ref[0, rs, :], wo_ref[0, SLAB:SLAB + HG_W, :])
             + _dot(yda_ref[0, rs, :], wo_ref[0, SLAB + HG_W:2 * SLAB + HG_W, :]))
        x1 = x_ref[0, rs, :] + _rms(y, mod(g1c_ref, g1x_ref, lo) * gpm_ref[0])
        h2s.append((_rms(x1, gpf_ref[0] * (1.0 + mod(a2c_ref, a2x_ref, lo))) + mod(s2c_ref, s2x_ref, lo)).astype(BF16))
        x1s.append(x1)
    for i in range(nsub):
        lo, rs = i * sub, slice(i * sub, (i + 1) * sub)
        act = _silu(_dot(h2s[i], wg_ref[0])) * _dot(h2s[i], wu_ref[0])
        f = _dot(act.astype(BF16), wd_ref[0])
        o_ref[0, rs, :] = x1s[i] + _rms(f, mod(g2c_ref, g2x_ref, lo) * gqf_ref[0])


def _mix_ffn(yml, yhg, yda, xs, mod4, w_out_pad, g_post_mix, g_pre_ffn, wg, wu, wd, g_post_ffn, layer, ctx_len,
             tm, nsub, t0):
    nb, n_all, d = xs.shape
    dff = wg.shape[-1]
    assert tm % nsub == 0 and (ctx_len % tm) % (tm // nsub) == 0
    kern = functools.partial(_mix_ffn_kernel, tm=tm, nsub=nsub, ctx_len=ctx_len, t0=t0)
    lsel = lambda b, t: (layer, 0, 0)
    tile = lambda width: pl.BlockSpec((1, tm, width), lambda b, t: (b, t + t0, 0))
    return pl.pallas_call(
        kern,
        grid=(nb, n_all // tm - t0),
        in_specs=[tile(SLAB), tile(HG_W), tile(SLAB), tile(d)] + _mod_specs(layer, nb, (2, 4, 3, 5), d) + [
            _resident((1, 2 * SLAB + HG_W, d), lsel),
            pl.BlockSpec((1, 1, d), lsel),
            pl.BlockSpec((1, 1, d), lsel),
            _resident((1, d, dff), lsel),
            _resident((1, d, dff), lsel),
            _resident((1, dff, d), lsel),
            pl.BlockSpec((1, 1, d), lsel)],
        out_specs=pl.BlockSpec((1, tm, d), lambda b, t: (b, t, 0)),
        out_shape=jax.ShapeDtypeStruct((nb, n_all - t0 * tm, d), F32),
        compiler_params=_cparams(("parallel", "arbitrary")),
        name="mix_ffn",
    )(yml, yhg, yda, xs, *([mod4] * 8), w_out_pad, g_post_mix, g_pre_ffn, wg, wu, wd, g_post_ffn)


def _pad_heads(w, hd):
    lead = w.shape[:-1]
    w = w.reshape(*lead, HEADS, hd)
    w = jnp.pad(w, [(0, 0)] * len(lead) + [(0, 0), (0, LANE - hd)])
    return w.reshape(*lead, SLAB)


def _pad_in_cols(w, extra_f=None):
    sizes = (4 * ML_HD,) * 4 + (2 * HEADS, 2 * HEADS, HEADS * HG_DK, 2 * HEADS * HG_DK, HEADS * HG_DV, HEADS * HG_DV,
             2 * HEADS * DA_HD, 2 * HEADS * DA_HD, HEADS * DA_VD)
    parts = []
    off = 0
    for s in sizes:
        parts.append(w[..., off:off + s])
        off += s
    gi, gf = parts[4], parts[5]
    if extra_f is not None:
        gf = gf + extra_f
    gate = jnp.concatenate([gi, gf], axis=-1)
    gate = jnp.pad(gate, [(0, 0)] * (gate.ndim - 1) + [(0, LANE - 4 * HEADS)])
    return jnp.concatenate([
        _pad_heads(parts[0], ML_HD), _pad_heads(parts[1], ML_HD), _pad_heads(parts[2], ML_HD), _pad_heads(parts[3], ML_HD),
        gate, parts[6], parts[7], parts[8], parts[9],
        _pad_heads(parts[10], DA_VD), _pad_heads(parts[11], DA_VD), _pad_heads(parts[12], DA_VD)], axis=-1)


def _pad_out_rows(w_out):
    depth, _, d = w_out.shape

    def pad_rows(w, hd):
        w = jnp.pad(w.reshape(depth, HEADS, hd, d), ((0, 0), (0, 0), (0, LANE - hd), (0, 0)))
        return w.reshape(depth, SLAB, d)

    ml, hg, da = w_out[:, :4 * ML_HD], w_out[:, 4 * ML_HD:4 * ML_HD + HG_W], w_out[:, 4 * ML_HD + HG_W:]
    return jnp.concatenate([pad_rows(ml, ML_HD), hg, pad_rows(da, DA_VD)], axis=1)


def _rope_tables(n_lat, ctx_len):
    rows = n_lat // GRID_W
    row = jnp.repeat(jnp.arange(rows), GRID_W).astype(F32)
    col = jnp.tile(jnp.arange(GRID_W), rows).astype(F32)
    half = DA_HD // 2
    inv = ROPE_BASE ** (-jnp.arange(0, half, 2, dtype=F32) / half)
    ang_r, ang_c = row[:, None] * inv, col[:, None] * inv
    zero = jnp.zeros_like(ang_r)

    def one_map(fr, fc, sel):
        r, c = fr(ang_r), fc(ang_c)
        if sel == "cos":
            return jnp.concatenate([r, r, c, c], axis=1)
        if sel == "a":
            return jnp.concatenate([-r, zero, -c, zero], axis=1)
        return jnp.concatenate([zero, r, zero, c], axis=1)

    tabs = []
    for sel, fn in (("cos", jnp.cos), ("a", jnp.sin), ("b", jnp.sin)):
        m = one_map(fn, fn, sel)
        lat = jnp.pad(jnp.concatenate([m, m], axis=1), ((0, 0), (0, LANE - 2 * DA_HD)))
        fill = 1.0 if sel == "cos" else 0.0
        ctx_rows = jnp.pad(jnp.full((ctx_len, 2 * DA_HD), fill, F32), ((0, 0), (0, LANE - 2 * DA_HD)))
        tabs.append(jnp.concatenate([ctx_rows, lat], axis=0))
    return tabs


def kernel(x, c, ctx, c_ctx, w_ada, b_ada, g_pre_mix, g_post_mix, g_pre_ffn, g_post_ffn, w_in, b_in, w_out,
           ml_f_bias, ml_norm, hg_lb, hg_norm, da_lambda, da_norm, w_ffn_gate, w_ffn_up, w_ffn_down):
    nb, n_lat, d = x.shape
    ctx_len = ctx.shape[1]
    depth = w_ada.shape[0]
    n_all = ctx_len + n_lat

    w_pad = _pad_in_cols(w_in).astype(BF16)
    b_pad = _pad_in_cols(b_in, extra_f=ml_f_bias)[:, None, :]
    w_out_pad = _pad_out_rows(w_out).astype(BF16)
    wg, wu, wd = w_ffn_gate.astype(BF16), w_ffn_up.astype(BF16), w_ffn_down.astype(BF16)
    ml_gain = _pad_heads(ml_norm, ML_HD).reshape(depth * HEADS, 1, LANE)
    hg_gain = hg_norm.reshape(-1, 1, LANE)
    da_gain = jnp.pad(da_norm, ((0, 0), (0, LANE - DA_VD)))[:, None, :]
    lam_pad = jnp.pad(da_lambda.astype(F32), ((0, 0), (0, 0), (0, LANE - DA_HD)))
    sm = jax.nn.softmax(hg_lb.astype(F32), axis=0)
    lbs = jnp.cumsum(sm, axis=0) - sm[0:1]
    lbf = jnp.maximum(lbs, LB_FLOOR)[:, None, :]
    oml = (1.0 - lbs)[:, None, :]
    rope_tabs = _rope_tables(n_lat, ctx_len)
    r3 = lambda g: g[:, None, :]

    mp = -(-(nb + 1) // 8) * 8
    cc = jnp.concatenate([c, c_ctx[None, :], jnp.zeros((mp - nb - 1, d), F32)], axis=0)
    mod4 = _ada(cc, w_ada, b_ada).reshape(depth, mp, 1, 6 * d)

    xs = jnp.concatenate([ctx, x], axis=1)
    for l in range(depth):
        lam_init = 0.8 - 0.6 * math.exp(-0.3 * l)
        (mlq, mlk, mlv, mlo, gates, hgq, hgf, hgv, hgg, daq, dak, dav_t) = _inproj(
            xs, mod4, r3(g_pre_mix), w_pad, b_pad, rope_tabs, lbf, oml, l, ctx_len,
            *((IN_TM, IN_SUB) if n_all % IN_TM == 0 else (IN_TM_ALT, IN_SUB_ALT)))
        gates_t = gates.reshape(nb, 4 * HEADS, n_all // ML_CHUNK, ML_CHUNK)
        yml = _mlstm(mlq, mlk, mlv, mlo, gates_t, ml_gain, l, ctx_len)
        yhg = _hgrn(hgq, hgf, hgv, hgg, hg_gain, l, ctx_len)
        yda = _attn(daq, dak, dav_t, lam_pad, da_gain, l, ctx_len, lam_init)
        last = l == depth - 1
        tm, nsub = (MIX_TM, MIX_SUB) if not last and n_all % MIX_TM == 0 else (MIX_TM_LAST, MIX_SUB_LAST)
        xs = _mix_ffn(yml, yhg, yda, xs, mod4, w_out_pad, r3(g_post_mix), r3(g_pre_ffn), wg, wu, wd,
                      r3(g_post_ffn), l, ctx_len, tm, nsub, ctx_len // tm if last else 0)
    return xs
```

```python
import functools
import math

import numpy as np
import jax
import jax.numpy as jnp
from jax import lax
from jax.experimental import pallas as pl
from jax.experimental.pallas import tpu as pltpu

F32 = jnp.float32
BF16 = jnp.bfloat16
HIGHEST = lax.Precision.HIGHEST

LANE = 128
VMEM_LIMIT = 52 * 1024 * 1024

EPS = 1e-6
NEG = -1e30
LB_FLOOR = 1e-30
M_INIT = -1e30
GRID_W = 64
ROPE_BASE = 10000.0

HEADS = 4
ML_HD = 96
HG_DV = 64
HG_DK = 128
DA_HD = 48
DA_VD = 2 * DA_HD
SLAB = HEADS * LANE
HG_W = HEADS * HG_DV

ML_CHUNK = 256
ML_HPS = 2
HG_CHUNK = 128
HG_DIAG = 8
HG_PAIRS, HG_UNROLL = 2, 3
ATT_TQ = 256
IN_CW = 256
IN_TM, IN_SUB = 768, 3
IN_TM_ALT, IN_SUB_ALT = 256, 2
MIX_TM, MIX_SUB = 768, 3
MIX_TM_LAST, MIX_SUB_LAST = 256, 2
ATT_KC = 256
ATT_HPS = 2
LOG2E = 1.4426950408889634

OFF_MLQ, OFF_MLK, OFF_MLV, OFF_MLO = 0, SLAB, 2 * SLAB, 3 * SLAB
OFF_GATE = 4 * SLAB
OFF_HGQ = OFF_GATE + LANE
OFF_HGF = OFF_HGQ + SLAB
OFF_HGV = OFF_HGF + 2 * SLAB
OFF_HGG = OFF_HGV + HG_W
OFF_DAQ = OFF_HGG + HG_W
OFF_DAK = OFF_DAQ + SLAB
OFF_DAV = OFF_DAK + SLAB
NP_IN = OFF_DAV + SLAB


def _cparams(sem):
    return pltpu.CompilerParams(dimension_semantics=sem, vmem_limit_bytes=VMEM_LIMIT)


def _silu(x):
    return x * jax.nn.sigmoid(x)


def _log_sigmoid(z):
    return jnp.minimum(z, 0.0) - jnp.log1p(jnp.exp(-jnp.abs(z)))


def _neg_abs(x):
    bits = lax.bitcast_convert_type(x, jnp.uint32) | jnp.uint32(0x80000000)
    return lax.bitcast_convert_type(bits, F32)


def _rms(x, g):
    return x * lax.rsqrt(jnp.mean(x * x, axis=-1, keepdims=True) + EPS) * g


def _dot(a, b):
    return jnp.dot(a, b, preferred_element_type=F32)


def _dot_nt(a, b):
    return lax.dot_general(a, b, (((1,), (1,)), ((), ())), preferred_element_type=F32)


def _dot_tn(a, b):
    return lax.dot_general(a, b, (((0,), (0,)), ((), ())), preferred_element_type=F32)


def _ada_kernel(s_ref, w_ref, b_ref, o_ref):
    s = _silu(s_ref[...])
    o_ref[0] = jnp.dot(s, w_ref[0], precision=HIGHEST, preferred_element_type=F32) + b_ref[0]


def _ada(cc, w_ada, b_ada):
    depth, d, d6 = w_ada.shape
    mp = cc.shape[0]
    tn = 2048
    return pl.pallas_call(
        _ada_kernel,
        grid=(depth, d6 // tn),
        in_specs=[pl.BlockSpec((mp, d), lambda l, j: (0, 0)),
                  pl.BlockSpec((1, d, tn), lambda l, j: (l, 0, j)),
                  pl.BlockSpec((1, 1, tn), lambda l, j: (l, 0, j))],
        out_specs=pl.BlockSpec((1, mp, tn), lambda l, j: (l, 0, j)),
        out_shape=jax.ShapeDtypeStruct((depth, mp, d6), F32),
        compiler_params=_cparams(("arbitrary", "arbitrary")),
        name="ada_mod",
    )(cc, w_ada, b_ada.reshape(depth, 1, d6))


def _mod_specs(layer, nb, idxs, d):
    specs = []
    for j in idxs:
        specs.append(pl.BlockSpec((1, 1, 1, d), lambda b, t, j=j: (layer, b, 0, j)))
        specs.append(pl.BlockSpec((1, 1, 1, d), lambda b, t, j=j: (layer, nb, 0, j)))
    return specs


def _inproj_kernel(x_ref, sx_ref, sc_ref, ax_ref, ac_ref, g_ref, w_ref, b_ref,
                   cos_ref, sna_ref, snb_ref, lbf_ref, oml_ref,
                   mlq_ref, mlk_ref, mlv_ref, mlo_ref, gate_ref,
                   hgq_ref, hgf_ref, hgv_ref, hgg_ref, daq_ref, dak_ref, dav_ref,
                   *, tm, nsub, ctx_len):
    split = min(ctx_len % tm if ctx_len % tm else tm, tm)
    top_is_ctx = pl.program_id(1) * tm < ctx_len
    sub = tm // nsub
    half = DA_HD // 4
    lane = lax.broadcasted_iota(jnp.int32, (sub, LANE), 1)

    def rope(p, rs):
        cos, sna, snb = cos_ref[rs, :], sna_ref[rs, :], snb_ref[rs, :]
        outs = []
        for i in range(p.shape[1] // LANE):
            xh = p[:, i * LANE:(i + 1) * LANE]
            outs.append(xh * cos + pltpu.roll(xh, LANE - half, 1) * sna + pltpu.roll(xh, half, 1) * snb)
        return jnp.concatenate(outs, axis=1)

    def log2_f(p, c0, rs):
        w = p.shape[1]
        return jnp.log2(lbf_ref[0, :, c0:c0 + w] + oml_ref[0, :, c0:c0 + w] * jax.nn.sigmoid(p))

    def gates(p, c0, rs):
        return jnp.where((lane >= 2 * HEADS) & (lane < 4 * HEADS), _log_sigmoid(p), p)

    plain = lambda p, c0, rs: p
    groups = {
        "mlq": (mlq_ref, OFF_MLQ, SLAB, plain),
        "mlk": (mlk_ref, OFF_MLK, SLAB, lambda p, c0, rs: p * (ML_HD ** -0.5)),
        "mlv": (mlv_ref, OFF_MLV, SLAB, plain),
        "mlo": (mlo_ref, OFF_MLO, SLAB, lambda p, c0, rs: jax.nn.sigmoid(p)),
        "gate": (gate_ref, OFF_GATE, LANE, gates),
        "hgq": (hgq_ref, OFF_HGQ, SLAB, lambda p, c0, rs: _silu(p)),
        "hgf": (hgf_ref, OFF_HGF, 2 * SLAB, log2_f),
        "hgv": (hgv_ref, OFF_HGV, HG_W, plain),
        "hgg": (hgg_ref, OFF_HGG, HG_W, lambda p, c0, rs: _silu(p)),
        "daq": (daq_ref, OFF_DAQ, SLAB, lambda p, c0, rs: rope(p, rs) * (LOG2E * DA_HD ** -0.5)),
        "dak": (dak_ref, OFF_DAK, SLAB, lambda p, c0, rs: rope(p, rs)),
        "dav": (dav_ref, OFF_DAV, SLAB, plain),
    }
    order = [("hgf", 0), ("mlq", 0), ("hgf", 1), ("mlq", 1), ("hgf", 2), ("mlk", 0), ("hgf", 3), ("mlk", 1),
             ("mlo", 0), ("mlv", 0), ("mlo", 1), ("mlv", 1), ("hgq", 0), ("dav", 0), ("hgq", 1), ("dav", 1),
             ("daq", 0), ("hgv", 0), ("daq", 1), ("gate", 0), ("dak", 0), ("hgg", 0), ("dak", 1)]
    for i in range(nsub):
        rs = slice(i * sub, (i + 1) * sub)
        ctx_rows = i * sub < split
        shift = jnp.where(top_is_ctx, sc_ref[0, 0], sx_ref[0, 0]) if ctx_rows else sx_ref[0, 0]
        scale = 1.0 + (jnp.where(top_is_ctx, ac_ref[0, 0], ax_ref[0, 0]) if ctx_rows else ax_ref[0, 0])
        h = (_rms(x_ref[0, rs, :], g_ref[0] * scale) + shift).astype(BF16)
        for name, ci in order:
            ref, off, width, fn = groups[name]
            cw = min(width, IN_CW)
            c0 = ci * cw
            p = _dot(h, w_ref[0, :, off + c0:off + c0 + cw]) + b_ref[0, :, off + c0:off + c0 + cw]
            if name == "dav":
                ref[0, c0:c0 + cw, rs] = p.T.astype(ref.dtype)
            elif name == "gate":
                ref[0, :, rs] = fn(p, c0, rs).T[0:4 * HEADS, :]
            else:
                ref[0, rs, c0:c0 + cw] = fn(p, c0, rs).astype(ref.dtype)


def _inproj_first_kernel(ctx_ref, x_ref, *refs, tm, nsub, ctx_len):
    xs_ref = refs[-1]
    xs_ref[0] = jnp.where(pl.program_id(1) == 0, ctx_ref[0], x_ref[0])
    _inproj_kernel(xs_ref, *refs[:-1], tm=tm, nsub=nsub, ctx_len=ctx_len)


def _inproj(xs, mod4, g_pre, w_pad, b_pad, rope_tabs, lbf, oml, layer, ctx_len, tm, nsub):
    first = isinstance(xs, tuple)
    if first:
        ctx, x = xs
        nb, d = x.shape[0], x.shape[2]
        n_all = ctx_len + x.shape[1]
        assert tm == ctx_len
    else:
        nb, n_all, d = xs.shape
    assert tm % nsub == 0 and (ctx_len % tm) % (tm // nsub) == 0
    kern = functools.partial(_inproj_first_kernel if first else _inproj_kernel, tm=tm, nsub=nsub, ctx_len=ctx_len)
    lsel = lambda b, t: (layer, 0, 0)
    tile = lambda width: pl.BlockSpec((1, tm, width), lambda b, t: (b, t, 0))
    tab = pl.BlockSpec((tm, LANE), lambda b, t: (t, 0))
    bf = lambda width: jax.ShapeDtypeStruct((nb, n_all, width), BF16)
    f32 = lambda width: jax.ShapeDtypeStruct((nb, n_all, width), F32)
    out_shapes = [bf(SLAB)] * 4 + [f32(LANE)] + [bf(SLAB), f32(2 * SLAB), bf(HG_W), bf(HG_W)] + [bf(SLAB)] * 3
    out_specs = [tile(s.shape[-1]) for s in out_shapes]
    out_shapes[-1] = jax.ShapeDtypeStruct((nb, SLAB, n_all), BF16)
    out_specs[-1] = pl.BlockSpec((1, SLAB, tm), lambda b, t: (b, 0, t))
    out_shapes[4] = jax.ShapeDtypeStruct((nb, 4 * HEADS, n_all), F32)
    out_specs[4] = pl.BlockSpec((1, 4 * HEADS, tm), lambda b, t: (b, 0, t))
    if first:
        x_specs = [pl.BlockSpec((1, tm, d), lambda b, t: (b, 0, 0)),
                   pl.BlockSpec((1, tm, d), lambda b, t: (b, jnp.maximum(t - 1, 0), 0))]
        out_shapes.append(f32(d))
        out_specs.append(tile(d))
    else:
        x_specs = [tile(d)]
    return pl.pallas_call(
        kern,
        grid=(nb, n_all // tm),
        in_specs=x_specs + _mod_specs(layer, nb, (0, 1), d) + [
            pl.BlockSpec((1, 1, d), lsel),
            pl.BlockSpec((1, d, NP_IN), lsel, pipeline_mode=pl.Buffered(1)),
            pl.BlockSpec((1, 1, NP_IN), lsel),
            tab, tab, tab,
            pl.BlockSpec((1, 1, 2 * SLAB), lsel),
            pl.BlockSpec((1, 1, 2 * SLAB), lsel)],
        out_specs=out_specs,
        out_shape=out_shapes,
        compiler_params=_cparams(("parallel", "arbitrary")),
        name="in_proj",
    )(*(xs if first else (xs,)), mod4, mod4, mod4, mod4, g_pre, w_pad, b_pad, *rope_tabs, lbf, oml)


def _ml_chunks(qs, ks, vs, i_rows, lf_rows, s_exts, ms, revs):
    n = len(qs)
    size = qs[0].shape[0]
    ii = lax.broadcasted_iota(jnp.int32, (size, size), 0)
    jj = lax.broadcasted_iota(jnp.int32, (size, size), 1)
    lane = lax.broadcasted_iota(jnp.int32, (size, LANE), 1)
    cums = {rev: jnp.where((ii >= jj) if rev else (ii <= jj), 1.0, 0.0).astype(BF16) for rev in set(revs)}
    seens = {rev: (jj >= ii) if rev else (jj <= ii) for rev in set(revs)}
    row16 = lax.broadcasted_iota(jnp.int32, (16, size), 0)

    b_cols, g_cols, g_rows = [], [], []
    for i in range(n):
        lf2 = lf_rows[i] * LOG2E
        hi = lf2.astype(BF16).astype(F32)
        mid = (lf2 - hi).astype(BF16).astype(F32)
        lo = (lf2 - hi) - mid
        parts = jnp.where(row16 == 0, hi, jnp.where(row16 == 1, mid, jnp.where(row16 == 2, lo, 0.0))).astype(BF16)
        cs = _dot(parts, cums[revs[i]])
        b_row = (cs[0:1] + cs[1:2]) + cs[2:3]
        g_row = i_rows[i] * LOG2E - b_row
        b_cols.append(jnp.sum(jnp.where(ii == jj, b_row, 0.0), axis=1, keepdims=True))
        g_cols.append(jnp.sum(jnp.where(ii == jj, g_row, 0.0), axis=1, keepdims=True))
        g_rows.append(g_row)

    mts, dws, w_inters = [], [], []
    for i in range(n):
        dmat = jnp.where(seens[revs[i]], b_cols[i] + g_rows[i], NEG)
        a_col = b_cols[i] + ms[i]
        mt = jnp.maximum(a_col, jnp.max(dmat, axis=1, keepdims=True))
        mts.append(mt)
        w_inters.append(jnp.exp2(a_col - mt))
        dws.append(jnp.exp2(dmat - mt))

    ress = []
    for i in range(n):
        qk = _dot_nt(qs[i], ks[i]) * dws[i]
        ress.append(_dot(qk.astype(BF16), vs[i]) + w_inters[i] * _dot(qs[i], s_exts[i].astype(BF16)))

    houts, s_news, m_news = [], [], []
    for i in range(n):
        den = jnp.sum(jnp.where(lane == ML_HD, ress[i], 0.0), axis=1, keepdims=True)
        houts.append(ress[i] * (1.0 / jnp.maximum(jnp.abs(den), jnp.exp2(-mts[i]))))
        last = 0 if revs[i] else size - 1
        m_new = mts[i][last:last + 1]
        b_last = b_cols[i][last:last + 1]
        decay = jnp.exp2(b_last + ms[i] - m_new)
        wk = jnp.exp2(b_last + g_cols[i] - m_new)
        s_news.append(decay * s_exts[i] + _dot_tn((wk * ks[i].astype(F32)).astype(BF16), vs[i]))
        m_news.append(m_new)
    return houts, s_news, m_news


def _mlstm_kernel(q_ref, k_ref, v_ref, o_ref, gt_ref, gain_ref, y_ref, hf_ref, hb_ref, *, nc_ctx, nc_all, size, nh):
    hd0 = pl.program_id(1) * nh
    lane = lax.broadcasted_iota(jnp.int32, (size, LANE), 1)
    nc_lat = nc_all - nc_ctx

    def step(c, carry):
        cb = jnp.where(c < nc_ctx, nc_ctx - 1 - c, nc_ctx + nc_lat - 1 - (c - nc_ctx))
        sf = pl.ds(pl.multiple_of(c * size, size), size)
        sb = pl.ds(pl.multiple_of(cb * size, size), size)
        qs, ks, vs, i_rows, lf_rows, revs = [], [], [], [], [], []
        for hh in range(nh):
            ln = slice(hh * LANE, (hh + 1) * LANE)
            for sl, cc, off in ((sf, c, 0), (sb, cb, HEADS)):
                v = v_ref[0, sl, ln]
                qs.append(q_ref[0, sl, ln])
                ks.append(k_ref[0, sl, ln])
                vs.append(jnp.where(lane == ML_HD, jnp.ones_like(v), v))
                i_rows.append(gt_ref[0, off + hd0 + hh, pl.ds(cc, 1), :])
                lf_rows.append(gt_ref[0, 2 * HEADS + off + hd0 + hh, pl.ds(cc, 1), :])
                revs.append(off > 0)
        houts, s_news, m_news = _ml_chunks(qs, ks, vs, i_rows, lf_rows, list(carry[0]), list(carry[1]), revs)
        for hh in range(nh):
            ln = slice(hh * LANE, (hh + 1) * LANE)
            hf_ref[sf, ln] = houts[2 * hh]
            hb_ref[sb, ln] = houts[2 * hh + 1]
        return tuple(s_news), tuple(m_news)

    s0 = jnp.zeros((LANE, LANE), F32)
    m0 = jnp.full((1, 1), M_INIT, F32)
    lax.fori_loop(0, nc_all, step, ((s0,) * (2 * nh), (m0,) * (2 * nh)), unroll=True)

    def finish(c, _):
        sl = pl.ds(pl.multiple_of(c * size, size), size)
        for hh in range(nh):
            ln = slice(hh * LANE, (hh + 1) * LANE)
            h = jnp.where(lane < ML_HD, hf_ref[sl, ln] + hb_ref[sl, ln], 0.0)
            y = h * lax.rsqrt(jnp.sum(h * h, axis=1, keepdims=True) * (1.0 / ML_HD) + EPS) * gain_ref[0, :, ln]
            y_ref[0, sl, ln] = (y * o_ref[0, sl, ln].astype(F32)).astype(BF16)
        return 0

    lax.fori_loop(0, nc_all, finish, 0, unroll=3)


def _mlstm(mlq, mlk, mlv, mlo, gates_t, gain, layer, ctx_len):
    nb, n_all, _ = mlq.shape
    size, nh = ML_CHUNK, ML_HPS
    nc_all = n_all // size
    width = nh * LANE
    ng = HEADS // nh
    kern = functools.partial(_mlstm_kernel, nc_ctx=ctx_len // size, nc_all=nc_all, size=size, nh=nh)
    slab = pl.BlockSpec((1, n_all, width), lambda b, h: (b, 0, h))
    return pl.pallas_call(
        kern,
        grid=(nb, ng),
        in_specs=[slab, slab, slab, slab,
                  pl.BlockSpec((1, 4 * HEADS, nc_all, size), lambda b, h: (b, 0, 0, 0)),
                  pl.BlockSpec((1, 1, width), lambda b, h: (layer * ng + h, 0, 0))],
        out_specs=slab,
        out_shape=jax.ShapeDtypeStruct((nb, n_all, SLAB), BF16),
        scratch_shapes=[pltpu.VMEM((n_all, width), F32), pltpu.VMEM((n_all, width), F32)],
        compiler_params=_cparams(("parallel", "arbitrary")),
        name="mlstm",
    )(mlq, mlk, mlv, mlo, gates_t, gain.reshape(-1, 1, width))


def _hg_tables(size):
    t = np.arange(size)
    tri = (t[None, :] <= t[:, None]).astype(np.float32)
    x = t[:, None] ^ t[None, :]
    lvl = np.where(x < HG_DIAG, 0, np.floor(np.log2(np.maximum(x, 1))).astype(np.int64) - 2)
    code = np.where(t[None, :] <= t[:, None], lvl, -1).astype(np.int32)
    j = np.arange(HG_DIAG * LANE) // LANE
    emat = (np.arange(size)[None, :] % HG_DIAG == j[:, None]).astype(np.float32)
    return (jnp.asarray(np.stack([tri, tri.T]), BF16), jnp.asarray(np.stack([code, code.T])),
            jnp.asarray(emat, BF16))


def _hg_chunks(qs, lfs, vs, sts, tris, codes, emat, revs, w_refs):
    n = len(qs)
    size = qs[0].shape[0]
    bc2s, kks = [], []
    for i in range(n):
        lf2 = lfs[i]
        hi = lf2.astype(BF16)
        r1 = lf2 - hi.astype(F32)
        mid = r1.astype(BF16)
        lo = (r1 - mid.astype(F32)).astype(BF16)
        cs = _dot(tris[i], jnp.concatenate([hi, mid, lo], axis=1))
        bc2s.append((cs[:, 0:LANE] + cs[:, LANE:2 * LANE]) + cs[:, 2 * LANE:3 * LANE])
        kks.append(jnp.maximum(1.0 - jnp.exp2(lf2), 0.0))

    atts = [jnp.zeros((size, size), F32) for _ in range(n)]
    c, lvl = HG_DIAG, 1
    while c < size:
        blk = 2 * c
        for i in range(n):
            ridx = c if revs[i] else c - 1
            b3 = bc2s[i].reshape(size // blk, blk, LANE)
            ref = jnp.broadcast_to(b3[:, ridx:ridx + 1, :], b3.shape).reshape(size, LANE)
            e = jnp.exp2(_neg_abs(bc2s[i] - ref))
            a = _dot_nt((qs[i] * e).astype(BF16), (kks[i] * e).astype(BF16))
            atts[i] = jnp.where(codes[i] == lvl, a, atts[i])
        c, lvl = blk, lvl + 1

    for i in range(n):
        w_refs[i][...] = bc2s[i] - jnp.log2(kks[i])
        ps = []
        for j in range(HG_DIAG):
            wj = jnp.concatenate([jnp.broadcast_to(w_refs[i][pl.ds(HG_DIAG * blk_i + j, 1), :], (HG_DIAG, LANE))
                                  for blk_i in range(size // HG_DIAG)], axis=0)
            ps.append((qs[i] * jnp.exp2(jnp.minimum(bc2s[i] - wj, 0.0))).astype(BF16))
        atts[i] = jnp.where(codes[i] == 0, _dot(jnp.concatenate(ps, axis=1), emat), atts[i])

    outs, new_sts = [], []
    for i in range(n):
        last = 0 if revs[i] else size - 1
        bl = bc2s[i][last:last + 1]
        outs.append(_dot(atts[i].astype(BF16), vs[i])
                    + _dot_nt((qs[i] * jnp.exp2(bc2s[i])).astype(BF16), sts[i].astype(BF16)))
        new_sts.append(sts[i] * jnp.exp2(bl) + _dot_tn(vs[i], (kks[i] * jnp.exp2(bl - bc2s[i])).astype(BF16)))
    return outs, new_sts


def _hgrn_kernel(q_ref, lf0_ref, lf1_ref, v_ref, g_ref, gain_ref, tri_ref, code_ref, emat_ref,
                 y_ref, of_ref, ob_ref, *w_refs, nc_ctx, nc_all, size, npair):
    nc_lat = nc_all - nc_ctx
    low = lax.broadcasted_iota(jnp.int32, (size, LANE), 1) < HG_DV

    def step(c, carry):
        cb = jnp.where(c < nc_ctx, nc_ctx - 1 - c, nc_ctx + nc_lat - 1 - (c - nc_ctx))
        sf = pl.ds(pl.multiple_of(c * size, size), size)
        sb = pl.ds(pl.multiple_of(cb * size, size), size)
        qs, lfs, vs, tris, codes, revs = [], [], [], [], [], []
        for hh in range(2 * npair):
            ln = slice(hh * LANE, (hh + 1) * LANE)
            pv = slice((hh // 2) * LANE, (hh // 2 + 1) * LANE)
            qs += [q_ref[0, sf, ln].astype(F32), q_ref[0, sb, ln].astype(F32)]
            lfs += [lf0_ref[0, sf, ln], lf1_ref[0, sb, ln]]
            vs += [v_ref[0, sf, pv], v_ref[0, sb, pv]]
            tris += [tri_ref[0], tri_ref[1]]
            codes += [code_ref[0], code_ref[1]]
            revs += [False, True]
        outs, sts = _hg_chunks(qs, lfs, vs, list(carry), tris, codes, emat_ref[...], revs, w_refs)
        for p in range(npair):
            pv = slice(p * LANE, (p + 1) * LANE)
            of_ref[sf, pv] = jnp.where(low, outs[4 * p], outs[4 * p + 2])
            ob_ref[sb, pv] = jnp.where(low, outs[4 * p + 1], outs[4 * p + 3])
        return tuple(sts)

    st0 = jnp.zeros((LANE, LANE), F32)
    lax.fori_loop(0, nc_all, step, (st0,) * (4 * npair), unroll=HG_UNROLL)

    def finish(c, _):
        sl = pl.ds(pl.multiple_of(c * size, size), size)
        for p in range(npair):
            pv = slice(p * LANE, (p + 1) * LANE)
            o = of_ref[sl, pv] + ob_ref[sl, pv]
            sq = o * o
            ms0 = jnp.sum(jnp.where(low, sq, 0.0), axis=1, keepdims=True) * (1.0 / HG_DV)
            ms1 = jnp.sum(jnp.where(low, 0.0, sq), axis=1, keepdims=True) * (1.0 / HG_DV)
            y = o * lax.rsqrt(jnp.where(low, ms0, ms1) + EPS) * gain_ref[0, :, pv]
            y_ref[0, sl, pv] = (y * g_ref[0, sl, pv].astype(F32)).astype(BF16)
        return 0

    lax.fori_loop(0, nc_all, finish, 0, unroll=6 // npair)


def _hgrn(hgq, hgf, hgv, hgg, gain, layer, ctx_len):
    nb, n_all, _ = hgq.shape
    size, npair = HG_CHUNK, HG_PAIRS
    nc_all = n_all // size
    ng = HEADS // (2 * npair)
    kern = functools.partial(_hgrn_kernel, nc_ctx=ctx_len // size, nc_all=nc_all, size=size, npair=npair)
    wide = pl.BlockSpec((1, n_all, 2 * npair * LANE), lambda b, h: (b, 0, h))
    slab = pl.BlockSpec((1, n_all, npair * LANE), lambda b, h: (b, 0, h))
    tri, code, emat = _hg_tables(size)
    return pl.pallas_call(
        kern,
        grid=(nb, ng),
        in_specs=[wide, wide,
                  pl.BlockSpec((1, n_all, 2 * npair * LANE), lambda b, h: (b, 0, ng + h)),
                  slab, slab,
                  pl.BlockSpec((1, 1, npair * LANE), lambda b, h: (layer * ng + h, 0, 0)),
                  pl.BlockSpec(tri.shape, lambda b, h: (0, 0, 0)),
                  pl.BlockSpec(code.shape, lambda b, h: (0, 0, 0)),
                  pl.BlockSpec(emat.shape, lambda b, h: (0, 0))],
        out_specs=slab,
        out_shape=jax.ShapeDtypeStruct((nb, n_all, HG_W), BF16),
        scratch_shapes=[pltpu.VMEM((n_all, npair * LANE), F32), pltpu.VMEM((n_all, npair * LANE), F32)]
        + [pltpu.VMEM((size, LANE), F32)] * (4 * npair),
        compiler_params=_cparams(("parallel", "arbitrary")),
        name="hgrn2",
    )(hgq, hgf, hgf, hgv, hgg, gain.reshape(-1, 1, npair * LANE), tri, code, emat)


def _attn_kernel(q_ref, k_ref, vt_ref, lam_ref, gain_ref, y_ref, *scratch, tq, kc, nh, ctx_len, lam_init):
    n_all = k_ref.shape[1]
    n_tiles = (n_all - ctx_len) // tq
    s_refs = [scratch[2 * hh:2 * hh + 2] for hh in range(nh)]
    m_refs = [scratch[2 * nh + 2 * hh:2 * nh + 2 * hh + 2] for hh in range(nh)]
    lv = lam_ref[0]
    lam = (jnp.exp(jnp.sum(lv[0:1] * lv[1:2], axis=1, keepdims=True))
           - jnp.exp(jnp.sum(lv[2:3] * lv[3:4], axis=1, keepdims=True)) + lam_init)
    lane = lax.broadcasted_iota(jnp.int32, (tq, LANE), 1)
    vrow = lax.broadcasted_iota(jnp.int32, (LANE, kc), 0)
    lanes = [slice(hh * LANE, (hh + 1) * LANE) for hh in range(nh)]

    def rows(t):
        return pl.ds(pl.multiple_of(ctx_len + t * tq, tq), tq)

    def load_qq(sl, hh):
        q = q_ref[0, sl, lanes[hh]]
        zero = jnp.zeros_like(q)
        return jnp.concatenate([jnp.where(lane < DA_HD, q, zero), jnp.where(lane < DA_HD, zero, q)], axis=0)

    def stage_a_chunk(qq, hh, slot, c, m):
        s = _dot_nt(k_ref[0, c * kc:(c + 1) * kc, lanes[hh]], qq)
        s_refs[hh][slot][c * kc:(c + 1) * kc, :] = s
        cm = jnp.max(s, axis=0, keepdims=True)
        return cm if m is None else jnp.maximum(m, cm)

    def stage_b_chunk(hh, slot, c, m, acc):
        p = jnp.exp2(s_refs[hh][slot][c * kc:(c + 1) * kc, :] - m).astype(BF16)
        vt = vt_ref[0, lanes[hh], c * kc:(c + 1) * kc]
        pv = _dot(jnp.where(vrow == DA_VD, jnp.ones_like(vt), vt), p)
        return pv if acc is None else acc + pv

    def finish(acc, sl, hh):
        r0 = 1.0 / acc[DA_VD:DA_VD + 1, 0:tq]
        r1 = lam / acc[DA_VD:DA_VD + 1, tq:2 * tq]
        o = (acc[:, 0:tq] * r0 - acc[:, tq:2 * tq] * r1).T
        o = jnp.where(lane < DA_VD, o, 0.0)
        y = o * lax.rsqrt(jnp.sum(o * o, axis=1, keepdims=True) * (1.0 / DA_VD) + EPS) * gain_ref[0]
        y_ref[0, sl, lanes[hh]] = (y * (1.0 - lam_init)).astype(BF16)

    def stages(sl_a, sl_b, slot_a, nk_a, nk_b):
        slot_b = 1 - slot_a
        qqs = [load_qq(sl_a, hh) for hh in range(nh)] if sl_a is not None else None
        m_bs = [m_refs[hh][slot_b][0:1, :] for hh in range(nh)] if sl_b is not None else None
        ms, accs = [None] * nh, [None] * nh
        for c in range(max(nk_a, nk_b) // kc):
            for hh in range(nh):
                if sl_a is not None and c < nk_a // kc:
                    ms[hh] = stage_a_chunk(qqs[hh], hh, slot_a, c, ms[hh])
                if sl_b is not None and c < nk_b // kc:
                    accs[hh] = stage_b_chunk(hh, slot_b, c, m_bs[hh], accs[hh])
        for hh in range(nh):
            if sl_a is not None:
                m_refs[hh][slot_a][...] = jnp.broadcast_to(ms[hh], (8, 2 * tq))
            if sl_b is not None:
                finish(accs[hh], sl_b, hh)

    ctx_rows = pl.ds(0, tq)
    stages(ctx_rows, None, 1, ctx_len, 0)
    stages(rows(0), ctx_rows, 0, n_all, ctx_len)

    def body(i, _):
        t = 1 + 2 * i
        stages(rows(t), rows(t - 1), 1, n_all, n_all)
        stages(rows(t + 1), rows(t), 0, n_all, n_all)
        return 0

    lax.fori_loop(0, (n_tiles - 2) // 2, body, 0)
    stages(rows(n_tiles - 1), rows(n_tiles - 2), 1, n_all, n_all)
    stages(None, rows(n_tiles - 1), 0, 0, n_all)


def _attn(daq, dak, dav_t, lam_pad, gain, layer, ctx_len, lam_init):
    nb, n_all, _ = daq.shape
    tq, kc, nh = ATT_TQ, ATT_KC, ATT_HPS
    n_tiles = (n_all - ctx_len) // tq
    assert n_tiles >= 2 and n_tiles % 2 == 0 and ctx_len == tq and ctx_len % kc == 0 and n_all % kc == 0
    kern = functools.partial(_attn_kernel, tq=tq, kc=kc, nh=nh, ctx_len=ctx_len, lam_init=lam_init)
    slab = pl.BlockSpec((1, n_all, nh * LANE), lambda b, h: (b, 0, h))
    sbuf, mbuf = pltpu.VMEM((n_all, 2 * tq), F32), pltpu.VMEM((8, 2 * tq), F32)
    return pl.pallas_call(
        kern,
        grid=(nb, HEADS // nh),
        in_specs=[slab, slab,
                  pl.BlockSpec((1, nh * LANE, n_all), lambda b, h: (b, h, 0)),
                  pl.BlockSpec((1, 4, LANE), lambda b, h: (layer, 0, 0)),
                  pl.BlockSpec((1, 1, LANE), lambda b, h: (layer, 0, 0))],
        out_specs=slab,
        out_shape=jax.ShapeDtypeStruct((nb, n_all, SLAB), BF16),
        scratch_shapes=[sbuf] * (2 * nh) + [mbuf] * (2 * nh),
        compiler_params=_cparams(("parallel", "arbitrary")),
        name="diff_attn",
    )(daq, dak, dav_t, lam_pad, gain)


def _resident(shape, index_map):
    return pl.BlockSpec(shape, index_map, pipeline_mode=pl.Buffered(1))


def _mix_ffn_kernel(yml_ref, yhg_ref, yda_ref, x_ref, g1x_ref, g1c_ref, a2x_ref, a2c_ref, s2x_ref, s2c_ref,
                    g2x_ref, g2c_ref, wo_ref, gpm_ref, gpf_ref, wg_ref, wu_ref, wd_ref, gqf_ref, o_ref, *, tm, nsub, ctx_len, t0):
    row0 = (pl.program_id(1) + t0) * tm
    split = min(ctx_len % tm if ctx_len % tm else tm, tm)
    top_is_ctx = row0 < ctx_len

    def mod(c_ref, x_ref_, lo):
        return jnp.where(top_is_ctx, c_ref[0, 0], x_ref_[0, 0]) if lo < split else x_ref_[0, 0]

    sub = tm // nsub
    x1s, h2s = [], []
    for i in range(nsub):
        lo, rs = i * sub, slice(i * sub, (i + 1) * sub)
        y = (_dot(yml_ref[0, rs, :], wo_ref[0, 0:SLAB, :]) + _dot(yhg_ref[0, rs, :], wo_ref[0, SLAB:SLAB + HG_W, :])
             + _dot(yda_ref[0, rs, :], wo_ref[0, SLAB + HG_W:2 * SLAB + HG_W, :]))
        x1 = x_ref[0, rs, :] + _rms(y, mod(g1c_ref, g1x_ref, lo) * gpm_ref[0])
        h2s.append((_rms(x1, gpf_ref[0] * (1.0 + mod(a2c_ref, a2x_ref, lo))) + mod(s2c_ref, s2x_ref, lo)).astype(BF16))
        x1s.append(x1)
    for i in range(nsub):
        lo, rs = i * sub, slice(i * sub, (i + 1) * sub)
        act = _silu(_dot(h2s[i], wg_ref[0])) * _dot(h2s[i], wu_ref[0])
        f = _dot(act.astype(BF16), wd_ref[0])
        o_ref[0, rs, :] = x1s[i] + _rms(f, mod(g2c_ref, g2x_ref, lo) * gqf_ref[0])


def _mix_ffn(yml, yhg, yda, xs, mod4, w_out_pad, g_post_mix, g_pre_ffn, wg, wu, wd, g_post_ffn, layer, ctx_len,
             tm, nsub, t0):
    nb, n_all, d = xs.shape
    dff = wg.shape[-1]
    assert tm % nsub == 0 and (ctx_len % tm) % (tm // nsub) == 0
    kern = functools.partial(_mix_ffn_kernel, tm=tm, nsub=nsub, ctx_len=ctx_len, t0=t0)
    lsel = lambda b, t: (layer, 0, 0)
    tile = lambda width: pl.BlockSpec((1, tm, width), lambda b, t: (b, t + t0, 0))
    return pl.pallas_call(
        kern,
        grid=(nb, n_all // tm - t0),
        in_specs=[tile(SLAB), tile(HG_W), tile(SLAB), tile(d)] + _mod_specs(layer, nb, (2, 4, 3, 5), d) + [
            _resident((1, 2 * SLAB + HG_W, d), lsel),
            pl.BlockSpec((1, 1, d), lsel),
            pl.BlockSpec((1, 1, d), lsel),
            _resident((1, d, dff), lsel),
            _resident((1, d, dff), lsel),
            _resident((1, dff, d), lsel),
            pl.BlockSpec((1, 1, d), lsel)],
        out_specs=pl.BlockSpec((1, tm, d), lambda b, t: (b, t, 0)),
        out_shape=jax.ShapeDtypeStruct((nb, n_all - t0 * tm, d), F32),
        compiler_params=_cparams(("parallel", "arbitrary")),
        name="mix_ffn",
    )(yml, yhg, yda, xs, *([mod4] * 8), w_out_pad, g_post_mix, g_pre_ffn, wg, wu, wd, g_post_ffn)


def _pad_heads(w, hd):
    lead = w.shape[:-1]
    w = w.reshape(*lead, HEADS, hd)
    w = jnp.pad(w, [(0, 0)] * len(lead) + [(0, 0), (0, LANE - hd)])
    return w.reshape(*lead, SLAB)


def _pad_in_cols(w, extra_f=None):
    sizes = (4 * ML_HD,) * 4 + (2 * HEADS, 2 * HEADS, HEADS * HG_DK, 2 * HEADS * HG_DK, HEADS * HG_DV, HEADS * HG_DV,
             2 * HEADS * DA_HD, 2 * HEADS * DA_HD, HEADS * DA_VD)
    parts = []
    off = 0
    for s in sizes:
        parts.append(w[..., off:off + s])
        off += s
    gi, gf = parts[4], parts[5]
    if extra_f is not None:
        gf = gf + extra_f
    gate = jnp.concatenate([gi, gf], axis=-1)
    gate = jnp.pad(gate, [(0, 0)] * (gate.ndim - 1) + [(0, LANE - 4 * HEADS)])
    return jnp.concatenate([
        _pad_heads(parts[0], ML_HD), _pad_heads(parts[1], ML_HD), _pad_heads(parts[2], ML_HD), _pad_heads(parts[3], ML_HD),
        gate, parts[6], parts[7], parts[8], parts[9],
        _pad_heads(parts[10], DA_VD), _pad_heads(parts[11], DA_VD), _pad_heads(parts[12], DA_VD)], axis=-1)


def _pad_out_rows(w_out):
    depth, _, d = w_out.shape

    def pad_rows(w, hd):
        w = jnp.pad(w.reshape(depth, HEADS, hd, d), ((0, 0), (0, 0), (0, LANE - hd), (0, 0)))
        return w.reshape(depth, SLAB, d)

    ml, hg, da = w_out[:, :4 * ML_HD], w_out[:, 4 * ML_HD:4 * ML_HD + HG_W], w_out[:, 4 * ML_HD + HG_W:]
    return jnp.concatenate([pad_rows(ml, ML_HD), hg, pad_rows(da, DA_VD)], axis=1)


def _rope_tables(n_lat, ctx_len):
    rows = n_lat // GRID_W
    row = jnp.repeat(jnp.arange(rows), GRID_W).astype(F32)
    col = jnp.tile(jnp.arange(GRID_W), rows).astype(F32)
    half = DA_HD // 2
    inv = ROPE_BASE ** (-jnp.arange(0, half, 2, dtype=F32) / half)
    ang_r, ang_c = row[:, None] * inv, col[:, None] * inv
    zero = jnp.zeros_like(ang_r)

    def one_map(fr, fc, sel):
        r, c = fr(ang_r), fc(ang_c)
        if sel == "cos":
            return jnp.concatenate([r, r, c, c], axis=1)
        if sel == "a":
            return jnp.concatenate([-r, zero, -c, zero], axis=1)
        return jnp.concatenate([zero, r, zero, c], axis=1)

    tabs = []
    for sel, fn in (("cos", jnp.cos), ("a", jnp.sin), ("b", jnp.sin)):
        m = one_map(fn, fn, sel)
        lat = jnp.pad(jnp.concatenate([m, m], axis=1), ((0, 0), (0, LANE - 2 * DA_HD)))
        fill = 1.0 if sel == "cos" else 0.0
        ctx_rows = jnp.pad(jnp.full((ctx_len, 2 * DA_HD), fill, F32), ((0, 0), (0, LANE - 2 * DA_HD)))
        tabs.append(jnp.concatenate([ctx_rows, lat], axis=0))
    return tabs


def kernel(x, c, ctx, c_ctx, w_ada, b_ada, g_pre_mix, g_post_mix, g_pre_ffn, g_post_ffn, w_in, b_in, w_out,
           ml_f_bias, ml_norm, hg_lb, hg_norm, da_lambda, da_norm, w_ffn_gate, w_ffn_up, w_ffn_down):
    nb, n_lat, d = x.shape
    ctx_len = ctx.shape[1]
    depth = w_ada.shape[0]
    n_all = ctx_len + n_lat

    w_pad = _pad_in_cols(w_in).astype(BF16)
    b_pad = _pad_in_cols(b_in, extra_f=ml_f_bias)[:, None, :]
    w_out_pad = _pad_out_rows(w_out).astype(BF16)
    wg, wu, wd = w_ffn_gate.astype(BF16), w_ffn_up.astype(BF16), w_ffn_down.astype(BF16)
    ml_gain = _pad_heads(ml_norm, ML_HD).reshape(depth * HEADS, 1, LANE)
    hg_gain = hg_norm.reshape(-1, 1, LANE)
    da_gain = jnp.pad(da_norm, ((0, 0), (0, LANE - DA_VD)))[:, None, :]
    lam_pad = jnp.pad(da_lambda.astype(F32), ((0, 0), (0, 0), (0, LANE - DA_HD)))
    sm = jax.nn.softmax(hg_lb.astype(F32), axis=0)
    lbs = jnp.cumsum(sm, axis=0) - sm[0:1]
    lbf = jnp.maximum(lbs, LB_FLOOR)[:, None, :]
    oml = (1.0 - lbs)[:, None, :]
    rope_tabs = _rope_tables(n_lat, ctx_len)
    r3 = lambda g: g[:, None, :]

    mp = -(-(nb + 1) // 8) * 8
    cc = jnp.concatenate([c, c_ctx[None, :], jnp.zeros((mp - nb - 1, d), F32)], axis=0)
    mod4 = _ada(cc, w_ada, b_ada).reshape(depth, mp, 1, 6 * d)

    fuse_concat = ctx_len == IN_TM_ALT and n_lat % IN_TM_ALT == 0
    xs = None if fuse_concat else jnp.concatenate([ctx, x], axis=1)
    for l in range(depth):
        lam_init = 0.8 - 0.6 * math.exp(-0.3 * l)
        if l == 0 and fuse_concat:
            (mlq, mlk, mlv, mlo, gates, hgq, hgf, hgv, hgg, daq, dak, dav_t, xs) = _inproj(
                (ctx, x), mod4, r3(g_pre_mix), w_pad, b_pad, rope_tabs, lbf, oml, l, ctx_len, IN_TM_ALT, IN_SUB_ALT)
        else:
            (mlq, mlk, mlv, mlo, gates, hgq, hgf, hgv, hgg, daq, dak, dav_t) = _inproj(
                xs, mod4, r3(g_pre_mix), w_pad, b_pad, rope_tabs, lbf, oml, l, ctx_len,
                *((IN_TM, IN_SUB) if n_all % IN_TM == 0 else (IN_TM_ALT, IN_SUB_ALT)))
        gates_t = gates.reshape(nb, 4 * HEADS, n_all // ML_CHUNK, ML_CHUNK)
        yml = _mlstm(mlq, mlk, mlv, mlo, gates_t, ml_gain, l, ctx_len)
        yhg = _hgrn(hgq, hgf, hgv, hgg, hg_gain, l, ctx_len)
        yda = _attn(daq, dak, dav_t, lam_pad, da_gain, l, ctx_len, lam_init)
        last = l == depth - 1
        tm, nsub = (MIX_TM, MIX_SUB) if not last and n_all % MIX_TM == 0 else (MIX_TM_LAST, MIX_SUB_LAST)
        xs = _mix_ffn(yml, yhg, yda, xs, mod4, w_out_pad, r3(g_post_mix), r3(g_pre_ffn), wg, wu, wd,
                      r3(g_post_ffn), l, ctx_len, tm, nsub, ctx_len // tm if last else 0)
    return xs
```

```python
import functools
import math

import numpy as np
import jax
import jax.numpy as jnp
from jax import lax
from jax.experimental import pallas as pl
from jax.experimental.pallas import tpu as pltpu

F32 = jnp.float32
BF16 = jnp.bfloat16
HIGHEST = lax.Precision.HIGHEST

LANE = 128
VMEM_LIMIT = 52 * 1024 * 1024

EPS = 1e-6
NEG = -1e30
LB_FLOOR = 1e-30
M_INIT = -1e30
GRID_W = 64
ROPE_BASE = 10000.0

HEADS = 4
ML_HD = 96
HG_DV = 64
HG_DK = 128
DA_HD = 48
DA_VD = 2 * DA_HD
SLAB = HEADS * LANE
HG_W = HEADS * HG_DV

ML_CHUNK = 256
ML_HPS = 2
HG_CHUNK = 128
HG_DIAG = 8
HG_PAIRS, HG_UNROLL = 2, 3
ATT_TQ = 256
IN_CW = 256
IN_TM, IN_SUB = 768, 3
IN_TM_ALT, IN_SUB_ALT = 256, 2
MIX_TM, MIX_SUB = 768, 3
MIX_TM_LAST, MIX_SUB_LAST = 256, 2
ATT_KC = 256
ATT_HPS = 2
LOG2E = 1.4426950408889634

OFF_MLQ, OFF_MLK, OFF_MLV, OFF_MLO = 0, SLAB, 2 * SLAB, 3 * SLAB
OFF_GATE = 4 * SLAB
OFF_HGQ = OFF_GATE + LANE
OFF_HGF = OFF_HGQ + SLAB
OFF_HGV = OFF_HGF + 2 * SLAB
OFF_HGG = OFF_HGV + HG_W
OFF_DAQ = OFF_HGG + HG_W
OFF_DAK = OFF_DAQ + SLAB
OFF_DAV = OFF_DAK + SLAB
NP_IN = OFF_DAV + SLAB


def _cparams(sem):
    return pltpu.CompilerParams(dimension_semantics=sem, vmem_limit_bytes=VMEM_LIMIT)


def _silu(x):
    return x * jax.nn.sigmoid(x)


def _log_sigmoid(z):
    return jnp.minimum(z, 0.0) - jnp.log1p(jnp.exp(-jnp.abs(z)))


def _neg_abs(x):
    bits = lax.bitcast_convert_type(x, jnp.uint32) | jnp.uint32(0x80000000)
    return lax.bitcast_convert_type(bits, F32)


def _rms(x, g):
    return x * lax.rsqrt(jnp.mean(x * x, axis=-1, keepdims=True) + EPS) * g


def _dot(a, b):
    return jnp.dot(a, b, preferred_element_type=F32)


def _dot_nt(a, b):
    return lax.dot_general(a, b, (((1,), (1,)), ((), ())), preferred_element_type=F32)


def _dot_tn(a, b):
    return lax.dot_general(a, b, (((0,), (0,)), ((), ())), preferred_element_type=F32)


def _ada_kernel(s_ref, w_ref, b_ref, o_ref):
    s = _silu(s_ref[...])
    o_ref[0] = jnp.dot(s, w_ref[0], precision=HIGHEST, preferred_element_type=F32) + b_ref[0]


def _ada(cc, w_ada, b_ada):
    depth, d, d6 = w_ada.shape
    mp = cc.shape[0]
    tn = 2048
    return pl.pallas_call(
        _ada_kernel,
        grid=(depth, d6 // tn),
        in_specs=[pl.BlockSpec((mp, d), lambda l, j: (0, 0)),
                  pl.BlockSpec((1, d, tn), lambda l, j: (l, 0, j)),
                  pl.BlockSpec((1, 1, tn), lambda l, j: (l, 0, j))],
        out_specs=pl.BlockSpec((1, mp, tn), lambda l, j: (l, 0, j)),
        out_shape=jax.ShapeDtypeStruct((depth, mp, d6), F32),
        compiler_params=_cparams(("arbitrary", "arbitrary")),
        name="ada_mod",
    )(cc, w_ada, b_ada.reshape(depth, 1, d6))


def _mod_specs(layer, nb, idxs, d):
    specs = []
    for j in idxs:
        specs.append(pl.BlockSpec((1, 1, 1, d), lambda b, t, j=j: (layer, b, 0, j)))
        specs.append(pl.BlockSpec((1, 1, 1, d), lambda b, t, j=j: (layer, nb, 0, j)))
    return specs


def _inproj_kernel(x_ref, sx_ref, sc_ref, ax_ref, ac_ref, g_ref, w_ref, b_ref,
                   cos_ref, sna_ref, snb_ref, lbf_ref, oml_ref,
                   mlq_ref, mlk_ref, mlv_ref, mlo_ref, gate_ref,
                   hgq_ref, hgf_ref, hgv_ref, hgg_ref, daq_ref, dak_ref, dav_ref,
                   *, tm, nsub, ctx_len):
    split = min(ctx_len % tm if ctx_len % tm else tm, tm)
    top_is_ctx = pl.program_id(1) * tm < ctx_len
    sub = tm // nsub
    half = DA_HD // 4
    lane = lax.broadcasted_iota(jnp.int32, (sub, LANE), 1)

    def rope(p, rs):
        cos, sna, snb = cos_ref[rs, :], sna_ref[rs, :], snb_ref[rs, :]
        outs = []
        for i in range(p.shape[1] // LANE):
            xh = p[:, i * LANE:(i + 1) * LANE]
            outs.append(xh * cos + pltpu.roll(xh, LANE - half, 1) * sna + pltpu.roll(xh, half, 1) * snb)
        return jnp.concatenate(outs, axis=1)

    def log2_f(p, c0, rs):
        w = p.shape[1]
        return jnp.log2(lbf_ref[0, :, c0:c0 + w] + oml_ref[0, :, c0:c0 + w] * jax.nn.sigmoid(p))

    def gates(p, c0, rs):
        return jnp.where((lane >= 2 * HEADS) & (lane < 4 * HEADS), _log_sigmoid(p), p)

    plain = lambda p, c0, rs: p
    groups = {
        "mlq": (mlq_ref, OFF_MLQ, SLAB, plain),
        "mlk": (mlk_ref, OFF_MLK, SLAB, lambda p, c0, rs: p * (ML_HD ** -0.5)),
        "mlv": (mlv_ref, OFF_MLV, SLAB, plain),
        "mlo": (mlo_ref, OFF_MLO, SLAB, lambda p, c0, rs: jax.nn.sigmoid(p)),
        "gate": (gate_ref, OFF_GATE, LANE, gates),
        "hgq": (hgq_ref, OFF_HGQ, SLAB, lambda p, c0, rs: _silu(p)),
        "hgf": (hgf_ref, OFF_HGF, 2 * SLAB, log2_f),
        "hgv": (hgv_ref, OFF_HGV, HG_W, plain),
        "hgg": (hgg_ref, OFF_HGG, HG_W, lambda p, c0, rs: _silu(p)),
        "daq": (daq_ref, OFF_DAQ, SLAB, lambda p, c0, rs: rope(p, rs) * (LOG2E * DA_HD ** -0.5)),
        "dak": (dak_ref, OFF_DAK, SLAB, lambda p, c0, rs: rope(p, rs)),
        "dav": (dav_ref, OFF_DAV, SLAB, plain),
    }
    order = [("hgf", 0), ("mlq", 0), ("hgf", 1), ("mlq", 1), ("hgf", 2), ("mlk", 0), ("hgf", 3), ("mlk", 1),
             ("mlo", 0), ("mlv", 0), ("mlo", 1), ("mlv", 1), ("hgq", 0), ("dav", 0), ("hgq", 1), ("dav", 1),
             ("daq", 0), ("hgv", 0), ("daq", 1), ("gate", 0), ("dak", 0), ("hgg", 0), ("dak", 1)]
    for i in range(nsub):
        rs = slice(i * sub, (i + 1) * sub)
        ctx_rows = i * sub < split
        shift = jnp.where(top_is_ctx, sc_ref[0, 0], sx_ref[0, 0]) if ctx_rows else sx_ref[0, 0]
        scale = 1.0 + (jnp.where(top_is_ctx, ac_ref[0, 0], ax_ref[0, 0]) if ctx_rows else ax_ref[0, 0])
        h = (_rms(x_ref[0, rs, :], g_ref[0] * scale) + shift).astype(BF16)
        for name, ci in order:
            ref, off, width, fn = groups[name]
            cw = min(width, IN_CW)
            c0 = ci * cw
            p = _dot(h, w_ref[0, :, off + c0:off + c0 + cw]) + b_ref[0, :, off + c0:off + c0 + cw]
            if name == "dav":
                ref[0, c0:c0 + cw, rs] = p.T.astype(ref.dtype)
            elif name == "gate":
                ref[0, :, rs] = fn(p, c0, rs).T[0:4 * HEADS, :]
            else:
                ref[0, rs, c0:c0 + cw] = fn(p, c0, rs).astype(ref.dtype)


def _inproj_first_kernel(ctx_ref, xa_ref, xb_ref, xc_ref, *refs, tm, nsub, ctx_len):
    xs_ref = refs[-1]
    xs_ref[0, 0:ctx_len, :] = jnp.where(pl.program_id(1) == 0, ctx_ref[0], xa_ref[0, 0])
    xs_ref[0, ctx_len:2 * ctx_len, :] = xb_ref[0, 0]
    xs_ref[0, 2 * ctx_len:3 * ctx_len, :] = xc_ref[0, 0]
    _inproj_kernel(xs_ref, *refs[:-1], tm=tm, nsub=nsub, ctx_len=ctx_len)


def _inproj(xs, mod4, g_pre, w_pad, b_pad, rope_tabs, lbf, oml, layer, ctx_len, tm, nsub):
    first = isinstance(xs, tuple)
    if first:
        ctx, x = xs
        nb, d = x.shape[0], x.shape[2]
        n_all = ctx_len + x.shape[1]
        assert tm == 3 * ctx_len and n_all % tm == 0
        x4 = x.reshape(nb, x.shape[1] // ctx_len, ctx_len, d)
    else:
        nb, n_all, d = xs.shape
    assert tm % nsub == 0 and (ctx_len % tm) % (tm // nsub) == 0
    kern = functools.partial(_inproj_first_kernel if first else _inproj_kernel, tm=tm, nsub=nsub, ctx_len=ctx_len)
    lsel = lambda b, t: (layer, 0, 0)
    tile = lambda width: pl.BlockSpec((1, tm, width), lambda b, t: (b, t, 0))
    tab = pl.BlockSpec((tm, LANE), lambda b, t: (t, 0))
    bf = lambda width: jax.ShapeDtypeStruct((nb, n_all, width), BF16)
    f32 = lambda width: jax.ShapeDtypeStruct((nb, n_all, width), F32)
    out_shapes = [bf(SLAB)] * 4 + [f32(LANE)] + [bf(SLAB), f32(2 * SLAB), bf(HG_W), bf(HG_W)] + [bf(SLAB)] * 3
    out_specs = [tile(s.shape[-1]) for s in out_shapes]
    out_shapes[-1] = jax.ShapeDtypeStruct((nb, SLAB, n_all), BF16)
    out_specs[-1] = pl.BlockSpec((1, SLAB, tm), lambda b, t: (b, 0, t))
    out_shapes[4] = jax.ShapeDtypeStruct((nb, 4 * HEADS, n_all), F32)
    out_specs[4] = pl.BlockSpec((1, 4 * HEADS, tm), lambda b, t: (b, 0, t))
    if first:
        unit = lambda f: pl.BlockSpec((1, 1, ctx_len, d), lambda b, t: (b, f(t), 0, 0))
        x_specs = [pl.BlockSpec((1, ctx_len, d), lambda b, t: (b, 0, 0)),
                   unit(lambda t: jnp.maximum(3 * t - 1, 0)), unit(lambda t: 3 * t), unit(lambda t: 3 * t + 1)]
        x_args = (ctx, x4, x4, x4)
        out_shapes.append(f32(d))
        out_specs.append(tile(d))
    else:
        x_specs, x_args = [tile(d)], (xs,)
    return pl.pallas_call(
        kern,
        grid=(nb, n_all // tm),
        in_specs=x_specs + _mod_specs(layer, nb, (0, 1), d) + [
            pl.BlockSpec((1, 1, d), lsel),
            pl.BlockSpec((1, d, NP_IN), lsel, pipeline_mode=pl.Buffered(1)),
            pl.BlockSpec((1, 1, NP_IN), lsel),
            tab, tab, tab,
            pl.BlockSpec((1, 1, 2 * SLAB), lsel),
            pl.BlockSpec((1, 1, 2 * SLAB), lsel)],
        out_specs=out_specs,
        out_shape=out_shapes,
        compiler_params=_cparams(("parallel", "arbitrary")),
        name="in_proj",
    )(*x_args, mod4, mod4, mod4, mod4, g_pre, w_pad, b_pad, *rope_tabs, lbf, oml)


def _ml_chunks(qs, ks, vs, i_rows, lf_rows, s_exts, ms, revs):
    n = len(qs)
    size = qs[0].shape[0]
    ii = lax.broadcasted_iota(jnp.int32, (size, size), 0)
    jj = lax.broadcasted_iota(jnp.int32, (size, size), 1)
    lane = lax.broadcasted_iota(jnp.int32, (size, LANE), 1)
    cums = {rev: jnp.where((ii >= jj) if rev else (ii <= jj), 1.0, 0.0).astype(BF16) for rev in set(revs)}
    seens = {rev: (jj >= ii) if rev else (jj <= ii) for rev in set(revs)}
    row16 = lax.broadcasted_iota(jnp.int32, (16, size), 0)

    b_cols, g_cols, g_rows = [], [], []
    for i in range(n):
        lf2 = lf_rows[i] * LOG2E
        hi = lf2.astype(BF16).astype(F32)
        mid = (lf2 - hi).astype(BF16).astype(F32)
        lo = (lf2 - hi) - mid
        parts = jnp.where(row16 == 0, hi, jnp.where(row16 == 1, mid, jnp.where(row16 == 2, lo, 0.0))).astype(BF16)
        cs = _dot(parts, cums[revs[i]])
        b_row = (cs[0:1] + cs[1:2]) + cs[2:3]
        g_row = i_rows[i] * LOG2E - b_row
        b_cols.append(jnp.sum(jnp.where(ii == jj, b_row, 0.0), axis=1, keepdims=True))
        g_cols.append(jnp.sum(jnp.where(ii == jj, g_row, 0.0), axis=1, keepdims=True))
        g_rows.append(g_row)

    mts, dws, w_inters = [], [], []
    for i in range(n):
        dmat = jnp.where(seens[revs[i]], b_cols[i] + g_rows[i], NEG)
        a_col = b_cols[i] + ms[i]
        mt = jnp.maximum(a_col, jnp.max(dmat, axis=1, keepdims=True))
        mts.append(mt)
        w_inters.append(jnp.exp2(a_col - mt))
        dws.append(jnp.exp2(dmat - mt))

    ress = []
    for i in range(n):
        qk = _dot_nt(qs[i], ks[i]) * dws[i]
        ress.append(_dot(qk.astype(BF16), vs[i]) + w_inters[i] * _dot(qs[i], s_exts[i].astype(BF16)))

    houts, s_news, m_news = [], [], []
    for i in range(n):
        den = jnp.sum(jnp.where(lane == ML_HD, ress[i], 0.0), axis=1, keepdims=True)
        houts.append(ress[i] * (1.0 / jnp.maximum(jnp.abs(den), jnp.exp2(-mts[i]))))
        last = 0 if revs[i] else size - 1
        m_new = mts[i][last:last + 1]
        b_last = b_cols[i][last:last + 1]
        decay = jnp.exp2(b_last + ms[i] - m_new)
        wk = jnp.exp2(b_last + g_cols[i] - m_new)
        s_news.append(decay * s_exts[i] + _dot_tn((wk * ks[i].astype(F32)).astype(BF16), vs[i]))
        m_news.append(m_new)
    return houts, s_news, m_news


def _mlstm_kernel(q_ref, k_ref, v_ref, o_ref, gt_ref, gain_ref, y_ref, hf_ref, hb_ref, *, nc_ctx, nc_all, size, nh):
    hd0 = pl.program_id(1) * nh
    lane = lax.broadcasted_iota(jnp.int32, (size, LANE), 1)
    nc_lat = nc_all - nc_ctx

    def step(c, carry):
        cb = jnp.where(c < nc_ctx, nc_ctx - 1 - c, nc_ctx + nc_lat - 1 - (c - nc_ctx))
        sf = pl.ds(pl.multiple_of(c * size, size), size)
        sb = pl.ds(pl.multiple_of(cb * size, size), size)
        qs, ks, vs, i_rows, lf_rows, revs = [], [], [], [], [], []
        for hh in range(nh):
            ln = slice(hh * LANE, (hh + 1) * LANE)
            for sl, cc, off in ((sf, c, 0), (sb, cb, HEADS)):
                v = v_ref[0, sl, ln]
                qs.append(q_ref[0, sl, ln])
                ks.append(k_ref[0, sl, ln])
                vs.append(jnp.where(lane == ML_HD, jnp.ones_like(v), v))
                i_rows.append(gt_ref[0, off + hd0 + hh, pl.ds(cc, 1), :])
                lf_rows.append(gt_ref[0, 2 * HEADS + off + hd0 + hh, pl.ds(cc, 1), :])
                revs.append(off > 0)
        houts, s_news, m_news = _ml_chunks(qs, ks, vs, i_rows, lf_rows, list(carry[0]), list(carry[1]), revs)
        for hh in range(nh):
            ln = slice(hh * LANE, (hh + 1) * LANE)
            hf_ref[sf, ln] = houts[2 * hh]
            hb_ref[sb, ln] = houts[2 * hh + 1]
        return tuple(s_news), tuple(m_news)

    s0 = jnp.zeros((LANE, LANE), F32)
    m0 = jnp.full((1, 1), M_INIT, F32)
    lax.fori_loop(0, nc_all, step, ((s0,) * (2 * nh), (m0,) * (2 * nh)), unroll=True)

    def finish(c, _):
        sl = pl.ds(pl.multiple_of(c * size, size), size)
        for hh in range(nh):
            ln = slice(hh * LANE, (hh + 1) * LANE)
            h = jnp.where(lane < ML_HD, hf_ref[sl, ln] + hb_ref[sl, ln], 0.0)
            y = h * lax.rsqrt(jnp.sum(h * h, axis=1, keepdims=True) * (1.0 / ML_HD) + EPS) * gain_ref[0, :, ln]
            y_ref[0, sl, ln] = (y * o_ref[0, sl, ln].astype(F32)).astype(BF16)
        return 0

    lax.fori_loop(0, nc_all, finish, 0, unroll=3)


def _mlstm(mlq, mlk, mlv, mlo, gates_t, gain, layer, ctx_len):
    nb, n_all, _ = mlq.shape
    size, nh = ML_CHUNK, ML_HPS
    nc_all = n_all // size
    width = nh * LANE
    ng = HEADS // nh
    kern = functools.partial(_mlstm_kernel, nc_ctx=ctx_len // size, nc_all=nc_all, size=size, nh=nh)
    slab = pl.BlockSpec((1, n_all, width), lambda b, h: (b, 0, h))
    return pl.pallas_call(
        kern,
        grid=(nb, ng),
        in_specs=[slab, slab, slab, slab,
                  pl.BlockSpec((1, 4 * HEADS, nc_all, size), lambda b, h: (b, 0, 0, 0)),
                  pl.BlockSpec((1, 1, width), lambda b, h: (layer * ng + h, 0, 0))],
        out_specs=slab,
        out_shape=jax.ShapeDtypeStruct((nb, n_all, SLAB), BF16),
        scratch_shapes=[pltpu.VMEM((n_all, width), F32), pltpu.VMEM((n_all, width), F32)],
        compiler_params=_cparams(("parallel", "arbitrary")),
        name="mlstm",
    )(mlq, mlk, mlv, mlo, gates_t, gain.reshape(-1, 1, width))


def _hg_tables(size):
    t = np.arange(size)
    tri = (t[None, :] <= t[:, None]).astype(np.float32)
    x = t[:, None] ^ t[None, :]
    lvl = np.where(x < HG_DIAG, 0, np.floor(np.log2(np.maximum(x, 1))).astype(np.int64) - 2)
    code = np.where(t[None, :] <= t[:, None], lvl, -1).astype(np.int32)
    j = np.arange(HG_DIAG * LANE) // LANE
    emat = (np.arange(size)[None, :] % HG_DIAG == j[:, None]).astype(np.float32)
    return (jnp.asarray(np.stack([tri, tri.T]), BF16), jnp.asarray(np.stack([code, code.T])),
            jnp.asarray(emat, BF16))


def _hg_chunks(qs, lfs, vs, sts, tris, codes, emat, revs, w_refs):
    n = len(qs)
    size = qs[0].shape[0]
    bc2s, kks = [], []
    for i in range(n):
        lf2 = lfs[i]
        hi = lf2.astype(BF16)
        r1 = lf2 - hi.astype(F32)
        mid = r1.astype(BF16)
        lo = (r1 - mid.astype(F32)).astype(BF16)
        cs = _dot(tris[i], jnp.concatenate([hi, mid, lo], axis=1))
        bc2s.append((cs[:, 0:LANE] + cs[:, LANE:2 * LANE]) + cs[:, 2 * LANE:3 * LANE])
        kks.append(jnp.maximum(1.0 - jnp.exp2(lf2), 0.0))

    atts = [jnp.zeros((size, size), F32) for _ in range(n)]
    c, lvl = HG_DIAG, 1
    while c < size:
        blk = 2 * c
        for i in range(n):
            ridx = c if revs[i] else c - 1
            b3 = bc2s[i].reshape(size // blk, blk, LANE)
            ref = jnp.broadcast_to(b3[:, ridx:ridx + 1, :], b3.shape).reshape(size, LANE)
            e = jnp.exp2(_neg_abs(bc2s[i] - ref))
            a = _dot_nt((qs[i] * e).astype(BF16), (kks[i] * e).astype(BF16))
            atts[i] = jnp.where(codes[i] == lvl, a, atts[i])
        c, lvl = blk, lvl + 1

    for i in range(n):
        w_refs[i][...] = bc2s[i] - jnp.log2(kks[i])
        ps = []
        for j in range(HG_DIAG):
            wj = jnp.concatenate([jnp.broadcast_to(w_refs[i][pl.ds(HG_DIAG * blk_i + j, 1), :], (HG_DIAG, LANE))
                                  for blk_i in range(size // HG_DIAG)], axis=0)
            ps.append((qs[i] * jnp.exp2(jnp.minimum(bc2s[i] - wj, 0.0))).astype(BF16))
        atts[i] = jnp.where(codes[i] == 0, _dot(jnp.concatenate(ps, axis=1), emat), atts[i])

    outs, new_sts = [], []
    for i in range(n):
        last = 0 if revs[i] else size - 1
        bl = bc2s[i][last:last + 1]
        outs.append(_dot(atts[i].astype(BF16), vs[i])
                    + _dot_nt((qs[i] * jnp.exp2(bc2s[i])).astype(BF16), sts[i].astype(BF16)))
        new_sts.append(sts[i] * jnp.exp2(bl) + _dot_tn(vs[i], (kks[i] * jnp.exp2(bl - bc2s[i])).astype(BF16)))
    return outs, new_sts


def _hgrn_kernel(q_ref, lf0_ref, lf1_ref, v_ref, g_ref, gain_ref, tri_ref, code_ref, emat_ref,
                 y_ref, of_ref, ob_ref, *w_refs, nc_ctx, nc_all, size, npair):
    nc_lat = nc_all - nc_ctx
    low = lax.broadcasted_iota(jnp.int32, (size, LANE), 1) < HG_DV

    def step(c, carry):
        cb = jnp.where(c < nc_ctx, nc_ctx - 1 - c, nc_ctx + nc_lat - 1 - (c - nc_ctx))
        sf = pl.ds(pl.multiple_of(c * size, size), size)
        sb = pl.ds(pl.multiple_of(cb * size, size), size)
        qs, lfs, vs, tris, codes, revs = [], [], [], [], [], []
        for hh in range(2 * npair):
            ln = slice(hh * LANE, (hh + 1) * LANE)
            pv = slice((hh // 2) * LANE, (hh // 2 + 1) * LANE)
            qs += [q_ref[0, sf, ln].astype(F32), q_ref[0, sb, ln].astype(F32)]
            lfs += [lf0_ref[0, sf, ln], lf1_ref[0, sb, ln]]
            vs += [v_ref[0, sf, pv], v_ref[0, sb, pv]]
            tris += [tri_ref[0], tri_ref[1]]
            codes += [code_ref[0], code_ref[1]]
            revs += [False, True]
        outs, sts = _hg_chunks(qs, lfs, vs, list(carry), tris, codes, emat_ref[...], revs, w_refs)
        for p in range(npair):
            pv = slice(p * LANE, (p + 1) * LANE)
            of_ref[sf, pv] = jnp.where(low, outs[4 * p], outs[4 * p + 2])
            ob_ref[sb, pv] = jnp.where(low, outs[4 * p + 1], outs[4 * p + 3])
        return tuple(sts)

    st0 = jnp.zeros((LANE, LANE), F32)
    lax.fori_loop(0, nc_all, step, (st0,) * (4 * npair), unroll=HG_UNROLL)

    def finish(c, _):
        sl = pl.ds(pl.multiple_of(c * size, size), size)
        for p in range(npair):
            pv = slice(p * LANE, (p + 1) * LANE)
            o = of_ref[sl, pv] + ob_ref[sl, pv]
            sq = o * o
            ms0 = jnp.sum(jnp.where(low, sq, 0.0), axis=1, keepdims=True) * (1.0 / HG_DV)
            ms1 = jnp.sum(jnp.where(low, 0.0, sq), axis=1, keepdims=True) * (1.0 / HG_DV)
            y = o * lax.rsqrt(jnp.where(low, ms0, ms1) + EPS) * gain_ref[0, :, pv]
            y_ref[0, sl, pv] = (y * g_ref[0, sl, pv].astype(F32)).astype(BF16)
        return 0

    lax.fori_loop(0, nc_all, finish, 0, unroll=6 // npair)


def _hgrn(hgq, hgf, hgv, hgg, gain, layer, ctx_len):
    nb, n_all, _ = hgq.shape
    size, npair = HG_CHUNK, HG_PAIRS
    nc_all = n_all // size
    ng = HEADS // (2 * npair)
    kern = functools.partial(_hgrn_kernel, nc_ctx=ctx_len // size, nc_all=nc_all, size=size, npair=npair)
    wide = pl.BlockSpec((1, n_all, 2 * npair * LANE), lambda b, h: (b, 0, h))
    slab = pl.BlockSpec((1, n_all, npair * LANE), lambda b, h: (b, 0, h))
    tri, code, emat = _hg_tables(size)
    return pl.pallas_call(
        kern,
        grid=(nb, ng),
        in_specs=[wide, wide,
                  pl.BlockSpec((1, n_all, 2 * npair * LANE), lambda b, h: (b, 0, ng + h)),
                  slab, slab,
                  pl.BlockSpec((1, 1, npair * LANE), lambda b, h: (layer * ng + h, 0, 0)),
                  pl.BlockSpec(tri.shape, lambda b, h: (0, 0, 0)),
                  pl.BlockSpec(code.shape, lambda b, h: (0, 0, 0)),
                  pl.BlockSpec(emat.shape, lambda b, h: (0, 0))],
        out_specs=slab,
        out_shape=jax.ShapeDtypeStruct((nb, n_all, HG_W), BF16),
        scratch_shapes=[pltpu.VMEM((n_all, npair * LANE), F32), pltpu.VMEM((n_all, npair * LANE), F32)]
        + [pltpu.VMEM((size, LANE), F32)] * (4 * npair),
        compiler_params=_cparams(("parallel", "arbitrary")),
        name="hgrn2",
    )(hgq, hgf, hgf, hgv, hgg, gain.reshape(-1, 1, npair * LANE), tri, code, emat)


def _attn_kernel(q_ref, k_ref, vt_ref, lam_ref, gain_ref, y_ref, *scratch, tq, kc, nh, ctx_len, lam_init):
    n_all = k_ref.shape[1]
    n_tiles = (n_all - ctx_len) // tq
    s_refs = [scratch[2 * hh:2 * hh + 2] for hh in range(nh)]
    m_refs = [scratch[2 * nh + 2 * hh:2 * nh + 2 * hh + 2] for hh in range(nh)]
    lv = lam_ref[0]
    lam = (jnp.exp(jnp.sum(lv[0:1] * lv[1:2], axis=1, keepdims=True))
           - jnp.exp(jnp.sum(lv[2:3] * lv[3:4], axis=1, keepdims=True)) + lam_init)
    lane = lax.broadcasted_iota(jnp.int32, (tq, LANE), 1)
    vrow = lax.broadcasted_iota(jnp.int32, (LANE, kc), 0)
    lanes = [slice(hh * LANE, (hh + 1) * LANE) for hh in range(nh)]

    def rows(t):
        return pl.ds(pl.multiple_of(ctx_len + t * tq, tq), tq)

    def load_qq(sl, hh):
        q = q_ref[0, sl, lanes[hh]]
        zero = jnp.zeros_like(q)
        return jnp.concatenate([jnp.where(lane < DA_HD, q, zero), jnp.where(lane < DA_HD, zero, q)], axis=0)

    def stage_a_chunk(qq, hh, slot, c, m):
        s = _dot_nt(k_ref[0, c * kc:(c + 1) * kc, lanes[hh]], qq)
        s_refs[hh][slot][c * kc:(c + 1) * kc, :] = s
        cm = jnp.max(s, axis=0, keepdims=True)
        return cm if m is None else jnp.maximum(m, cm)

    def stage_b_chunk(hh, slot, c, m, acc):
        p = jnp.exp2(s_refs[hh][slot][c * kc:(c + 1) * kc, :] - m).astype(BF16)
        vt = vt_ref[0, lanes[hh], c * kc:(c + 1) * kc]
        pv = _dot(jnp.where(vrow == DA_VD, jnp.ones_like(vt), vt), p)
        return pv if acc is None else acc + pv

    def finish(acc, sl, hh):
        r0 = 1.0 / acc[DA_VD:DA_VD + 1, 0:tq]
        r1 = lam / acc[DA_VD:DA_VD + 1, tq:2 * tq]
        o = (acc[:, 0:tq] * r0 - acc[:, tq:2 * tq] * r1).T
        o = jnp.where(lane < DA_VD, o, 0.0)
        y = o * lax.rsqrt(jnp.sum(o * o, axis=1, keepdims=True) * (1.0 / DA_VD) + EPS) * gain_ref[0]
        y_ref[0, sl, lanes[hh]] = (y * (1.0 - lam_init)).astype(BF16)

    def stages(sl_a, sl_b, slot_a, nk_a, nk_b):
        slot_b = 1 - slot_a
        qqs = [load_qq(sl_a, hh) for hh in range(nh)] if sl_a is not None else None
        m_bs = [m_refs[hh][slot_b][0:1, :] for hh in range(nh)] if sl_b is not None else None
        ms, accs = [None] * nh, [None] * nh
        for c in range(max(nk_a, nk_b) // kc):
            for hh in range(nh):
                if sl_a is not None and c < nk_a // kc:
                    ms[hh] = stage_a_chunk(qqs[hh], hh, slot_a, c, ms[hh])
                if sl_b is not None and c < nk_b // kc:
                    accs[hh] = stage_b_chunk(hh, slot_b, c, m_bs[hh], accs[hh])
        for hh in range(nh):
            if sl_a is not None:
                m_refs[hh][slot_a][...] = jnp.broadcast_to(ms[hh], (8, 2 * tq))
            if sl_b is not None:
                finish(accs[hh], sl_b, hh)

    ctx_rows = pl.ds(0, tq)
    stages(ctx_rows, None, 1, ctx_len, 0)
    stages(rows(0), ctx_rows, 0, n_all, ctx_len)

    def body(i, _):
        t = 1 + 2 * i
        stages(rows(t), rows(t - 1), 1, n_all, n_all)
        stages(rows(t + 1), rows(t), 0, n_all, n_all)
        return 0

    lax.fori_loop(0, (n_tiles - 2) // 2, body, 0)
    stages(rows(n_tiles - 1), rows(n_tiles - 2), 1, n_all, n_all)
    stages(None, rows(n_tiles - 1), 0, 0, n_all)


def _attn(daq, dak, dav_t, lam_pad, gain, layer, ctx_len, lam_init):
    nb, n_all, _ = daq.shape
    tq, kc, nh = ATT_TQ, ATT_KC, ATT_HPS
    n_tiles = (n_all - ctx_len) // tq
    assert n_tiles >= 2 and n_tiles % 2 == 0 and ctx_len == tq and ctx_len % kc == 0 and n_all % kc == 0
    kern = functools.partial(_attn_kernel, tq=tq, kc=kc, nh=nh, ctx_len=ctx_len, lam_init=lam_init)
    slab = pl.BlockSpec((1, n_all, nh * LANE), lambda b, h: (b, 0, h))
    sbuf, mbuf = pltpu.VMEM((n_all, 2 * tq), F32), pltpu.VMEM((8, 2 * tq), F32)
    return pl.pallas_call(
        kern,
        grid=(nb, HEADS // nh),
        in_specs=[slab, slab,
                  pl.BlockSpec((1, nh * LANE, n_all), lambda b, h: (b, h, 0)),
                  pl.BlockSpec((1, 4, LANE), lambda b, h: (layer, 0, 0)),
                  pl.BlockSpec((1, 1, LANE), lambda b, h: (layer, 0, 0))],
        out_specs=slab,
        out_shape=jax.ShapeDtypeStruct((nb, n_all, SLAB), BF16),
        scratch_shapes=[sbuf] * (2 * nh) + [mbuf] * (2 * nh),
        compiler_params=_cparams(("parallel", "arbitrary")),
        name="diff_attn",
    )(daq, dak, dav_t, lam_pad, gain)


def _resident(shape, index_map):
    return pl.BlockSpec(shape, index_map, pipeline_mode=pl.Buffered(1))


def _mix_ffn_kernel(yml_ref, yhg_ref, yda_ref, x_ref, g1x_ref, g1c_ref, a2x_ref, a2c_ref, s2x_ref, s2c_ref,
                    g2x_ref, g2c_ref, wo_ref, gpm_ref, gpf_ref, wg_ref, wu_ref, wd_ref, gqf_ref, o_ref, *, tm, nsub, ctx_len, t0):
    row0 = (pl.program_id(1) + t0) * tm
    split = min(ctx_len % tm if ctx_len % tm else tm, tm)
    top_is_ctx = row0 < ctx_len

    def mod(c_ref, x_ref_, lo):
        return jnp.where(top_is_ctx, c_ref[0, 0], x_ref_[0, 0]) if lo < split else x_ref_[0, 0]

    sub = tm // nsub
    x1s, h2s = [], []
    for i in range(nsub):
        lo, rs = i * sub, slice(i * sub, (i + 1) * sub)
        y = (_dot(yml_ref[0, rs, :], wo_ref[0, 0:SLAB, :]) + _dot(yhg_ref[0, rs, :], wo_ref[0, SLAB:SLAB + HG_W, :])
             + _dot(yda_ref[0, rs, :], wo_ref[0, SLAB + HG_W:2 * SLAB + HG_W, :]))
        x1 = x_ref[0, rs, :] + _rms(y, mod(g1c_ref, g1x_ref, lo) * gpm_ref[0])
        h2s.append((_rms(x1, gpf_ref[0] * (1.0 + mod(a2c_ref, a2x_ref, lo))) + mod(s2c_ref, s2x_ref, lo)).astype(BF16))
        x1s.append(x1)
    for i in range(nsub):
        lo, rs = i * sub, slice(i * sub, (i + 1) * sub)
        act = _silu(_dot(h2s[i], wg_ref[0])) * _dot(h2s[i], wu_ref[0])
        f = _dot(act.astype(BF16), wd_ref[0])
        o_ref[0, rs, :] = x1s[i] + _rms(f, mod(g2c_ref, g2x_ref, lo) * gqf_ref[0])


def _mix_ffn(yml, yhg, yda, xs, mod4, w_out_pad, g_post_mix, g_pre_ffn, wg, wu, wd, g_post_ffn, layer, ctx_len,
             tm, nsub, t0):
    nb, n_all, d = xs.shape
    dff = wg.shape[-1]
    assert tm % nsub == 0 and (ctx_len % tm) % (tm // nsub) == 0
    kern = functools.partial(_mix_ffn_kernel, tm=tm, nsub=nsub, ctx_len=ctx_len, t0=t0)
    lsel = lambda b, t: (layer, 0, 0)
    tile = lambda width: pl.BlockSpec((1, tm, width), lambda b, t: (b, t + t0, 0))
    return pl.pallas_call(
        kern,
        grid=(nb, n_all // tm - t0),
        in_specs=[tile(SLAB), tile(HG_W), tile(SLAB), tile(d)] + _mod_specs(layer, nb, (2, 4, 3, 5), d) + [
            _resident((1, 2 * SLAB + HG_W, d), lsel),
            pl.BlockSpec((1, 1, d), lsel),
            pl.BlockSpec((1, 1, d), lsel),
            _resident((1, d, dff), lsel),
            _resident((1, d, dff), lsel),
            _resident((1, dff, d), lsel),
            pl.BlockSpec((1, 1, d), lsel)],
        out_specs=pl.BlockSpec((1, tm, d), lambda b, t: (b, t, 0)),
        out_shape=jax.ShapeDtypeStruct((nb, n_all - t0 * tm, d), F32),
        compiler_params=_cparams(("parallel", "arbitrary")),
        name="mix_ffn",
    )(yml, yhg, yda, xs, *([mod4] * 8), w_out_pad, g_post_mix, g_pre_ffn, wg, wu, wd, g_post_ffn)


def _pad_heads(w, hd):
    lead = w.shape[:-1]
    w = w.reshape(*lead, HEADS, hd)
    w = jnp.pad(w, [(0, 0)] * len(lead) + [(0, 0), (0, LANE - hd)])
    return w.reshape(*lead, SLAB)


def _pad_in_cols(w, extra_f=None):
    sizes = (4 * ML_HD,) * 4 + (2 * HEADS, 2 * HEADS, HEADS * HG_DK, 2 * HEADS * HG_DK, HEADS * HG_DV, HEADS * HG_DV,
             2 * HEADS * DA_HD, 2 * HEADS * DA_HD, HEADS * DA_VD)
    parts = []
    off = 0
    for s in sizes:
        parts.append(w[..., off:off + s])
        off += s
    gi, gf = parts[4], parts[5]
    if extra_f is not None:
        gf = gf + extra_f
    gate = jnp.concatenate([gi, gf], axis=-1)
    gate = jnp.pad(gate, [(0, 0)] * (gate.ndim - 1) + [(0, LANE - 4 * HEADS)])
    return jnp.concatenate([
        _pad_heads(parts[0], ML_HD), _pad_heads(parts[1], ML_HD), _pad_heads(parts[2], ML_HD), _pad_heads(parts[3], ML_HD),
        gate, parts[6], parts[7], parts[8], parts[9],
        _pad_heads(parts[10], DA_VD), _pad_heads(parts[11], DA_VD), _pad_heads(parts[12], DA_VD)], axis=-1)


def _pad_out_rows(w_out):
    depth, _, d = w_out.shape

    def pad_rows(w, hd):
        w = jnp.pad(w.reshape(depth, HEADS, hd, d), ((0, 0), (0, 0), (0, LANE - hd), (0, 0)))
        return w.reshape(depth, SLAB, d)

    ml, hg, da = w_out[:, :4 * ML_HD], w_out[:, 4 * ML_HD:4 * ML_HD + HG_W], w_out[:, 4 * ML_HD + HG_W:]
    return jnp.concatenate([pad_rows(ml, ML_HD), hg, pad_rows(da, DA_VD)], axis=1)


def _rope_tables(n_lat, ctx_len):
    rows = n_lat // GRID_W
    row = jnp.repeat(jnp.arange(rows), GRID_W).astype(F32)
    col = jnp.tile(jnp.arange(GRID_W), rows).astype(F32)
    half = DA_HD // 2
    inv = ROPE_BASE ** (-jnp.arange(0, half, 2, dtype=F32) / half)
    ang_r, ang_c = row[:, None] * inv, col[:, None] * inv
    zero = jnp.zeros_like(ang_r)

    def one_map(fr, fc, sel):
        r, c = fr(ang_r), fc(ang_c)
        if sel == "cos":
            return jnp.concatenate([r, r, c, c], axis=1)
        if sel == "a":
            return jnp.concatenate([-r, zero, -c, zero], axis=1)
        return jnp.concatenate([zero, r, zero, c], axis=1)

    tabs = []
    for sel, fn in (("cos", jnp.cos), ("a", jnp.sin), ("b", jnp.sin)):
        m = one_map(fn, fn, sel)
        lat = jnp.pad(jnp.concatenate([m, m], axis=1), ((0, 0), (0, LANE - 2 * DA_HD)))
        fill = 1.0 if sel == "cos" else 0.0
        ctx_rows = jnp.pad(jnp.full((ctx_len, 2 * DA_HD), fill, F32), ((0, 0), (0, LANE - 2 * DA_HD)))
        tabs.append(jnp.concatenate([ctx_rows, lat], axis=0))
    return tabs


def kernel(x, c, ctx, c_ctx, w_ada, b_ada, g_pre_mix, g_post_mix, g_pre_ffn, g_post_ffn, w_in, b_in, w_out,
           ml_f_bias, ml_norm, hg_lb, hg_norm, da_lambda, da_norm, w_ffn_gate, w_ffn_up, w_ffn_down):
    nb, n_lat, d = x.shape
    ctx_len = ctx.shape[1]
    depth = w_ada.shape[0]
    n_all = ctx_len + n_lat

    w_pad = _pad_in_cols(w_in).astype(BF16)
    b_pad = _pad_in_cols(b_in, extra_f=ml_f_bias)[:, None, :]
    w_out_pad = _pad_out_rows(w_out).astype(BF16)
    wg, wu, wd = w_ffn_gate.astype(BF16), w_ffn_up.astype(BF16), w_ffn_down.astype(BF16)
    ml_gain = _pad_heads(ml_norm, ML_HD).reshape(depth * HEADS, 1, LANE)
    hg_gain = hg_norm.reshape(-1, 1, LANE)
    da_gain = jnp.pad(da_norm, ((0, 0), (0, LANE - DA_VD)))[:, None, :]
    lam_pad = jnp.pad(da_lambda.astype(F32), ((0, 0), (0, 0), (0, LANE - DA_HD)))
    sm = jax.nn.softmax(hg_lb.astype(F32), axis=0)
    lbs = jnp.cumsum(sm, axis=0) - sm[0:1]
    lbf = jnp.maximum(lbs, LB_FLOOR)[:, None, :]
    oml = (1.0 - lbs)[:, None, :]
    rope_tabs = _rope_tables(n_lat, ctx_len)
    r3 = lambda g: g[:, None, :]

    mp = -(-(nb + 1) // 8) * 8
    cc = jnp.concatenate([c, c_ctx[None, :], jnp.zeros((mp - nb - 1, d), F32)], axis=0)
    mod4 = _ada(cc, w_ada, b_ada).reshape(depth, mp, 1, 6 * d)

    fuse_concat = IN_TM == 3 * ctx_len and n_all % IN_TM == 0
    xs = None if fuse_concat else jnp.concatenate([ctx, x], axis=1)
    for l in range(depth):
        lam_init = 0.8 - 0.6 * math.exp(-0.3 * l)
        if l == 0 and fuse_concat:
            (mlq, mlk, mlv, mlo, gates, hgq, hgf, hgv, hgg, daq, dak, dav_t, xs) = _inproj(
                (ctx, x), mod4, r3(g_pre_mix), w_pad, b_pad, rope_tabs, lbf, oml, l, ctx_len, IN_TM, IN_SUB)
        else:
            (mlq, mlk, mlv, mlo, gates, hgq, hgf, hgv, hgg, daq, dak, dav_t) = _inproj(
                xs, mod4, r3(g_pre_mix), w_pad, b_pad, rope_tabs, lbf, oml, l, ctx_len,
                *((IN_TM, IN_SUB) if n_all % IN_TM == 0 else (IN_TM_ALT, IN_SUB_ALT)))
        gates_t = gates.reshape(nb, 4 * HEADS, n_all // ML_CHUNK, ML_CHUNK)
        yml = _mlstm(mlq, mlk, mlv, mlo, gates_t, ml_gain, l, ctx_len)
        yhg = _hgrn(hgq, hgf, hgv, hgg, hg_gain, l, ctx_len)
        yda = _attn(daq, dak, dav_t, lam_pad, da_gain, l, ctx_len, lam_init)
        last = l == depth - 1
        tm, nsub = (MIX_TM, MIX_SUB) if not last and n_all % MIX_TM == 0 else (MIX_TM_LAST, MIX_SUB_LAST)
        xs = _mix_ffn(yml, yhg, yda, xs, mod4, w_out_pad, r3(g_post_mix), r3(g_pre_ffn), wg, wu, wd,
                      r3(g_post_ffn), l, ctx_len, tm, nsub, ctx_len // tm if last else 0)
    return xs
```
